```python
import jax
import jax.numpy as jnp
from jax import lax
import numpy as np

D_MODEL = 1024
BATCH = 4
SEQ = 4096
DEPTH = 2

CTX_LEN = 256
GRID_W = 64
CHUNK = 64
CONV_W = 3
EPS = 1e-6
M_INIT = -1e30
N_MOD = 6
N_EVEN = (DEPTH + 1) // 2
N_ODD = DEPTH // 2

GLA_HEADS = 4
GLA_DK = D_MODEL // (2 * GLA_HEADS)
GLA_DV = D_MODEL // GLA_HEADS
GLA_QK = GLA_HEADS * GLA_DK
GLA_V = GLA_HEADS * GLA_DV
GLA_GATE_RANK = 16
GLA_GATE_TAU = 16.0

SSD_HEADDIM = 64
SSD_HEADS = D_MODEL // SSD_HEADDIM
SSD_STATE = 128
SSD_GROUPS = 2
SSD_INNER = SSD_HEADS * SSD_HEADDIM
SSD_BC = SSD_GROUPS * SSD_STATE
SSD_XBC = SSD_INNER + 2 * SSD_BC

A_SIZES = (GLA_QK, GLA_QK, GLA_V, GLA_V, 2 * GLA_GATE_RANK, SSD_INNER, SSD_XBC, 2 * SSD_HEADS)
A_SPLITS = [sum(A_SIZES[:i + 1]) for i in range(len(A_SIZES) - 1)]
A_IN = sum(A_SIZES)
A_MIX = GLA_V + SSD_INNER

MLSTM_INNER = 2 * D_MODEL
MLSTM_HEADS = 4
MLSTM_DH = MLSTM_INNER // MLSTM_HEADS
QKV_BLOCK = 4

D_FF = -(-(8 * D_MODEL) // (3 * 256)) * 256

kernel_name = 'hybrid_gla_ssd_mlstm_prefix_trunk'


def rmsnorm(x, w):
    xf = x.astype(jnp.float32)
    xf = xf * lax.rsqrt(jnp.mean(xf * xf, axis=-1, keepdims=True) + EPS)
    return (xf * w.astype(jnp.float32)).astype(x.dtype)


def layernorm_nobias(x, w):
    xf = x.astype(jnp.float32)
    mu = jnp.mean(xf, axis=-1, keepdims=True)
    var = jnp.mean(jnp.square(xf - mu), axis=-1, keepdims=True)
    return ((xf - mu) * lax.rsqrt(var + EPS) * w.astype(jnp.float32)).astype(x.dtype)


def modulate(h, shift, scale):
    return h * (1.0 + scale) + shift


def swiglu(h, w1, w3, w2):
    return (jax.nn.silu(h @ w1) * (h @ w3)) @ w2


def dwconv_centred(x, w, b):
    left = (CONV_W - 1) // 2
    y = lax.conv_general_dilated(x, w[:, None, :], window_strides=(1,),
                                 padding=[(left, CONV_W - 1 - left)],
                                 dimension_numbers=('NWC', 'WIO', 'NWC'),
                                 feature_group_count=x.shape[-1])
    return y + b


def to_heads(t, n):
    b, l, f = t.shape
    return t.reshape(b, l, n, f // n).transpose(0, 2, 1, 3)


def flip_seq(t):
    return jnp.flip(t, axis=2)


def to_colmajor(x):
    b, l, d = x.shape
    rows = l // GRID_W
    return x.reshape(b, rows, GRID_W, d).transpose(0, 2, 1, 3).reshape(b, l, d)


def from_colmajor(x):
    b, l, d = x.shape
    rows = l // GRID_W
    return x.reshape(b, GRID_W, rows, d).transpose(0, 2, 1, 3).reshape(b, l, d)


def _chunks(t):
    return jnp.moveaxis(t.reshape(t.shape[:2] + (t.shape[2] // CHUNK, CHUNK) + t.shape[3:]), 2, 0)


def _unchunk(t):
    t = jnp.moveaxis(t, 0, 2)
    return t.reshape(t.shape[:2] + (-1,) + t.shape[4:])


def gated_linear_scan(q, k, v, log_a, s0):
    f32 = jnp.float32
    causal = jnp.tril(jnp.ones((CHUNK, CHUNK), dtype=bool))
    vector_decay = log_a.shape[-1] != 1

    def step(s, inp):
        qc, kc, vc, gc = inp
        g_cum = jnp.cumsum(gc, axis=2)
        g_last = g_cum[:, :, -1:, :]
        rel = jnp.exp(jnp.where(causal[:, :, None],
                                g_cum[:, :, :, None, :] - g_cum[:, :, None, :, :], -jnp.inf))
        if vector_decay:
            scores = jnp.einsum('bhik,bhjk,bhijk->bhij', qc, kc, rel)
        else:
            scores = jnp.einsum('bhik,bhjk->bhij', qc, kc) * rel[..., 0]
        o = (jnp.einsum('bhij,bhjv->bhiv', scores, vc)
             + jnp.einsum('bhik,bhkv->bhiv', qc * jnp.exp(g_cum), s))
        s = (s * jnp.exp(g_last)[:, :, 0, :, None]
             + jnp.einsum('bhjk,bhjv->bhkv', kc * jnp.exp(g_last - g_cum), vc))
        return s, o

    xs = tuple(_chunks(t.astype(f32)) for t in (q, k, v, log_a))
    s_fin, o = lax.scan(step, s0.astype(f32), xs)
    return _unchunk(o).astype(v.dtype), s_fin


def bidir_scan(q, k_f, k_b, v, la_f, la_b, s0_f, s0_b):
    o_f, s_f = gated_linear_scan(q, k_f, v, la_f, s0_f)
    o_b, s_b = gated_linear_scan(flip_seq(q), flip_seq(k_b), flip_seq(v), flip_seq(la_b), s0_b)
    return o_f + flip_seq(o_b), s_f, s_b


def mlstm_scan(q, k, v, log_i, log_f, state):
    f32 = jnp.float32
    causal = jnp.tril(jnp.ones((CHUNK, CHUNK), dtype=bool))

    def step(carry, inp):
        c0, n0, m0 = carry
        qc, kc, vc, ic, fc = inp
        b = jnp.cumsum(fc, axis=-1)
        log_d = jnp.where(causal, b[..., :, None] - b[..., None, :] + ic[..., None, :], -jnp.inf)
        log_inter = b + m0[..., None]
        m_row = jnp.maximum(log_inter, jnp.max(log_d, axis=-1))
        s = jnp.einsum('bhid,bhjd->bhij', qc, kc) * jnp.exp(log_d - m_row[..., None])
        w_inter = jnp.exp(log_inter - m_row)
        num = (jnp.einsum('bhij,bhje->bhie', s, vc)
               + w_inter[..., None] * jnp.einsum('bhid,bhde->bhie', qc, c0))
        den = jnp.sum(s, axis=-1) + w_inter * jnp.einsum('bhid,bhd->bhi', qc, n0)
        h = num / jnp.maximum(jnp.abs(den), jnp.exp(-m_row))[..., None]
        log_w = b[..., -1:] - b + ic
        m_new = jnp.maximum(b[..., -1] + m0, jnp.max(log_w, axis=-1))
        w_k = jnp.exp(log_w - m_new[..., None])[..., None]
        decay0 = jnp.exp(b[..., -1] + m0 - m_new)
        c_new = decay0[..., None, None] * c0 + jnp.einsum('bhjd,bhje->bhde', kc * w_k, vc)
        n_new = decay0[..., None] * n0 + jnp.sum(kc * w_k, axis=2)
        return (c_new, n_new, m_new), h

    xs = tuple(_chunks(t.astype(f32)) for t in (q, k, v, log_i, log_f))
    final, h = lax.scan(step, state, xs)
    return _unchunk(h).astype(v.dtype), final


def zero_states_gla_ssd(b):
    f32 = jnp.float32
    gla = jnp.zeros((b, GLA_HEADS, GLA_DK, GLA_DV), f32)
    ssd = jnp.zeros((b, SSD_HEADS, SSD_STATE, SSD_HEADDIM), f32)
    return (gla, gla, ssd, ssd)


def zero_states_mlstm(b):
    f32 = jnp.float32
    st = (jnp.zeros((b, MLSTM_HEADS, MLSTM_DH, MLSTM_DH), f32),
          jnp.zeros((b, MLSTM_HEADS, MLSTM_DH), f32),
          jnp.full((b, MLSTM_HEADS), M_INIT, f32))
    return (st, st)


def gla_ssd_features(h, init, w_in, gla_w_gate, gla_b_gate, gla_norm, conv_w, conv_b,
                     a_log, dt_bias, d_skip, ssd_norm):
    b, l, _ = h.shape
    f32 = jnp.float32
    q, k, v, g, r, z, xbc, dt = jnp.split(h @ w_in, A_SPLITS, axis=-1)
    qh = to_heads(q, GLA_HEADS) * (GLA_DK ** -0.5)
    kh = to_heads(k, GLA_HEADS)
    vh = to_heads(v, GLA_HEADS)
    gate_logit = jnp.einsum('bldr,drk->bldk', r.reshape(b, l, 2, GLA_GATE_RANK), gla_w_gate) + gla_b_gate
    log_alpha = jax.nn.log_sigmoid(gate_logit.astype(f32)) / GLA_GATE_TAU
    o_gla, sg_f, sg_b = bidir_scan(qh, kh, kh, vh,
                                   to_heads(log_alpha[:, :, 0], GLA_HEADS),
                                   to_heads(log_alpha[:, :, 1], GLA_HEADS), init[0], init[1])
    o_gla = rmsnorm(o_gla.transpose(0, 2, 1, 3), gla_norm).reshape(b, l, GLA_V) * jax.nn.silu(g)
    xbc = jax.nn.silu(dwconv_centred(xbc, conv_w, conv_b))
    xs, bm, cm = jnp.split(xbc, [SSD_INNER, SSD_INNER + SSD_BC], axis=-1)
    dt = jax.nn.softplus(dt.reshape(b, l, 2, SSD_HEADS).astype(f32) + dt_bias.astype(f32))
    dt = dt.transpose(2, 0, 3, 1)[..., None]
    log_a = dt * (-jnp.exp(a_log.astype(f32)))[:, None, :, None, None]
    rep = SSD_HEADS // SSD_GROUPS
    xh = to_heads(xs, SSD_HEADS)
    bh = jnp.repeat(to_heads(bm, SSD_GROUPS), rep, axis=1)
    ch = jnp.repeat(to_heads(cm, SSD_GROUPS), rep, axis=1)
    y, ss_f, ss_b = bidir_scan(ch, bh * dt[0], bh * dt[1], xh, log_a[0], log_a[1], init[2], init[3])
    y = y + xh * d_skip[None, :, None, None]
    y = y.transpose(0, 2, 1, 3).reshape(b, l, SSD_INNER) * jax.nn.silu(z)
    y = rmsnorm(y.reshape(b, l, SSD_GROUPS, SSD_INNER // SSD_GROUPS),
                ssd_norm.reshape(SSD_GROUPS, SSD_INNER // SSD_GROUPS)).reshape(b, l, SSD_INNER)
    return jnp.concatenate([o_gla, y.astype(o_gla.dtype)], axis=-1), (sg_f, sg_b, ss_f, ss_b)


def mlstm_features(h, init, w_up, conv_w, conv_b, wq, wk, wv, w_if, b_if, mh_norm, skip):
    b, l, _ = h.shape
    f32 = jnp.float32
    xm, z = jnp.split(h @ w_up, 2, axis=-1)
    xc = jax.nn.silu(dwconv_centred(xm, conv_w, conv_b))

    def blockdiag(t, w):
        return jnp.einsum('blnd,nde->blne', t.reshape(b, l, -1, QKV_BLOCK), w).reshape(b, l, MLSTM_INNER)

    q, k, v = blockdiag(xc, wq), blockdiag(xc, wk), blockdiag(xm, wv)
    gates = sum(jnp.einsum('blc,dcg->dbgl', t, w_if[:, i * MLSTM_INNER:(i + 1) * MLSTM_INNER]).astype(f32)
                for i, t in enumerate((q, k, v)))
    gates = gates + b_if.astype(f32)[:, None, :, None]
    log_i = gates[:, :, :MLSTM_HEADS]
    log_f = jax.nn.log_sigmoid(gates[:, :, MLSTM_HEADS:])
    qh = to_heads(q, MLSTM_HEADS)
    kh = to_heads(k, MLSTM_HEADS) * (MLSTM_DH ** -0.5)
    vh = to_heads(v, MLSTM_HEADS)
    h_f, st_f = mlstm_scan(qh, kh, vh, log_i[0], log_f[0], init[0])
    h_b, st_b = mlstm_scan(flip_seq(qh), flip_seq(kh), flip_seq(vh),
                           flip_seq(log_i[1]), flip_seq(log_f[1]), init[1])
    hs = (h_f + flip_seq(h_b)).transpose(0, 2, 1, 3)
    hs = layernorm_nobias(hs, mh_norm.reshape(MLSTM_HEADS, MLSTM_DH)).reshape(b, l, MLSTM_INNER)
    return (hs + skip * xc) * jax.nn.silu(z), (st_f, st_b)


def setup_inputs(seed: int = 0) -> dict:
    key = jax.random.key(seed)
    keys = iter(jax.random.split(key, 48))

    def nrm(shape, scale):
        return jax.random.normal(next(keys), shape, jnp.float32) * scale

    def gain(shape):
        return 1.0 + nrm(shape, 0.05)

    def uni(shape, lo, hi):
        return jax.random.uniform(next(keys), shape, jnp.float32, lo, hi)

    dt0 = jnp.exp(uni((N_EVEN, 2, SSD_HEADS), float(np.log(1e-3)), float(np.log(1e-1))))
    qkv_shape = (N_ODD, MLSTM_INNER // QKV_BLOCK, QKV_BLOCK, QKV_BLOCK)
    return {
        'x': nrm((BATCH, SEQ, D_MODEL), 1.0),
        'c': nrm((BATCH, D_MODEL), 1.0),
        'ctx': nrm((BATCH, CTX_LEN, D_MODEL), 1.0),
        'c_ctx': nrm((D_MODEL,), 1.0),
        'w_mod': nrm((DEPTH, D_MODEL, N_MOD * D_MODEL), D_MODEL ** -0.5),
        'b_mod': nrm((DEPTH, N_MOD * D_MODEL), 0.02),
        'norm_mix': gain((DEPTH, D_MODEL)),
        'norm_ffn': gain((DEPTH, D_MODEL)),
        'ffn_w1': nrm((DEPTH, D_MODEL, D_FF), D_MODEL ** -0.5),
        'ffn_w3': nrm((DEPTH, D_MODEL, D_FF), D_MODEL ** -0.5),
        'ffn_w2': nrm((DEPTH, D_FF, D_MODEL), D_FF ** -0.5),
        'a_w_in': nrm((N_EVEN, D_MODEL, A_IN), D_MODEL ** -0.5),
        'a_gla_w_gate': nrm((N_EVEN, 2, GLA_GATE_RANK, GLA_QK), GLA_GATE_RANK ** -0.5),
        'a_gla_b_gate': nrm((N_EVEN, 2, GLA_QK), 0.5),
        'a_gla_norm': gain((N_EVEN, GLA_DV)),
        'a_ssd_conv_w': nrm((N_EVEN, CONV_W, SSD_XBC), CONV_W ** -0.5),
        'a_ssd_conv_b': nrm((N_EVEN, SSD_XBC), 0.02),
        'a_ssd_A_log': jnp.log(uni((N_EVEN, 2, SSD_HEADS), 1.0, 16.0)),
        'a_ssd_dt_bias': dt0 + jnp.log(-jnp.expm1(-dt0)),
        'a_ssd_D': gain((N_EVEN, SSD_HEADS)),
        'a_ssd_norm': gain((N_EVEN, SSD_INNER)),
        'a_w_out': nrm((N_EVEN, A_MIX, D_MODEL), A_MIX ** -0.5),
        'b_w_up': nrm((N_ODD, D_MODEL, 2 * MLSTM_INNER), D_MODEL ** -0.5),
        'b_conv_w': nrm((N_ODD, CONV_W, MLSTM_INNER), CONV_W ** -0.5),
        'b_conv_b': nrm((N_ODD, MLSTM_INNER), 0.02),
        'b_wq': nrm(qkv_shape, QKV_BLOCK ** -0.5),
        'b_wk': nrm(qkv_shape, QKV_BLOCK ** -0.5),
        'b_wv': nrm(qkv_shape, QKV_BLOCK ** -0.5),
        'b_w_if': nrm((N_ODD, 2, 3 * MLSTM_INNER, 2 * MLSTM_HEADS), (3 * MLSTM_INNER) ** -0.5),
        'b_b_if': jnp.concatenate([nrm((N_ODD, 2, MLSTM_HEADS), 0.1),
                                   jnp.linspace(3.0, 6.0, MLSTM_HEADS) + nrm((N_ODD, 2, MLSTM_HEADS), 0.1)],
                                  axis=-1),
        'b_mh_norm': gain((N_ODD, MLSTM_INNER)),
        'b_skip': gain((N_ODD, MLSTM_INNER)),
        'b_w_down': nrm((N_ODD, MLSTM_INNER, D_MODEL), MLSTM_INNER ** -0.5),
        'final_norm': gain((D_MODEL,)),
    }


def reference(x, c, ctx, c_ctx, w_mod, b_mod, norm_mix, norm_ffn, ffn_w1, ffn_w3, ffn_w2,
              a_w_in, a_gla_w_gate, a_gla_b_gate, a_gla_norm, a_ssd_conv_w, a_ssd_conv_b,
              a_ssd_A_log, a_ssd_dt_bias, a_ssd_D, a_ssd_norm, a_w_out,
              b_w_up, b_conv_w, b_conv_b, b_wq, b_wk, b_wv, b_w_if, b_b_if, b_mh_norm, b_skip,
              b_w_down, final_norm):
    bsz = x.shape[0]
    for layer in range(DEPTH):
        last = layer == DEPTH - 1
        j = layer // 2
        mx = jnp.split((jax.nn.silu(c) @ w_mod[layer] + b_mod[layer])[:, None, :], N_MOD, axis=-1)
        mc = jnp.split(jax.nn.silu(c_ctx) @ w_mod[layer] + b_mod[layer], N_MOD, axis=-1)
        hx = modulate(rmsnorm(x, norm_mix[layer]), mx[0], mx[1])
        hc = modulate(rmsnorm(ctx, norm_mix[layer]), mc[0], mc[1])
        if layer % 2 == 0:
            pa = (a_w_in[j], a_gla_w_gate[j], a_gla_b_gate[j], a_gla_norm[j], a_ssd_conv_w[j],
                  a_ssd_conv_b[j], a_ssd_A_log[j], a_ssd_dt_bias[j], a_ssd_D[j], a_ssd_norm[j])
            fc, st = gla_ssd_features(hc, zero_states_gla_ssd(bsz), *pa)
            fx, _ = gla_ssd_features(hx, st, *pa)
            w_o = a_w_out[j]
        else:
            pb = (b_w_up[j], b_conv_w[j], b_conv_b[j], b_wq[j], b_wk[j], b_wv[j],
                  b_w_if[j], b_b_if[j], b_mh_norm[j], b_skip[j])
            fc, st = mlstm_features(hc, zero_states_mlstm(bsz), *pb)
            fx, _ = mlstm_features(to_colmajor(hx), st, *pb)
            fx = from_colmajor(fx)
            w_o = b_w_down[j]
        x = x + mx[2] * (fx @ w_o)
        x = x + mx[5] * swiglu(modulate(rmsnorm(x, norm_ffn[layer]), mx[3], mx[4]),
                               ffn_w1[layer], ffn_w3[layer], ffn_w2[layer])
        if not last:
            ctx = ctx + mc[2] * (fc @ w_o)
            ctx = ctx + mc[5] * swiglu(modulate(rmsnorm(ctx, norm_ffn[layer]), mc[3], mc[4]),
                                       ffn_w1[layer], ffn_w3[layer], ffn_w2[layer])
    return rmsnorm(x, final_norm)
```

```python
import functools

import jax
import jax.numpy as jnp
from jax import lax
from jax.experimental import pallas as pl
from jax.experimental.pallas import tpu as pltpu

F32 = jnp.float32
BF16 = jnp.bfloat16

D_MODEL = 1024
GRID_W = 64
EPS = 1e-6
N_MOD = 6

GLA_HEADS = 4
GLA_DK = 128
GLA_DV = 256
GLA_QK = GLA_HEADS * GLA_DK
GLA_V = GLA_HEADS * GLA_DV
GLA_GATE_RANK = 16
GLA_GATE_TAU = 16.0

SSD_HEADDIM = 64
SSD_HEADS = 16
SSD_STATE = 128
SSD_GROUPS = 2
SSD_INNER = SSD_HEADS * SSD_HEADDIM
SSD_BC = SSD_GROUPS * SSD_STATE
SSD_XBC = SSD_INNER + 2 * SSD_BC
HEADS_PER_GROUP = SSD_HEADS // SSD_GROUPS

MLSTM_INNER = 2 * D_MODEL
MLSTM_HEADS = 4
MLSTM_DH = MLSTM_INNER // MLSTM_HEADS
QKV_BLOCK = 4
BD_TILE = 256
N_BD_TILES = MLSTM_INNER // BD_TILE
MLSTM_EXT = MLSTM_DH + 128

D_FF = 2816

GLA_CHUNK = 64
SSD_CHUNK = 128
MLSTM_CHUNK = 256
HALO = 16
LANES = 128

VMEM_LIMIT = 56 * 1024 * 1024


def _cparams(sem):
    return pltpu.CompilerParams(dimension_semantics=sem, vmem_limit_bytes=VMEM_LIMIT)


def _const_spec(shape):
    nd = len(shape)
    return pl.BlockSpec(shape, lambda *_: (0,) * nd, pipeline_mode=pl.Buffered(1))


def _softplus(x):
    return jnp.maximum(x, 0.0) + jnp.log(1.0 + jnp.exp(-jnp.abs(x)))


def _log_sigmoid(x):
    return -_softplus(-x)


def _silu(x):
    return x / (1.0 + jnp.exp(-x))


def _seg_cumsum(x, seg, axis, reverse):
    n = x.shape[axis]
    idx = lax.broadcasted_iota(jnp.int32, x.shape, axis) % seg
    s = 1
    while s < seg:
        if reverse:
            shifted = pltpu.roll(x, n - s, axis)
            keep = idx < seg - s
        else:
            shifted = pltpu.roll(x, s, axis)
            keep = idx >= s
        x = x + jnp.where(keep, shifted, 0.0)
        s *= 2
    return x


def _norm_mod(xv, nw, shift, scale):
    ms = jnp.mean(xv * xv, axis=-1, keepdims=True)
    return (xv * lax.rsqrt(ms + EPS) * nw) * (1.0 + scale) + shift


def _dot(a, b):
    return jnp.dot(a, b, preferred_element_type=F32)


def _dot_nt(a, b):
    return lax.dot_general(a, b, (((1,), (1,)), ((), ())), preferred_element_type=F32)


def _dot_tn(a, b):
    return lax.dot_general(a, b, (((0,), (0,)), ((), ())), preferred_element_type=F32)


def _mod_kernel(c_ref, w_ref, b_ref, o_ref):
    a = _silu(c_ref[...]).astype(BF16)
    o_ref[0] = _dot(a, w_ref[0].astype(BF16)) + b_ref[0]


def _modulation(cc, w_mod, b_mod):
    depth = w_mod.shape[0]
    n = w_mod.shape[2]
    tn = n // 4
    rows = cc.shape[0]
    return pl.pallas_call(
        _mod_kernel,
        grid=(depth, n // tn),
        in_specs=[
            pl.BlockSpec((rows, D_MODEL), lambda l, j: (0, 0)),
            pl.BlockSpec((1, D_MODEL, tn), lambda l, j: (l, 0, j)),
            pl.BlockSpec((1, 1, tn), lambda l, j: (l, 0, j)),
        ],
        out_specs=pl.BlockSpec((1, rows, tn), lambda l, j: (l, 0, j)),
        out_shape=jax.ShapeDtypeStruct((depth, rows, n), F32),
        compiler_params=_cparams(("arbitrary", "arbitrary")),
        name="modulation",
    )(cc, w_mod, b_mod.reshape(depth, 1, n))


def _conv3(scr, tm, cw_ref, cb_ref, first, last):
    row = lax.broadcasted_iota(jnp.int32, (tm, 1), 0)
    prev = scr[HALO - 1:HALO - 1 + tm, :]
    cur = scr[HALO:HALO + tm, :]
    nxt = scr[HALO + 1:HALO + 1 + tm, :]
    prev = jnp.where(row == jnp.where(first, 0, -1), 0.0, prev)
    nxt = jnp.where(row == jnp.where(last, tm - 1, -1), 0.0, nxt)
    return cw_ref[0:1, :] * prev + cw_ref[1:2, :] * cur + cw_ref[2:3, :] * nxt + cb_ref[...]


def _inproj0_kernel(x_ref, xp_ref, xn_ref, mod_ref, nw_ref, wmain_ref, wsmall_ref, wdtT_ref, wg_ref, bg_ref,
                    cw_ref, cb_ref, dtb_ref, alog_ref, dtbT_ref, alogT_ref,
                    q_ref, k_ref, v_ref, g_ref, z_ref, gcum_ref, xs_ref, bc_ref, bT_ref, sd_ref, sdT_ref,
                    hext, xbc_scr, *, tm):
    i = pl.program_id(1)
    first = i == 0
    last = i == pl.num_programs(1) - 1
    shift = mod_ref[0, 0:1, :]
    scale = mod_ref[0, 1:2, :]
    nw = nw_ref[...]
    hext[0:HALO, :] = _norm_mod(xp_ref[0], nw, shift, scale).astype(BF16)
    hext[HALO:HALO + tm, :] = _norm_mod(x_ref[0], nw, shift, scale).astype(BF16)
    hext[HALO + tm:2 * HALO + tm, :] = _norm_mod(xn_ref[0], nw, shift, scale).astype(BF16)
    h = hext[HALO:HALO + tm, :]

    o = 0
    q_ref[0] = (_dot(h, wmain_ref[:, o:o + GLA_QK]) * (GLA_DK ** -0.5)).astype(BF16)
    o += GLA_QK
    k_ref[0] = _dot(h, wmain_ref[:, o:o + GLA_QK]).astype(BF16)
    o += GLA_QK
    v_ref[0] = _dot(h, wmain_ref[:, o:o + GLA_V]).astype(BF16)
    o += GLA_V
    g_ref[0] = _dot(h, wmain_ref[:, o:o + GLA_V]).astype(BF16)
    o += GLA_V
    z_ref[0] = _dot(h, wmain_ref[:, o:o + SSD_INNER]).astype(BF16)
    o += SSD_INNER

    xbc_scr[...] = _dot(hext[...], wmain_ref[:, o:o + SSD_XBC])
    y = _silu(_conv3(xbc_scr, tm, cw_ref, cb_ref, first, last))
    xs_ref[0] = y[:, :SSD_INNER].astype(BF16)
    bc_ref[0] = y[:, SSD_INNER:].astype(BF16)
    bT_ref[0] = jnp.transpose(y[:, SSD_INNER:SSD_INNER + SSD_BC]).astype(BF16)

    small = _dot(h, wsmall_ref[...])
    r = small[:, :2 * GLA_GATE_RANK].astype(BF16)
    la = _log_sigmoid(_dot(r, wg_ref[...]) + bg_ref[...]) * (1.0 / GLA_GATE_TAU)
    gcum_ref[0, :, :GLA_QK] = _seg_cumsum(la[:, :GLA_QK], GLA_CHUNK, 0, False)
    gcum_ref[0, :, GLA_QK:] = _seg_cumsum(la[:, GLA_QK:], GLA_CHUNK, 0, True)

    nd = 2 * SSD_HEADS
    dtT = _softplus(_dot_nt(wdtT_ref[...], h) + dtbT_ref[...])
    laT = dtT * (-jnp.exp(alogT_ref[...]))
    cumT = jnp.concatenate([_seg_cumsum(laT[:SSD_HEADS], SSD_CHUNK, 1, False),
                            _seg_cumsum(laT[SSD_HEADS:], SSD_CHUNK, 1, True)], axis=0)
    packed = jnp.concatenate([dtT, cumT, jnp.zeros((LANES - 2 * nd, tm), F32)], axis=0)
    sdT_ref[0] = packed[:2 * nd]
    sd_ref[0] = jnp.transpose(packed)


def _inproj0(x, mod, nw, wp, tm):
    b, l, _ = x.shape
    tm = min(tm, l)
    nt = l // tm
    hb = tm // HALO
    nhb = l // HALO
    bm = mod.shape[0]
    mod_idx = (lambda bi, i: (bi, 0, 0)) if bm > 1 else (lambda bi, i: (0, 0, 0))
    tok = lambda w: pl.BlockSpec((1, tm, w), lambda bi, i: (bi, i, 0))
    in_specs = [
        tok(D_MODEL),
        pl.BlockSpec((1, HALO, D_MODEL), lambda bi, i: (bi, jnp.maximum(i * hb - 1, 0), 0)),
        pl.BlockSpec((1, HALO, D_MODEL), lambda bi, i: (bi, jnp.minimum((i + 1) * hb, nhb - 1), 0)),
        pl.BlockSpec((1, N_MOD, D_MODEL), mod_idx),
        _const_spec((1, D_MODEL)),
        _const_spec(wp["w_main"].shape), _const_spec(wp["w_small"].shape), _const_spec(wp["w_dtT"].shape),
        _const_spec(wp["w_gate"].shape), _const_spec(wp["b_gate"].shape),
        _const_spec(wp["conv_w"].shape), _const_spec(wp["conv_b"].shape),
        _const_spec(wp["dt_bias"].shape), _const_spec(wp["a_log"].shape),
        _const_spec(wp["dt_biasT"].shape), _const_spec(wp["a_logT"].shape),
    ]
    outs = [
        ("q", GLA_QK, BF16), ("k", GLA_QK, BF16), ("v", GLA_V, BF16), ("g", GLA_V, BF16),
        ("z", SSD_INNER, BF16), ("gcum", 2 * GLA_QK, F32), ("xs", SSD_INNER, BF16),
        ("bc", 2 * SSD_BC, BF16),
    ]
    out_specs = [tok(w) for _, w, _ in outs]
    out_shape = [jax.ShapeDtypeStruct((b, l, w), dt) for _, w, dt in outs]
    out_specs += [pl.BlockSpec((1, SSD_BC, tm), lambda bi, i: (bi, 0, i)),
                  tok(LANES),
                  pl.BlockSpec((1, 4 * SSD_HEADS, tm), lambda bi, i: (bi, 0, i))]
    out_shape += [jax.ShapeDtypeStruct((b, SSD_BC, l), BF16),
                  jax.ShapeDtypeStruct((b, l, LANES), F32),
                  jax.ShapeDtypeStruct((b, 4 * SSD_HEADS, l), F32)]
    res = pl.pallas_call(
        functools.partial(_inproj0_kernel, tm=tm),
        grid=(b, nt),
        in_specs=in_specs,
        out_specs=out_specs,
        out_shape=out_shape,
        scratch_shapes=[pltpu.VMEM((tm + 2 * HALO, D_MODEL), BF16),
                        pltpu.VMEM((tm + 2 * HALO, SSD_XBC), F32)],
        compiler_params=_cparams(("parallel", "arbitrary")),
        name="inproj0",
    )(x, x, x, mod, nw, wp["w_main"], wp["w_small"], wp["w_dtT"], wp["w_gate"], wp["b_gate"],
      wp["conv_w"], wp["conv_b"], wp["dt_bias"], wp["a_log"], wp["dt_biasT"], wp["a_logT"])
    names = [n for n, _, _ in outs] + ["bT", "sd", "sdT"]
    return dict(zip(names, res))


def _level_ref(g, hsz, reverse):
    c, w = g.shape
    grp = 2 * hsz
    r = hsz if reverse else hsz - 1
    if grp >= 8:
        g3 = g.reshape(c // grp, grp, w)
        return jnp.broadcast_to(g3[:, r:r + 1, :], g3.shape).reshape(c, w)
    g3 = g.reshape(c // 8, 8, w)
    sub = lax.broadcasted_iota(jnp.int32, g3.shape, 1)
    out = None
    for t in range(8 // grp):
        cand = jnp.broadcast_to(g3[:, t * grp + r:t * grp + r + 1, :], g3.shape)
        out = cand if out is None else jnp.where(sub >= t * grp, cand, out)
    return out.reshape(c, w)


def _gla_chunk(q, k, v, g, s, reverse):
    c = q.shape[0]
    row = lax.broadcasted_iota(jnp.int32, (c, 1), 0)
    ii = lax.broadcasted_iota(jnp.int32, (c, c), 0)
    jj = lax.broadcasted_iota(jnp.int32, (c, c), 1)
    a = jnp.where(ii == jj, _dot_nt(q.astype(BF16), k.astype(BF16)), 0.0)
    hsz = 1
    while hsz < c:
        e = jnp.exp(-jnp.abs(g - _level_ref(g, hsz, reverse)))
        later = (row // hsz) % 2 == (0 if reverse else 1)
        ql = jnp.where(later, q * e, 0.0).astype(BF16)
        kl = jnp.where(later, 0.0, k * e).astype(BF16)
        a = a + jnp.where((ii // (2 * hsz)) == (jj // (2 * hsz)), _dot_nt(ql, kl), 0.0)
        hsz *= 2
    g_tot = g[0:1, :] if reverse else g[c - 1:c, :]
    qi = (q * jnp.exp(g)).astype(BF16)
    ki = (k * jnp.exp(g_tot - g)).astype(BF16)
    o = _dot(a.astype(BF16), v) + _dot(qi, s.astype(BF16))
    dk = g.shape[1]
    decay_col = jnp.transpose(jnp.broadcast_to(jnp.exp(g_tot), (dk, dk)))[:, 0:1]
    s_new = s * decay_col + _dot_tn(ki, v)
    return o, s_new


def _gla_kernel(qf_ref, kf_ref, vf_ref, gf_ref, qb_ref, kb_ref, vb_ref, gb_ref, s0_ref,
                of_ref, ob_ref, sfin_ref, s_scr):
    n = pl.program_id(1)

    @pl.when(n == 0)
    def _():
        s_scr[...] = s0_ref[0]

    for d, (q_ref, k_ref, v_ref, g_ref, o_ref) in enumerate(
            ((qf_ref, kf_ref, vf_ref, gf_ref, of_ref), (qb_ref, kb_ref, vb_ref, gb_ref, ob_ref))):
        for hd in range(GLA_HEADS):
            ks = slice(hd * GLA_DK, (hd + 1) * GLA_DK)
            vs = slice(hd * GLA_DV, (hd + 1) * GLA_DV)
            o, s_new = _gla_chunk(q_ref[0, :, ks].astype(F32), k_ref[0, :, ks].astype(F32),
                                  v_ref[0, :, vs], g_ref[0, :, ks], s_scr[d * GLA_HEADS + hd], d == 1)
            o_ref[0, :, vs] = o.astype(BF16)
            s_scr[d * GLA_HEADS + hd] = s_new

    @pl.when(n == pl.num_programs(1) - 1)
    def _():
        sfin_ref[0] = s_scr[...]


def _gla_scan(p, s0):
    b, l, _ = p["q"].shape
    c = GLA_CHUNK
    nc = l // c
    fwd = lambda w, col=0: pl.BlockSpec((1, c, w), lambda bi, n: (bi, n, col))
    bwd = lambda w, col=0: pl.BlockSpec((1, c, w), lambda bi, n: (bi, nc - 1 - n, col))
    st = pl.BlockSpec((1, 2 * GLA_HEADS, GLA_DK, GLA_DV), lambda bi, n: (bi, 0, 0, 0))
    of, ob, sfin = pl.pallas_call(
        _gla_kernel,
        grid=(b, nc),
        in_specs=[fwd(GLA_QK), fwd(GLA_QK), fwd(GLA_V), fwd(GLA_QK, 0),
                  bwd(GLA_QK), bwd(GLA_QK), bwd(GLA_V), bwd(GLA_QK, 1), st],
        out_specs=[fwd(GLA_V), bwd(GLA_V), st],
        out_shape=[jax.ShapeDtypeStruct((b, l, GLA_V), BF16), jax.ShapeDtypeStruct((b, l, GLA_V), BF16),
                   jax.ShapeDtypeStruct(s0.shape, F32)],
        scratch_shapes=[pltpu.VMEM((2 * GLA_HEADS, GLA_DK, GLA_DV), F32)],
        compiler_params=_cparams(("parallel", "arbitrary")),
        name="gla_scan",
    )(p["q"], p["k"], p["v"], p["gcum"], p["q"], p["k"], p["v"], p["gcum"], s0)
    return of, ob, sfin


def _ssd_dir(xs_ref, bc_ref, bT_ref, sd_ref, sdT_ref, y_ref, s_scr, d, reverse):
    c = SSD_CHUNK
    nh = SSD_HEADS
    sd = sd_ref[0]
    sdT = sdT_ref[0]
    ii = lax.broadcasted_iota(jnp.int32, (c, c), 0)
    jj = lax.broadcasted_iota(jnp.int32, (c, c), 1)
    causal = (jj >= ii) if reverse else (jj <= ii)
    lane = lax.broadcasted_iota(jnp.int32, (1, LANES), 1)
    lo_half = lane < SSD_HEADDIM
    e_in = jnp.exp(sd[:, 2 * nh:4 * nh])
    cum_tot = sd[0:1, 2 * nh:4 * nh] if reverse else sd[c - 1:c, 2 * nh:4 * nh]
    w_st = sd[:, 0:2 * nh] * jnp.exp(cum_tot - sd[:, 2 * nh:4 * nh])
    e_tot = jnp.exp(cum_tot)
    for gi in range(SSD_GROUPS):
        bm = bc_ref[0, :, gi * SSD_STATE:(gi + 1) * SSD_STATE]
        cm = bc_ref[0, :, SSD_BC + gi * SSD_STATE:SSD_BC + (gi + 1) * SSD_STATE]
        bmT = bT_ref[0, gi * SSD_STATE:(gi + 1) * SSD_STATE, :]
        gmat = _dot_nt(cm, bm)
        s = s_scr[d * SSD_GROUPS + gi]
        y_inter = _dot(cm, s.astype(BF16))
        wx_parts = []
        dec_parts = []
        for hp in range(HEADS_PER_GROUP // 2):
            col0 = (gi * HEADS_PER_GROUP + 2 * hp) * SSD_HEADDIM
            xpair = xs_ref[0, :, col0:col0 + LANES]
            y_pair = jnp.zeros((c, LANES), F32)
            for t in range(2):
                hidx = d * nh + gi * HEADS_PER_GROUP + 2 * hp + t
                ci = sd[:, 2 * nh + hidx:2 * nh + hidx + 1]
                cj = sdT[2 * nh + hidx:2 * nh + hidx + 1, :]
                dtj = sdT[hidx:hidx + 1, :]
                w = jnp.where(causal, jnp.exp(jnp.minimum(ci - cj, 0.0)) * dtj, 0.0)
                xh = jnp.where(lo_half if t == 0 else jnp.logical_not(lo_half), xpair, jnp.zeros_like(xpair))
                y_pair = y_pair + _dot((gmat * w).astype(BF16), xh)
            h0 = d * nh + gi * HEADS_PER_GROUP + 2 * hp
            pick = lambda arr: jnp.where(lo_half, arr[:, h0:h0 + 1], arr[:, h0 + 1:h0 + 2])
            lc = hp * LANES
            y_pair = y_pair + y_inter[:, lc:lc + LANES] * pick(e_in)
            y_ref[0, :, col0:col0 + LANES] = y_pair.astype(BF16)
            wx_parts.append((xpair.astype(F32) * pick(w_st)).astype(BF16))
            dec_parts.append(pick(e_tot))
        wx = jnp.concatenate(wx_parts, axis=1)
        dec = jnp.concatenate(dec_parts, axis=1)
        s_scr[d * SSD_GROUPS + gi] = s * dec + _dot(bmT, wx)


def _ssd_kernel(xsf_ref, bcf_ref, bTf_ref, sdf_ref, sdTf_ref, xsb_ref, bcb_ref, bTb_ref, sdb_ref, sdTb_ref,
                s0_ref, yf_ref, yb_ref, sfin_ref, s_scr):
    n = pl.program_id(1)

    @pl.when(n == 0)
    def _():
        s_scr[...] = s0_ref[0]

    _ssd_dir(xsf_ref, bcf_ref, bTf_ref, sdf_ref, sdTf_ref, yf_ref, s_scr, 0, False)
    _ssd_dir(xsb_ref, bcb_ref, bTb_ref, sdb_ref, sdTb_ref, yb_ref, s_scr, 1, True)

    @pl.when(n == pl.num_programs(1) - 1)
    def _():
        sfin_ref[0] = s_scr[...]


def _ssd_scan(p, s0):
    b, l, _ = p["xs"].shape
    c = SSD_CHUNK
    nc = l // c
    specs = []
    for rev in (False, True):
        idx = (lambda n: nc - 1 - n) if rev else (lambda n: n)
        tok = lambda w, idx=idx: pl.BlockSpec((1, c, w), lambda bi, n: (bi, idx(n), 0))
        chan = lambda r, idx=idx: pl.BlockSpec((1, r, c), lambda bi, n: (bi, 0, idx(n)))
        specs.append([tok(SSD_INNER), tok(2 * SSD_BC), chan(SSD_BC), tok(LANES), chan(4 * SSD_HEADS)])
    st = pl.BlockSpec((1, 2 * SSD_GROUPS, SSD_STATE, HEADS_PER_GROUP * SSD_HEADDIM), lambda bi, n: (bi, 0, 0, 0))
    args = (p["xs"], p["bc"], p["bT"], p["sd"], p["sdT"])
    yf, yb, sfin = pl.pallas_call(
        _ssd_kernel,
        grid=(b, nc),
        in_specs=specs[0] + specs[1] + [st],
        out_specs=[specs[0][0], specs[1][0], st],
        out_shape=[jax.ShapeDtypeStruct((b, l, SSD_INNER), BF16), jax.ShapeDtypeStruct((b, l, SSD_INNER), BF16),
                   jax.ShapeDtypeStruct(s0.shape, F32)],
        scratch_shapes=[pltpu.VMEM((2 * SSD_GROUPS, SSD_STATE, HEADS_PER_GROUP * SSD_HEADDIM), F32)],
        compiler_params=_cparams(("parallel", "arbitrary")),
        name="ssd_scan",
    )(*args, *args, s0)
    return yf, yb, sfin


def _post0_kernel(of_ref, ob_ref, g_ref, yf_ref, yb_ref, xs_ref, z_ref, x_ref, mod_ref,
                  gn_ref, dsk_ref, sn_ref, wo_ref, out_ref):
    o = of_ref[0].astype(F32) + ob_ref[0].astype(F32)
    parts = []
    for hd in range(GLA_HEADS):
        oh = o[:, hd * GLA_DV:(hd + 1) * GLA_DV]
        parts.append(oh * lax.rsqrt(jnp.mean(oh * oh, axis=-1, keepdims=True) + EPS))
    gla = jnp.concatenate(parts, axis=1) * gn_ref[...] * _silu(g_ref[0].astype(F32))
    xs = xs_ref[0].astype(F32)
    y = (yf_ref[0].astype(F32) + yb_ref[0].astype(F32) + xs * dsk_ref[...]) * _silu(z_ref[0].astype(F32))
    gw = SSD_INNER // SSD_GROUPS
    parts = []
    for gi in range(SSD_GROUPS):
        yg = y[:, gi * gw:(gi + 1) * gw]
        parts.append(yg * lax.rsqrt(jnp.mean(yg * yg, axis=-1, keepdims=True) + EPS))
    ssd = jnp.concatenate(parts, axis=1) * sn_ref[...]
    res = _dot(gla.astype(BF16), wo_ref[:GLA_V, :]) + _dot(ssd.astype(BF16), wo_ref[GLA_V:, :])
    out_ref[0] = x_ref[0] + mod_ref[0, 2:3, :] * res


def _post0(p, of, ob, yf, yb, x, mod, wp, tm):
    b, l, _ = x.shape
    tm = min(tm, l)
    bm = mod.shape[0]
    mod_idx = (lambda bi, i: (bi, 0, 0)) if bm > 1 else (lambda bi, i: (0, 0, 0))
    tok = lambda w: pl.BlockSpec((1, tm, w), lambda bi, i: (bi, i, 0))
    return pl.pallas_call(
        _post0_kernel,
        grid=(b, l // tm),
        in_specs=[tok(GLA_V), tok(GLA_V), tok(GLA_V), tok(SSD_INNER), tok(SSD_INNER), tok(SSD_INNER),
                  tok(SSD_INNER), tok(D_MODEL), pl.BlockSpec((1, N_MOD, D_MODEL), mod_idx),
                  _const_spec((1, GLA_V)), _const_spec((1, SSD_INNER)), _const_spec((1, SSD_INNER)),
                  _const_spec(wp["w_out"].shape)],
        out_specs=tok(D_MODEL),
        out_shape=jax.ShapeDtypeStruct((b, l, D_MODEL), F32),
        compiler_params=_cparams(("parallel", "arbitrary")),
        name="post0",
    )(of, ob, p["g"], yf, yb, p["xs"], p["z"], x, mod, wp["gla_norm"], wp["d_skip"], wp["ssd_norm"], wp["w_out"])


def _ffn_kernel(x_ref, mod_ref, nw_ref, w1_ref, w3_ref, w2_ref, fn_ref, out_ref, *, final):
    x = x_ref[0]
    h = _norm_mod(x, nw_ref[...], mod_ref[0, 3:4, :], mod_ref[0, 4:5, :]).astype(BF16)
    half = D_FF // 2
    y = jnp.zeros_like(x)
    for j in range(2):
        cs = slice(j * half, (j + 1) * half)
        u = (_silu(_dot(h, w1_ref[:, cs])) * _dot(h, w3_ref[:, cs])).astype(BF16)
        y = y + _dot(u, w2_ref[cs, :])
    out = x + mod_ref[0, 5:6, :] * y
    if final:
        out = out * lax.rsqrt(jnp.mean(out * out, axis=-1, keepdims=True) + EPS) * fn_ref[...]
    out_ref[0] = out


def _ffn(x, mod, nw, w1, w3, w2, final_norm, final, tm):
    b, l, _ = x.shape
    tm = min(tm, l)
    bm = mod.shape[0]
    mod_idx = (lambda bi, i: (bi, 0, 0)) if bm > 1 else (lambda bi, i: (0, 0, 0))
    tok = pl.BlockSpec((1, tm, D_MODEL), lambda bi, i: (bi, i, 0))
    return pl.pallas_call(
        functools.partial(_ffn_kernel, final=final),
        grid=(b, l // tm),
        in_specs=[tok, pl.BlockSpec((1, N_MOD, D_MODEL), mod_idx), _const_spec((1, D_MODEL)),
                  _const_spec(w1.shape), _const_spec(w3.shape), _const_spec(w2.shape), _const_spec((1, D_MODEL))],
        out_specs=tok,
        out_shape=jax.ShapeDtypeStruct((b, l, D_MODEL), F32),
        compiler_params=_cparams(("parallel", "arbitrary")),
        name="ffn_final" if final else "ffn",
    )(x, mod, nw, w1, w3, w2, final_norm)


def _inproj1_kernel(x_ref, xp_ref, xn_ref, mod_ref, nw_ref, wup_ref, cw_ref, cb_ref, bdq_ref, bdk_ref, bdkT_ref,
                    bdv_ref, wifq_ref, wifk_ref, wifv_ref, bif_ref,
                    q_ref, kT_ref, v_ref, xc_ref, z_ref, gc_ref, gT_ref,
                    hext, xm_scr, *, tm, ncol):
    i = pl.program_id(1)
    first = i == 0
    last = i == pl.num_programs(1) - 1
    shift = mod_ref[0, 0:1, :]
    scale = mod_ref[0, 1:2, :]
    nw = nw_ref[...]
    hext[0:HALO, :] = _norm_mod(xp_ref[0], nw, shift, scale).astype(BF16)
    if ncol:
        rows = tm // ncol
        for j in range(ncol):
            hext[HALO + j * rows:HALO + (j + 1) * rows, :] = _norm_mod(
                x_ref[0, :, j * D_MODEL:(j + 1) * D_MODEL], nw, shift, scale).astype(BF16)
    else:
        hext[HALO:HALO + tm, :] = _norm_mod(x_ref[0], nw, shift, scale).astype(BF16)
    hext[HALO + tm:2 * HALO + tm, :] = _norm_mod(xn_ref[0], nw, shift, scale).astype(BF16)
    h = hext[HALO:HALO + tm, :]

    z_ref[0] = _dot(h, wup_ref[:, MLSTM_INNER:]).astype(BF16)
    xm_scr[...] = _dot(hext[...], wup_ref[:, :MLSTM_INNER])
    xc = _silu(_conv3(xm_scr, tm, cw_ref, cb_ref, first, last))
    xc_ref[0] = xc.astype(BF16)

    gates = jnp.zeros((tm, LANES), F32) + bif_ref[...]
    for t in range(N_BD_TILES):
        cs = slice(t * BD_TILE, (t + 1) * BD_TILE)
        xct = xc[:, cs].astype(BF16)
        xmt = xm_scr[HALO:HALO + tm, cs].astype(BF16)
        qt = _dot(xct, bdq_ref[t])
        kt = _dot(xct, bdk_ref[t])
        vt = _dot(xmt, bdv_ref[t])
        q_ref[0, :, cs] = qt.astype(BF16)
        v_ref[0, :, cs] = vt.astype(BF16)
        kT_ref[0, cs, :] = (_dot_nt(bdkT_ref[t], xct) * (MLSTM_DH ** -0.5)).astype(BF16)
        gates = gates + _dot(qt.astype(BF16), wifq_ref[cs, :]) + _dot(kt.astype(BF16), wifk_ref[cs, :]) \
            + _dot(vt.astype(BF16), wifv_ref[cs, :])

    gt = jnp.transpose(gates)[:4 * MLSTM_HEADS]
    r = lax.broadcasted_iota(jnp.int32, gt.shape, 0)
    lf = _log_sigmoid(gt)
    cum = jnp.where(r < 2 * MLSTM_HEADS, _seg_cumsum(lf, MLSTM_CHUNK, 1, False),
                    _seg_cumsum(lf, MLSTM_CHUNK, 1, True))
    packed = jnp.where(r % (2 * MLSTM_HEADS) < MLSTM_HEADS, gt, cum)
    gT_ref[0] = packed
    gc_ref[0] = jnp.transpose(jnp.concatenate([packed, jnp.zeros((LANES - 4 * MLSTM_HEADS, tm), F32)], axis=0))


def _inproj1(x, mod, nw, wp, tm, colmajor):
    b, l, _ = x.shape
    tm = min(tm, l)
    nt = l // tm
    bm = mod.shape[0]
    mod_idx = (lambda bi, i: (bi, 0, 0)) if bm > 1 else (lambda bi, i: (0, 0, 0))
    if colmajor:
        rows = l // GRID_W
        ncol = tm // rows
        xv = x.reshape(b, rows, GRID_W * D_MODEL)
        rb = rows // HALO
        x_spec = pl.BlockSpec((1, rows, ncol * D_MODEL), lambda bi, i: (bi, 0, i))
        xp_spec = pl.BlockSpec((1, HALO, D_MODEL), lambda bi, i: (bi, rb - 1, jnp.maximum(i * ncol - 1, 0)))
        xn_spec = pl.BlockSpec((1, HALO, D_MODEL), lambda bi, i: (bi, 0, jnp.minimum((i + 1) * ncol, GRID_W - 1)))
    else:
        ncol = 0
        xv = x
        hb = tm // HALO
        nhb = l // HALO
        x_spec = pl.BlockSpec((1, tm, D_MODEL), lambda bi, i: (bi, i, 0))
        xp_spec = pl.BlockSpec((1, HALO, D_MODEL), lambda bi, i: (bi, jnp.maximum(i * hb - 1, 0), 0))
        xn_spec = pl.BlockSpec((1, HALO, D_MODEL), lambda bi, i: (bi, jnp.minimum((i + 1) * hb, nhb - 1), 0))
    tok = lambda w: pl.BlockSpec((1, tm, w), lambda bi, i: (bi, i, 0))
    names = ["w_up", "conv_w", "conv_b", "bdq", "bdk", "bdkT", "bdv", "wif_q", "wif_k", "wif_v", "b_if"]
    q, kT, v, xc, z, gc, gT = pl.pallas_call(
        functools.partial(_inproj1_kernel, tm=tm, ncol=ncol),
        grid=(b, nt),
        in_specs=[x_spec, xp_spec, xn_spec, pl.BlockSpec((1, N_MOD, D_MODEL), mod_idx), _const_spec((1, D_MODEL))]
        + [_const_spec(wp[n].shape) for n in names],
        out_specs=[tok(MLSTM_INNER), pl.BlockSpec((1, MLSTM_INNER, tm), lambda bi, i: (bi, 0, i)),
                   tok(MLSTM_INNER), tok(MLSTM_INNER), tok(MLSTM_INNER), tok(LANES),
                   pl.BlockSpec((1, 4 * MLSTM_HEADS, tm), lambda bi, i: (bi, 0, i))],
        out_shape=[jax.ShapeDtypeStruct((b, l, MLSTM_INNER), BF16), jax.ShapeDtypeStruct((b, MLSTM_INNER, l), BF16),
                   jax.ShapeDtypeStruct((b, l, MLSTM_INNER), BF16), jax.ShapeDtypeStruct((b, l, MLSTM_INNER), BF16),
                   jax.ShapeDtypeStruct((b, l, MLSTM_INNER), BF16), jax.ShapeDtypeStruct((b, l, LANES), F32),
                   jax.ShapeDtypeStruct((b, 4 * MLSTM_HEADS, l), F32)],
        scratch_shapes=[pltpu.VMEM((tm + 2 * HALO, D_MODEL), BF16),
                        pltpu.VMEM((tm + 2 * HALO, MLSTM_INNER), F32)],
        compiler_params=_cparams(("parallel", "arbitrary")),
        name="inproj1",
    )(xv, xv, xv, mod, nw, *[wp[n] for n in names])
    return dict(q=q, kT=kT, v=v, xc=xc, z=z, gc=gc, gT=gT)


def _mlstm_dir(q_ref, kT_ref, v_ref, gc_ref, gT_ref, h_ref, c_scr, m_scr, d, hd, reverse):
    c = MLSTM_CHUNK
    dh = MLSTM_DH
    q = q_ref[0]
    kT = kT_ref[0]
    v = v_ref[0]
    ri = d * 2 * MLSTM_HEADS + hd
    rb = ri + MLSTM_HEADS
    gc = gc_ref[0]
    gT = gT_ref[0]
    bi = gc[:, rb:rb + 1]
    bj = gT[rb:rb + 1, :]
    ij = gT[ri:ri + 1, :]
    m0 = m_scr[d, 0:1, 0:1]
    ii = lax.broadcasted_iota(jnp.int32, (c, c), 0)
    jj = lax.broadcasted_iota(jnp.int32, (c, c), 1)
    causal = (jj >= ii) if reverse else (jj <= ii)
    log_d = jnp.where(causal, bi - bj + ij, -jnp.inf)
    log_inter = bi + m0
    m_row = jnp.maximum(log_inter, jnp.max(log_d, axis=-1, keepdims=True))
    s = (_dot(q, kT) * jnp.exp(log_d - m_row)).astype(BF16)
    w_inter = jnp.exp(log_inter - m_row)
    ones = jnp.ones((c, MLSTM_EXT - dh), BF16)
    v_ext = jnp.concatenate([v, ones], axis=1)
    state = c_scr[d]
    both = _dot(s, v_ext) + w_inter * _dot(q, state.astype(BF16))
    num = both[:, :dh]
    den = both[:, dh:dh + 1]
    h_ref[0] = (num / jnp.maximum(jnp.abs(den), jnp.exp(-m_row))).astype(BF16)

    b_tot = bj[:, 0:1] if reverse else bj[:, c - 1:c]
    log_w = b_tot - bj + ij
    m_new = jnp.maximum(b_tot + m0, jnp.max(log_w, axis=-1, keepdims=True))
    kw = (kT.astype(F32) * jnp.exp(log_w - m_new)).astype(BF16)
    c_scr[d] = jnp.exp(b_tot + m0 - m_new) * state + _dot(kw, v_ext)
    m_scr[d] = jnp.broadcast_to(m_new, m_scr.shape[1:])


def _mlstm_kernel(qf_ref, kTf_ref, vf_ref, gcf_ref, gTf_ref, qb_ref, kTb_ref, vb_ref, gcb_ref, gTb_ref,
                  c0_ref, m0_ref, hf_ref, hb_ref, cfin_ref, mfin_ref, c_scr, m_scr):
    hd = pl.program_id(1)
    n = pl.program_id(2)

    @pl.when(n == 0)
    def _():
        c_scr[...] = c0_ref[0, 0]
        m_scr[...] = m0_ref[0, 0]

    for k in range(MLSTM_HEADS):
        @pl.when(hd == k)
        def _(k=k):
            _mlstm_dir(qf_ref, kTf_ref, vf_ref, gcf_ref, gTf_ref, hf_ref, c_scr, m_scr, 0, k, False)
            _mlstm_dir(qb_ref, kTb_ref, vb_ref, gcb_ref, gTb_ref, hb_ref, c_scr, m_scr, 1, k, True)

    @pl.when(n == pl.num_programs(2) - 1)
    def _():
        cfin_ref[0, 0] = c_scr[...]
        mfin_ref[0, 0] = m_scr[...]


def _mlstm_scan(p, c0, m0):
    b, l, _ = p["q"].shape
    c = MLSTM_CHUNK
    nc = l // c
    dh = MLSTM_DH
    specs = []
    for rev in (False, True):
        idx = (lambda n: nc - 1 - n) if rev else (lambda n: n)
        specs.append([
            pl.BlockSpec((1, c, dh), lambda bi, hd, n, idx=idx: (bi, idx(n), hd)),
            pl.BlockSpec((1, dh, c), lambda bi, hd, n, idx=idx: (bi, hd, idx(n))),
            pl.BlockSpec((1, c, dh), lambda bi, hd, n, idx=idx: (bi, idx(n), hd)),
            pl.BlockSpec((1, c, LANES), lambda bi, hd, n, idx=idx: (bi, idx(n), 0)),
            pl.BlockSpec((1, 4 * MLSTM_HEADS, c), lambda bi, hd, n, idx=idx: (bi, 0, idx(n))),
        ])
    cst = pl.BlockSpec((1, 1, 2, dh, MLSTM_EXT), lambda bi, hd, n: (bi, hd, 0, 0, 0))
    mst = pl.BlockSpec((1, 1, 2, 8, LANES), lambda bi, hd, n: (bi, hd, 0, 0, 0))
    args = (p["q"], p["kT"], p["v"], p["gc"], p["gT"])
    hf, hb, cfin, mfin = pl.pallas_call(
        _mlstm_kernel,
        grid=(b, MLSTM_HEADS, nc),
        in_specs=specs[0] + specs[1] + [cst, mst],
        out_specs=[specs[0][0], specs[1][0], cst, mst],
        out_shape=[jax.ShapeDtypeStruct((b, l, MLSTM_INNER), BF16), jax.ShapeDtypeStruct((b, l, MLSTM_INNER), BF16),
                   jax.ShapeDtypeStruct(c0.shape, F32), jax.ShapeDtypeStruct(m0.shape, F32)],
        scratch_shapes=[pltpu.VMEM((2, dh, MLSTM_EXT), F32), pltpu.VMEM((2, 8, LANES), F32)],
        compiler_params=_cparams(("parallel", "parallel", "arbitrary")),
        name="mlstm_scan",
    )(*args, *args, c0, m0)
    return hf, hb, cfin, mfin


def _post1_kernel(hf_ref, hb_ref, xc_ref, z_ref, x_ref, mod_ref, mh_ref, sk_ref, wd_ref, out_ref, *, tm, ncol):
    hs = hf_ref[0].astype(F32) + hb_ref[0].astype(F32)
    parts = []
    for hd in range(MLSTM_HEADS):
        hh = hs[:, hd * MLSTM_DH:(hd + 1) * MLSTM_DH]
        mu = jnp.mean(hh, axis=-1, keepdims=True)
        cen = hh - mu
        parts.append(cen * lax.rsqrt(jnp.mean(cen * cen, axis=-1, keepdims=True) + EPS))
    feat = (jnp.concatenate(parts, axis=1) * mh_ref[...] + sk_ref[...] * xc_ref[0].astype(F32)) \
        * _silu(z_ref[0].astype(F32))
    res = mod_ref[0, 2:3, :] * _dot(feat.astype(BF16), wd_ref[...])
    if ncol:
        rows = tm // ncol
        for j in range(ncol):
            cs = slice(j * D_MODEL, (j + 1) * D_MODEL)
            out_ref[0, :, cs] = x_ref[0, :, cs] + res[j * rows:(j + 1) * rows, :]
    else:
        out_ref[0] = x_ref[0] + res


def _post1(p, hf, hb, x, mod, wp, tm, colmajor):
    b, l, _ = x.shape
    tm = min(tm, l)
    bm = mod.shape[0]
    mod_idx = (lambda bi, i: (bi, 0, 0)) if bm > 1 else (lambda bi, i: (0, 0, 0))
    tok = lambda w: pl.BlockSpec((1, tm, w), lambda bi, i: (bi, i, 0))
    if colmajor:
        rows = l // GRID_W
        ncol = tm // rows
        xv = x.reshape(b, rows, GRID_W * D_MODEL)
        x_spec = pl.BlockSpec((1, rows, ncol * D_MODEL), lambda bi, i: (bi, 0, i))
    else:
        ncol = 0
        xv = x
        x_spec = tok(D_MODEL)
    out = pl.pallas_call(
        functools.partial(_post1_kernel, tm=tm, ncol=ncol),
        grid=(b, l // tm),
        in_specs=[tok(MLSTM_INNER), tok(MLSTM_INNER), tok(MLSTM_INNER), tok(MLSTM_INNER), x_spec,
                  pl.BlockSpec((1, N_MOD, D_MODEL), mod_idx),
                  _const_spec((1, MLSTM_INNER)), _const_spec((1, MLSTM_INNER)), _const_spec(wp["w_down"].shape)],
        out_specs=x_spec,
        out_shape=jax.ShapeDtypeStruct(xv.shape, F32),
        compiler_params=_cparams(("parallel", "arbitrary")),
        name="post1",
    )(hf, hb, p["xc"], p["z"], xv, mod, wp["mh_norm"], wp["skip"], wp["w_down"])
    return out.reshape(b, l, D_MODEL)


def _prep_even(w_in, w_gate, b_gate, gla_norm, conv_w, conv_b, a_log, dt_bias, d_skip, ssd_norm, w_out):
    sizes = (GLA_QK, GLA_QK, GLA_V, GLA_V, 2 * GLA_GATE_RANK, SSD_INNER, SSD_XBC, 2 * SSD_HEADS)
    offs = [0]
    for s in sizes:
        offs.append(offs[-1] + s)
    cols = lambda i: w_in[:, offs[i]:offs[i + 1]]
    w_main = jnp.concatenate([cols(0), cols(1), cols(2), cols(3), cols(5), cols(6)], axis=1).astype(BF16)
    pad = jnp.zeros((D_MODEL, LANES - 2 * GLA_GATE_RANK - 2 * SSD_HEADS), w_in.dtype)
    w_small = jnp.concatenate([cols(4), cols(7), pad], axis=1).astype(BF16)
    zero = jnp.zeros((GLA_GATE_RANK, GLA_QK), w_gate.dtype)
    wg = jnp.concatenate([jnp.concatenate([w_gate[0], zero], axis=1),
                          jnp.concatenate([zero, w_gate[1]], axis=1)], axis=0).astype(BF16)
    return dict(
        w_main=w_main, w_small=w_small, w_dtT=jnp.transpose(cols(7)).astype(BF16),
        w_gate=wg, b_gate=b_gate.reshape(1, 2 * GLA_QK),
        conv_w=conv_w, conv_b=conv_b.reshape(1, SSD_XBC),
        dt_bias=dt_bias.reshape(1, 2 * SSD_HEADS), a_log=a_log.reshape(1, 2 * SSD_HEADS),
        dt_biasT=dt_bias.reshape(2 * SSD_HEADS, 1), a_logT=a_log.reshape(2 * SSD_HEADS, 1),
        gla_norm=jnp.tile(gla_norm, GLA_HEADS).reshape(1, GLA_V),
        d_skip=jnp.repeat(d_skip, SSD_HEADDIM).reshape(1, SSD_INNER),
        ssd_norm=ssd_norm.reshape(1, SSD_INNER),
        w_out=w_out.astype(BF16),
    )


def _block_diag_tiles(w):
    per = BD_TILE // QKV_BLOCK
    w4 = w.reshape(N_BD_TILES, per, QKV_BLOCK, QKV_BLOCK)
    eye = jnp.eye(per, dtype=w.dtype)
    return jnp.einsum("tade,ab->tadbe", w4, eye).reshape(N_BD_TILES, BD_TILE, BD_TILE)


def _prep_odd(w_up, conv_w, conv_b, wq, wk, wv, w_if, b_if, mh_norm, skip, w_down):
    ng = 4 * MLSTM_HEADS
    wif = jnp.transpose(w_if, (1, 0, 2)).reshape(3 * MLSTM_INNER, ng)
    wif = jnp.concatenate([wif, jnp.zeros((3 * MLSTM_INNER, LANES - ng), w_if.dtype)], axis=1).astype(BF16)
    bif = jnp.concatenate([b_if.reshape(1, ng), jnp.zeros((1, LANES - ng), b_if.dtype)], axis=1)
    bdk = _block_diag_tiles(wk)
    return dict(
        w_up=w_up.astype(BF16), conv_w=conv_w, conv_b=conv_b.reshape(1, MLSTM_INNER),
        bdq=_block_diag_tiles(wq).astype(BF16), bdk=bdk.astype(BF16),
        bdkT=jnp.transpose(bdk, (0, 2, 1)).astype(BF16), bdv=_block_diag_tiles(wv).astype(BF16),
        wif_q=wif[:MLSTM_INNER], wif_k=wif[MLSTM_INNER:2 * MLSTM_INNER], wif_v=wif[2 * MLSTM_INNER:], b_if=bif,
        mh_norm=mh_norm.reshape(1, MLSTM_INNER), skip=skip.reshape(1, MLSTM_INNER),
        w_down=w_down.astype(BF16),
    )


TOKEN_TILE = 512


def kernel(x, c, ctx, c_ctx, w_mod, b_mod, norm_mix, norm_ffn, ffn_w1, ffn_w3, ffn_w2, a_w_in, a_gla_w_gate,
           a_gla_b_gate, a_gla_norm, a_ssd_conv_w, a_ssd_conv_b, a_ssd_A_log, a_ssd_dt_bias, a_ssd_D, a_ssd_norm,
           a_w_out, b_w_up, b_conv_w, b_conv_b, b_wq, b_wk, b_wv, b_w_if, b_b_if, b_mh_norm, b_skip, b_w_down,
           final_norm):
    tile = TOKEN_TILE
    bsz = x.shape[0]
    depth = w_mod.shape[0]
    pad_rows = (-(bsz + 1)) % 8
    cc = jnp.concatenate([c, c_ctx[None, :], jnp.zeros((pad_rows, D_MODEL), c.dtype)], axis=0)
    mods = _modulation(cc, w_mod, b_mod).reshape(depth, bsz + 1 + pad_rows, N_MOD, D_MODEL)
    fnorm = final_norm.reshape(1, D_MODEL)
    for layer in range(depth):
        last = layer == depth - 1
        j = layer // 2
        mx = mods[layer, :bsz]
        mc = mods[layer, bsz:bsz + 1]
        nmix = norm_mix[layer].reshape(1, D_MODEL)
        nffn = norm_ffn[layer].reshape(1, D_MODEL)
        w1 = ffn_w1[layer].astype(BF16)
        w3 = ffn_w3[layer].astype(BF16)
        w2 = ffn_w2[layer].astype(BF16)
        if layer % 2 == 0:
            wp = _prep_even(a_w_in[j], a_gla_w_gate[j], a_gla_b_gate[j], a_gla_norm[j], a_ssd_conv_w[j],
                            a_ssd_conv_b[j], a_ssd_A_log[j], a_ssd_dt_bias[j], a_ssd_D[j], a_ssd_norm[j],
                            a_w_out[j])
            pc = _inproj0(ctx, mc, nmix, wp, tile)
            gla0 = jnp.zeros((bsz, 2 * GLA_HEADS, GLA_DK, GLA_DV), F32)
            ssd0 = jnp.zeros((bsz, 2 * SSD_GROUPS, SSD_STATE, HEADS_PER_GROUP * SSD_HEADDIM), F32)
            ofc, obc, gla_st = _gla_scan(pc, gla0)
            yfc, ybc, ssd_st = _ssd_scan(pc, ssd0)
            px = _inproj0(x, mx, nmix, wp, tile)
            ofx, obx, _ = _gla_scan(px, gla_st)
            yfx, ybx, _ = _ssd_scan(px, ssd_st)
            x = _post0(px, ofx, obx, yfx, ybx, x, mx, wp, tile)
            if not last:
                ctx = _post0(pc, ofc, obc, yfc, ybc, ctx, mc, wp, tile)
        else:
            wp = _prep_odd(b_w_up[j], b_conv_w[j], b_conv_b[j], b_wq[j], b_wk[j], b_wv[j], b_w_if[j], b_b_if[j],
                           b_mh_norm[j], b_skip[j], b_w_down[j])
            pc = _inproj1(ctx, mc, nmix, wp, tile, colmajor=False)
            c0 = jnp.zeros((bsz, MLSTM_HEADS, 2, MLSTM_DH, MLSTM_EXT), F32)
            m0 = jnp.full((bsz, MLSTM_HEADS, 2, 8, LANES), -1e30, F32)
            hfc, hbc, c_st, m_st = _mlstm_scan(pc, c0, m0)
            px = _inproj1(x, mx, nmix, wp, tile, colmajor=True)
            hfx, hbx, _, _ = _mlstm_scan(px, c_st, m_st)
            x = _post1(px, hfx, hbx, x, mx, wp, tile, colmajor=True)
            if not last:
                ctx = _post1(pc, hfc, hbc, ctx, mc, wp, tile, colmajor=False)
        x = _ffn(x, mx, nffn, w1, w3, w2, fnorm, last, tile)
        if not last:
            ctx = _ffn(ctx, mc, nffn, w1, w3, w2, fnorm, False, tile)
    return x
```

```python
import functools

import jax
import jax.numpy as jnp
from jax import lax
from jax.experimental import pallas as pl
from jax.experimental.pallas import tpu as pltpu

F32 = jnp.float32
BF16 = jnp.bfloat16

D_MODEL = 1024
GRID_W = 64
EPS = 1e-6
M_INIT = -1e30
N_MOD = 6

GLA_HEADS = 4
GLA_DK = 128
GLA_DV = 256
GLA_QK = GLA_HEADS * GLA_DK
GLA_V = GLA_HEADS * GLA_DV
GLA_GATE_RANK = 16
GLA_GATE_TAU = 16.0

SSD_HEADDIM = 64
SSD_HEADS = 16
SSD_STATE = 128
SSD_GROUPS = 2
SSD_INNER = SSD_HEADS * SSD_HEADDIM
SSD_BC = SSD_GROUPS * SSD_STATE
SSD_XBC = SSD_INNER + 2 * SSD_BC
HEADS_PER_GROUP = SSD_HEADS // SSD_GROUPS

MLSTM_INNER = 2 * D_MODEL
MLSTM_HEADS = 4
MLSTM_DH = MLSTM_INNER // MLSTM_HEADS
QKV_BLOCK = 4
BD_TILE = 256
N_BD_TILES = MLSTM_INNER // BD_TILE
MLSTM_GATE_ROWS = 32

D_FF = 2816

GLA_CHUNK = 64
SSD_CHUNK = 128
MLSTM_CHUNK = 256
HALO = 16
LANES = 128
SUBLANES = 8

VMEM_LIMIT = 56 * 1024 * 1024


def _cparams(sem):
    return pltpu.CompilerParams(dimension_semantics=sem, vmem_limit_bytes=VMEM_LIMIT)


def _const_spec(shape):
    nd = len(shape)
    return pl.BlockSpec(shape, lambda *_: (0,) * nd, pipeline_mode=pl.Buffered(1))


def _softplus(x):
    return jnp.maximum(x, 0.0) + jnp.log(1.0 + jnp.exp(-jnp.abs(x)))


def _log_sigmoid(x):
    return -_softplus(-x)


def _silu(x):
    return x / (1.0 + jnp.exp(-x))


def _seg_scan(x, seg, axis, reverse, use_max=False):
    n = x.shape[axis]
    idx = lax.broadcasted_iota(jnp.int32, x.shape, axis) % seg
    s = 1
    while s < seg:
        if reverse:
            shifted = pltpu.roll(x, n - s, axis)
            keep = idx < seg - s
        else:
            shifted = pltpu.roll(x, s, axis)
            keep = idx >= s
        if use_max:
            x = jnp.where(keep, jnp.maximum(x, shifted), x)
        else:
            x = x + jnp.where(keep, shifted, 0.0)
        s *= 2
    return x


def _seg_cumsum(x, seg, axis, reverse):
    return _seg_scan(x, seg, axis, reverse)


def _norm_mod(xv, nw, shift, scale):
    ms = jnp.mean(xv * xv, axis=-1, keepdims=True)
    return (xv * lax.rsqrt(ms + EPS) * nw) * (1.0 + scale) + shift


def _dot(a, b):
    return jnp.dot(a, b, preferred_element_type=F32)


def _dot_nt(a, b):
    return lax.dot_general(a, b, (((1,), (1,)), ((), ())), preferred_element_type=F32)


def _dot_tn(a, b):
    return lax.dot_general(a, b, (((0,), (0,)), ((), ())), preferred_element_type=F32)


def _mod_kernel(c_ref, w_ref, b_ref, o_ref):
    a = _silu(c_ref[...]).astype(BF16)
    o_ref[0] = _dot(a, w_ref[0].astype(BF16)) + b_ref[0]


def _modulation(cc, w_mod, b_mod):
    depth = w_mod.shape[0]
    n = w_mod.shape[2]
    tn = n // 4
    rows = cc.shape[0]
    return pl.pallas_call(
        _mod_kernel,
        grid=(depth, n // tn),
        in_specs=[
            pl.BlockSpec((rows, D_MODEL), lambda l, j: (0, 0)),
            pl.BlockSpec((1, D_MODEL, tn), lambda l, j: (l, 0, j)),
            pl.BlockSpec((1, 1, tn), lambda l, j: (l, 0, j)),
        ],
        out_specs=pl.BlockSpec((1, rows, tn), lambda l, j: (l, 0, j)),
        out_shape=jax.ShapeDtypeStruct((depth, rows, n), F32),
        compiler_params=_cparams(("arbitrary", "arbitrary")),
        name="modulation",
    )(cc, w_mod, b_mod.reshape(depth, 1, n))


def _conv3(scr, tm, cw_ref, cb_ref, first, last):
    row = lax.broadcasted_iota(jnp.int32, (tm, 1), 0)
    prev = scr[HALO - 1:HALO - 1 + tm, :]
    cur = scr[HALO:HALO + tm, :]
    nxt = scr[HALO + 1:HALO + 1 + tm, :]
    prev = jnp.where(row == jnp.where(first, 0, -1), 0.0, prev)
    nxt = jnp.where(row == jnp.where(last, tm - 1, -1), 0.0, nxt)
    return cw_ref[0:1, :] * prev + cw_ref[1:2, :] * cur + cw_ref[2:3, :] * nxt + cb_ref[...]


def _inproj0_kernel(x_ref, xp_ref, xn_ref, mod_ref, nw_ref, wmain_ref, wsmall_ref, wdtT_ref, wg_ref, bg_ref,
                    cw_ref, cb_ref, dtb_ref, alog_ref, dtbT_ref, alogT_ref,
                    q_ref, k_ref, v_ref, g_ref, z_ref, gcum_ref, xs_ref, bc_ref, bT_ref, sd_ref, sdT_ref,
                    hext, xbc_scr, *, tm):
    i = pl.program_id(1)
    first = i == 0
    last = i == pl.num_programs(1) - 1
    shift = mod_ref[0, 0:1, :]
    scale = mod_ref[0, 1:2, :]
    nw = nw_ref[...]
    hext[0:HALO, :] = _norm_mod(xp_ref[0], nw, shift, scale).astype(BF16)
    hext[HALO:HALO + tm, :] = _norm_mod(x_ref[0], nw, shift, scale).astype(BF16)
    hext[HALO + tm:2 * HALO + tm, :] = _norm_mod(xn_ref[0], nw, shift, scale).astype(BF16)
    h = hext[HALO:HALO + tm, :]

    o = 0
    q_ref[0] = (_dot(h, wmain_ref[:, o:o + GLA_QK]) * (GLA_DK ** -0.5)).astype(BF16)
    o += GLA_QK
    k_ref[0] = _dot(h, wmain_ref[:, o:o + GLA_QK]).astype(BF16)
    o += GLA_QK
    v_ref[0] = _dot(h, wmain_ref[:, o:o + GLA_V]).astype(BF16)
    o += GLA_V
    g_ref[0] = _dot(h, wmain_ref[:, o:o + GLA_V]).astype(BF16)
    o += GLA_V
    z_ref[0] = _dot(h, wmain_ref[:, o:o + SSD_INNER]).astype(BF16)
    o += SSD_INNER

    xbc_scr[...] = _dot(hext[...], wmain_ref[:, o:o + SSD_XBC])
    y = _silu(_conv3(xbc_scr, tm, cw_ref, cb_ref, first, last))
    xs_ref[0] = y[:, :SSD_INNER].astype(BF16)
    bc_ref[0] = y[:, SSD_INNER:].astype(BF16)
    bT_ref[0] = jnp.transpose(y[:, SSD_INNER:SSD_INNER + SSD_BC]).astype(BF16)

    small = _dot(h, wsmall_ref[...])
    r = small[:, :2 * GLA_GATE_RANK].astype(BF16)
    la = _log_sigmoid(_dot(r, wg_ref[...]) + bg_ref[...]) * (1.0 / GLA_GATE_TAU)
    gcum_ref[0, :, :GLA_QK] = _seg_cumsum(la[:, :GLA_QK], GLA_CHUNK, 0, False)
    gcum_ref[0, :, GLA_QK:] = _seg_cumsum(la[:, GLA_QK:], GLA_CHUNK, 0, True)

    nd = 2 * SSD_HEADS
    dtT = _softplus(_dot_nt(wdtT_ref[...], h) + dtbT_ref[...])
    laT = dtT * (-jnp.exp(alogT_ref[...]))
    cumT = jnp.concatenate([_seg_cumsum(laT[:SSD_HEADS], SSD_CHUNK, 1, False),
                            _seg_cumsum(laT[SSD_HEADS:], SSD_CHUNK, 1, True)], axis=0)
    packed = jnp.concatenate([dtT, cumT, jnp.zeros((LANES - 2 * nd, tm), F32)], axis=0)
    sdT_ref[0] = packed[:2 * nd]
    sd_ref[0] = jnp.transpose(packed)


def _inproj0(x, mod, nw, wp, tm):
    b, l, _ = x.shape
    tm = min(tm, l)
    nt = l // tm
    hb = tm // HALO
    nhb = l // HALO
    bm = mod.shape[0]
    mod_idx = (lambda bi, i: (bi, 0, 0)) if bm > 1 else (lambda bi, i: (0, 0, 0))
    tok = lambda w: pl.BlockSpec((1, tm, w), lambda bi, i: (bi, i, 0))
    in_specs = [
        tok(D_MODEL),
        pl.BlockSpec((1, HALO, D_MODEL), lambda bi, i: (bi, jnp.maximum(i * hb - 1, 0), 0)),
        pl.BlockSpec((1, HALO, D_MODEL), lambda bi, i: (bi, jnp.minimum((i + 1) * hb, nhb - 1), 0)),
        pl.BlockSpec((1, N_MOD, D_MODEL), mod_idx),
        _const_spec((1, D_MODEL)),
        _const_spec(wp["w_main"].shape), _const_spec(wp["w_small"].shape), _const_spec(wp["w_dtT"].shape),
        _const_spec(wp["w_gate"].shape), _const_spec(wp["b_gate"].shape),
        _const_spec(wp["conv_w"].shape), _const_spec(wp["conv_b"].shape),
        _const_spec(wp["dt_bias"].shape), _const_spec(wp["a_log"].shape),
        _const_spec(wp["dt_biasT"].shape), _const_spec(wp["a_logT"].shape),
    ]
    outs = [
        ("q", GLA_QK, BF16), ("k", GLA_QK, BF16), ("v", GLA_V, BF16), ("g", GLA_V, BF16),
        ("z", SSD_INNER, BF16), ("gcum", 2 * GLA_QK, F32), ("xs", SSD_INNER, BF16),
        ("bc", 2 * SSD_BC, BF16),
    ]
    out_specs = [tok(w) for _, w, _ in outs]
    out_shape = [jax.ShapeDtypeStruct((b, l, w), dt) for _, w, dt in outs]
    out_specs += [pl.BlockSpec((1, SSD_BC, tm), lambda bi, i: (bi, 0, i)),
                  tok(LANES),
                  pl.BlockSpec((1, 4 * SSD_HEADS, tm), lambda bi, i: (bi, 0, i))]
    out_shape += [jax.ShapeDtypeStruct((b, SSD_BC, l), BF16),
                  jax.ShapeDtypeStruct((b, l, LANES), F32),
                  jax.ShapeDtypeStruct((b, 4 * SSD_HEADS, l), F32)]
    res = pl.pallas_call(
        functools.partial(_inproj0_kernel, tm=tm),
        grid=(b, nt),
        in_specs=in_specs,
        out_specs=out_specs,
        out_shape=out_shape,
        scratch_shapes=[pltpu.VMEM((tm + 2 * HALO, D_MODEL), BF16),
                        pltpu.VMEM((tm + 2 * HALO, SSD_XBC), F32)],
        compiler_params=_cparams(("parallel", "arbitrary")),
        name="inproj0",
    )(x, x, x, mod, nw, wp["w_main"], wp["w_small"], wp["w_dtT"], wp["w_gate"], wp["b_gate"],
      wp["conv_w"], wp["conv_b"], wp["dt_bias"], wp["a_log"], wp["dt_biasT"], wp["a_logT"])
    names = [n for n, _, _ in outs] + ["bT", "sd", "sdT"]
    return dict(zip(names, res))


def _level_ref(g, hsz, reverse):
    c, w = g.shape
    grp = 2 * hsz
    r = hsz if reverse else hsz - 1
    if grp >= 8:
        g3 = g.reshape(c // grp, grp, w)
        return jnp.broadcast_to(g3[:, r:r + 1, :], g3.shape).reshape(c, w)
    g3 = g.reshape(c // 8, 8, w)
    sub = lax.broadcasted_iota(jnp.int32, g3.shape, 1)
    out = None
    for t in range(8 // grp):
        cand = jnp.broadcast_to(g3[:, t * grp + r:t * grp + r + 1, :], g3.shape)
        out = cand if out is None else jnp.where(sub >= t * grp, cand, out)
    return out.reshape(c, w)


def _gla_masks(c, w, reverse):
    row = lax.broadcasted_iota(jnp.int32, (c, w), 0)
    ii = lax.broadcasted_iota(jnp.int32, (c, c), 0)
    jj = lax.broadcasted_iota(jnp.int32, (c, c), 1)
    par = 0 if reverse else 1
    levels = []
    hsz = 1
    while hsz < c:
        is_q = (row // hsz) % 2 == par
        keep = ((ii // (2 * hsz)) == (jj // (2 * hsz))) & ((ii // hsz) % 2 == par) & ((jj // hsz) % 2 != par)
        levels.append((hsz, is_q, keep))
        hsz *= 2
    return ii == jj, levels


def _gla_chunk(q, k, v, g, s, masks, reverse):
    c = q.shape[0]
    diag, levels = masks
    a = jnp.where(diag, _dot_nt(q.astype(BF16), k.astype(BF16)), 0.0)
    for hsz, is_q, keep in levels:
        e = jnp.exp(-jnp.abs(g - _level_ref(g, hsz, reverse)))
        mixed = (jnp.where(is_q, q, k) * e).astype(BF16)
        a = a + jnp.where(keep, _dot_nt(mixed, mixed), 0.0)
    g_tot = g[0:1, :] if reverse else g[c - 1:c, :]
    qi = (q * jnp.exp(g)).astype(BF16)
    ki = (k * jnp.exp(g_tot - g)).astype(BF16)
    o = _dot(a.astype(BF16), v) + _dot(qi, s.astype(BF16))
    dk = g.shape[1]
    decay_col = jnp.transpose(jnp.broadcast_to(jnp.exp(g_tot), (dk, dk)))[:, 0:1]
    s_new = s * decay_col + _dot_tn(ki, v)
    return o, s_new


def _gla_kernel(qf_ref, kf_ref, vf_ref, gf_ref, qb_ref, kb_ref, vb_ref, gb_ref, s0_ref,
                of_ref, ob_ref, sfin_ref, *s_scr):
    n = pl.program_id(1)

    @pl.when(n == 0)
    def _():
        for k, s_ref in enumerate(s_scr):
            s_ref[...] = s0_ref[0, k]

    for d, (q_ref, k_ref, v_ref, g_ref, o_ref) in enumerate(
            ((qf_ref, kf_ref, vf_ref, gf_ref, of_ref), (qb_ref, kb_ref, vb_ref, gb_ref, ob_ref))):
        masks = _gla_masks(GLA_CHUNK, GLA_DK, d == 1)
        for hd in range(GLA_HEADS):
            ks = slice(hd * GLA_DK, (hd + 1) * GLA_DK)
            vs = slice(hd * GLA_DV, (hd + 1) * GLA_DV)
            s_ref = s_scr[d * GLA_HEADS + hd]
            o, s_new = _gla_chunk(q_ref[0, :, ks].astype(F32), k_ref[0, :, ks].astype(F32),
                                  v_ref[0, :, vs], g_ref[0, :, ks], s_ref[...], masks, d == 1)
            o_ref[0, :, vs] = o.astype(BF16)
            s_ref[...] = s_new

    @pl.when(n == pl.num_programs(1) - 1)
    def _():
        for k, s_ref in enumerate(s_scr):
            sfin_ref[0, k] = s_ref[...]


def _gla_scan(p, s0):
    b, l, _ = p["q"].shape
    c = GLA_CHUNK
    nc = l // c
    fwd = lambda w, col=0: pl.BlockSpec((1, c, w), lambda bi, n: (bi, n, col))
    bwd = lambda w, col=0: pl.BlockSpec((1, c, w), lambda bi, n: (bi, nc - 1 - n, col))
    st = pl.BlockSpec((1, 2 * GLA_HEADS, GLA_DK, GLA_DV), lambda bi, n: (bi, 0, 0, 0))
    of, ob, sfin = pl.pallas_call(
        _gla_kernel,
        grid=(b, nc),
        in_specs=[fwd(GLA_QK), fwd(GLA_QK), fwd(GLA_V), fwd(GLA_QK, 0),
                  bwd(GLA_QK), bwd(GLA_QK), bwd(GLA_V), bwd(GLA_QK, 1), st],
        out_specs=[fwd(GLA_V), bwd(GLA_V), st],
        out_shape=[jax.ShapeDtypeStruct((b, l, GLA_V), BF16), jax.ShapeDtypeStruct((b, l, GLA_V), BF16),
                   jax.ShapeDtypeStruct(s0.shape, F32)],
        scratch_shapes=[pltpu.VMEM((GLA_DK, GLA_DV), F32)] * (2 * GLA_HEADS),
        compiler_params=_cparams(("parallel", "arbitrary")),
        name="gla_scan",
    )(p["q"], p["k"], p["v"], p["gcum"], p["q"], p["k"], p["v"], p["gcum"], s0)
    return of, ob, sfin


def _split_hi_lo(x):
    hi = x.astype(BF16)
    lo = (x - hi.astype(F32)).astype(BF16)
    return jnp.concatenate([hi, lo], axis=1)


def _ssd_factors(sd, ex, d):
    c = sd.shape[0]
    nh = SSD_HEADS
    lane = lax.broadcasted_iota(jnp.int32, (1, LANES), 1)
    sel = (lane >= (2 + d) * nh) & (lane < (3 + d) * nh)
    tot = sd[0:1, :] if d == 1 else sd[c - 1:c, :]
    dt_under_cum = pltpu.roll(sd, 2 * nh, 1)
    e_in = jnp.where(sel, jnp.exp(sd), 0.0)
    w_st = jnp.where(sel, dt_under_cum * jnp.exp(tot - sd), 0.0)
    e_tot = jnp.broadcast_to(jnp.where(sel, jnp.exp(tot), 0.0), (SUBLANES, LANES))
    expand = lambda f: _dot(_split_hi_lo(f), ex)
    return expand(e_in), expand(w_st), expand(e_tot)[0:1, :]


def _ssd_kernel(xsa_ref, bca_ref, bTa_ref, sda_ref, sdTa_ref, xsb_ref, bcb_ref, bTb_ref, sdb_ref,
                ex_ref, s0_ref, ya_ref, yb_ref, sfin_ref, *s_scr):
    c = SSD_CHUNK
    nh = SSD_HEADS
    gw = HEADS_PER_GROUP * SSD_HEADDIM
    n = pl.program_id(1)

    @pl.when(n == 0)
    def _():
        for k, s_ref in enumerate(s_scr):
            s_ref[...] = s0_ref[0, k]

    sda = sda_ref[0]
    sdTa = sdTa_ref[0]
    ein_f, wst_f, dec_f = _ssd_factors(sda, ex_ref[0], 0)
    ein_b, wst_b, dec_b = _ssd_factors(sdb_ref[0], ex_ref[1], 1)
    ii = lax.broadcasted_iota(jnp.int32, (c, c), 0)
    jj = lax.broadcasted_iota(jnp.int32, (c, c), 1)
    lower = jj <= ii
    upper = jj >= ii
    lo_half = lax.broadcasted_iota(jnp.int32, (1, LANES), 1) < SSD_HEADDIM

    def weights(h, d, mask):
        col = (2 + d) * nh + h
        ci = sda[:, col:col + 1]
        cj = sdTa[col:col + 1, :]
        dtj = sdTa[d * nh + h:d * nh + h + 1, :]
        return jnp.where(mask, jnp.exp(jnp.minimum(ci - cj, 0.0)) * dtj, 0.0)

    for gi in range(SSD_GROUPS):
        gs = slice(gi * gw, (gi + 1) * gw)
        bm = bca_ref[0, :, gi * SSD_STATE:(gi + 1) * SSD_STATE]
        cm = bca_ref[0, :, SSD_BC + gi * SSD_STATE:SSD_BC + (gi + 1) * SSD_STATE]
        gmat = _dot_nt(cm, bm)
        s_f = s_scr[gi]
        s_b = s_scr[SSD_GROUPS + gi]
        y_inter = _dot(cm, s_f[...].astype(BF16)) * ein_f[:, gs]
        for hp in range(HEADS_PER_GROUP // 2):
            col0 = (gi * HEADS_PER_GROUP + 2 * hp) * SSD_HEADDIM
            xpair = xsa_ref[0, :, col0:col0 + LANES]
            scores = []
            for t in range(2):
                h = gi * HEADS_PER_GROUP + 2 * hp + t
                scores.append((gmat * (weights(h, 0, lower) + weights(h, 1, upper))).astype(BF16))
            zero = jnp.zeros_like(xpair)
            rhs = jnp.concatenate([jnp.where(lo_half, xpair, zero), jnp.where(lo_half, zero, xpair)], axis=0)
            y_pair = _dot(jnp.concatenate(scores, axis=1), rhs) + y_inter[:, hp * LANES:(hp + 1) * LANES]
            ya_ref[0, :, col0:col0 + LANES] = y_pair.astype(BF16)
        wx = (xsa_ref[0, :, gs].astype(F32) * wst_f[:, gs]).astype(BF16)
        s_f[...] = s_f[...] * dec_f[:, gs] + _dot(bTa_ref[0, gi * SSD_STATE:(gi + 1) * SSD_STATE, :], wx)

        cmb = bcb_ref[0, :, SSD_BC + gi * SSD_STATE:SSD_BC + (gi + 1) * SSD_STATE]
        yb_ref[0, :, gs] = (_dot(cmb, s_b[...].astype(BF16)) * ein_b[:, gs]).astype(BF16)
        wxb = (xsb_ref[0, :, gs].astype(F32) * wst_b[:, gs]).astype(BF16)
        s_b[...] = s_b[...] * dec_b[:, gs] + _dot(bTb_ref[0, gi * SSD_STATE:(gi + 1) * SSD_STATE, :], wxb)

    @pl.when(n == pl.num_programs(1) - 1)
    def _():
        for k, s_ref in enumerate(s_scr):
            sfin_ref[0, k] = s_ref[...]


def _ssd_expander():
    r = lax.broadcasted_iota(jnp.int32, (2, 2 * LANES, SSD_INNER), 1) % LANES
    d = lax.broadcasted_iota(jnp.int32, (2, 2 * LANES, SSD_INNER), 0)
    h = lax.broadcasted_iota(jnp.int32, (2, 2 * LANES, SSD_INNER), 2) // SSD_HEADDIM
    return (r == (2 + d) * SSD_HEADS + h).astype(BF16)


def _ssd_scan(p, s0):
    b, l, _ = p["xs"].shape
    c = SSD_CHUNK
    nc = l // c
    specs = []
    for rev in (False, True):
        idx = (lambda n: nc - 1 - n) if rev else (lambda n: n)
        tok = lambda w, idx=idx: pl.BlockSpec((1, c, w), lambda bi, n: (bi, idx(n), 0))
        chan = lambda r, idx=idx: pl.BlockSpec((1, r, c), lambda bi, n: (bi, 0, idx(n)))
        specs.append([tok(SSD_INNER), tok(2 * SSD_BC), chan(SSD_BC), tok(LANES), chan(4 * SSD_HEADS)])
    st = pl.BlockSpec((1, 2 * SSD_GROUPS, SSD_STATE, HEADS_PER_GROUP * SSD_HEADDIM), lambda bi, n: (bi, 0, 0, 0))
    ex = _ssd_expander()
    ya, yb, sfin = pl.pallas_call(
        _ssd_kernel,
        grid=(b, nc),
        in_specs=specs[0] + specs[1][:4] + [_const_spec(ex.shape), st],
        out_specs=[specs[0][0], specs[1][0], st],
        out_shape=[jax.ShapeDtypeStruct((b, l, SSD_INNER), BF16), jax.ShapeDtypeStruct((b, l, SSD_INNER), BF16),
                   jax.ShapeDtypeStruct(s0.shape, F32)],
        scratch_shapes=[pltpu.VMEM((SSD_STATE, HEADS_PER_GROUP * SSD_HEADDIM), F32)] * (2 * SSD_GROUPS),
        compiler_params=_cparams(("parallel", "arbitrary")),
        name="ssd_scan",
    )(p["xs"], p["bc"], p["bT"], p["sd"], p["sdT"], p["xs"], p["bc"], p["bT"], p["sd"], ex, s0)
    return ya, yb, sfin


def _post0_kernel(of_ref, ob_ref, g_ref, yf_ref, yb_ref, xs_ref, z_ref, x_ref, mod_ref,
                  gn_ref, dsk_ref, sn_ref, wo_ref, out_ref):
    o = of_ref[0].astype(F32) + ob_ref[0].astype(F32)
    parts = []
    for hd in range(GLA_HEADS):
        oh = o[:, hd * GLA_DV:(hd + 1) * GLA_DV]
        parts.append(oh * lax.rsqrt(jnp.mean(oh * oh, axis=-1, keepdims=True) + EPS))
    gla = jnp.concatenate(parts, axis=1) * gn_ref[...] * _silu(g_ref[0].astype(F32))
    xs = xs_ref[0].astype(F32)
    y = (yf_ref[0].astype(F32) + yb_ref[0].astype(F32) + xs * dsk_ref[...]) * _silu(z_ref[0].astype(F32))
    gw = SSD_INNER // SSD_GROUPS
    parts = []
    for gi in range(SSD_GROUPS):
        yg = y[:, gi * gw:(gi + 1) * gw]
        parts.append(yg * lax.rsqrt(jnp.mean(yg * yg, axis=-1, keepdims=True) + EPS))
    ssd = jnp.concatenate(parts, axis=1) * sn_ref[...]
    res = _dot(gla.astype(BF16), wo_ref[:GLA_V, :]) + _dot(ssd.astype(BF16), wo_ref[GLA_V:, :])
    out_ref[0] = x_ref[0] + mod_ref[0, 2:3, :] * res


def _post0(p, of, ob, yf, yb, x, mod, wp, tm):
    b, l, _ = x.shape
    tm = min(tm, l)
    bm = mod.shape[0]
    mod_idx = (lambda bi, i: (bi, 0, 0)) if bm > 1 else (lambda bi, i: (0, 0, 0))
    tok = lambda w: pl.BlockSpec((1, tm, w), lambda bi, i: (bi, i, 0))
    return pl.pallas_call(
        _post0_kernel,
        grid=(b, l // tm),
        in_specs=[tok(GLA_V), tok(GLA_V), tok(GLA_V), tok(SSD_INNER), tok(SSD_INNER), tok(SSD_INNER),
                  tok(SSD_INNER), tok(D_MODEL), pl.BlockSpec((1, N_MOD, D_MODEL), mod_idx),
                  _const_spec((1, GLA_V)), _const_spec((1, SSD_INNER)), _const_spec((1, SSD_INNER)),
                  _const_spec(wp["w_out"].shape)],
        out_specs=tok(D_MODEL),
        out_shape=jax.ShapeDtypeStruct((b, l, D_MODEL), F32),
        compiler_params=_cparams(("parallel", "arbitrary")),
        name="post0",
    )(of, ob, p["g"], yf, yb, p["xs"], p["z"], x, mod, wp["gla_norm"], wp["d_skip"], wp["ssd_norm"], wp["w_out"])


def _ffn_kernel(x_ref, mod_ref, nw_ref, w1_ref, w3_ref, w2_ref, fn_ref, out_ref, *, final):
    x = x_ref[0]
    h = _norm_mod(x, nw_ref[...], mod_ref[0, 3:4, :], mod_ref[0, 4:5, :]).astype(BF16)
    half = D_FF // 2
    y = jnp.zeros_like(x)
    for j in range(2):
        cs = slice(j * half, (j + 1) * half)
        u = (_silu(_dot(h, w1_ref[:, cs])) * _dot(h, w3_ref[:, cs])).astype(BF16)
        y = y + _dot(u, w2_ref[cs, :])
    out = x + mod_ref[0, 5:6, :] * y
    if final:
        out = out * lax.rsqrt(jnp.mean(out * out, axis=-1, keepdims=True) + EPS) * fn_ref[...]
    out_ref[0] = out


def _ffn(x, mod, nw, w1, w3, w2, final_norm, final, tm):
    b, l, _ = x.shape
    tm = min(tm, l)
    bm = mod.shape[0]
    mod_idx = (lambda bi, i: (bi, 0, 0)) if bm > 1 else (lambda bi, i: (0, 0, 0))
    tok = pl.BlockSpec((1, tm, D_MODEL), lambda bi, i: (bi, i, 0))
    return pl.pallas_call(
        functools.partial(_ffn_kernel, final=final),
        grid=(b, l // tm),
        in_specs=[tok, pl.BlockSpec((1, N_MOD, D_MODEL), mod_idx), _const_spec((1, D_MODEL)),
                  _const_spec(w1.shape), _const_spec(w3.shape), _const_spec(w2.shape), _const_spec((1, D_MODEL))],
        out_specs=tok,
        out_shape=jax.ShapeDtypeStruct((b, l, D_MODEL), F32),
        compiler_params=_cparams(("parallel", "arbitrary")),
        name="ffn_final" if final else "ffn",
    )(x, mod, nw, w1, w3, w2, final_norm)


def _inproj1_kernel(x_ref, xp_ref, xn_ref, mod_ref, nw_ref, wup_ref, cw_ref, cb_ref, bdq_ref, bdk_ref, bdkT_ref,
                    bdv_ref, wifq_ref, wifk_ref, wifv_ref, bif_ref,
                    q_ref, kT_ref, v_ref, xc_ref, z_ref, gc_ref, gT_ref,
                    hext, xm_scr, *, tm, ncol):
    i = pl.program_id(1)
    first = i == 0
    last = i == pl.num_programs(1) - 1
    shift = mod_ref[0, 0:1, :]
    scale = mod_ref[0, 1:2, :]
    nw = nw_ref[...]
    if ncol:
        rows = tm // ncol
        hext[0:HALO, :] = _norm_mod(xp_ref[0, :, SUBLANES - 1, :], nw, shift, scale).astype(BF16)
        for j in range(ncol):
            hext[HALO + j * rows:HALO + (j + 1) * rows, :] = _norm_mod(
                x_ref[0, :, j, :], nw, shift, scale).astype(BF16)
        hext[HALO + tm:2 * HALO + tm, :] = _norm_mod(xn_ref[0, :, 0, :], nw, shift, scale).astype(BF16)
    else:
        hext[0:HALO, :] = _norm_mod(xp_ref[0], nw, shift, scale).astype(BF16)
        hext[HALO:HALO + tm, :] = _norm_mod(x_ref[0], nw, shift, scale).astype(BF16)
        hext[HALO + tm:2 * HALO + tm, :] = _norm_mod(xn_ref[0], nw, shift, scale).astype(BF16)
    h = hext[HALO:HALO + tm, :]

    z_ref[0] = _dot(h, wup_ref[:, MLSTM_INNER:]).astype(BF16)
    xm_scr[...] = _dot(hext[...], wup_ref[:, :MLSTM_INNER])
    xc = _silu(_conv3(xm_scr, tm, cw_ref, cb_ref, first, last))
    xc_ref[0] = xc.astype(BF16)

    gates = jnp.zeros((tm, LANES), F32) + bif_ref[...]
    for t in range(N_BD_TILES):
        cs = slice(t * BD_TILE, (t + 1) * BD_TILE)
        xct = xc[:, cs].astype(BF16)
        xmt = xm_scr[HALO:HALO + tm, cs].astype(BF16)
        qt = _dot(xct, bdq_ref[t])
        kt = _dot(xct, bdk_ref[t])
        vt = _dot(xmt, bdv_ref[t])
        q_ref[0, :, cs] = qt.astype(BF16)
        v_ref[0, :, cs] = vt.astype(BF16)
        kT_ref[0, cs, :] = (_dot_nt(bdkT_ref[t], xct) * (MLSTM_DH ** -0.5)).astype(BF16)
        gates = gates + _dot(qt.astype(BF16), wifq_ref[cs, :]) + _dot(kt.astype(BF16), wifk_ref[cs, :]) \
            + _dot(vt.astype(BF16), wifv_ref[cs, :])

    gt = jnp.transpose(gates)[:4 * MLSTM_HEADS]
    r = lax.broadcasted_iota(jnp.int32, gt.shape, 0)
    lf = _log_sigmoid(gt)
    cum = jnp.where(r < 2 * MLSTM_HEADS, _seg_cumsum(lf, MLSTM_CHUNK, 1, False),
                    _seg_cumsum(lf, MLSTM_CHUNK, 1, True))
    packed = jnp.where(r % (2 * MLSTM_HEADS) < MLSTM_HEADS, gt, cum)
    a = gt - pltpu.roll(cum, 4 * MLSTM_HEADS - MLSTM_HEADS, 0)
    amax = jnp.where(r < 2 * MLSTM_HEADS, _seg_scan(a, MLSTM_CHUNK, 1, False, use_max=True),
                     _seg_scan(a, MLSTM_CHUNK, 1, True, use_max=True))
    full = jnp.concatenate([packed, amax], axis=0)
    gT_ref[0] = full
    gc_ref[0] = jnp.transpose(jnp.concatenate([full, jnp.zeros((LANES - MLSTM_GATE_ROWS, tm), F32)], axis=0))


def _inproj1(x, mod, nw, wp, tm, colmajor):
    b, l, _ = x.shape
    tm = min(tm, l)
    nt = l // tm
    bm = mod.shape[0]
    mod_idx = (lambda bi, i: (bi, 0, 0)) if bm > 1 else (lambda bi, i: (0, 0, 0))
    if colmajor:
        rows = l // GRID_W
        ncol = tm // rows
        assert ncol % SUBLANES == 0 and rows % HALO == 0
        xv = x.reshape(b, rows, GRID_W, D_MODEL)
        rb = rows // HALO
        cb = ncol // SUBLANES
        ncb = GRID_W // SUBLANES
        x_spec = pl.BlockSpec((1, rows, ncol, D_MODEL), lambda bi, i: (bi, 0, i, 0))
        xp_spec = pl.BlockSpec((1, HALO, SUBLANES, D_MODEL),
                               lambda bi, i: (bi, rb - 1, jnp.maximum(i * cb - 1, 0), 0))
        xn_spec = pl.BlockSpec((1, HALO, SUBLANES, D_MODEL),
                               lambda bi, i: (bi, 0, jnp.minimum((i + 1) * cb, ncb - 1), 0))
    else:
        ncol = 0
        xv = x
        hb = tm // HALO
        nhb = l // HALO
        x_spec = pl.BlockSpec((1, tm, D_MODEL), lambda bi, i: (bi, i, 0))
        xp_spec = pl.BlockSpec((1, HALO, D_MODEL), lambda bi, i: (bi, jnp.maximum(i * hb - 1, 0), 0))
        xn_spec = pl.BlockSpec((1, HALO, D_MODEL), lambda bi, i: (bi, jnp.minimum((i + 1) * hb, nhb - 1), 0))
    tok = lambda w: pl.BlockSpec((1, tm, w), lambda bi, i: (bi, i, 0))
    names = ["w_up", "conv_w", "conv_b", "bdq", "bdk", "bdkT", "bdv", "wif_q", "wif_k", "wif_v", "b_if"]
    q, kT, v, xc, z, gc, gT = pl.pallas_call(
        functools.partial(_inproj1_kernel, tm=tm, ncol=ncol),
        grid=(b, nt),
        in_specs=[x_spec, xp_spec, xn_spec, pl.BlockSpec((1, N_MOD, D_MODEL), mod_idx), _const_spec((1, D_MODEL))]
        + [_const_spec(wp[n].shape) for n in names],
        out_specs=[tok(MLSTM_INNER), pl.BlockSpec((1, MLSTM_INNER, tm), lambda bi, i: (bi, 0, i)),
                   tok(MLSTM_INNER), tok(MLSTM_INNER), tok(MLSTM_INNER), tok(LANES),
                   pl.BlockSpec((1, MLSTM_GATE_ROWS, tm), lambda bi, i: (bi, 0, i))],
        out_shape=[jax.ShapeDtypeStruct((b, l, MLSTM_INNER), BF16), jax.ShapeDtypeStruct((b, MLSTM_INNER, l), BF16),
                   jax.ShapeDtypeStruct((b, l, MLSTM_INNER), BF16), jax.ShapeDtypeStruct((b, l, MLSTM_INNER), BF16),
                   jax.ShapeDtypeStruct((b, l, MLSTM_INNER), BF16), jax.ShapeDtypeStruct((b, l, LANES), F32),
                   jax.ShapeDtypeStruct((b, MLSTM_GATE_ROWS, l), F32)],
        scratch_shapes=[pltpu.VMEM((tm + 2 * HALO, D_MODEL), BF16),
                        pltpu.VMEM((tm + 2 * HALO, MLSTM_INNER), F32)],
        compiler_params=_cparams(("parallel", "arbitrary")),
        name="inproj1",
    )(xv, xv, xv, mod, nw, *[wp[n] for n in names])
    return dict(q=q, kT=kT, v=v, xc=xc, z=z, gc=gc, gT=gT)


def _mlstm_head(q, kT, v, gc, gT, c_scr, cb_scr, n_scr, m_scr, d, hd, reverse):
    c = MLSTM_CHUNK
    dh = MLSTM_DH
    idx = d * MLSTM_HEADS + hd
    ri = d * 2 * MLSTM_HEADS + hd
    rb = ri + MLSTM_HEADS
    rm = 4 * MLSTM_HEADS + ri
    c_scr, cb_scr, n_scr, m_scr = c_scr[idx], cb_scr[idx], n_scr[idx], m_scr[idx]
    bi = gc[:, rb:rb + 1]
    am_i = gc[:, rm:rm + 1]
    bj = gT[rb:rb + 1, :]
    a_j = gT[ri:ri + 1, :] - bj
    am_j = gT[rm:rm + 1, :]
    m0 = m_scr[0:1, 0:1]
    big_m = jnp.maximum(m0, am_i)
    ii = lax.broadcasted_iota(jnp.int32, (c, c), 0)
    jj = lax.broadcasted_iota(jnp.int32, (c, c), 1)
    causal = (jj >= ii) if reverse else (jj <= ii)
    s = _dot(q, kT) * jnp.where(causal, jnp.exp(a_j - big_m), 0.0)
    w_inter = jnp.exp(m0 - big_m)
    qf = q.astype(F32)
    den = jnp.sum(s, axis=-1, keepdims=True) \
        + w_inter * jnp.sum(qf * n_scr[0:1, :], axis=-1, keepdims=True)
    num = _dot(s.astype(BF16), v) + _dot((qf * w_inter).astype(BF16), cb_scr[...])
    h = num * (1.0 / jnp.maximum(jnp.abs(den), jnp.exp(-(bi + big_m))))

    last = 0 if reverse else c - 1
    m_inner = jnp.maximum(m0, am_j[:, last:last + 1])
    kw = kT.astype(F32) * jnp.exp(a_j - m_inner)
    decay = jnp.exp(m0 - m_inner)
    kwb = kw.astype(BF16)
    c_new = decay * c_scr[...] + _dot(kwb, v)
    c_scr[...] = c_new
    cb_scr[...] = c_new.astype(BF16)
    n_scr[...] = decay * n_scr[...] + _dot_nt(jnp.ones((SUBLANES, c), BF16), kwb)
    m_scr[...] = jnp.broadcast_to(bj[:, last:last + 1] + m_inner, m_scr.shape)
    return h


def _mlstm_kernel(*refs, has_init, emit_final):
    ins = refs[:10]
    pos = 10
    if has_init:
        c0_ref, n0_ref, m0_ref = refs[pos:pos + 3]
        pos += 3
    hf_ref, hb_ref = refs[pos:pos + 2]
    pos += 2
    if emit_final:
        cfin_ref, nfin_ref, mfin_ref = refs[pos:pos + 3]
        pos += 3
    nst = 2 * MLSTM_HEADS
    c_scr, cb_scr, n_scr, m_scr = (refs[pos + k * nst:pos + (k + 1) * nst] for k in range(4))
    n = pl.program_id(1)

    @pl.when(n == 0)
    def _():
        for k in range(nst):
            if has_init:
                c_scr[k][...] = c0_ref[0, k]
                cb_scr[k][...] = c0_ref[0, k].astype(BF16)
                n_scr[k][...] = n0_ref[0, k]
                m_scr[k][...] = m0_ref[0, k]
            else:
                c_scr[k][...] = jnp.zeros_like(c_scr[k])
                cb_scr[k][...] = jnp.zeros_like(cb_scr[k])
                n_scr[k][...] = jnp.zeros_like(n_scr[k])
                m_scr[k][...] = jnp.full(m_scr[k].shape, M_INIT, F32)

    for d, h_ref in enumerate((hf_ref, hb_ref)):
        q_ref, kT_ref, v_ref, gc_ref, gT_ref = ins[5 * d:5 * d + 5]
        gc = gc_ref[0]
        gT = gT_ref[0]
        for hd in range(MLSTM_HEADS):
            cs = slice(hd * MLSTM_DH, (hd + 1) * MLSTM_DH)
            h = _mlstm_head(q_ref[0, :, cs], kT_ref[0, cs, :], v_ref[0, :, cs], gc, gT,
                            c_scr, cb_scr, n_scr, m_scr, d, hd, d == 1)
            h_ref[0, :, cs] = h.astype(BF16)

    if emit_final:
        @pl.when(n == pl.num_programs(1) - 1)
        def _():
            for k in range(nst):
                cfin_ref[0, k] = c_scr[k][...]
                nfin_ref[0, k] = n_scr[k][...]
                mfin_ref[0, k] = m_scr[k][...]


def _mlstm_scan(p, init, emit_final):
    b, l, _ = p["q"].shape
    c = MLSTM_CHUNK
    nc = l // c
    dh = MLSTM_DH
    nst = 2 * MLSTM_HEADS
    specs = []
    for rev in (False, True):
        idx = (lambda n: nc - 1 - n) if rev else (lambda n: n)
        specs.append([
            pl.BlockSpec((1, c, MLSTM_INNER), lambda bi, n, idx=idx: (bi, idx(n), 0)),
            pl.BlockSpec((1, MLSTM_INNER, c), lambda bi, n, idx=idx: (bi, 0, idx(n))),
            pl.BlockSpec((1, c, MLSTM_INNER), lambda bi, n, idx=idx: (bi, idx(n), 0)),
            pl.BlockSpec((1, c, LANES), lambda bi, n, idx=idx: (bi, idx(n), 0)),
            pl.BlockSpec((1, MLSTM_GATE_ROWS, c), lambda bi, n, idx=idx: (bi, 0, idx(n))),
        ])
    st_shapes = [(b, nst, dh, dh), (b, nst, SUBLANES, dh), (b, nst, SUBLANES, LANES)]
    st_index = lambda bi, n: (bi, 0, 0, 0)
    args = (p["q"], p["kT"], p["v"], p["gc"], p["gT"])
    operands = [*args, *args]
    in_specs = specs[0] + specs[1]
    if init is not None:
        operands += list(init)
        in_specs += [pl.BlockSpec((1,) + s[1:], st_index, pipeline_mode=pl.Buffered(1)) for s in st_shapes]
    out_specs = [specs[0][0], specs[1][0]]
    out_shape = [jax.ShapeDtypeStruct((b, l, MLSTM_INNER), BF16), jax.ShapeDtypeStruct((b, l, MLSTM_INNER), BF16)]
    if emit_final:
        out_specs += [pl.BlockSpec((1,) + s[1:], st_index) for s in st_shapes]
        out_shape += [jax.ShapeDtypeStruct(s, F32) for s in st_shapes]
    res = pl.pallas_call(
        functools.partial(_mlstm_kernel, has_init=init is not None, emit_final=emit_final),
        grid=(b, nc),
        in_specs=in_specs,
        out_specs=out_specs,
        out_shape=out_shape,
        scratch_shapes=[pltpu.VMEM((dh, dh), F32)] * nst + [pltpu.VMEM((dh, dh), BF16)] * nst
        + [pltpu.VMEM((SUBLANES, dh), F32)] * nst + [pltpu.VMEM((SUBLANES, LANES), F32)] * nst,
        compiler_params=_cparams(("parallel", "arbitrary")),
        name="mlstm_scan",
    )(*operands)
    return res[0], res[1], tuple(res[2:])


def _post1_kernel(hf_ref, hb_ref, xc_ref, z_ref, x_ref, mod_ref, mh_ref, sk_ref, wd_ref, out_ref, *, tm, ncol):
    hs = hf_ref[0].astype(F32) + hb_ref[0].astype(F32)
    parts = []
    for hd in range(MLSTM_HEADS):
        hh = hs[:, hd * MLSTM_DH:(hd + 1) * MLSTM_DH]
        mu = jnp.mean(hh, axis=-1, keepdims=True)
        cen = hh - mu
        parts.append(cen * lax.rsqrt(jnp.mean(cen * cen, axis=-1, keepdims=True) + EPS))
    feat = (jnp.concatenate(parts, axis=1) * mh_ref[...] + sk_ref[...] * xc_ref[0].astype(F32)) \
        * _silu(z_ref[0].astype(F32))
    res = mod_ref[0, 2:3, :] * _dot(feat.astype(BF16), wd_ref[...])
    if ncol:
        rows = tm // ncol
        for j in range(ncol):
            out_ref[0, :, j, :] = x_ref[0, :, j, :] + res[j * rows:(j + 1) * rows, :]
    else:
        out_ref[0] = x_ref[0] + res


def _post1(p, hf, hb, x, mod, wp, tm, colmajor):
    b, l, _ = x.shape
    tm = min(tm, l)
    bm = mod.shape[0]
    mod_idx = (lambda bi, i: (bi, 0, 0)) if bm > 1 else (lambda bi, i: (0, 0, 0))
    tok = lambda w: pl.BlockSpec((1, tm, w), lambda bi, i: (bi, i, 0))
    if colmajor:
        rows = l // GRID_W
        ncol = tm // rows
        xv = x.reshape(b, rows, GRID_W, D_MODEL)
        x_spec = pl.BlockSpec((1, rows, ncol, D_MODEL), lambda bi, i: (bi, 0, i, 0))
    else:
        ncol = 0
        xv = x
        x_spec = tok(D_MODEL)
    out = pl.pallas_call(
        functools.partial(_post1_kernel, tm=tm, ncol=ncol),
        grid=(b, l // tm),
        in_specs=[tok(MLSTM_INNER), tok(MLSTM_INNER), tok(MLSTM_INNER), tok(MLSTM_INNER), x_spec,
                  pl.BlockSpec((1, N_MOD, D_MODEL), mod_idx),
                  _const_spec((1, MLSTM_INNER)), _const_spec((1, MLSTM_INNER)), _const_spec(wp["w_down"].shape)],
        out_specs=x_spec,
        out_shape=jax.ShapeDtypeStruct(xv.shape, F32),
        compiler_params=_cparams(("parallel", "arbitrary")),
        name="post1",
    )(hf, hb, p["xc"], p["z"], xv, mod, wp["mh_norm"], wp["skip"], wp["w_down"])
    return out.reshape(b, l, D_MODEL)


def _prep_even(w_in, w_gate, b_gate, gla_norm, conv_w, conv_b, a_log, dt_bias, d_skip, ssd_norm, w_out):
    sizes = (GLA_QK, GLA_QK, GLA_V, GLA_V, 2 * GLA_GATE_RANK, SSD_INNER, SSD_XBC, 2 * SSD_HEADS)
    offs = [0]
    for s in sizes:
        offs.append(offs[-1] + s)
    cols = lambda i: w_in[:, offs[i]:offs[i + 1]]
    w_main = jnp.concatenate([cols(0), cols(1), cols(2), cols(3), cols(5), cols(6)], axis=1).astype(BF16)
    pad = jnp.zeros((D_MODEL, LANES - 2 * GLA_GATE_RANK - 2 * SSD_HEADS), w_in.dtype)
    w_small = jnp.concatenate([cols(4), cols(7), pad], axis=1).astype(BF16)
    zero = jnp.zeros((GLA_GATE_RANK, GLA_QK), w_gate.dtype)
    wg = jnp.concatenate([jnp.concatenate([w_gate[0], zero], axis=1),
                          jnp.concatenate([zero, w_gate[1]], axis=1)], axis=0).astype(BF16)
    return dict(
        w_main=w_main, w_small=w_small, w_dtT=jnp.transpose(cols(7)).astype(BF16),
        w_gate=wg, b_gate=b_gate.reshape(1, 2 * GLA_QK),
        conv_w=conv_w, conv_b=conv_b.reshape(1, SSD_XBC),
        dt_bias=dt_bias.reshape(1, 2 * SSD_HEADS), a_log=a_log.reshape(1, 2 * SSD_HEADS),
        dt_biasT=dt_bias.reshape(2 * SSD_HEADS, 1), a_logT=a_log.reshape(2 * SSD_HEADS, 1),
        gla_norm=jnp.tile(gla_norm, GLA_HEADS).reshape(1, GLA_V),
        d_skip=jnp.repeat(d_skip, SSD_HEADDIM).reshape(1, SSD_INNER),
        ssd_norm=ssd_norm.reshape(1, SSD_INNER),
        w_out=w_out.astype(BF16),
    )


def _block_diag_tiles(w):
    per = BD_TILE // QKV_BLOCK
    w4 = w.reshape(N_BD_TILES, per, QKV_BLOCK, QKV_BLOCK)
    eye = jnp.eye(per, dtype=w.dtype)
    return jnp.einsum("tade,ab->tadbe", w4, eye).reshape(N_BD_TILES, BD_TILE, BD_TILE)


def _prep_odd(w_up, conv_w, conv_b, wq, wk, wv, w_if, b_if, mh_norm, skip, w_down):
    ng = 4 * MLSTM_HEADS
    wif = jnp.transpose(w_if, (1, 0, 2)).reshape(3 * MLSTM_INNER, ng)
    wif = jnp.concatenate([wif, jnp.zeros((3 * MLSTM_INNER, LANES - ng), w_if.dtype)], axis=1).astype(BF16)
    bif = jnp.concatenate([b_if.reshape(1, ng), jnp.zeros((1, LANES - ng), b_if.dtype)], axis=1)
    bdk = _block_diag_tiles(wk)
    return dict(
        w_up=w_up.astype(BF16), conv_w=conv_w, conv_b=conv_b.reshape(1, MLSTM_INNER),
        bdq=_block_diag_tiles(wq).astype(BF16), bdk=bdk.astype(BF16),
        bdkT=jnp.transpose(bdk, (0, 2, 1)).astype(BF16), bdv=_block_diag_tiles(wv).astype(BF16),
        wif_q=wif[:MLSTM_INNER], wif_k=wif[MLSTM_INNER:2 * MLSTM_INNER], wif_v=wif[2 * MLSTM_INNER:], b_if=bif,
        mh_norm=mh_norm.reshape(1, MLSTM_INNER), skip=skip.reshape(1, MLSTM_INNER),
        w_down=w_down.astype(BF16),
    )


TOKEN_TILE = 512


def kernel(x, c, ctx, c_ctx, w_mod, b_mod, norm_mix, norm_ffn, ffn_w1, ffn_w3, ffn_w2, a_w_in, a_gla_w_gate,
           a_gla_b_gate, a_gla_norm, a_ssd_conv_w, a_ssd_conv_b, a_ssd_A_log, a_ssd_dt_bias, a_ssd_D, a_ssd_norm,
           a_w_out, b_w_up, b_conv_w, b_conv_b, b_wq, b_wk, b_wv, b_w_if, b_b_if, b_mh_norm, b_skip, b_w_down,
           final_norm):
    tile = TOKEN_TILE
    bsz = x.shape[0]
    depth = w_mod.shape[0]
    pad_rows = (-(bsz + 1)) % 8
    cc = jnp.concatenate([c, c_ctx[None, :], jnp.zeros((pad_rows, D_MODEL), c.dtype)], axis=0)
    mods = _modulation(cc, w_mod, b_mod).reshape(depth, bsz + 1 + pad_rows, N_MOD, D_MODEL)
    fnorm = final_norm.reshape(1, D_MODEL)
    for layer in range(depth):
        last = layer == depth - 1
        j = layer // 2
        mx = mods[layer, :bsz]
        mc = mods[layer, bsz:bsz + 1]
        nmix = norm_mix[layer].reshape(1, D_MODEL)
        nffn = norm_ffn[layer].reshape(1, D_MODEL)
        w1 = ffn_w1[layer].astype(BF16)
        w3 = ffn_w3[layer].astype(BF16)
        w2 = ffn_w2[layer].astype(BF16)
        if layer % 2 == 0:
            wp = _prep_even(a_w_in[j], a_gla_w_gate[j], a_gla_b_gate[j], a_gla_norm[j], a_ssd_conv_w[j],
                            a_ssd_conv_b[j], a_ssd_A_log[j], a_ssd_dt_bias[j], a_ssd_D[j], a_ssd_norm[j],
                            a_w_out[j])
            pc = _inproj0(ctx, mc, nmix, wp, tile)
            gla0 = jnp.zeros((bsz, 2 * GLA_HEADS, GLA_DK, GLA_DV), F32)
            ssd0 = jnp.zeros((bsz, 2 * SSD_GROUPS, SSD_STATE, HEADS_PER_GROUP * SSD_HEADDIM), F32)
            ofc, obc, gla_st = _gla_scan(pc, gla0)
            yfc, ybc, ssd_st = _ssd_scan(pc, ssd0)
            px = _inproj0(x, mx, nmix, wp, tile)
            ofx, obx, _ = _gla_scan(px, gla_st)
            yfx, ybx, _ = _ssd_scan(px, ssd_st)
            x = _post0(px, ofx, obx, yfx, ybx, x, mx, wp, tile)
            if not last:
                ctx = _post0(pc, ofc, obc, yfc, ybc, ctx, mc, wp, tile)
        else:
            wp = _prep_odd(b_w_up[j], b_conv_w[j], b_conv_b[j], b_wq[j], b_wk[j], b_wv[j], b_w_if[j], b_b_if[j],
                           b_mh_norm[j], b_skip[j], b_w_down[j])
            pc = _inproj1(ctx, mc, nmix, wp, tile, colmajor=False)
            hfc, hbc, state = _mlstm_scan(pc, None, True)
            px = _inproj1(x, mx, nmix, wp, tile, colmajor=True)
            hfx, hbx, _ = _mlstm_scan(px, state, False)
            x = _post1(px, hfx, hbx, x, mx, wp, tile, colmajor=True)
            if not last:
                ctx = _post1(pc, hfc, hbc, ctx, mc, wp, tile, colmajor=False)
        x = _ffn(x, mx, nffn, w1, w3, w2, fnorm, last, tile)
        if not last:
            ctx = _ffn(ctx, mc, nffn, w1, w3, w2, fnorm, False, tile)
    return x
```

```python
import functools

import jax
import jax.numpy as jnp
from jax import lax
from jax.experimental import pallas as pl
from jax.experimental.pallas import tpu as pltpu

F32 = jnp.float32
BF16 = jnp.bfloat16

D_MODEL = 1024
GRID_W = 64
EPS = 1e-6
M_INIT = -1e30
N_MOD = 6

GLA_HEADS = 4
GLA_DK = 128
GLA_DV = 256
GLA_QK = GLA_HEADS * GLA_DK
GLA_V = GLA_HEADS * GLA_DV
GLA_GATE_RANK = 16
GLA_GATE_TAU = 16.0

SSD_HEADDIM = 64
SSD_HEADS = 16
SSD_STATE = 128
SSD_GROUPS = 2
SSD_INNER = SSD_HEADS * SSD_HEADDIM
SSD_BC = SSD_GROUPS * SSD_STATE
SSD_XBC = SSD_INNER + 2 * SSD_BC
HEADS_PER_GROUP = SSD_HEADS // SSD_GROUPS

MLSTM_INNER = 2 * D_MODEL
MLSTM_HEADS = 4
MLSTM_DH = MLSTM_INNER // MLSTM_HEADS
QKV_BLOCK = 4
BD_TILE = 256
N_BD_TILES = MLSTM_INNER // BD_TILE
MLSTM_GATE_ROWS = 32

D_FF = 2816

GLA_CHUNK = 64
GLA_CHUNKS_PER_STEP = 2
SSD_CHUNK = 128
MLSTM_CHUNK = 256
HALO = 16
LANES = 128
SUBLANES = 8

VMEM_LIMIT = 56 * 1024 * 1024


def _cparams(sem):
    return pltpu.CompilerParams(dimension_semantics=sem, vmem_limit_bytes=VMEM_LIMIT)


def _const_spec(shape):
    nd = len(shape)
    return pl.BlockSpec(shape, lambda *_: (0,) * nd, pipeline_mode=pl.Buffered(1))


def _softplus(x):
    return jnp.maximum(x, 0.0) + jnp.log(1.0 + jnp.exp(-jnp.abs(x)))


def _log_sigmoid(x):
    return -_softplus(-x)


def _silu(x):
    return x / (1.0 + jnp.exp(-x))


def _seg_scan(x, seg, axis, reverse, use_max=False):
    n = x.shape[axis]
    idx = lax.broadcasted_iota(jnp.int32, x.shape, axis) % seg
    s = 1
    while s < seg:
        if reverse:
            shifted = pltpu.roll(x, n - s, axis)
            keep = idx < seg - s
        else:
            shifted = pltpu.roll(x, s, axis)
            keep = idx >= s
        if use_max:
            x = jnp.where(keep, jnp.maximum(x, shifted), x)
        else:
            x = x + jnp.where(keep, shifted, 0.0)
        s *= 2
    return x


def _seg_cumsum(x, seg, axis, reverse):
    return _seg_scan(x, seg, axis, reverse)


def _norm_mod(xv, nw, shift, scale):
    ms = jnp.mean(xv * xv, axis=-1, keepdims=True)
    return (xv * lax.rsqrt(ms + EPS) * nw) * (1.0 + scale) + shift


def _dot(a, b):
    return jnp.dot(a, b, preferred_element_type=F32)


def _dot_nt(a, b):
    return lax.dot_general(a, b, (((1,), (1,)), ((), ())), preferred_element_type=F32)


def _dot_tn(a, b):
    return lax.dot_general(a, b, (((0,), (0,)), ((), ())), preferred_element_type=F32)


def _mod_kernel(c_ref, w_ref, b_ref, o_ref):
    a = _silu(c_ref[...]).astype(BF16)
    o_ref[0] = _dot(a, w_ref[0].astype(BF16)) + b_ref[0]


def _modulation(cc, w_mod, b_mod):
    depth = w_mod.shape[0]
    n = w_mod.shape[2]
    tn = n // 4
    rows = cc.shape[0]
    return pl.pallas_call(
        _mod_kernel,
        grid=(depth, n // tn),
        in_specs=[
            pl.BlockSpec((rows, D_MODEL), lambda l, j: (0, 0)),
            pl.BlockSpec((1, D_MODEL, tn), lambda l, j: (l, 0, j)),
            pl.BlockSpec((1, 1, tn), lambda l, j: (l, 0, j)),
        ],
        out_specs=pl.BlockSpec((1, rows, tn), lambda l, j: (l, 0, j)),
        out_shape=jax.ShapeDtypeStruct((depth, rows, n), F32),
        compiler_params=_cparams(("arbitrary", "arbitrary")),
        name="modulation",
    )(cc, w_mod, b_mod.reshape(depth, 1, n))


def _conv3(scr, tm, cw_ref, cb_ref, first, last):
    scr[HALO - 1:HALO, :] = scr[HALO - 1:HALO, :] * jnp.where(first, 0.0, 1.0)
    scr[HALO + tm:HALO + tm + 1, :] = scr[HALO + tm:HALO + tm + 1, :] * jnp.where(last, 0.0, 1.0)

    prev = scr[HALO - 1:HALO - 1 + tm, :]
    cur = scr[HALO:HALO + tm, :]
    nxt = scr[HALO + 1:HALO + 1 + tm, :]
    return cw_ref[0:1, :] * prev + cw_ref[1:2, :] * cur + cw_ref[2:3, :] * nxt + cb_ref[...]


def _inproj0_kernel(x_ref, xp_ref, xn_ref, mod_ref, nw_ref, wmain_ref, wsmall_ref, wdtT_ref, wg_ref, bg_ref,
                    cw_ref, cb_ref, dtb_ref, alog_ref, dtbT_ref, alogT_ref,
                    q_ref, k_ref, v_ref, g_ref, z_ref, gcum_ref, xs_ref, bc_ref, bT_ref, sd_ref, sdT_ref,
                    hext, xbc_scr, *, tm):
    i = pl.program_id(1)
    first = i == 0
    last = i == pl.num_programs(1) - 1
    shift = mod_ref[0, 0:1, :]
    scale = mod_ref[0, 1:2, :]
    nw = nw_ref[...]
    hext[0:HALO, :] = _norm_mod(xp_ref[0], nw, shift, scale).astype(BF16)
    hext[HALO:HALO + tm, :] = _norm_mod(x_ref[0], nw, shift, scale).astype(BF16)
    hext[HALO + tm:2 * HALO + tm, :] = _norm_mod(xn_ref[0], nw, shift, scale).astype(BF16)

    o_k = GLA_QK
    o_v = o_k + GLA_QK
    o_g = o_v + GLA_V
    o_z = o_g + GLA_V
    o_x = o_z + SSD_INNER
    nd = 2 * SSD_HEADS
    h = hext[HALO:HALO + tm, :]

    q_ref[0] = (_dot(h, wmain_ref[:, 0:o_k]) * (GLA_DK ** -0.5)).astype(BF16)
    k_ref[0] = _dot(h, wmain_ref[:, o_k:o_v]).astype(BF16)
    v_ref[0] = _dot(h, wmain_ref[:, o_v:o_g]).astype(BF16)
    g_ref[0] = _dot(h, wmain_ref[:, o_g:o_z]).astype(BF16)
    z_ref[0] = _dot(h, wmain_ref[:, o_z:o_x]).astype(BF16)

    xbc_scr[...] = _dot(hext[...], wmain_ref[:, o_x:o_x + SSD_XBC])
    y = _silu(_conv3(xbc_scr, tm, cw_ref, cb_ref, first, last))
    xs_ref[0] = y[:, :SSD_INNER].astype(BF16)
    bc_ref[0] = y[:, SSD_INNER:].astype(BF16)
    bT_ref[0] = jnp.transpose(y[:, SSD_INNER:SSD_INNER + SSD_BC]).astype(BF16)

    small = _dot(h, wsmall_ref[...])
    r = small[:, :2 * GLA_GATE_RANK].astype(BF16)
    la = _log_sigmoid(_dot(r, wg_ref[...]) + bg_ref[...]) * (1.0 / GLA_GATE_TAU)
    gcum_ref[0, :, :GLA_QK] = _seg_cumsum(la[:, :GLA_QK], GLA_CHUNK, 0, False)
    gcum_ref[0, :, GLA_QK:] = _seg_cumsum(la[:, GLA_QK:], GLA_CHUNK, 0, True)

    dtT = _softplus(_dot_nt(wdtT_ref[...], h) + dtbT_ref[...])
    laT = dtT * (-jnp.exp(alogT_ref[...]))
    cumT = jnp.concatenate([_seg_cumsum(laT[:SSD_HEADS], SSD_CHUNK, 1, False),
                            _seg_cumsum(laT[SSD_HEADS:], SSD_CHUNK, 1, True)], axis=0)
    packed = jnp.concatenate([dtT, cumT, jnp.zeros((LANES - 2 * nd, tm), F32)], axis=0)
    sdT_ref[0] = packed[:2 * nd]
    sd_ref[0] = jnp.transpose(packed)


def _inproj0(x, mod, nw, wp, tm):
    b, l, _ = x.shape
    tm = min(tm, l)
    nt = l // tm
    hb = tm // HALO
    nhb = l // HALO
    bm = mod.shape[0]
    mod_idx = (lambda bi, i: (bi, 0, 0)) if bm > 1 else (lambda bi, i: (0, 0, 0))
    tok = lambda w: pl.BlockSpec((1, tm, w), lambda bi, i: (bi, i, 0))
    in_specs = [
        tok(D_MODEL),
        pl.BlockSpec((1, HALO, D_MODEL), lambda bi, i: (bi, jnp.maximum(i * hb - 1, 0), 0)),
        pl.BlockSpec((1, HALO, D_MODEL), lambda bi, i: (bi, jnp.minimum((i + 1) * hb, nhb - 1), 0)),
        pl.BlockSpec((1, N_MOD, D_MODEL), mod_idx),
        _const_spec((1, D_MODEL)),
        _const_spec(wp["w_main"].shape), _const_spec(wp["w_small"].shape), _const_spec(wp["w_dtT"].shape),
        _const_spec(wp["w_gate"].shape), _const_spec(wp["b_gate"].shape),
        _const_spec(wp["conv_w"].shape), _const_spec(wp["conv_b"].shape),
        _const_spec(wp["dt_bias"].shape), _const_spec(wp["a_log"].shape),
        _const_spec(wp["dt_biasT"].shape), _const_spec(wp["a_logT"].shape),
    ]
    outs = [
        ("q", GLA_QK, BF16), ("k", GLA_QK, BF16), ("v", GLA_V, BF16), ("g", GLA_V, BF16),
        ("z", SSD_INNER, BF16), ("gcum", 2 * GLA_QK, F32), ("xs", SSD_INNER, BF16),
        ("bc", 2 * SSD_BC, BF16),
    ]
    out_specs = [tok(w) for _, w, _ in outs]
    out_shape = [jax.ShapeDtypeStruct((b, l, w), dt) for _, w, dt in outs]
    out_specs += [pl.BlockSpec((1, SSD_BC, tm), lambda bi, i: (bi, 0, i)),
                  tok(LANES),
                  pl.BlockSpec((1, 4 * SSD_HEADS, tm), lambda bi, i: (bi, 0, i))]
    out_shape += [jax.ShapeDtypeStruct((b, SSD_BC, l), BF16),
                  jax.ShapeDtypeStruct((b, l, LANES), F32),
                  jax.ShapeDtypeStruct((b, 4 * SSD_HEADS, l), F32)]
    res = pl.pallas_call(
        functools.partial(_inproj0_kernel, tm=tm),
        grid=(b, nt),
        in_specs=in_specs,
        out_specs=out_specs,
        out_shape=out_shape,
        scratch_shapes=[pltpu.VMEM((tm + 2 * HALO, D_MODEL), BF16),
                        pltpu.VMEM((tm + 2 * HALO, SSD_XBC), F32)],
        compiler_params=_cparams(("parallel", "arbitrary")),
        name="inproj0",
    )(x, x, x, mod, nw, wp["w_main"], wp["w_small"], wp["w_dtT"], wp["w_gate"], wp["b_gate"],
      wp["conv_w"], wp["conv_b"], wp["dt_bias"], wp["a_log"], wp["dt_biasT"], wp["a_logT"])
    names = [n for n, _, _ in outs] + ["bT", "sd", "sdT"]
    return dict(zip(names, res))


def _level_ref(g, hsz, reverse):
    c, w = g.shape
    grp = 2 * hsz
    r = hsz if reverse else hsz - 1
    if grp >= 8:
        g3 = g.reshape(c // grp, grp, w)
        return jnp.broadcast_to(g3[:, r:r + 1, :], g3.shape).reshape(c, w)
    g3 = g.reshape(c // 8, 8, w)
    sub = lax.broadcasted_iota(jnp.int32, g3.shape, 1)
    out = None
    for t in range(8 // grp):
        cand = jnp.broadcast_to(g3[:, t * grp + r:t * grp + r + 1, :], g3.shape)
        out = cand if out is None else jnp.where(sub >= t * grp, cand, out)
    return out.reshape(c, w)


def _gla_masks(c, w, reverse):
    row = lax.broadcasted_iota(jnp.int32, (c, w), 0)
    ii = lax.broadcasted_iota(jnp.int32, (c, c), 0)
    jj = lax.broadcasted_iota(jnp.int32, (c, c), 1)
    par = 0 if reverse else 1
    levels = []
    hsz = 1
    while hsz < c:
        is_q = (row // hsz) % 2 == par
        keep = ((ii // (2 * hsz)) == (jj // (2 * hsz))) & ((ii // hsz) % 2 == par) & ((jj // hsz) % 2 != par)
        levels.append((hsz, is_q, keep.astype(F32)))
        hsz *= 2
    return (ii == jj).astype(F32), levels


def _gla_chunk(q, k, v, g, s, masks, reverse):
    c = q.shape[0]
    diag, levels = masks
    a = diag * _dot_nt(q.astype(BF16), k.astype(BF16))
    for hsz, is_q, keep in levels:
        e = jnp.exp(-jnp.abs(g - _level_ref(g, hsz, reverse)))
        mixed = (jnp.where(is_q, q, k) * e).astype(BF16)
        a = a + keep * _dot_nt(mixed, mixed)
    g_tot = g[0:1, :] if reverse else g[c - 1:c, :]
    qi = (q * jnp.exp(g)).astype(BF16)
    ki = (k * jnp.exp(g_tot - g)).astype(BF16)
    o = _dot(a.astype(BF16), v) + _dot(qi, s.astype(BF16))
    dk = g.shape[1]
    decay_col = jnp.transpose(jnp.broadcast_to(jnp.exp(g_tot), (dk, dk)))[:, 0:1]
    s_new = s * decay_col + _dot_tn(ki, v)
    return o, s_new


def _gla_kernel(qf_ref, kf_ref, vf_ref, gf_ref, qb_ref, kb_ref, vb_ref, gb_ref, s0_ref,
                of_ref, ob_ref, sfin_ref, *s_scr):
    n = pl.program_id(1)

    @pl.when(n == 0)
    def _():
        for k, s_ref in enumerate(s_scr):
            s_ref[...] = s0_ref[0, k]

    c = GLA_CHUNK
    for d, (q_ref, k_ref, v_ref, g_ref, o_ref) in enumerate(
            ((qf_ref, kf_ref, vf_ref, gf_ref, of_ref), (qb_ref, kb_ref, vb_ref, gb_ref, ob_ref))):
        masks = _gla_masks(c, GLA_DK, d == 1)
        order = range(GLA_CHUNKS_PER_STEP) if d == 0 else range(GLA_CHUNKS_PER_STEP - 1, -1, -1)
        for sc in order:
            rs = slice(sc * c, (sc + 1) * c)
            for hd in range(GLA_HEADS):
                ks = slice(hd * GLA_DK, (hd + 1) * GLA_DK)
                vs = slice(hd * GLA_DV, (hd + 1) * GLA_DV)
                s_ref = s_scr[d * GLA_HEADS + hd]
                o, s_new = _gla_chunk(q_ref[0, rs, ks].astype(F32), k_ref[0, rs, ks].astype(F32),
                                      v_ref[0, rs, vs], g_ref[0, rs, ks], s_ref[...], masks, d == 1)
                o_ref[0, rs, vs] = o.astype(BF16)
                s_ref[...] = s_new

    @pl.when(n == pl.num_programs(1) - 1)
    def _():
        for k, s_ref in enumerate(s_scr):
            sfin_ref[0, k] = s_ref[...]


def _gla_scan(p, s0):
    b, l, _ = p["q"].shape
    c = GLA_CHUNK * GLA_CHUNKS_PER_STEP
    nc = l // c
    fwd = lambda w, col=0: pl.BlockSpec((1, c, w), lambda bi, n: (bi, n, col))
    bwd = lambda w, col=0: pl.BlockSpec((1, c, w), lambda bi, n: (bi, nc - 1 - n, col))
    st = pl.BlockSpec((1, 2 * GLA_HEADS, GLA_DK, GLA_DV), lambda bi, n: (bi, 0, 0, 0))
    of, ob, sfin = pl.pallas_call(
        _gla_kernel,
        grid=(b, nc),
        in_specs=[fwd(GLA_QK), fwd(GLA_QK), fwd(GLA_V), fwd(GLA_QK, 0),
                  bwd(GLA_QK), bwd(GLA_QK), bwd(GLA_V), bwd(GLA_QK, 1), st],
        out_specs=[fwd(GLA_V), bwd(GLA_V), st],
        out_shape=[jax.ShapeDtypeStruct((b, l, GLA_V), BF16), jax.ShapeDtypeStruct((b, l, GLA_V), BF16),
                   jax.ShapeDtypeStruct(s0.shape, F32)],
        scratch_shapes=[pltpu.VMEM((GLA_DK, GLA_DV), F32)] * (2 * GLA_HEADS),
        compiler_params=_cparams(("parallel", "arbitrary")),
        name="gla_scan",
    )(p["q"], p["k"], p["v"], p["gcum"], p["q"], p["k"], p["v"], p["gcum"], s0)
    return of, ob, sfin


def _split_hi_lo(x):
    hi = x.astype(BF16)
    lo = (x - hi.astype(F32)).astype(BF16)
    return jnp.concatenate([hi, lo], axis=1)


def _ssd_factors(sd, ex, d):
    c = sd.shape[0]
    nh = SSD_HEADS
    lane = lax.broadcasted_iota(jnp.int32, (1, LANES), 1)
    sel = (lane >= (2 + d) * nh) & (lane < (3 + d) * nh)
    tot = sd[0:1, :] if d == 1 else sd[c - 1:c, :]
    dt_under_cum = pltpu.roll(sd, 2 * nh, 1)
    e_in = jnp.where(sel, jnp.exp(sd), 0.0)
    w_st = jnp.where(sel, dt_under_cum * jnp.exp(tot - sd), 0.0)
    e_tot = jnp.broadcast_to(jnp.where(sel, jnp.exp(tot), 0.0), (SUBLANES, LANES))
    expand = lambda f: _dot(_split_hi_lo(f), ex)
    return expand(e_in), expand(w_st), expand(e_tot)[0:1, :]


def _ssd_kernel(xsa_ref, bca_ref, bTa_ref, sda_ref, sdTa_ref, xsb_ref, bcb_ref, bTb_ref, sdb_ref,
                ex_ref, s0_ref, ya_ref, yb_ref, sfin_ref, *s_scr):
    c = SSD_CHUNK
    nh = SSD_HEADS
    gw = HEADS_PER_GROUP * SSD_HEADDIM
    n = pl.program_id(1)

    @pl.when(n == 0)
    def _():
        for k, s_ref in enumerate(s_scr):
            s_ref[...] = s0_ref[0, k]

    sda = sda_ref[0]
    sdTa = sdTa_ref[0]
    ein_f, wst_f, dec_f = _ssd_factors(sda, ex_ref[0], 0)
    ein_b, wst_b, dec_b = _ssd_factors(sdb_ref[0], ex_ref[1], 1)
    ii = lax.broadcasted_iota(jnp.int32, (c, c), 0)
    jj = lax.broadcasted_iota(jnp.int32, (c, c), 1)
    lower = jj <= ii
    upper = jj >= ii
    lo_half = lax.broadcasted_iota(jnp.int32, (1, LANES), 1) < SSD_HEADDIM

    def weights(h, d, mask):
        col = (2 + d) * nh + h
        ci = sda[:, col:col + 1]
        cj = sdTa[col:col + 1, :]
        dtj = sdTa[d * nh + h:d * nh + h + 1, :]
        return jnp.where(mask, jnp.exp(jnp.minimum(ci - cj, 0.0)) * dtj, 0.0)

    for gi in range(SSD_GROUPS):
        gs = slice(gi * gw, (gi + 1) * gw)
        bm = bca_ref[0, :, gi * SSD_STATE:(gi + 1) * SSD_STATE]
        cm = bca_ref[0, :, SSD_BC + gi * SSD_STATE:SSD_BC + (gi + 1) * SSD_STATE]
        gmat = _dot_nt(cm, bm)
        s_f = s_scr[gi]
        s_b = s_scr[SSD_GROUPS + gi]
        y_inter = _dot(cm, s_f[...].astype(BF16)) * ein_f[:, gs]
        for hp in range(HEADS_PER_GROUP // 2):
            col0 = (gi * HEADS_PER_GROUP + 2 * hp) * SSD_HEADDIM
            xpair = xsa_ref[0, :, col0:col0 + LANES]
            scores = []
            for t in range(2):
                h = gi * HEADS_PER_GROUP + 2 * hp + t
                scores.append((gmat * (weights(h, 0, lower) + weights(h, 1, upper))).astype(BF16))
            zero = jnp.zeros_like(xpair)
            rhs = jnp.concatenate([jnp.where(lo_half, xpair, zero), jnp.where(lo_half, zero, xpair)], axis=0)
            y_pair = _dot(jnp.concatenate(scores, axis=1), rhs) + y_inter[:, hp * LANES:(hp + 1) * LANES]
            ya_ref[0, :, col0:col0 + LANES] = y_pair.astype(BF16)
        wx = (xsa_ref[0, :, gs].astype(F32) * wst_f[:, gs]).astype(BF16)
        s_f[...] = s_f[...] * dec_f[:, gs] + _dot(bTa_ref[0, gi * SSD_STATE:(gi + 1) * SSD_STATE, :], wx)

        cmb = bcb_ref[0, :, SSD_BC + gi * SSD_STATE:SSD_BC + (gi + 1) * SSD_STATE]
        yb_ref[0, :, gs] = (_dot(cmb, s_b[...].astype(BF16)) * ein_b[:, gs]).astype(BF16)
        wxb = (xsb_ref[0, :, gs].astype(F32) * wst_b[:, gs]).astype(BF16)
        s_b[...] = s_b[...] * dec_b[:, gs] + _dot(bTb_ref[0, gi * SSD_STATE:(gi + 1) * SSD_STATE, :], wxb)

    @pl.when(n == pl.num_programs(1) - 1)
    def _():
        for k, s_ref in enumerate(s_scr):
            sfin_ref[0, k] = s_ref[...]


def _ssd_expander():
    r = lax.broadcasted_iota(jnp.int32, (2, 2 * LANES, SSD_INNER), 1) % LANES
    d = lax.broadcasted_iota(jnp.int32, (2, 2 * LANES, SSD_INNER), 0)
    h = lax.broadcasted_iota(jnp.int32, (2, 2 * LANES, SSD_INNER), 2) // SSD_HEADDIM
    return (r == (2 + d) * SSD_HEADS + h).astype(BF16)


def _ssd_scan(p, s0):
    b, l, _ = p["xs"].shape
    c = SSD_CHUNK
    nc = l // c
    specs = []
    for rev in (False, True):
        idx = (lambda n: nc - 1 - n) if rev else (lambda n: n)
        tok = lambda w, idx=idx: pl.BlockSpec((1, c, w), lambda bi, n: (bi, idx(n), 0))
        chan = lambda r, idx=idx: pl.BlockSpec((1, r, c), lambda bi, n: (bi, 0, idx(n)))
        specs.append([tok(SSD_INNER), tok(2 * SSD_BC), chan(SSD_BC), tok(LANES), chan(4 * SSD_HEADS)])
    st = pl.BlockSpec((1, 2 * SSD_GROUPS, SSD_STATE, HEADS_PER_GROUP * SSD_HEADDIM), lambda bi, n: (bi, 0, 0, 0))
    ex = _ssd_expander()
    ya, yb, sfin = pl.pallas_call(
        _ssd_kernel,
        grid=(b, nc),
        in_specs=specs[0] + specs[1][:4] + [_const_spec(ex.shape), st],
        out_specs=[specs[0][0], specs[1][0], st],
        out_shape=[jax.ShapeDtypeStruct((b, l, SSD_INNER), BF16), jax.ShapeDtypeStruct((b, l, SSD_INNER), BF16),
                   jax.ShapeDtypeStruct(s0.shape, F32)],
        scratch_shapes=[pltpu.VMEM((SSD_STATE, HEADS_PER_GROUP * SSD_HEADDIM), F32)] * (2 * SSD_GROUPS),
        compiler_params=_cparams(("parallel", "arbitrary")),
        name="ssd_scan",
    )(p["xs"], p["bc"], p["bT"], p["sd"], p["sdT"], p["xs"], p["bc"], p["bT"], p["sd"], ex, s0)
    return ya, yb, sfin


def _post0_kernel(of_ref, ob_ref, g_ref, yf_ref, yb_ref, xs_ref, z_ref, x_ref, mod_ref,
                  gn_ref, dsk_ref, sn_ref, wo_ref, out_ref):
    o = of_ref[0].astype(F32) + ob_ref[0].astype(F32)
    parts = []
    for hd in range(GLA_HEADS):
        oh = o[:, hd * GLA_DV:(hd + 1) * GLA_DV]
        parts.append(oh * lax.rsqrt(jnp.mean(oh * oh, axis=-1, keepdims=True) + EPS))
    gla = jnp.concatenate(parts, axis=1) * gn_ref[...] * _silu(g_ref[0].astype(F32))
    xs = xs_ref[0].astype(F32)
    y = (yf_ref[0].astype(F32) + yb_ref[0].astype(F32) + xs * dsk_ref[...]) * _silu(z_ref[0].astype(F32))
    gw = SSD_INNER // SSD_GROUPS
    parts = []
    for gi in range(SSD_GROUPS):
        yg = y[:, gi * gw:(gi + 1) * gw]
        parts.append(yg * lax.rsqrt(jnp.mean(yg * yg, axis=-1, keepdims=True) + EPS))
    ssd = jnp.concatenate(parts, axis=1) * sn_ref[...]
    res = _dot(gla.astype(BF16), wo_ref[:GLA_V, :]) + _dot(ssd.astype(BF16), wo_ref[GLA_V:, :])
    out_ref[0] = x_ref[0] + mod_ref[0, 2:3, :] * res


def _post0(p, of, ob, yf, yb, x, mod, wp, tm):
    b, l, _ = x.shape
    tm = min(tm, l)
    bm = mod.shape[0]
    mod_idx = (lambda bi, i: (bi, 0, 0)) if bm > 1 else (lambda bi, i: (0, 0, 0))
    tok = lambda w: pl.BlockSpec((1, tm, w), lambda bi, i: (bi, i, 0))
    return pl.pallas_call(
        _post0_kernel,
        grid=(b, l // tm),
        in_specs=[tok(GLA_V), tok(GLA_V), tok(GLA_V), tok(SSD_INNER), tok(SSD_INNER), tok(SSD_INNER),
                  tok(SSD_INNER), tok(D_MODEL), pl.BlockSpec((1, N_MOD, D_MODEL), mod_idx),
                  _const_spec((1, GLA_V)), _const_spec((1, SSD_INNER)), _const_spec((1, SSD_INNER)),
                  _const_spec(wp["w_out"].shape)],
        out_specs=tok(D_MODEL),
        out_shape=jax.ShapeDtypeStruct((b, l, D_MODEL), F32),
        compiler_params=_cparams(("parallel", "arbitrary")),
        name="post0",
    )(of, ob, p["g"], yf, yb, p["xs"], p["z"], x, mod, wp["gla_norm"], wp["d_skip"], wp["ssd_norm"], wp["w_out"])


def _ffn_kernel(x_ref, mod_ref, nw_ref, w1_ref, w3_ref, w2_ref, fn_ref, out_ref, *, final):
    x = x_ref[0]
    h = _norm_mod(x, nw_ref[...], mod_ref[0, 3:4, :], mod_ref[0, 4:5, :]).astype(BF16)
    half = D_FF // 2
    y = jnp.zeros_like(x)
    for j in range(2):
        cs = slice(j * half, (j + 1) * half)
        u = (_silu(_dot(h, w1_ref[:, cs])) * _dot(h, w3_ref[:, cs])).astype(BF16)
        y = y + _dot(u, w2_ref[cs, :])
    out = x + mod_ref[0, 5:6, :] * y
    if final:
        out = out * lax.rsqrt(jnp.mean(out * out, axis=-1, keepdims=True) + EPS) * fn_ref[...]
    out_ref[0] = out


def _ffn(x, mod, nw, w1, w3, w2, final_norm, final, tm):
    b, l, _ = x.shape
    tm = min(tm, l)
    bm = mod.shape[0]
    mod_idx = (lambda bi, i: (bi, 0, 0)) if bm > 1 else (lambda bi, i: (0, 0, 0))
    tok = pl.BlockSpec((1, tm, D_MODEL), lambda bi, i: (bi, i, 0))
    return pl.pallas_call(
        functools.partial(_ffn_kernel, final=final),
        grid=(b, l // tm),
        in_specs=[tok, pl.BlockSpec((1, N_MOD, D_MODEL), mod_idx), _const_spec((1, D_MODEL)),
                  _const_spec(w1.shape), _const_spec(w3.shape), _const_spec(w2.shape), _const_spec((1, D_MODEL))],
        out_specs=tok,
        out_shape=jax.ShapeDtypeStruct((b, l, D_MODEL), F32),
        compiler_params=_cparams(("parallel", "arbitrary")),
        name="ffn_final" if final else "ffn",
    )(x, mod, nw, w1, w3, w2, final_norm)


def _inproj1_kernel(*refs, tm, ncol):
    if ncol:
        x_ref, xp_ref, xn_ref, perm_ref = refs[:4]
        refs = refs[4:]
    else:
        x_ref, xp_ref, xn_ref = refs[:3]
        refs = refs[3:]
    (mod_ref, nw_ref, wup_ref, cw_ref, cb_ref, bdq_ref, bdk_ref, bdkT_ref, bdv_ref, wifq_ref, wifk_ref, wifv_ref,
     bif_ref, q_ref, kT_ref, v_ref, xc_ref, z_ref, gc_ref, gT_ref, hext, xm_scr) = refs
    i = pl.program_id(1)
    first = i == 0
    last = i == pl.num_programs(1) - 1
    shift = mod_ref[0, 0:1, :]
    scale = mod_ref[0, 1:2, :]
    nw = nw_ref[...]
    if ncol:
        hext[0:HALO, :] = _norm_mod(xp_ref[0, :, SUBLANES - 1, :], nw, shift, scale).astype(BF16)
        hn = _norm_mod(x_ref[0].reshape(tm, D_MODEL), nw, shift, scale).astype(BF16)
        hext[HALO:HALO + tm, :] = _dot(perm_ref[...], hn).astype(BF16)
        hext[HALO + tm:2 * HALO + tm, :] = _norm_mod(xn_ref[0, :, 0, :], nw, shift, scale).astype(BF16)
    else:
        hext[0:HALO, :] = _norm_mod(xp_ref[0], nw, shift, scale).astype(BF16)
        hext[HALO:HALO + tm, :] = _norm_mod(x_ref[0], nw, shift, scale).astype(BF16)
        hext[HALO + tm:2 * HALO + tm, :] = _norm_mod(xn_ref[0], nw, shift, scale).astype(BF16)
    h = hext[HALO:HALO + tm, :]

    xm_scr[...] = _dot(hext[...], wup_ref[:, :MLSTM_INNER])
    xc = _silu(_conv3(xm_scr, tm, cw_ref, cb_ref, first, last))
    xc_ref[0] = xc.astype(BF16)

    gates = jnp.zeros((tm, LANES), F32) + bif_ref[...]
    for t in range(N_BD_TILES):
        cs = slice(t * BD_TILE, (t + 1) * BD_TILE)
        xct = xc[:, cs].astype(BF16)
        xmt = xm_scr[HALO:HALO + tm, cs].astype(BF16)
        qt = _dot(xct, bdq_ref[t])
        kt = _dot(xct, bdk_ref[t])
        vt = _dot(xmt, bdv_ref[t])
        q_ref[0, :, cs] = qt.astype(BF16)
        v_ref[0, :, cs] = vt.astype(BF16)
        kT_ref[0, cs, :] = (_dot_nt(bdkT_ref[t], xct) * (MLSTM_DH ** -0.5)).astype(BF16)
        gates = gates + _dot(qt.astype(BF16), wifq_ref[cs, :]) + _dot(kt.astype(BF16), wifk_ref[cs, :]) \
            + _dot(vt.astype(BF16), wifv_ref[cs, :])

    gt = jnp.transpose(gates)[:4 * MLSTM_HEADS]
    r = lax.broadcasted_iota(jnp.int32, gt.shape, 0)
    lf = _log_sigmoid(gt)
    cum = jnp.where(r < 2 * MLSTM_HEADS, _seg_cumsum(lf, MLSTM_CHUNK, 1, False),
                    _seg_cumsum(lf, MLSTM_CHUNK, 1, True))
    packed = jnp.where(r % (2 * MLSTM_HEADS) < MLSTM_HEADS, gt, cum)
    a = gt - pltpu.roll(cum, 4 * MLSTM_HEADS - MLSTM_HEADS, 0)
    amax = jnp.where(r < 2 * MLSTM_HEADS, _seg_scan(a, MLSTM_CHUNK, 1, False, use_max=True),
                     _seg_scan(a, MLSTM_CHUNK, 1, True, use_max=True))
    full = jnp.concatenate([packed, amax], axis=0)
    gT_ref[0] = full
    gc_ref[0] = jnp.transpose(jnp.concatenate([full, jnp.zeros((LANES - MLSTM_GATE_ROWS, tm), F32)], axis=0))

    z_ref[0] = _dot(h, wup_ref[:, MLSTM_INNER:]).astype(BF16)


def _inproj1(x, mod, nw, wp, tm, colmajor):
    b, l, _ = x.shape
    tm = min(tm, l)
    nt = l // tm
    bm = mod.shape[0]
    mod_idx = (lambda bi, i: (bi, 0, 0)) if bm > 1 else (lambda bi, i: (0, 0, 0))
    if colmajor:
        rows = l // GRID_W
        ncol = tm // rows
        assert ncol % SUBLANES == 0 and rows % HALO == 0
        xv = x.reshape(b, rows, GRID_W, D_MODEL)
        rb = rows // HALO
        cb = ncol // SUBLANES
        ncb = GRID_W // SUBLANES
        x_spec = pl.BlockSpec((1, rows, ncol, D_MODEL), lambda bi, i: (bi, 0, i, 0))
        xp_spec = pl.BlockSpec((1, HALO, SUBLANES, D_MODEL),
                               lambda bi, i: (bi, rb - 1, jnp.maximum(i * cb - 1, 0), 0))
        xn_spec = pl.BlockSpec((1, HALO, SUBLANES, D_MODEL),
                               lambda bi, i: (bi, 0, jnp.minimum((i + 1) * cb, ncb - 1), 0))
        dst = lax.broadcasted_iota(jnp.int32, (tm, tm), 0)
        src = lax.broadcasted_iota(jnp.int32, (tm, tm), 1)
        perm = (src == (dst % rows) * ncol + dst // rows).astype(BF16)
        lead_specs = [x_spec, xp_spec, xn_spec, _const_spec((tm, tm))]
        lead_args = [xv, xv, xv, perm]
    else:
        ncol = 0
        xv = x
        hb = tm // HALO
        nhb = l // HALO
        x_spec = pl.BlockSpec((1, tm, D_MODEL), lambda bi, i: (bi, i, 0))
        xp_spec = pl.BlockSpec((1, HALO, D_MODEL), lambda bi, i: (bi, jnp.maximum(i * hb - 1, 0), 0))
        xn_spec = pl.BlockSpec((1, HALO, D_MODEL), lambda bi, i: (bi, jnp.minimum((i + 1) * hb, nhb - 1), 0))
        lead_specs = [x_spec, xp_spec, xn_spec]
        lead_args = [xv, xv, xv]
    tok = lambda w: pl.BlockSpec((1, tm, w), lambda bi, i: (bi, i, 0))
    names = ["w_up", "conv_w", "conv_b", "bdq", "bdk", "bdkT", "bdv", "wif_q", "wif_k", "wif_v", "b_if"]
    q, kT, v, xc, z, gc, gT = pl.pallas_call(
        functools.partial(_inproj1_kernel, tm=tm, ncol=ncol),
        grid=(b, nt),
        in_specs=lead_specs + [pl.BlockSpec((1, N_MOD, D_MODEL), mod_idx), _const_spec((1, D_MODEL))]
        + [_const_spec(wp[n].shape) for n in names],
        out_specs=[tok(MLSTM_INNER), pl.BlockSpec((1, MLSTM_INNER, tm), lambda bi, i: (bi, 0, i)),
                   tok(MLSTM_INNER), tok(MLSTM_INNER), tok(MLSTM_INNER), tok(LANES),
                   pl.BlockSpec((1, MLSTM_GATE_ROWS, tm), lambda bi, i: (bi, 0, i))],
        out_shape=[jax.ShapeDtypeStruct((b, l, MLSTM_INNER), BF16), jax.ShapeDtypeStruct((b, MLSTM_INNER, l), BF16),
                   jax.ShapeDtypeStruct((b, l, MLSTM_INNER), BF16), jax.ShapeDtypeStruct((b, l, MLSTM_INNER), BF16),
                   jax.ShapeDtypeStruct((b, l, MLSTM_INNER), BF16), jax.ShapeDtypeStruct((b, l, LANES), F32),
                   jax.ShapeDtypeStruct((b, MLSTM_GATE_ROWS, l), F32)],
        scratch_shapes=[pltpu.VMEM((tm + 2 * HALO, D_MODEL), BF16),
                        pltpu.VMEM((tm + 2 * HALO, MLSTM_INNER), F32)],
        compiler_params=_cparams(("parallel", "arbitrary")),
        name="inproj1",
    )(*lead_args, mod, nw, *[wp[n] for n in names])
    return dict(q=q, kT=kT, v=v, xc=xc, z=z, gc=gc, gT=gT)


def _mlstm_head(q, kT, v, gc, gT, c_scr, cb_scr, n_scr, m_scr, d, hd, reverse):
    c = MLSTM_CHUNK
    dh = MLSTM_DH
    idx = d * MLSTM_HEADS + hd
    ri = d * 2 * MLSTM_HEADS + hd
    rb = ri + MLSTM_HEADS
    rm = 4 * MLSTM_HEADS + ri
    c_scr, cb_scr, n_scr, m_scr = c_scr[idx], cb_scr[idx], n_scr[idx], m_scr[idx]
    bi = gc[:, rb:rb + 1]
    am_i = gc[:, rm:rm + 1]
    bj = gT[rb:rb + 1, :]
    a_j = gT[ri:ri + 1, :] - bj
    am_j = gT[rm:rm + 1, :]
    m0 = m_scr[0:1, 0:1]
    big_m = jnp.maximum(m0, am_i)
    ii = lax.broadcasted_iota(jnp.int32, (c, c), 0)
    jj = lax.broadcasted_iota(jnp.int32, (c, c), 1)
    causal = (jj >= ii) if reverse else (jj <= ii)
    s = _dot(q, kT) * jnp.where(causal, jnp.exp(a_j - big_m), 0.0)
    w_inter = jnp.exp(m0 - big_m)
    qf = q.astype(F32)
    den = jnp.sum(s, axis=-1, keepdims=True) \
        + w_inter * jnp.sum(qf * n_scr[0:1, :], axis=-1, keepdims=True)
    num = _dot(s.astype(BF16), v) + _dot((qf * w_inter).astype(BF16), cb_scr[...])
    h = num * (1.0 / jnp.maximum(jnp.abs(den), jnp.exp(-(bi + big_m))))

    last = 0 if reverse else c - 1
    m_inner = jnp.maximum(m0, am_j[:, last:last + 1])
    kw = kT.astype(F32) * jnp.exp(a_j - m_inner)
    decay = jnp.exp(m0 - m_inner)
    kwb = kw.astype(BF16)
    c_new = decay * c_scr[...] + _dot(kwb, v)
    c_scr[...] = c_new
    cb_scr[...] = c_new.astype(BF16)
    n_scr[...] = decay * n_scr[...] + _dot_nt(jnp.ones((SUBLANES, c), BF16), kwb)
    m_scr[...] = jnp.broadcast_to(bj[:, last:last + 1] + m_inner, m_scr.shape)
    return h


def _mlstm_kernel(*refs, has_init, emit_final):
    ins = refs[:10]
    pos = 10
    if has_init:
        c0_ref, n0_ref, m0_ref = refs[pos:pos + 3]
        pos += 3
    hf_ref, hb_ref = refs[pos:pos + 2]
    pos += 2
    if emit_final:
        cfin_ref, nfin_ref, mfin_ref = refs[pos:pos + 3]
        pos += 3
    nst = 2 * MLSTM_HEADS
    c_scr, cb_scr, n_scr, m_scr = (refs[pos + k * nst:pos + (k + 1) * nst] for k in range(4))
    n = pl.program_id(1)

    @pl.when(n == 0)
    def _():
        for k in range(nst):
            if has_init:
                c_scr[k][...] = c0_ref[0, k]
                cb_scr[k][...] = c0_ref[0, k].astype(BF16)
                n_scr[k][...] = n0_ref[0, k]
                m_scr[k][...] = m0_ref[0, k]
            else:
                c_scr[k][...] = jnp.zeros_like(c_scr[k])
                cb_scr[k][...] = jnp.zeros_like(cb_scr[k])
                n_scr[k][...] = jnp.zeros_like(n_scr[k])
                m_scr[k][...] = jnp.full(m_scr[k].shape, M_INIT, F32)

    for d, h_ref in enumerate((hf_ref, hb_ref)):
        q_ref, kT_ref, v_ref, gc_ref, gT_ref = ins[5 * d:5 * d + 5]
        gc = gc_ref[0]
        gT = gT_ref[0]
        for hd in range(MLSTM_HEADS):
            cs = slice(hd * MLSTM_DH, (hd + 1) * MLSTM_DH)
            h = _mlstm_head(q_ref[0, :, cs], kT_ref[0, cs, :], v_ref[0, :, cs], gc, gT,
                            c_scr, cb_scr, n_scr, m_scr, d, hd, d == 1)
            h_ref[0, :, cs] = h.astype(BF16)

    if emit_final:
        @pl.when(n == pl.num_programs(1) - 1)
        def _():
            for k in range(nst):
                cfin_ref[0, k] = c_scr[k][...]
                nfin_ref[0, k] = n_scr[k][...]
                mfin_ref[0, k] = m_scr[k][...]


def _mlstm_scan(p, init, emit_final):
    b, l, _ = p["q"].shape
    c = MLSTM_CHUNK
    nc = l // c
    dh = MLSTM_DH
    nst = 2 * MLSTM_HEADS
    specs = []
    for rev in (False, True):
        idx = (lambda n: nc - 1 - n) if rev else (lambda n: n)
        specs.append([
            pl.BlockSpec((1, c, MLSTM_INNER), lambda bi, n, idx=idx: (bi, idx(n), 0)),
            pl.BlockSpec((1, MLSTM_INNER, c), lambda bi, n, idx=idx: (bi, 0, idx(n))),
            pl.BlockSpec((1, c, MLSTM_INNER), lambda bi, n, idx=idx: (bi, idx(n), 0)),
            pl.BlockSpec((1, c, LANES), lambda bi, n, idx=idx: (bi, idx(n), 0)),
            pl.BlockSpec((1, MLSTM_GATE_ROWS, c), lambda bi, n, idx=idx: (bi, 0, idx(n))),
        ])
    st_shapes = [(b, nst, dh, dh), (b, nst, SUBLANES, dh), (b, nst, SUBLANES, LANES)]
    st_index = lambda bi, n: (bi, 0, 0, 0)
    args = (p["q"], p["kT"], p["v"], p["gc"], p["gT"])
    operands = [*args, *args]
    in_specs = specs[0] + specs[1]
    if init is not None:
        operands += list(init)
        in_specs += [pl.BlockSpec((1,) + s[1:], st_index, pipeline_mode=pl.Buffered(1)) for s in st_shapes]
    out_specs = [specs[0][0], specs[1][0]]
    out_shape = [jax.ShapeDtypeStruct((b, l, MLSTM_INNER), BF16), jax.ShapeDtypeStruct((b, l, MLSTM_INNER), BF16)]
    if emit_final:
        out_specs += [pl.BlockSpec((1,) + s[1:], st_index) for s in st_shapes]
        out_shape += [jax.ShapeDtypeStruct(s, F32) for s in st_shapes]
    res = pl.pallas_call(
        functools.partial(_mlstm_kernel, has_init=init is not None, emit_final=emit_final),
        grid=(b, nc),
        in_specs=in_specs,
        out_specs=out_specs,
        out_shape=out_shape,
        scratch_shapes=[pltpu.VMEM((dh, dh), F32)] * nst + [pltpu.VMEM((dh, dh), BF16)] * nst
        + [pltpu.VMEM((SUBLANES, dh), F32)] * nst + [pltpu.VMEM((SUBLANES, LANES), F32)] * nst,
        compiler_params=_cparams(("parallel", "arbitrary")),
        name="mlstm_scan",
    )(*operands)
    return res[0], res[1], tuple(res[2:])


def _post1_kernel(hf_ref, hb_ref, xc_ref, z_ref, x_ref, mod_ref, mh_ref, sk_ref, wd_ref, out_ref, *, tm, ncol):
    hs = hf_ref[0].astype(F32) + hb_ref[0].astype(F32)
    parts = []
    for hd in range(MLSTM_HEADS):
        hh = hs[:, hd * MLSTM_DH:(hd + 1) * MLSTM_DH]
        mu = jnp.mean(hh, axis=-1, keepdims=True)
        cen = hh - mu
        parts.append(cen * lax.rsqrt(jnp.mean(cen * cen, axis=-1, keepdims=True) + EPS))
    feat = (jnp.concatenate(parts, axis=1) * mh_ref[...] + sk_ref[...] * xc_ref[0].astype(F32)) \
        * _silu(z_ref[0].astype(F32))
    res = mod_ref[0, 2:3, :] * _dot(feat.astype(BF16), wd_ref[...])
    if ncol:
        rows = tm // ncol
        for j in range(ncol):
            out_ref[0, :, j, :] = x_ref[0, :, j, :] + res[j * rows:(j + 1) * rows, :]
    else:
        out_ref[0] = x_ref[0] + res


def _post1(p, hf, hb, x, mod, wp, tm, colmajor):
    b, l, _ = x.shape
    tm = min(tm, l)
    bm = mod.shape[0]
    mod_idx = (lambda bi, i: (bi, 0, 0)) if bm > 1 else (lambda bi, i: (0, 0, 0))
    tok = lambda w: pl.BlockSpec((1, tm, w), lambda bi, i: (bi, i, 0))
    if colmajor:
        rows = l // GRID_W
        ncol = tm // rows
        xv = x.reshape(b, rows, GRID_W, D_MODEL)
        x_spec = pl.BlockSpec((1, rows, ncol, D_MODEL), lambda bi, i: (bi, 0, i, 0))
    else:
        ncol = 0
        xv = x
        x_spec = tok(D_MODEL)
    out = pl.pallas_call(
        functools.partial(_post1_kernel, tm=tm, ncol=ncol),
        grid=(b, l // tm),
        in_specs=[tok(MLSTM_INNER), tok(MLSTM_INNER), tok(MLSTM_INNER), tok(MLSTM_INNER), x_spec,
                  pl.BlockSpec((1, N_MOD, D_MODEL), mod_idx),
                  _const_spec((1, MLSTM_INNER)), _const_spec((1, MLSTM_INNER)), _const_spec(wp["w_down"].shape)],
        out_specs=x_spec,
        out_shape=jax.ShapeDtypeStruct(xv.shape, F32),
        compiler_params=_cparams(("parallel", "arbitrary")),
        name="post1",
    )(hf, hb, p["xc"], p["z"], xv, mod, wp["mh_norm"], wp["skip"], wp["w_down"])
    return out.reshape(b, l, D_MODEL)


def _prep_even(w_in, w_gate, b_gate, gla_norm, conv_w, conv_b, a_log, dt_bias, d_skip, ssd_norm, w_out):
    sizes = (GLA_QK, GLA_QK, GLA_V, GLA_V, 2 * GLA_GATE_RANK, SSD_INNER, SSD_XBC, 2 * SSD_HEADS)
    offs = [0]
    for s in sizes:
        offs.append(offs[-1] + s)
    cols = lambda i: w_in[:, offs[i]:offs[i + 1]]
    w_main = jnp.concatenate([cols(0), cols(1), cols(2), cols(3), cols(5), cols(6)], axis=1).astype(BF16)
    pad = jnp.zeros((D_MODEL, LANES - 2 * GLA_GATE_RANK - 2 * SSD_HEADS), w_in.dtype)
    w_small = jnp.concatenate([cols(4), cols(7), pad], axis=1).astype(BF16)
    zero = jnp.zeros((GLA_GATE_RANK, GLA_QK), w_gate.dtype)
    wg = jnp.concatenate([jnp.concatenate([w_gate[0], zero], axis=1),
                          jnp.concatenate([zero, w_gate[1]], axis=1)], axis=0).astype(BF16)
    return dict(
        w_main=w_main, w_small=w_small, w_dtT=jnp.transpose(cols(7)).astype(BF16),
        w_gate=wg, b_gate=b_gate.reshape(1, 2 * GLA_QK),
        conv_w=conv_w, conv_b=conv_b.reshape(1, SSD_XBC),
        dt_bias=dt_bias.reshape(1, 2 * SSD_HEADS), a_log=a_log.reshape(1, 2 * SSD_HEADS),
        dt_biasT=dt_bias.reshape(2 * SSD_HEADS, 1), a_logT=a_log.reshape(2 * SSD_HEADS, 1),
        gla_norm=jnp.tile(gla_norm, GLA_HEADS).reshape(1, GLA_V),
        d_skip=jnp.repeat(d_skip, SSD_HEADDIM).reshape(1, SSD_INNER),
        ssd_norm=ssd_norm.reshape(1, SSD_INNER),
        w_out=w_out.astype(BF16),
    )


def _block_diag_tiles(w):
    per = BD_TILE // QKV_BLOCK
    w4 = w.reshape(N_BD_TILES, per, QKV_BLOCK, QKV_BLOCK)
    eye = jnp.eye(per, dtype=w.dtype)
    return jnp.einsum("tade,ab->tadbe", w4, eye).reshape(N_BD_TILES, BD_TILE, BD_TILE)


def _prep_odd(w_up, conv_w, conv_b, wq, wk, wv, w_if, b_if, mh_norm, skip, w_down):
    ng = 4 * MLSTM_HEADS
    wif = jnp.transpose(w_if, (1, 0, 2)).reshape(3 * MLSTM_INNER, ng)
    wif = jnp.concatenate([wif, jnp.zeros((3 * MLSTM_INNER, LANES - ng), w_if.dtype)], axis=1).astype(BF16)
    bif = jnp.concatenate([b_if.reshape(1, ng), jnp.zeros((1, LANES - ng), b_if.dtype)], axis=1)
    bdk = _block_diag_tiles(wk)
    return dict(
        w_up=w_up.astype(BF16), conv_w=conv_w, conv_b=conv_b.reshape(1, MLSTM_INNER),
        bdq=_block_diag_tiles(wq).astype(BF16), bdk=bdk.astype(BF16),
        bdkT=jnp.transpose(bdk, (0, 2, 1)).astype(BF16), bdv=_block_diag_tiles(wv).astype(BF16),
        wif_q=wif[:MLSTM_INNER], wif_k=wif[MLSTM_INNER:2 * MLSTM_INNER], wif_v=wif[2 * MLSTM_INNER:], b_if=bif,
        mh_norm=mh_norm.reshape(1, MLSTM_INNER), skip=skip.reshape(1, MLSTM_INNER),
        w_down=w_down.astype(BF16),
    )


TOKEN_TILE = 512


def kernel(x, c, ctx, c_ctx, w_mod, b_mod, norm_mix, norm_ffn, ffn_w1, ffn_w3, ffn_w2, a_w_in, a_gla_w_gate,
           a_gla_b_gate, a_gla_norm, a_ssd_conv_w, a_ssd_conv_b, a_ssd_A_log, a_ssd_dt_bias, a_ssd_D, a_ssd_norm,
           a_w_out, b_w_up, b_conv_w, b_conv_b, b_wq, b_wk, b_wv, b_w_if, b_b_if, b_mh_norm, b_skip, b_w_down,
           final_norm):
    tile = TOKEN_TILE
    bsz = x.shape[0]
    depth = w_mod.shape[0]
    pad_rows = (-(bsz + 1)) % 8
    cc = jnp.concatenate([c, c_ctx[None, :], jnp.zeros((pad_rows, D_MODEL), c.dtype)], axis=0)
    mods = _modulation(cc, w_mod, b_mod).reshape(depth, bsz + 1 + pad_rows, N_MOD, D_MODEL)
    fnorm = final_norm.reshape(1, D_MODEL)
    for layer in range(depth):
        last = layer == depth - 1
        j = layer // 2
        mx = mods[layer, :bsz]
        mc = mods[layer, bsz:bsz + 1]
        nmix = norm_mix[layer].reshape(1, D_MODEL)
        nffn = norm_ffn[layer].reshape(1, D_MODEL)
        w1 = ffn_w1[layer].astype(BF16)
        w3 = ffn_w3[layer].astype(BF16)
        w2 = ffn_w2[layer].astype(BF16)
        if layer % 2 == 0:
            wp = _prep_even(a_w_in[j], a_gla_w_gate[j], a_gla_b_gate[j], a_gla_norm[j], a_ssd_conv_w[j],
                            a_ssd_conv_b[j], a_ssd_A_log[j], a_ssd_dt_bias[j], a_ssd_D[j], a_ssd_norm[j],
                            a_w_out[j])
            pc = _inproj0(ctx, mc, nmix, wp, tile)
            gla0 = jnp.zeros((bsz, 2 * GLA_HEADS, GLA_DK, GLA_DV), F32)
            ssd0 = jnp.zeros((bsz, 2 * SSD_GROUPS, SSD_STATE, HEADS_PER_GROUP * SSD_HEADDIM), F32)
            ofc, obc, gla_st = _gla_scan(pc, gla0)
            yfc, ybc, ssd_st = _ssd_scan(pc, ssd0)
            px = _inproj0(x, mx, nmix, wp, tile)
            ofx, obx, _ = _gla_scan(px, gla_st)
            yfx, ybx, _ = _ssd_scan(px, ssd_st)
            x = _post0(px, ofx, obx, yfx, ybx, x, mx, wp, tile)
            if not last:
                ctx = _post0(pc, ofc, obc, yfc, ybc, ctx, mc, wp, tile)
        else:
            wp = _prep_odd(b_w_up[j], b_conv_w[j], b_conv_b[j], b_wq[j], b_wk[j], b_wv[j], b_w_if[j], b_b_if[j],
                           b_mh_norm[j], b_skip[j], b_w_down[j])
            pc = _inproj1(ctx, mc, nmix, wp, tile, colmajor=False)
            hfc, hbc, state = _mlstm_scan(pc, None, True)
            px = _inproj1(x, mx, nmix, wp, tile, colmajor=True)
            hfx, hbx, _ = _mlstm_scan(px, state, False)
            x = _post1(px, hfx, hbx, x, mx, wp, tile, colmajor=True)
            if not last:
                ctx = _post1(pc, hfc, hbc, ctx, mc, wp, tile, colmajor=False)
        x = _ffn(x, mx, nffn, w1, w3, w2, fnorm, last, tile)
        if not last:
            ctx = _ffn(ctx, mc, nffn, w1, w3, w2, fnorm, False, tile)
    return x
```

```python
import functools

import jax
import jax.numpy as jnp
from jax import lax
from jax.experimental import pallas as pl
from jax.experimental.pallas import tpu as pltpu

F32 = jnp.float32
BF16 = jnp.bfloat16

D_MODEL = 1024
GRID_W = 64
EPS = 1e-6
LOG2_E = 1.4426950408889634
M_INIT = -1e30
N_MOD = 6

GLA_HEADS = 4
GLA_DK = 128
GLA_DV = 256
GLA_QK = GLA_HEADS * GLA_DK
GLA_V = GLA_HEADS * GLA_DV
GLA_GATE_RANK = 16
GLA_GATE_TAU = 16.0

SSD_HEADDIM = 64
SSD_HEADS = 16
SSD_STATE = 128
SSD_GROUPS = 2
SSD_INNER = SSD_HEADS * SSD_HEADDIM
SSD_BC = SSD_GROUPS * SSD_STATE
SSD_XBC = SSD_INNER + 2 * SSD_BC
HEADS_PER_GROUP = SSD_HEADS // SSD_GROUPS

MLSTM_INNER = 2 * D_MODEL
MLSTM_HEADS = 4
MLSTM_DH = MLSTM_INNER // MLSTM_HEADS
QKV_BLOCK = 4
BD_TILE = 256
N_BD_TILES = MLSTM_INNER // BD_TILE
MLSTM_GATE_ROWS = 32

D_FF = 2816

GLA_CHUNK = 64
GLA_CHUNKS_PER_STEP = 4
GLA_INTRA_TILE = 256
SSD_CHUNK = 128
MLSTM_CHUNK = 256
HALO = 16
LANES = 128
SUBLANES = 8

VMEM_LIMIT = 56 * 1024 * 1024


def _cparams(sem):
    return pltpu.CompilerParams(dimension_semantics=sem, vmem_limit_bytes=VMEM_LIMIT)


def _const_spec(shape):
    nd = len(shape)
    return pl.BlockSpec(shape, lambda *_: (0,) * nd, pipeline_mode=pl.Buffered(1))


def _softplus(x):
    return jnp.maximum(x, 0.0) + jnp.log(1.0 + jnp.exp(-jnp.abs(x)))


def _log_sigmoid(x):
    return -_softplus(-x)


def _silu(x):
    return x / (1.0 + jnp.exp(-x))


def _seg_scan(x, seg, axis, reverse, use_max=False):
    n = x.shape[axis]
    idx = lax.broadcasted_iota(jnp.int32, x.shape, axis) % seg
    s = 1
    while s < seg:
        if reverse:
            shifted = pltpu.roll(x, n - s, axis)
            keep = idx < seg - s
        else:
            shifted = pltpu.roll(x, s, axis)
            keep = idx >= s
        if use_max:
            x = jnp.where(keep, jnp.maximum(x, shifted), x)
        else:
            x = x + jnp.where(keep, shifted, 0.0)
        s *= 2
    return x


def _seg_cumsum(x, seg, axis, reverse):
    return _seg_scan(x, seg, axis, reverse)


def _norm_mod(xv, nw, shift, scale):
    ms = jnp.mean(xv * xv, axis=-1, keepdims=True)
    return (xv * lax.rsqrt(ms + EPS) * nw) * (1.0 + scale) + shift


def _dot(a, b):
    return jnp.dot(a, b, preferred_element_type=F32)


def _dot_nt(a, b):
    return lax.dot_general(a, b, (((1,), (1,)), ((), ())), preferred_element_type=F32)


def _dot_tn(a, b):
    return lax.dot_general(a, b, (((0,), (0,)), ((), ())), preferred_element_type=F32)


def _mod_kernel(c_ref, w_ref, b_ref, o_ref):
    a = _silu(c_ref[...]).astype(BF16)
    o_ref[0] = _dot(a, w_ref[0].astype(BF16)) + b_ref[0]


def _modulation(cc, w_mod, b_mod):
    depth = w_mod.shape[0]
    n = w_mod.shape[2]
    tn = n // 4
    rows = cc.shape[0]
    return pl.pallas_call(
        _mod_kernel,
        grid=(depth, n // tn),
        in_specs=[
            pl.BlockSpec((rows, D_MODEL), lambda l, j: (0, 0)),
            pl.BlockSpec((1, D_MODEL, tn), lambda l, j: (l, 0, j)),
            pl.BlockSpec((1, 1, tn), lambda l, j: (l, 0, j)),
        ],
        out_specs=pl.BlockSpec((1, rows, tn), lambda l, j: (l, 0, j)),
        out_shape=jax.ShapeDtypeStruct((depth, rows, n), F32),
        compiler_params=_cparams(("arbitrary", "arbitrary")),
        name="modulation",
    )(cc, w_mod, b_mod.reshape(depth, 1, n))


def _conv3(scr, tm, cw_ref, cb_ref, first, last):
    scr[HALO - 1:HALO, :] = scr[HALO - 1:HALO, :] * jnp.where(first, 0.0, 1.0)
    scr[HALO + tm:HALO + tm + 1, :] = scr[HALO + tm:HALO + tm + 1, :] * jnp.where(last, 0.0, 1.0)

    prev = scr[HALO - 1:HALO - 1 + tm, :]
    cur = scr[HALO:HALO + tm, :]
    nxt = scr[HALO + 1:HALO + 1 + tm, :]
    return cw_ref[0:1, :] * prev + cw_ref[1:2, :] * cur + cw_ref[2:3, :] * nxt + cb_ref[...]


def _inproj0_kernel(x_ref, xp_ref, xn_ref, mod_ref, nw_ref, wmain_ref, wsmall_ref, wdtT_ref, wg_ref, bg_ref,
                    cw_ref, cb_ref, dtb_ref, alog_ref, dtbT_ref, alogT_ref,
                    q_ref, k_ref, v_ref, g_ref, z_ref, gcum_ref, xs_ref, bc_ref, bT_ref, sd_ref, sdT_ref,
                    hext, xbc_scr, *, tm):
    i = pl.program_id(1)
    first = i == 0
    last = i == pl.num_programs(1) - 1
    shift = mod_ref[0, 0:1, :]
    scale = mod_ref[0, 1:2, :]
    nw = nw_ref[...]
    hext[0:HALO, :] = _norm_mod(xp_ref[0], nw, shift, scale).astype(BF16)
    hext[HALO:HALO + tm, :] = _norm_mod(x_ref[0], nw, shift, scale).astype(BF16)
    hext[HALO + tm:2 * HALO + tm, :] = _norm_mod(xn_ref[0], nw, shift, scale).astype(BF16)

    o_k = GLA_QK
    o_v = o_k + GLA_QK
    o_g = o_v + GLA_V
    o_z = o_g + GLA_V
    o_x = o_z + SSD_INNER
    nd = 2 * SSD_HEADS
    h = hext[HALO:HALO + tm, :]

    q_ref[0] = (_dot(h, wmain_ref[:, 0:o_k]) * (GLA_DK ** -0.5)).astype(BF16)
    k_ref[0] = _dot(h, wmain_ref[:, o_k:o_v]).astype(BF16)
    v_ref[0] = _dot(h, wmain_ref[:, o_v:o_g]).astype(BF16)
    g_ref[0] = _dot(h, wmain_ref[:, o_g:o_z]).astype(BF16)
    z_ref[0] = _dot(h, wmain_ref[:, o_z:o_x]).astype(BF16)

    xbc_scr[...] = _dot(hext[...], wmain_ref[:, o_x:o_x + SSD_XBC])
    y = _silu(_conv3(xbc_scr, tm, cw_ref, cb_ref, first, last))
    xs_ref[0] = y[:, :SSD_INNER].astype(BF16)
    bc_ref[0] = y[:, SSD_INNER:].astype(BF16)
    bT_ref[0] = jnp.transpose(y[:, SSD_INNER:SSD_INNER + SSD_BC]).astype(BF16)

    small = _dot(h, wsmall_ref[...])
    r = small[:, :2 * GLA_GATE_RANK].astype(BF16)
    la = _log_sigmoid(_dot(r, wg_ref[...]) + bg_ref[...]) * (LOG2_E / GLA_GATE_TAU)
    gcum_ref[0, :, :GLA_QK] = _seg_cumsum(la[:, :GLA_QK], GLA_CHUNK, 0, False)
    gcum_ref[0, :, GLA_QK:] = _seg_cumsum(la[:, GLA_QK:], GLA_CHUNK, 0, True)

    dtT = _softplus(_dot_nt(wdtT_ref[...], h) + dtbT_ref[...])
    laT = dtT * (-jnp.exp(alogT_ref[...]))
    cumT = jnp.concatenate([_seg_cumsum(laT[:SSD_HEADS], SSD_CHUNK, 1, False),
                            _seg_cumsum(laT[SSD_HEADS:], SSD_CHUNK, 1, True)], axis=0)
    packed = jnp.concatenate([dtT, cumT, jnp.zeros((LANES - 2 * nd, tm), F32)], axis=0)
    sdT_ref[0] = packed[:2 * nd]
    sd_ref[0] = jnp.transpose(packed)


def _inproj0(x, mod, nw, wp, tm):
    b, l, _ = x.shape
    tm = min(tm, l)
    nt = l // tm
    hb = tm // HALO
    nhb = l // HALO
    bm = mod.shape[0]
    mod_idx = (lambda bi, i: (bi, 0, 0)) if bm > 1 else (lambda bi, i: (0, 0, 0))
    tok = lambda w: pl.BlockSpec((1, tm, w), lambda bi, i: (bi, i, 0))
    in_specs = [
        tok(D_MODEL),
        pl.BlockSpec((1, HALO, D_MODEL), lambda bi, i: (bi, jnp.maximum(i * hb - 1, 0), 0)),
        pl.BlockSpec((1, HALO, D_MODEL), lambda bi, i: (bi, jnp.minimum((i + 1) * hb, nhb - 1), 0)),
        pl.BlockSpec((1, N_MOD, D_MODEL), mod_idx),
        _const_spec((1, D_MODEL)),
        _const_spec(wp["w_main"].shape), _const_spec(wp["w_small"].shape), _const_spec(wp["w_dtT"].shape),
        _const_spec(wp["w_gate"].shape), _const_spec(wp["b_gate"].shape),
        _const_spec(wp["conv_w"].shape), _const_spec(wp["conv_b"].shape),
        _const_spec(wp["dt_bias"].shape), _const_spec(wp["a_log"].shape),
        _const_spec(wp["dt_biasT"].shape), _const_spec(wp["a_logT"].shape),
    ]
    outs = [
        ("q", GLA_QK, BF16), ("k", GLA_QK, BF16), ("v", GLA_V, BF16), ("g", GLA_V, BF16),
        ("z", SSD_INNER, BF16), ("gcum", 2 * GLA_QK, F32), ("xs", SSD_INNER, BF16),
        ("bc", 2 * SSD_BC, BF16),
    ]
    out_specs = [tok(w) for _, w, _ in outs]
    out_shape = [jax.ShapeDtypeStruct((b, l, w), dt) for _, w, dt in outs]
    out_specs += [pl.BlockSpec((1, SSD_BC, tm), lambda bi, i: (bi, 0, i)),
                  tok(LANES),
                  pl.BlockSpec((1, 4 * SSD_HEADS, tm), lambda bi, i: (bi, 0, i))]
    out_shape += [jax.ShapeDtypeStruct((b, SSD_BC, l), BF16),
                  jax.ShapeDtypeStruct((b, l, LANES), F32),
                  jax.ShapeDtypeStruct((b, 4 * SSD_HEADS, l), F32)]
    res = pl.pallas_call(
        functools.partial(_inproj0_kernel, tm=tm),
        grid=(b, nt),
        in_specs=in_specs,
        out_specs=out_specs,
        out_shape=out_shape,
        scratch_shapes=[pltpu.VMEM((tm + 2 * HALO, D_MODEL), BF16),
                        pltpu.VMEM((tm + 2 * HALO, SSD_XBC), F32)],
        compiler_params=_cparams(("parallel", "arbitrary")),
        name="inproj0",
    )(x, x, x, mod, nw, wp["w_main"], wp["w_small"], wp["w_dtT"], wp["w_gate"], wp["b_gate"],
      wp["conv_w"], wp["conv_b"], wp["dt_bias"], wp["a_log"], wp["dt_biasT"], wp["a_logT"])
    names = [n for n, _, _ in outs] + ["bT", "sd", "sdT"]
    return dict(zip(names, res))


def _level_ref(g, hsz, reverse):
    c, w = g.shape
    grp = 2 * hsz
    r = hsz if reverse else hsz - 1
    if grp >= 8:
        g3 = g.reshape(c // grp, grp, w)
        return jnp.broadcast_to(g3[:, r:r + 1, :], g3.shape).reshape(c, w)
    g3 = g.reshape(c // 8, 8, w)
    sub = lax.broadcasted_iota(jnp.int32, g3.shape, 1)
    out = None
    for t in range(8 // grp):
        cand = jnp.broadcast_to(g3[:, t * grp + r:t * grp + r + 1, :], g3.shape)
        out = cand if out is None else jnp.where(sub >= t * grp, cand, out)
    return out.reshape(c, w)


def _gla_masks(c, w, reverse):
    row = lax.broadcasted_iota(jnp.int32, (c, w), 0)
    ii = lax.broadcasted_iota(jnp.int32, (c, c), 0)
    jj = lax.broadcasted_iota(jnp.int32, (c, c), 1)
    par = 0 if reverse else 1
    levels = []
    hsz = 1
    while hsz < c:
        is_q = (row // hsz) % 2 == par
        keep = ((ii // (2 * hsz)) == (jj // (2 * hsz))) & ((ii // hsz) % 2 == par) & ((jj // hsz) % 2 != par)
        levels.append((hsz, is_q, keep.astype(F32)))
        hsz *= 2
    return (ii == jj).astype(F32), levels


def _gla_level_operand(q, k, g, hsz, reverse, is_q):
    c = g.shape[0]
    par = 0 if reverse else 1
    if hsz >= SUBLANES:
        parts = []
        for grp in range(c // (2 * hsz)):
            base = grp * 2 * hsz
            ref_row = base + (hsz if reverse else hsz - 1)
            ref = g[ref_row:ref_row + 1, :]
            for half in range(2):
                rows = slice(base + half * hsz, base + (half + 1) * hsz)
                if half == par:
                    parts.append(q[rows] * jnp.exp2(g[rows] - ref))
                else:
                    parts.append(k[rows] * jnp.exp2(ref - g[rows]))
        return jnp.concatenate(parts, axis=0).astype(BF16)
    if hsz == 1:
        step = g - pltpu.roll(g, (c - 1) if reverse else 1, 0)
        return jnp.where(is_q, q * jnp.exp2(step), k).astype(BF16)
    e = jnp.exp2(-jnp.abs(g - _level_ref(g, hsz, reverse)))
    return (jnp.where(is_q, q, k) * e).astype(BF16)


def _gla_intra_kernel(q_ref, k_ref, g_ref, a_ref):
    c = GLA_CHUNK
    masks = [_gla_masks(c, GLA_DK, d == 1) for d in range(2)]
    for sc in range(q_ref.shape[1] // c):
        rs = slice(sc * c, (sc + 1) * c)
        for hd in range(GLA_HEADS):
            ks = slice(hd * GLA_DK, (hd + 1) * GLA_DK)
            qb = q_ref[0, rs, ks]
            kb = k_ref[0, rs, ks]
            q = qb.astype(F32)
            k = kb.astype(F32)
            a = (2.0 * masks[0][0]) * _dot_nt(qb, kb)
            for d in range(2):
                g = g_ref[0, rs, d * GLA_QK + hd * GLA_DK:d * GLA_QK + (hd + 1) * GLA_DK]
                for hsz, is_q, keep in masks[d][1]:
                    mixed = _gla_level_operand(q, k, g, hsz, d == 1, is_q)
                    a = a + keep * _dot_nt(mixed, mixed)
            a_ref[0, hd, rs, :] = a.astype(BF16)


def _gla_intra(p, tm):
    b, l, _ = p["q"].shape
    tm = min(tm, l)
    tok = lambda w: pl.BlockSpec((1, tm, w), lambda bi, i: (bi, i, 0))
    return pl.pallas_call(
        _gla_intra_kernel,
        grid=(b, l // tm),
        in_specs=[tok(GLA_QK), tok(GLA_QK), tok(2 * GLA_QK)],
        out_specs=pl.BlockSpec((1, GLA_HEADS, tm, GLA_CHUNK), lambda bi, i: (bi, 0, i, 0)),
        out_shape=jax.ShapeDtypeStruct((b, GLA_HEADS, l, GLA_CHUNK), BF16),
        compiler_params=_cparams(("parallel", "arbitrary")),
        name="gla_intra",
    )(p["q"], p["k"], p["gcum"])


def _gla_inter(q, k, v, g, s, reverse):
    c = q.shape[0]
    dk = g.shape[1]
    g_tot = g[0:1, :] if reverse else g[c - 1:c, :]
    qi = (q.astype(F32) * jnp.exp2(g)).astype(BF16)
    ki = (k.astype(F32) * jnp.exp2(g_tot - g)).astype(BF16)
    decay = jnp.transpose(jnp.broadcast_to(jnp.exp2(g_tot), (dk, dk)))
    decay = jnp.concatenate([decay] * (v.shape[1] // dk), axis=1)
    return _dot(qi, s.astype(BF16)), s * decay + _dot_tn(ki, v)


def _gla_kernel(qf_ref, kf_ref, vf_ref, gf_ref, af_ref, qb_ref, kb_ref, vb_ref, gb_ref, s0_ref,
                oa_ref, ob_ref, sfin_ref, *s_scr):
    n = pl.program_id(1)

    @pl.when(n == 0)
    def _():
        for k, s_ref in enumerate(s_scr):
            s_ref[...] = s0_ref[0, k]

    c = GLA_CHUNK
    for sc in range(GLA_CHUNKS_PER_STEP):
        for hd in range(GLA_HEADS):
            ks = slice(hd * GLA_DK, (hd + 1) * GLA_DK)
            vs = slice(hd * GLA_DV, (hd + 1) * GLA_DV)
            rs = slice(sc * c, (sc + 1) * c)
            s_ref = s_scr[hd]
            v = vf_ref[0, rs, vs]
            inter, s_new = _gla_inter(qf_ref[0, rs, ks], kf_ref[0, rs, ks], v, gf_ref[0, rs, ks], s_ref[...], False)
            oa_ref[0, rs, vs] = (_dot(af_ref[0, hd, rs, :], v) + inter).astype(BF16)
            s_ref[...] = s_new

            rs = slice((GLA_CHUNKS_PER_STEP - 1 - sc) * c, (GLA_CHUNKS_PER_STEP - sc) * c)
            s_ref = s_scr[GLA_HEADS + hd]
            inter, s_new = _gla_inter(qb_ref[0, rs, ks], kb_ref[0, rs, ks], vb_ref[0, rs, vs], gb_ref[0, rs, ks],
                                      s_ref[...], True)
            ob_ref[0, rs, vs] = inter.astype(BF16)
            s_ref[...] = s_new

    @pl.when(n == pl.num_programs(1) - 1)
    def _():
        for k, s_ref in enumerate(s_scr):
            sfin_ref[0, k] = s_ref[...]


def _gla_scan(p, s0, tm):
    b, l, _ = p["q"].shape
    amat = _gla_intra(p, tm)
    c = GLA_CHUNK * GLA_CHUNKS_PER_STEP
    nc = l // c
    fwd = lambda w, col=0: pl.BlockSpec((1, c, w), lambda bi, n: (bi, n, col))
    bwd = lambda w, col=0: pl.BlockSpec((1, c, w), lambda bi, n: (bi, nc - 1 - n, col))
    st = pl.BlockSpec((1, 2 * GLA_HEADS, GLA_DK, GLA_DV), lambda bi, n: (bi, 0, 0, 0))
    a_spec = pl.BlockSpec((1, GLA_HEADS, c, GLA_CHUNK), lambda bi, n: (bi, 0, n, 0))
    oa, ob, sfin = pl.pallas_call(
        _gla_kernel,
        grid=(b, nc),
        in_specs=[fwd(GLA_QK), fwd(GLA_QK), fwd(GLA_V), fwd(GLA_QK, 0), a_spec,
                  bwd(GLA_QK), bwd(GLA_QK), bwd(GLA_V), bwd(GLA_QK, 1), st],
        out_specs=[fwd(GLA_V), bwd(GLA_V), st],
        out_shape=[jax.ShapeDtypeStruct((b, l, GLA_V), BF16), jax.ShapeDtypeStruct((b, l, GLA_V), BF16),
                   jax.ShapeDtypeStruct(s0.shape, F32)],
        scratch_shapes=[pltpu.VMEM((GLA_DK, GLA_DV), F32)] * (2 * GLA_HEADS),
        compiler_params=_cparams(("parallel", "arbitrary")),
        name="gla_scan",
    )(p["q"], p["k"], p["v"], p["gcum"], amat, p["q"], p["k"], p["v"], p["gcum"], s0)
    return oa, ob, sfin


def _split_hi_lo(x):
    hi = x.astype(BF16)
    lo = (x - hi.astype(F32)).astype(BF16)
    return jnp.concatenate([hi, lo], axis=1)


def _ssd_factors(sd, ex, d):
    c = sd.shape[0]
    nh = SSD_HEADS
    lane = lax.broadcasted_iota(jnp.int32, (1, LANES), 1)
    sel = (lane >= (2 + d) * nh) & (lane < (3 + d) * nh)
    tot = sd[0:1, :] if d == 1 else sd[c - 1:c, :]
    dt_under_cum = pltpu.roll(sd, 2 * nh, 1)
    e_in = jnp.where(sel, jnp.exp(sd), 0.0)
    w_st = jnp.where(sel, dt_under_cum * jnp.exp(tot - sd), 0.0)
    e_tot = jnp.broadcast_to(jnp.where(sel, jnp.exp(tot), 0.0), (SUBLANES, LANES))
    expand = lambda f: _dot(_split_hi_lo(f), ex)
    return expand(e_in), expand(w_st), expand(e_tot)[0:1, :]


def _ssd_kernel(xsa_ref, bca_ref, bTa_ref, sda_ref, sdTa_ref, xsb_ref, bcb_ref, bTb_ref, sdb_ref,
                ex_ref, s0_ref, ya_ref, yb_ref, sfin_ref, *s_scr):
    c = SSD_CHUNK
    nh = SSD_HEADS
    gw = HEADS_PER_GROUP * SSD_HEADDIM
    n = pl.program_id(1)

    @pl.when(n == 0)
    def _():
        for k, s_ref in enumerate(s_scr):
            s_ref[...] = s0_ref[0, k]

    sda = sda_ref[0]
    sdTa = sdTa_ref[0]
    ein_f, wst_f, dec_f = _ssd_factors(sda, ex_ref[0], 0)
    ein_b, wst_b, dec_b = _ssd_factors(sdb_ref[0], ex_ref[1], 1)
    ii = lax.broadcasted_iota(jnp.int32, (c, c), 0)
    jj = lax.broadcasted_iota(jnp.int32, (c, c), 1)
    lower = jj <= ii
    upper = jj >= ii
    lo_half = lax.broadcasted_iota(jnp.int32, (1, LANES), 1) < SSD_HEADDIM

    def weights(h, d, mask):
        col = (2 + d) * nh + h
        ci = sda[:, col:col + 1]
        cj = sdTa[col:col + 1, :]
        dtj = sdTa[d * nh + h:d * nh + h + 1, :]
        return jnp.where(mask, jnp.exp(jnp.minimum(ci - cj, 0.0)) * dtj, 0.0)

    for gi in range(SSD_GROUPS):
        gs = slice(gi * gw, (gi + 1) * gw)
        bm = bca_ref[0, :, gi * SSD_STATE:(gi + 1) * SSD_STATE]
        cm = bca_ref[0, :, SSD_BC + gi * SSD_STATE:SSD_BC + (gi + 1) * SSD_STATE]
        gmat = _dot_nt(cm, bm)
        s_f = s_scr[gi]
        s_b = s_scr[SSD_GROUPS + gi]
        y_inter = _dot(cm, s_f[...].astype(BF16)) * ein_f[:, gs]
        for hp in range(HEADS_PER_GROUP // 2):
            col0 = (gi * HEADS_PER_GROUP + 2 * hp) * SSD_HEADDIM
            xpair = xsa_ref[0, :, col0:col0 + LANES]
            scores = []
            for t in range(2):
                h = gi * HEADS_PER_GROUP + 2 * hp + t
                scores.append((gmat * (weights(h, 0, lower) + weights(h, 1, upper))).astype(BF16))
            zero = jnp.zeros_like(xpair)
            rhs = jnp.concatenate([jnp.where(lo_half, xpair, zero), jnp.where(lo_half, zero, xpair)], axis=0)
            y_pair = _dot(jnp.concatenate(scores, axis=1), rhs) + y_inter[:, hp * LANES:(hp + 1) * LANES]
            ya_ref[0, :, col0:col0 + LANES] = y_pair.astype(BF16)
        wx = (xsa_ref[0, :, gs].astype(F32) * wst_f[:, gs]).astype(BF16)
        s_f[...] = s_f[...] * dec_f[:, gs] + _dot(bTa_ref[0, gi * SSD_STATE:(gi + 1) * SSD_STATE, :], wx)

        cmb = bcb_ref[0, :, SSD_BC + gi * SSD_STATE:SSD_BC + (gi + 1) * SSD_STATE]
        yb_ref[0, :, gs] = (_dot(cmb, s_b[...].astype(BF16)) * ein_b[:, gs]).astype(BF16)
        wxb = (xsb_ref[0, :, gs].astype(F32) * wst_b[:, gs]).astype(BF16)
        s_b[...] = s_b[...] * dec_b[:, gs] + _dot(bTb_ref[0, gi * SSD_STATE:(gi + 1) * SSD_STATE, :], wxb)

    @pl.when(n == pl.num_programs(1) - 1)
    def _():
        for k, s_ref in enumerate(s_scr):
            sfin_ref[0, k] = s_ref[...]


def _ssd_expander():
    r = lax.broadcasted_iota(jnp.int32, (2, 2 * LANES, SSD_INNER), 1) % LANES
    d = lax.broadcasted_iota(jnp.int32, (2, 2 * LANES, SSD_INNER), 0)
    h = lax.broadcasted_iota(jnp.int32, (2, 2 * LANES, SSD_INNER), 2) // SSD_HEADDIM
    return (r == (2 + d) * SSD_HEADS + h).astype(BF16)


def _ssd_scan(p, s0):
    b, l, _ = p["xs"].shape
    c = SSD_CHUNK
    nc = l // c
    specs = []
    for rev in (False, True):
        idx = (lambda n: nc - 1 - n) if rev else (lambda n: n)
        tok = lambda w, idx=idx: pl.BlockSpec((1, c, w), lambda bi, n: (bi, idx(n), 0))
        chan = lambda r, idx=idx: pl.BlockSpec((1, r, c), lambda bi, n: (bi, 0, idx(n)))
        specs.append([tok(SSD_INNER), tok(2 * SSD_BC), chan(SSD_BC), tok(LANES), chan(4 * SSD_HEADS)])
    st = pl.BlockSpec((1, 2 * SSD_GROUPS, SSD_STATE, HEADS_PER_GROUP * SSD_HEADDIM), lambda bi, n: (bi, 0, 0, 0))
    ex = _ssd_expander()
    ya, yb, sfin = pl.pallas_call(
        _ssd_kernel,
        grid=(b, nc),
        in_specs=specs[0] + specs[1][:4] + [_const_spec(ex.shape), st],
        out_specs=[specs[0][0], specs[1][0], st],
        out_shape=[jax.ShapeDtypeStruct((b, l, SSD_INNER), BF16), jax.ShapeDtypeStruct((b, l, SSD_INNER), BF16),
                   jax.ShapeDtypeStruct(s0.shape, F32)],
        scratch_shapes=[pltpu.VMEM((SSD_STATE, HEADS_PER_GROUP * SSD_HEADDIM), F32)] * (2 * SSD_GROUPS),
        compiler_params=_cparams(("parallel", "arbitrary")),
        name="ssd_scan",
    )(p["xs"], p["bc"], p["bT"], p["sd"], p["sdT"], p["xs"], p["bc"], p["bT"], p["sd"], ex, s0)
    return ya, yb, sfin


def _post0_kernel(of_ref, ob_ref, g_ref, yf_ref, yb_ref, xs_ref, z_ref, x_ref, mod_ref,
                  gn_ref, dsk_ref, sn_ref, wo_ref, out_ref):
    o = of_ref[0].astype(F32) + ob_ref[0].astype(F32)
    parts = []
    for hd in range(GLA_HEADS):
        oh = o[:, hd * GLA_DV:(hd + 1) * GLA_DV]
        parts.append(oh * lax.rsqrt(jnp.mean(oh * oh, axis=-1, keepdims=True) + EPS))
    gla = jnp.concatenate(parts, axis=1) * gn_ref[...] * _silu(g_ref[0].astype(F32))
    xs = xs_ref[0].astype(F32)
    y = (yf_ref[0].astype(F32) + yb_ref[0].astype(F32) + xs * dsk_ref[...]) * _silu(z_ref[0].astype(F32))
    gw = SSD_INNER // SSD_GROUPS
    parts = []
    for gi in range(SSD_GROUPS):
        yg = y[:, gi * gw:(gi + 1) * gw]
        parts.append(yg * lax.rsqrt(jnp.mean(yg * yg, axis=-1, keepdims=True) + EPS))
    ssd = jnp.concatenate(parts, axis=1) * sn_ref[...]
    res = _dot(gla.astype(BF16), wo_ref[:GLA_V, :]) + _dot(ssd.astype(BF16), wo_ref[GLA_V:, :])
    out_ref[0] = x_ref[0] + mod_ref[0, 2:3, :] * res


def _post0(p, of, ob, yf, yb, x, mod, wp, tm):
    b, l, _ = x.shape
    tm = min(tm, l)
    bm = mod.shape[0]
    mod_idx = (lambda bi, i: (bi, 0, 0)) if bm > 1 else (lambda bi, i: (0, 0, 0))
    tok = lambda w: pl.BlockSpec((1, tm, w), lambda bi, i: (bi, i, 0))
    return pl.pallas_call(
        _post0_kernel,
        grid=(b, l // tm),
        in_specs=[tok(GLA_V), tok(GLA_V), tok(GLA_V), tok(SSD_INNER), tok(SSD_INNER), tok(SSD_INNER),
                  tok(SSD_INNER), tok(D_MODEL), pl.BlockSpec((1, N_MOD, D_MODEL), mod_idx),
                  _const_spec((1, GLA_V)), _const_spec((1, SSD_INNER)), _const_spec((1, SSD_INNER)),
                  _const_spec(wp["w_out"].shape)],
        out_specs=tok(D_MODEL),
        out_shape=jax.ShapeDtypeStruct((b, l, D_MODEL), F32),
        compiler_params=_cparams(("parallel", "arbitrary")),
        name="post0",
    )(of, ob, p["g"], yf, yb, p["xs"], p["z"], x, mod, wp["gla_norm"], wp["d_skip"], wp["ssd_norm"], wp["w_out"])


def _ffn_kernel(x_ref, mod_ref, nw_ref, w1_ref, w3_ref, w2_ref, fn_ref, out_ref, *, final):
    x = x_ref[0]
    h = _norm_mod(x, nw_ref[...], mod_ref[0, 3:4, :], mod_ref[0, 4:5, :]).astype(BF16)
    half = D_FF // 2
    y = jnp.zeros_like(x)
    for j in range(2):
        cs = slice(j * half, (j + 1) * half)
        u = (_silu(_dot(h, w1_ref[:, cs])) * _dot(h, w3_ref[:, cs])).astype(BF16)
        y = y + _dot(u, w2_ref[cs, :])
    out = x + mod_ref[0, 5:6, :] * y
    if final:
        out = out * lax.rsqrt(jnp.mean(out * out, axis=-1, keepdims=True) + EPS) * fn_ref[...]
    out_ref[0] = out


def _ffn(x, mod, nw, w1, w3, w2, final_norm, final, tm):
    b, l, _ = x.shape
    tm = min(tm, l)
    bm = mod.shape[0]
    mod_idx = (lambda bi, i: (bi, 0, 0)) if bm > 1 else (lambda bi, i: (0, 0, 0))
    tok = pl.BlockSpec((1, tm, D_MODEL), lambda bi, i: (bi, i, 0))
    return pl.pallas_call(
        functools.partial(_ffn_kernel, final=final),
        grid=(b, l // tm),
        in_specs=[tok, pl.BlockSpec((1, N_MOD, D_MODEL), mod_idx), _const_spec((1, D_MODEL)),
                  _const_spec(w1.shape), _const_spec(w3.shape), _const_spec(w2.shape), _const_spec((1, D_MODEL))],
        out_specs=tok,
        out_shape=jax.ShapeDtypeStruct((b, l, D_MODEL), F32),
        compiler_params=_cparams(("parallel", "arbitrary")),
        name="ffn_final" if final else "ffn",
    )(x, mod, nw, w1, w3, w2, final_norm)


def _inproj1_kernel(*refs, tm, ncol):
    if ncol:
        x_ref, xp_ref, xn_ref, perm_ref = refs[:4]
        refs = refs[4:]
    else:
        x_ref, xp_ref, xn_ref = refs[:3]
        refs = refs[3:]
    (mod_ref, nw_ref, wup_ref, cw_ref, cb_ref, bdq_ref, bdk_ref, bdkT_ref, bdv_ref, wifq_ref, wifk_ref, wifv_ref,
     bif_ref, q_ref, kT_ref, v_ref, xc_ref, z_ref, gc_ref, gT_ref, hext, xm_scr) = refs
    i = pl.program_id(1)
    first = i == 0
    last = i == pl.num_programs(1) - 1
    shift = mod_ref[0, 0:1, :]
    scale = mod_ref[0, 1:2, :]
    nw = nw_ref[...]
    if ncol:
        hext[0:HALO, :] = _norm_mod(xp_ref[0, :, SUBLANES - 1, :], nw, shift, scale).astype(BF16)
        hn = _norm_mod(x_ref[0].reshape(tm, D_MODEL), nw, shift, scale).astype(BF16)
        hext[HALO:HALO + tm, :] = _dot(perm_ref[...], hn).astype(BF16)
        hext[HALO + tm:2 * HALO + tm, :] = _norm_mod(xn_ref[0, :, 0, :], nw, shift, scale).astype(BF16)
    else:
        hext[0:HALO, :] = _norm_mod(xp_ref[0], nw, shift, scale).astype(BF16)
        hext[HALO:HALO + tm, :] = _norm_mod(x_ref[0], nw, shift, scale).astype(BF16)
        hext[HALO + tm:2 * HALO + tm, :] = _norm_mod(xn_ref[0], nw, shift, scale).astype(BF16)
    h = hext[HALO:HALO + tm, :]

    xm_scr[...] = _dot(hext[...], wup_ref[:, :MLSTM_INNER])
    xc = _silu(_conv3(xm_scr, tm, cw_ref, cb_ref, first, last))
    xc_ref[0] = xc.astype(BF16)

    gates = jnp.zeros((tm, LANES), F32) + bif_ref[...]
    for t in range(N_BD_TILES):
        cs = slice(t * BD_TILE, (t + 1) * BD_TILE)
        xct = xc[:, cs].astype(BF16)
        xmt = xm_scr[HALO:HALO + tm, cs].astype(BF16)
        qt = _dot(xct, bdq_ref[t])
        kt = _dot(xct, bdk_ref[t])
        vt = _dot(xmt, bdv_ref[t])
        q_ref[0, :, cs] = qt.astype(BF16)
        v_ref[0, :, cs] = vt.astype(BF16)
        kT_ref[0, cs, :] = (_dot_nt(bdkT_ref[t], xct) * (MLSTM_DH ** -0.5)).astype(BF16)
        gates = gates + _dot(qt.astype(BF16), wifq_ref[cs, :]) + _dot(kt.astype(BF16), wifk_ref[cs, :]) \
            + _dot(vt.astype(BF16), wifv_ref[cs, :])

    gt = jnp.transpose(gates)[:4 * MLSTM_HEADS]
    r = lax.broadcasted_iota(jnp.int32, gt.shape, 0)
    lf = _log_sigmoid(gt)
    cum = jnp.where(r < 2 * MLSTM_HEADS, _seg_cumsum(lf, MLSTM_CHUNK, 1, False),
                    _seg_cumsum(lf, MLSTM_CHUNK, 1, True))
    packed = jnp.where(r % (2 * MLSTM_HEADS) < MLSTM_HEADS, gt, cum)
    a = gt - pltpu.roll(cum, 4 * MLSTM_HEADS - MLSTM_HEADS, 0)
    amax = jnp.where(r < 2 * MLSTM_HEADS, _seg_scan(a, MLSTM_CHUNK, 1, False, use_max=True),
                     _seg_scan(a, MLSTM_CHUNK, 1, True, use_max=True))
    full = jnp.concatenate([packed, amax], axis=0)
    gT_ref[0] = full
    gc_ref[0] = jnp.transpose(jnp.concatenate([full, jnp.zeros((LANES - MLSTM_GATE_ROWS, tm), F32)], axis=0))

    z_ref[0] = _dot(h, wup_ref[:, MLSTM_INNER:]).astype(BF16)


def _inproj1(x, mod, nw, wp, tm, colmajor):
    b, l, _ = x.shape
    tm = min(tm, l)
    nt = l // tm
    bm = mod.shape[0]
    mod_idx = (lambda bi, i: (bi, 0, 0)) if bm > 1 else (lambda bi, i: (0, 0, 0))
    if colmajor:
        rows = l // GRID_W
        ncol = tm // rows
        assert ncol % SUBLANES == 0 and rows % HALO == 0
        xv = x.reshape(b, rows, GRID_W, D_MODEL)
        rb = rows // HALO
        cb = ncol // SUBLANES
        ncb = GRID_W // SUBLANES
        x_spec = pl.BlockSpec((1, rows, ncol, D_MODEL), lambda bi, i: (bi, 0, i, 0))
        xp_spec = pl.BlockSpec((1, HALO, SUBLANES, D_MODEL),
                               lambda bi, i: (bi, rb - 1, jnp.maximum(i * cb - 1, 0), 0))
        xn_spec = pl.BlockSpec((1, HALO, SUBLANES, D_MODEL),
                               lambda bi, i: (bi, 0, jnp.minimum((i + 1) * cb, ncb - 1), 0))
        dst = lax.broadcasted_iota(jnp.int32, (tm, tm), 0)
        src = lax.broadcasted_iota(jnp.int32, (tm, tm), 1)
        perm = (src == (dst % rows) * ncol + dst // rows).astype(BF16)
        lead_specs = [x_spec, xp_spec, xn_spec, _const_spec((tm, tm))]
        lead_args = [xv, xv, xv, perm]
    else:
        ncol = 0
        xv = x
        hb = tm // HALO
        nhb = l // HALO
        x_spec = pl.BlockSpec((1, tm, D_MODEL), lambda bi, i: (bi, i, 0))
        xp_spec = pl.BlockSpec((1, HALO, D_MODEL), lambda bi, i: (bi, jnp.maximum(i * hb - 1, 0), 0))
        xn_spec = pl.BlockSpec((1, HALO, D_MODEL), lambda bi, i: (bi, jnp.minimum((i + 1) * hb, nhb - 1), 0))
        lead_specs = [x_spec, xp_spec, xn_spec]
        lead_args = [xv, xv, xv]
    tok = lambda w: pl.BlockSpec((1, tm, w), lambda bi, i: (bi, i, 0))
    names = ["w_up", "conv_w", "conv_b", "bdq", "bdk", "bdkT", "bdv", "wif_q", "wif_k", "wif_v", "b_if"]
    q, kT, v, xc, z, gc, gT = pl.pallas_call(
        functools.partial(_inproj1_kernel, tm=tm, ncol=ncol),
        grid=(b, nt),
        in_specs=lead_specs + [pl.BlockSpec((1, N_MOD, D_MODEL), mod_idx), _const_spec((1, D_MODEL))]
        + [_const_spec(wp[n].shape) for n in names],
        out_specs=[tok(MLSTM_INNER), pl.BlockSpec((1, MLSTM_INNER, tm), lambda bi, i: (bi, 0, i)),
                   tok(MLSTM_INNER), tok(MLSTM_INNER), tok(MLSTM_INNER), tok(LANES),
                   pl.BlockSpec((1, MLSTM_GATE_ROWS, tm), lambda bi, i: (bi, 0, i))],
        out_shape=[jax.ShapeDtypeStruct((b, l, MLSTM_INNER), BF16), jax.ShapeDtypeStruct((b, MLSTM_INNER, l), BF16),
                   jax.ShapeDtypeStruct((b, l, MLSTM_INNER), BF16), jax.ShapeDtypeStruct((b, l, MLSTM_INNER), BF16),
                   jax.ShapeDtypeStruct((b, l, MLSTM_INNER), BF16), jax.ShapeDtypeStruct((b, l, LANES), F32),
                   jax.ShapeDtypeStruct((b, MLSTM_GATE_ROWS, l), F32)],
        scratch_shapes=[pltpu.VMEM((tm + 2 * HALO, D_MODEL), BF16),
                        pltpu.VMEM((tm + 2 * HALO, MLSTM_INNER), F32)],
        compiler_params=_cparams(("parallel", "arbitrary")),
        name="inproj1",
    )(*lead_args, mod, nw, *[wp[n] for n in names])
    return dict(q=q, kT=kT, v=v, xc=xc, z=z, gc=gc, gT=gT)


def _mlstm_head(q, kT, v, gc, gT, c_scr, cb_scr, n_scr, m_scr, d, hd, reverse):
    c = MLSTM_CHUNK
    dh = MLSTM_DH
    idx = d * MLSTM_HEADS + hd
    ri = d * 2 * MLSTM_HEADS + hd
    rb = ri + MLSTM_HEADS
    rm = 4 * MLSTM_HEADS + ri
    c_scr, cb_scr, n_scr, m_scr = c_scr[idx], cb_scr[idx], n_scr[idx], m_scr[idx]
    bi = gc[:, rb:rb + 1]
    am_i = gc[:, rm:rm + 1]
    bj = gT[rb:rb + 1, :]
    a_j = gT[ri:ri + 1, :] - bj
    am_j = gT[rm:rm + 1, :]
    m0 = m_scr[0:1, 0:1]
    big_m = jnp.maximum(m0, am_i)
    ii = lax.broadcasted_iota(jnp.int32, (c, c), 0)
    jj = lax.broadcasted_iota(jnp.int32, (c, c), 1)
    causal = (jj >= ii) if reverse else (jj <= ii)
    s = _dot(q, kT) * jnp.where(causal, jnp.exp(a_j - big_m), 0.0)
    w_inter = jnp.exp(m0 - big_m)
    qf = q.astype(F32)
    den = jnp.sum(s, axis=-1, keepdims=True) \
        + w_inter * jnp.sum(qf * n_scr[0:1, :], axis=-1, keepdims=True)
    num = _dot(s.astype(BF16), v) + _dot((qf * w_inter).astype(BF16), cb_scr[...])
    h = num * (1.0 / jnp.maximum(jnp.abs(den), jnp.exp(-(bi + big_m))))

    last = 0 if reverse else c - 1
    m_inner = jnp.maximum(m0, am_j[:, last:last + 1])
    kw = kT.astype(F32) * jnp.exp(a_j - m_inner)
    decay = jnp.exp(m0 - m_inner)
    kwb = kw.astype(BF16)
    c_new = decay * c_scr[...] + _dot(kwb, v)
    c_scr[...] = c_new
    cb_scr[...] = c_new.astype(BF16)
    n_scr[...] = decay * n_scr[...] + _dot_nt(jnp.ones((SUBLANES, c), BF16), kwb)
    m_scr[...] = jnp.broadcast_to(bj[:, last:last + 1] + m_inner, m_scr.shape)
    return h


def _mlstm_kernel(*refs, has_init, emit_final):
    ins = refs[:10]
    pos = 10
    if has_init:
        c0_ref, n0_ref, m0_ref = refs[pos:pos + 3]
        pos += 3
    hf_ref, hb_ref = refs[pos:pos + 2]
    pos += 2
    if emit_final:
        cfin_ref, nfin_ref, mfin_ref = refs[pos:pos + 3]
        pos += 3
    nst = 2 * MLSTM_HEADS
    c_scr, cb_scr, n_scr, m_scr = (refs[pos + k * nst:pos + (k + 1) * nst] for k in range(4))
    n = pl.program_id(1)

    @pl.when(n == 0)
    def _():
        for k in range(nst):
            if has_init:
                c_scr[k][...] = c0_ref[0, k]
                cb_scr[k][...] = c0_ref[0, k].astype(BF16)
                n_scr[k][...] = n0_ref[0, k]
                m_scr[k][...] = m0_ref[0, k]
            else:
                c_scr[k][...] = jnp.zeros_like(c_scr[k])
                cb_scr[k][...] = jnp.zeros_like(cb_scr[k])
                n_scr[k][...] = jnp.zeros_like(n_scr[k])
                m_scr[k][...] = jnp.full(m_scr[k].shape, M_INIT, F32)

    for d, h_ref in enumerate((hf_ref, hb_ref)):
        q_ref, kT_ref, v_ref, gc_ref, gT_ref = ins[5 * d:5 * d + 5]
        gc = gc_ref[0]
        gT = gT_ref[0]
        for hd in range(MLSTM_HEADS):
            cs = slice(hd * MLSTM_DH, (hd + 1) * MLSTM_DH)
            h = _mlstm_head(q_ref[0, :, cs], kT_ref[0, cs, :], v_ref[0, :, cs], gc, gT,
                            c_scr, cb_scr, n_scr, m_scr, d, hd, d == 1)
            h_ref[0, :, cs] = h.astype(BF16)

    if emit_final:
        @pl.when(n == pl.num_programs(1) - 1)
        def _():
            for k in range(nst):
                cfin_ref[0, k] = c_scr[k][...]
                nfin_ref[0, k] = n_scr[k][...]
                mfin_ref[0, k] = m_scr[k][...]


def _mlstm_scan(p, init, emit_final):
    b, l, _ = p["q"].shape
    c = MLSTM_CHUNK
    nc = l // c
    dh = MLSTM_DH
    nst = 2 * MLSTM_HEADS
    specs = []
    for rev in (False, True):
        idx = (lambda n: nc - 1 - n) if rev else (lambda n: n)
        specs.append([
            pl.BlockSpec((1, c, MLSTM_INNER), lambda bi, n, idx=idx: (bi, idx(n), 0)),
            pl.BlockSpec((1, MLSTM_INNER, c), lambda bi, n, idx=idx: (bi, 0, idx(n))),
            pl.BlockSpec((1, c, MLSTM_INNER), lambda bi, n, idx=idx: (bi, idx(n), 0)),
            pl.BlockSpec((1, c, LANES), lambda bi, n, idx=idx: (bi, idx(n), 0)),
            pl.BlockSpec((1, MLSTM_GATE_ROWS, c), lambda bi, n, idx=idx: (bi, 0, idx(n))),
        ])
    st_shapes = [(b, nst, dh, dh), (b, nst, SUBLANES, dh), (b, nst, SUBLANES, LANES)]
    st_index = lambda bi, n: (bi, 0, 0, 0)
    args = (p["q"], p["kT"], p["v"], p["gc"], p["gT"])
    operands = [*args, *args]
    in_specs = specs[0] + specs[1]
    if init is not None:
        operands += list(init)
        in_specs += [pl.BlockSpec((1,) + s[1:], st_index, pipeline_mode=pl.Buffered(1)) for s in st_shapes]
    out_specs = [specs[0][0], specs[1][0]]
    out_shape = [jax.ShapeDtypeStruct((b, l, MLSTM_INNER), BF16), jax.ShapeDtypeStruct((b, l, MLSTM_INNER), BF16)]
    if emit_final:
        out_specs += [pl.BlockSpec((1,) + s[1:], st_index) for s in st_shapes]
        out_shape += [jax.ShapeDtypeStruct(s, F32) for s in st_shapes]
    res = pl.pallas_call(
        functools.partial(_mlstm_kernel, has_init=init is not None, emit_final=emit_final),
        grid=(b, nc),
        in_specs=in_specs,
        out_specs=out_specs,
        out_shape=out_shape,
        scratch_shapes=[pltpu.VMEM((dh, dh), F32)] * nst + [pltpu.VMEM((dh, dh), BF16)] * nst
        + [pltpu.VMEM((SUBLANES, dh), F32)] * nst + [pltpu.VMEM((SUBLANES, LANES), F32)] * nst,
        compiler_params=_cparams(("parallel", "arbitrary")),
        name="mlstm_scan",
    )(*operands)
    return res[0], res[1], tuple(res[2:])


def _post1_kernel(hf_ref, hb_ref, xc_ref, z_ref, x_ref, mod_ref, mh_ref, sk_ref, wd_ref, out_ref, *, tm, ncol):
    hs = hf_ref[0].astype(F32) + hb_ref[0].astype(F32)
    parts = []
    for hd in range(MLSTM_HEADS):
        hh = hs[:, hd * MLSTM_DH:(hd + 1) * MLSTM_DH]
        mu = jnp.mean(hh, axis=-1, keepdims=True)
        cen = hh - mu
        parts.append(cen * lax.rsqrt(jnp.mean(cen * cen, axis=-1, keepdims=True) + EPS))
    feat = (jnp.concatenate(parts, axis=1) * mh_ref[...] + sk_ref[...] * xc_ref[0].astype(F32)) \
        * _silu(z_ref[0].astype(F32))
    res = mod_ref[0, 2:3, :] * _dot(feat.astype(BF16), wd_ref[...])
    if ncol:
        rows = tm // ncol
        for j in range(ncol):
            out_ref[0, :, j, :] = x_ref[0, :, j, :] + res[j * rows:(j + 1) * rows, :]
    else:
        out_ref[0] = x_ref[0] + res


def _post1(p, hf, hb, x, mod, wp, tm, colmajor):
    b, l, _ = x.shape
    tm = min(tm, l)
    bm = mod.shape[0]
    mod_idx = (lambda bi, i: (bi, 0, 0)) if bm > 1 else (lambda bi, i: (0, 0, 0))
    tok = lambda w: pl.BlockSpec((1, tm, w), lambda bi, i: (bi, i, 0))
    if colmajor:
        rows = l // GRID_W
        ncol = tm // rows
        xv = x.reshape(b, rows, GRID_W, D_MODEL)
        x_spec = pl.BlockSpec((1, rows, ncol, D_MODEL), lambda bi, i: (bi, 0, i, 0))
    else:
        ncol = 0
        xv = x
        x_spec = tok(D_MODEL)
    out = pl.pallas_call(
        functools.partial(_post1_kernel, tm=tm, ncol=ncol),
        grid=(b, l // tm),
        in_specs=[tok(MLSTM_INNER), tok(MLSTM_INNER), tok(MLSTM_INNER), tok(MLSTM_INNER), x_spec,
                  pl.BlockSpec((1, N_MOD, D_MODEL), mod_idx),
                  _const_spec((1, MLSTM_INNER)), _const_spec((1, MLSTM_INNER)), _const_spec(wp["w_down"].shape)],
        out_specs=x_spec,
        out_shape=jax.ShapeDtypeStruct(xv.shape, F32),
        compiler_params=_cparams(("parallel", "arbitrary")),
        name="post1",
    )(hf, hb, p["xc"], p["z"], xv, mod, wp["mh_norm"], wp["skip"], wp["w_down"])
    return out.reshape(b, l, D_MODEL)


def _prep_even(w_in, w_gate, b_gate, gla_norm, conv_w, conv_b, a_log, dt_bias, d_skip, ssd_norm, w_out):
    sizes = (GLA_QK, GLA_QK, GLA_V, GLA_V, 2 * GLA_GATE_RANK, SSD_INNER, SSD_XBC, 2 * SSD_HEADS)
    offs = [0]
    for s in sizes:
        offs.append(offs[-1] + s)
    cols = lambda i: w_in[:, offs[i]:offs[i + 1]]
    w_main = jnp.concatenate([cols(0), cols(1), cols(2), cols(3), cols(5), cols(6)], axis=1).astype(BF16)
    pad = jnp.zeros((D_MODEL, LANES - 2 * GLA_GATE_RANK - 2 * SSD_HEADS), w_in.dtype)
    w_small = jnp.concatenate([cols(4), cols(7), pad], axis=1).astype(BF16)
    zero = jnp.zeros((GLA_GATE_RANK, GLA_QK), w_gate.dtype)
    wg = jnp.concatenate([jnp.concatenate([w_gate[0], zero], axis=1),
                          jnp.concatenate([zero, w_gate[1]], axis=1)], axis=0).astype(BF16)
    return dict(
        w_main=w_main, w_small=w_small, w_dtT=jnp.transpose(cols(7)).astype(BF16),
        w_gate=wg, b_gate=b_gate.reshape(1, 2 * GLA_QK),
        conv_w=conv_w, conv_b=conv_b.reshape(1, SSD_XBC),
        dt_bias=dt_bias.reshape(1, 2 * SSD_HEADS), a_log=a_log.reshape(1, 2 * SSD_HEADS),
        dt_biasT=dt_bias.reshape(2 * SSD_HEADS, 1), a_logT=a_log.reshape(2 * SSD_HEADS, 1),
        gla_norm=jnp.tile(gla_norm, GLA_HEADS).reshape(1, GLA_V),
        d_skip=jnp.repeat(d_skip, SSD_HEADDIM).reshape(1, SSD_INNER),
        ssd_norm=ssd_norm.reshape(1, SSD_INNER),
        w_out=w_out.astype(BF16),
    )


def _block_diag_tiles(w):
    per = BD_TILE // QKV_BLOCK
    w4 = w.reshape(N_BD_TILES, per, QKV_BLOCK, QKV_BLOCK)
    eye = jnp.eye(per, dtype=w.dtype)
    return jnp.einsum("tade,ab->tadbe", w4, eye).reshape(N_BD_TILES, BD_TILE, BD_TILE)


def _prep_odd(w_up, conv_w, conv_b, wq, wk, wv, w_if, b_if, mh_norm, skip, w_down):
    ng = 4 * MLSTM_HEADS
    wif = jnp.transpose(w_if, (1, 0, 2)).reshape(3 * MLSTM_INNER, ng)
    wif = jnp.concatenate([wif, jnp.zeros((3 * MLSTM_INNER, LANES - ng), w_if.dtype)], axis=1).astype(BF16)
    bif = jnp.concatenate([b_if.reshape(1, ng), jnp.zeros((1, LANES - ng), b_if.dtype)], axis=1)
    bdk = _block_diag_tiles(wk)
    return dict(
        w_up=w_up.astype(BF16), conv_w=conv_w, conv_b=conv_b.reshape(1, MLSTM_INNER),
        bdq=_block_diag_tiles(wq).astype(BF16), bdk=bdk.astype(BF16),
        bdkT=jnp.transpose(bdk, (0, 2, 1)).astype(BF16), bdv=_block_diag_tiles(wv).astype(BF16),
        wif_q=wif[:MLSTM_INNER], wif_k=wif[MLSTM_INNER:2 * MLSTM_INNER], wif_v=wif[2 * MLSTM_INNER:], b_if=bif,
        mh_norm=mh_norm.reshape(1, MLSTM_INNER), skip=skip.reshape(1, MLSTM_INNER),
        w_down=w_down.astype(BF16),
    )


TOKEN_TILE = 512


def kernel(x, c, ctx, c_ctx, w_mod, b_mod, norm_mix, norm_ffn, ffn_w1, ffn_w3, ffn_w2, a_w_in, a_gla_w_gate,
           a_gla_b_gate, a_gla_norm, a_ssd_conv_w, a_ssd_conv_b, a_ssd_A_log, a_ssd_dt_bias, a_ssd_D, a_ssd_norm,
           a_w_out, b_w_up, b_conv_w, b_conv_b, b_wq, b_wk, b_wv, b_w_if, b_b_if, b_mh_norm, b_skip, b_w_down,
           final_norm):
    tile = TOKEN_TILE
    bsz = x.shape[0]
    depth = w_mod.shape[0]
    pad_rows = (-(bsz + 1)) % 8
    cc = jnp.concatenate([c, c_ctx[None, :], jnp.zeros((pad_rows, D_MODEL), c.dtype)], axis=0)
    mods = _modulation(cc, w_mod, b_mod).reshape(depth, bsz + 1 + pad_rows, N_MOD, D_MODEL)
    fnorm = final_norm.reshape(1, D_MODEL)
    for layer in range(depth):
        last = layer == depth - 1
        j = layer // 2
        mx = mods[layer, :bsz]
        mc = mods[layer, bsz:bsz + 1]
        nmix = norm_mix[layer].reshape(1, D_MODEL)
        nffn = norm_ffn[layer].reshape(1, D_MODEL)
        w1 = ffn_w1[layer].astype(BF16)
        w3 = ffn_w3[layer].astype(BF16)
        w2 = ffn_w2[layer].astype(BF16)
        if layer % 2 == 0:
            wp = _prep_even(a_w_in[j], a_gla_w_gate[j], a_gla_b_gate[j], a_gla_norm[j], a_ssd_conv_w[j],
                            a_ssd_conv_b[j], a_ssd_A_log[j], a_ssd_dt_bias[j], a_ssd_D[j], a_ssd_norm[j],
                            a_w_out[j])
            pc = _inproj0(ctx, mc, nmix, wp, tile)
            gla0 = jnp.zeros((bsz, 2 * GLA_HEADS, GLA_DK, GLA_DV), F32)
            ssd0 = jnp.zeros((bsz, 2 * SSD_GROUPS, SSD_STATE, HEADS_PER_GROUP * SSD_HEADDIM), F32)
            ofc, obc, gla_st = _gla_scan(pc, gla0, GLA_INTRA_TILE)
            yfc, ybc, ssd_st = _ssd_scan(pc, ssd0)
            px = _inproj0(x, mx, nmix, wp, tile)
            ofx, obx, _ = _gla_scan(px, gla_st, GLA_INTRA_TILE)
            yfx, ybx, _ = _ssd_scan(px, ssd_st)
            x = _post0(px, ofx, obx, yfx, ybx, x, mx, wp, tile)
            if not last:
                ctx = _post0(pc, ofc, obc, yfc, ybc, ctx, mc, wp, tile)
        else:
            wp = _prep_odd(b_w_up[j], b_conv_w[j], b_conv_b[j], b_wq[j], b_wk[j], b_wv[j], b_w_if[j], b_b_if[j],
                           b_mh_norm[j], b_skip[j], b_w_down[j])
            pc = _inproj1(ctx, mc, nmix, wp, tile, colmajor=False)
            hfc, hbc, state = _mlstm_scan(pc, None, True)
            px = _inproj1(x, mx, nmix, wp, tile, colmajor=True)
            hfx, hbx, _ = _mlstm_scan(px, state, False)
            x = _post1(px, hfx, hbx, x, mx, wp, tile, colmajor=True)
            if not last:
                ctx = _post1(pc, hfc, hbc, ctx, mc, wp, tile, colmajor=False)
        x = _ffn(x, mx, nffn, w1, w3, w2, fnorm, last, tile)
        if not last:
            ctx = _ffn(ctx, mc, nffn, w1, w3, w2, fnorm, False, tile)
    return x
```

```python
import functools

import jax
import jax.numpy as jnp
from jax import lax
from jax.experimental import pallas as pl
from jax.experimental.pallas import tpu as pltpu

F32 = jnp.float32
BF16 = jnp.bfloat16

D_MODEL = 1024
GRID_W = 64
EPS = 1e-6
LOG2_E = 1.4426950408889634
M_INIT = -1e30
N_MOD = 6

GLA_HEADS = 4
GLA_DK = 128
GLA_DV = 256
GLA_QK = GLA_HEADS * GLA_DK
GLA_V = GLA_HEADS * GLA_DV
GLA_GATE_RANK = 16
GLA_GATE_TAU = 16.0

SSD_HEADDIM = 64
SSD_HEADS = 16
SSD_STATE = 128
SSD_GROUPS = 2
SSD_INNER = SSD_HEADS * SSD_HEADDIM
SSD_BC = SSD_GROUPS * SSD_STATE
SSD_XBC = SSD_INNER + 2 * SSD_BC
HEADS_PER_GROUP = SSD_HEADS // SSD_GROUPS

MLSTM_INNER = 2 * D_MODEL
MLSTM_HEADS = 4
MLSTM_DH = MLSTM_INNER // MLSTM_HEADS
QKV_BLOCK = 4
BD_TILE = 256
N_BD_TILES = MLSTM_INNER // BD_TILE
MLSTM_GATE_ROWS = 32

D_FF = 2816

GLA_CHUNK = 64
GLA_CHUNKS_PER_STEP = 4
GLA_INTRA_TILE = 256
SSD_CHUNK = 128
SSD_CHUNKS_PER_STEP = 2
MLSTM_CHUNK = 256
HALO = 16
LANES = 128
SUBLANES = 8

VMEM_LIMIT = 56 * 1024 * 1024


def _cparams(sem):
    return pltpu.CompilerParams(dimension_semantics=sem, vmem_limit_bytes=VMEM_LIMIT)


def _const_spec(shape):
    nd = len(shape)
    return pl.BlockSpec(shape, lambda *_: (0,) * nd, pipeline_mode=pl.Buffered(1))


def _softplus(x):
    return jnp.maximum(x, 0.0) + jnp.log(1.0 + jnp.exp(-jnp.abs(x)))


def _log_sigmoid(x):
    return -_softplus(-x)


def _silu(x):
    return x / (1.0 + jnp.exp(-x))


def _seg_scan(x, seg, axis, reverse, use_max=False):
    n = x.shape[axis]
    idx = lax.broadcasted_iota(jnp.int32, x.shape, axis) % seg
    s = 1
    while s < seg:
        if reverse:
            shifted = pltpu.roll(x, n - s, axis)
            keep = idx < seg - s
        else:
            shifted = pltpu.roll(x, s, axis)
            keep = idx >= s
        if use_max:
            x = jnp.where(keep, jnp.maximum(x, shifted), x)
        else:
            x = x + jnp.where(keep, shifted, 0.0)
        s *= 2
    return x


def _seg_cumsum(x, seg, axis, reverse):
    return _seg_scan(x, seg, axis, reverse)


def _norm_mod(xv, nw, shift, scale):
    ms = jnp.mean(xv * xv, axis=-1, keepdims=True)
    return (xv * lax.rsqrt(ms + EPS) * nw) * (1.0 + scale) + shift


def _dot(a, b):
    return jnp.dot(a, b, preferred_element_type=F32)


def _dot_nt(a, b):
    return lax.dot_general(a, b, (((1,), (1,)), ((), ())), preferred_element_type=F32)


def _dot_tn(a, b):
    return lax.dot_general(a, b, (((0,), (0,)), ((), ())), preferred_element_type=F32)


def _mod_kernel(c_ref, w_ref, b_ref, o_ref):
    a = _silu(c_ref[...]).astype(BF16)
    o_ref[0] = _dot(a, w_ref[0].astype(BF16)) + b_ref[0]


def _modulation(cc, w_mod, b_mod):
    depth = w_mod.shape[0]
    n = w_mod.shape[2]
    tn = n // 4
    rows = cc.shape[0]
    return pl.pallas_call(
        _mod_kernel,
        grid=(depth, n // tn),
        in_specs=[
            pl.BlockSpec((rows, D_MODEL), lambda l, j: (0, 0)),
            pl.BlockSpec((1, D_MODEL, tn), lambda l, j: (l, 0, j)),
            pl.BlockSpec((1, 1, tn), lambda l, j: (l, 0, j)),
        ],
        out_specs=pl.BlockSpec((1, rows, tn), lambda l, j: (l, 0, j)),
        out_shape=jax.ShapeDtypeStruct((depth, rows, n), F32),
        compiler_params=_cparams(("arbitrary", "arbitrary")),
        name="modulation",
    )(cc, w_mod, b_mod.reshape(depth, 1, n))


def _conv3(scr, tm, cw_ref, cb_ref, first, last):
    scr[HALO - 1:HALO, :] = scr[HALO - 1:HALO, :] * jnp.where(first, 0.0, 1.0)
    scr[HALO + tm:HALO + tm + 1, :] = scr[HALO + tm:HALO + tm + 1, :] * jnp.where(last, 0.0, 1.0)

    prev = scr[HALO - 1:HALO - 1 + tm, :]
    cur = scr[HALO:HALO + tm, :]
    nxt = scr[HALO + 1:HALO + 1 + tm, :]
    return cw_ref[0:1, :] * prev + cw_ref[1:2, :] * cur + cw_ref[2:3, :] * nxt + cb_ref[...]


def _inproj0_kernel(x_ref, xp_ref, xn_ref, mod_ref, nw_ref, wmain_ref, wsmall_ref, wdtT_ref, wg_ref, bg_ref,
                    cw_ref, cb_ref, dtb_ref, alog_ref, dtbT_ref, alogT_ref,
                    q_ref, k_ref, v_ref, g_ref, z_ref, gcum_ref, xs_ref, bc_ref, bT_ref, sd_ref, sdT_ref,
                    hext, xbc_scr, *, tm):
    i = pl.program_id(1)
    first = i == 0
    last = i == pl.num_programs(1) - 1
    shift = mod_ref[0, 0:1, :]
    scale = mod_ref[0, 1:2, :]
    nw = nw_ref[...]
    hext[0:HALO, :] = _norm_mod(xp_ref[0], nw, shift, scale).astype(BF16)
    hext[HALO:HALO + tm, :] = _norm_mod(x_ref[0], nw, shift, scale).astype(BF16)
    hext[HALO + tm:2 * HALO + tm, :] = _norm_mod(xn_ref[0], nw, shift, scale).astype(BF16)

    o_k = GLA_QK
    o_v = o_k + GLA_QK
    o_g = o_v + GLA_V
    o_z = o_g + GLA_V
    o_x = o_z + SSD_INNER
    nd = 2 * SSD_HEADS
    h = hext[HALO:HALO + tm, :]

    q_ref[0] = (_dot(h, wmain_ref[:, 0:o_k]) * (GLA_DK ** -0.5)).astype(BF16)
    k_ref[0] = _dot(h, wmain_ref[:, o_k:o_v]).astype(BF16)
    v_ref[0] = _dot(h, wmain_ref[:, o_v:o_g]).astype(BF16)
    g_ref[0] = _dot(h, wmain_ref[:, o_g:o_z]).astype(BF16)
    z_ref[0] = _dot(h, wmain_ref[:, o_z:o_x]).astype(BF16)

    xbc_scr[...] = _dot(hext[...], wmain_ref[:, o_x:o_x + SSD_XBC])
    y = _silu(_conv3(xbc_scr, tm, cw_ref, cb_ref, first, last))
    xs_ref[0] = y[:, :SSD_INNER].astype(BF16)
    bc_ref[0] = y[:, SSD_INNER:].astype(BF16)
    bT_ref[0] = jnp.transpose(y[:, SSD_INNER:SSD_INNER + SSD_BC]).astype(BF16)

    small = _dot(h, wsmall_ref[...])
    r = small[:, :2 * GLA_GATE_RANK].astype(BF16)
    la = _log_sigmoid(_dot(r, wg_ref[...]) + bg_ref[...]) * (LOG2_E / GLA_GATE_TAU)
    gcum_ref[0, :, :GLA_QK] = _seg_cumsum(la[:, :GLA_QK], GLA_CHUNK, 0, False)
    gcum_ref[0, :, GLA_QK:] = _seg_cumsum(la[:, GLA_QK:], GLA_CHUNK, 0, True)

    dtT = _softplus(_dot_nt(wdtT_ref[...], h) + dtbT_ref[...])
    laT = dtT * (-jnp.exp(alogT_ref[...]))
    cumT = jnp.concatenate([_seg_cumsum(laT[:SSD_HEADS], SSD_CHUNK, 1, False),
                            _seg_cumsum(laT[SSD_HEADS:], SSD_CHUNK, 1, True)], axis=0)
    packed = jnp.concatenate([dtT, cumT, jnp.zeros((LANES - 2 * nd, tm), F32)], axis=0)
    sdT_ref[0] = packed[:2 * nd]
    sd_ref[0] = jnp.transpose(packed)


def _inproj0(x, mod, nw, wp, tm):
    b, l, _ = x.shape
    tm = min(tm, l)
    nt = l // tm
    hb = tm // HALO
    nhb = l // HALO
    bm = mod.shape[0]
    mod_idx = (lambda bi, i: (bi, 0, 0)) if bm > 1 else (lambda bi, i: (0, 0, 0))
    tok = lambda w: pl.BlockSpec((1, tm, w), lambda bi, i: (bi, i, 0))
    in_specs = [
        tok(D_MODEL),
        pl.BlockSpec((1, HALO, D_MODEL), lambda bi, i: (bi, jnp.maximum(i * hb - 1, 0), 0)),
        pl.BlockSpec((1, HALO, D_MODEL), lambda bi, i: (bi, jnp.minimum((i + 1) * hb, nhb - 1), 0)),
        pl.BlockSpec((1, N_MOD, D_MODEL), mod_idx),
        _const_spec((1, D_MODEL)),
        _const_spec(wp["w_main"].shape), _const_spec(wp["w_small"].shape), _const_spec(wp["w_dtT"].shape),
        _const_spec(wp["w_gate"].shape), _const_spec(wp["b_gate"].shape),
        _const_spec(wp["conv_w"].shape), _const_spec(wp["conv_b"].shape),
        _const_spec(wp["dt_bias"].shape), _const_spec(wp["a_log"].shape),
        _const_spec(wp["dt_biasT"].shape), _const_spec(wp["a_logT"].shape),
    ]
    outs = [
        ("q", GLA_QK, BF16), ("k", GLA_QK, BF16), ("v", GLA_V, BF16), ("g", GLA_V, BF16),
        ("z", SSD_INNER, BF16), ("gcum", 2 * GLA_QK, F32), ("xs", SSD_INNER, BF16),
        ("bc", 2 * SSD_BC, BF16),
    ]
    out_specs = [tok(w) for _, w, _ in outs]
    out_shape = [jax.ShapeDtypeStruct((b, l, w), dt) for _, w, dt in outs]
    out_specs += [pl.BlockSpec((1, SSD_BC, tm), lambda bi, i: (bi, 0, i)),
                  tok(LANES),
                  pl.BlockSpec((1, 4 * SSD_HEADS, tm), lambda bi, i: (bi, 0, i))]
    out_shape += [jax.ShapeDtypeStruct((b, SSD_BC, l), BF16),
                  jax.ShapeDtypeStruct((b, l, LANES), F32),
                  jax.ShapeDtypeStruct((b, 4 * SSD_HEADS, l), F32)]
    res = pl.pallas_call(
        functools.partial(_inproj0_kernel, tm=tm),
        grid=(b, nt),
        in_specs=in_specs,
        out_specs=out_specs,
        out_shape=out_shape,
        scratch_shapes=[pltpu.VMEM((tm + 2 * HALO, D_MODEL), BF16),
                        pltpu.VMEM((tm + 2 * HALO, SSD_XBC), F32)],
        compiler_params=_cparams(("parallel", "arbitrary")),
        name="inproj0",
    )(x, x, x, mod, nw, wp["w_main"], wp["w_small"], wp["w_dtT"], wp["w_gate"], wp["b_gate"],
      wp["conv_w"], wp["conv_b"], wp["dt_bias"], wp["a_log"], wp["dt_biasT"], wp["a_logT"])
    names = [n for n, _, _ in outs] + ["bT", "sd", "sdT"]
    return dict(zip(names, res))


def _level_ref(g, hsz, reverse):
    c, w = g.shape
    grp = 2 * hsz
    r = hsz if reverse else hsz - 1
    if grp >= 8:
        g3 = g.reshape(c // grp, grp, w)
        return jnp.broadcast_to(g3[:, r:r + 1, :], g3.shape).reshape(c, w)
    g3 = g.reshape(c // 8, 8, w)
    sub = lax.broadcasted_iota(jnp.int32, g3.shape, 1)
    out = None
    for t in range(8 // grp):
        cand = jnp.broadcast_to(g3[:, t * grp + r:t * grp + r + 1, :], g3.shape)
        out = cand if out is None else jnp.where(sub >= t * grp, cand, out)
    return out.reshape(c, w)


def _gla_masks(c, w, reverse):
    row = lax.broadcasted_iota(jnp.int32, (c, w), 0)
    ii = lax.broadcasted_iota(jnp.int32, (c, c), 0)
    jj = lax.broadcasted_iota(jnp.int32, (c, c), 1)
    par = 0 if reverse else 1
    levels = []
    hsz = 1
    while hsz < c:
        is_q = (row // hsz) % 2 == par
        keep = ((ii // (2 * hsz)) == (jj // (2 * hsz))) & ((ii // hsz) % 2 == par) & ((jj // hsz) % 2 != par)
        levels.append((hsz, is_q, keep.astype(F32)))
        hsz *= 2
    return (ii == jj).astype(F32), levels


def _gla_level_operand(q, k, g, hsz, reverse, is_q):
    c = g.shape[0]
    par = 0 if reverse else 1
    if hsz >= SUBLANES:
        parts = []
        for grp in range(c // (2 * hsz)):
            base = grp * 2 * hsz
            ref_row = base + (hsz if reverse else hsz - 1)
            ref = g[ref_row:ref_row + 1, :]
            for half in range(2):
                rows = slice(base + half * hsz, base + (half + 1) * hsz)
                if half == par:
                    parts.append(q[rows] * jnp.exp2(g[rows] - ref))
                else:
                    parts.append(k[rows] * jnp.exp2(ref - g[rows]))
        return jnp.concatenate(parts, axis=0).astype(BF16)
    if hsz == 1:
        step = g - pltpu.roll(g, (c - 1) if reverse else 1, 0)
        return jnp.where(is_q, q * jnp.exp2(step), k).astype(BF16)
    e = jnp.exp2(-jnp.abs(g - _level_ref(g, hsz, reverse)))
    return (jnp.where(is_q, q, k) * e).astype(BF16)


def _gla_intra_kernel(q_ref, k_ref, g_ref, a_ref):
    c = GLA_CHUNK
    masks = [_gla_masks(c, GLA_DK, d == 1) for d in range(2)]
    for sc in range(q_ref.shape[1] // c):
        rs = slice(sc * c, (sc + 1) * c)
        for hd in range(GLA_HEADS):
            ks = slice(hd * GLA_DK, (hd + 1) * GLA_DK)
            qb = q_ref[0, rs, ks]
            kb = k_ref[0, rs, ks]
            q = qb.astype(F32)
            k = kb.astype(F32)
            a = (2.0 * masks[0][0]) * _dot_nt(qb, kb)
            for d in range(2):
                g = g_ref[0, rs, d * GLA_QK + hd * GLA_DK:d * GLA_QK + (hd + 1) * GLA_DK]
                for hsz, is_q, keep in masks[d][1]:
                    mixed = _gla_level_operand(q, k, g, hsz, d == 1, is_q)
                    a = a + keep * _dot_nt(mixed, mixed)
            a_ref[0, hd, rs, :] = a.astype(BF16)


def _gla_intra(p, tm):
    b, l, _ = p["q"].shape
    tm = min(tm, l)
    tok = lambda w: pl.BlockSpec((1, tm, w), lambda bi, i: (bi, i, 0))
    return pl.pallas_call(
        _gla_intra_kernel,
        grid=(b, l // tm),
        in_specs=[tok(GLA_QK), tok(GLA_QK), tok(2 * GLA_QK)],
        out_specs=pl.BlockSpec((1, GLA_HEADS, tm, GLA_CHUNK), lambda bi, i: (bi, 0, i, 0)),
        out_shape=jax.ShapeDtypeStruct((b, GLA_HEADS, l, GLA_CHUNK), BF16),
        compiler_params=_cparams(("parallel", "arbitrary")),
        name="gla_intra",
    )(p["q"], p["k"], p["gcum"])


def _gla_inter(q, k, v, g, s, reverse):
    c = q.shape[0]
    dk = g.shape[1]
    g_tot = g[0:1, :] if reverse else g[c - 1:c, :]
    qi = (q.astype(F32) * jnp.exp2(g)).astype(BF16)
    ki = (k.astype(F32) * jnp.exp2(g_tot - g)).astype(BF16)
    decay = jnp.transpose(jnp.broadcast_to(jnp.exp2(g_tot), (dk, dk)))
    decay = jnp.concatenate([decay] * (v.shape[1] // dk), axis=1)
    return _dot(qi, s.astype(BF16)), s * decay + _dot_tn(ki, v)


def _gla_kernel(qf_ref, kf_ref, vf_ref, gf_ref, af_ref, qb_ref, kb_ref, vb_ref, gb_ref, s0_ref,
                oa_ref, ob_ref, sfin_ref, *s_scr):
    n = pl.program_id(1)

    @pl.when(n == 0)
    def _():
        for k, s_ref in enumerate(s_scr):
            s_ref[...] = s0_ref[0, k]

    c = GLA_CHUNK
    for sc in range(GLA_CHUNKS_PER_STEP):
        for hd in range(GLA_HEADS):
            ks = slice(hd * GLA_DK, (hd + 1) * GLA_DK)
            vs = slice(hd * GLA_DV, (hd + 1) * GLA_DV)
            rs = slice(sc * c, (sc + 1) * c)
            s_ref = s_scr[hd]
            v = vf_ref[0, rs, vs]
            inter, s_new = _gla_inter(qf_ref[0, rs, ks], kf_ref[0, rs, ks], v, gf_ref[0, rs, ks], s_ref[...], False)
            oa_ref[0, rs, vs] = (_dot(af_ref[0, hd, rs, :], v) + inter).astype(BF16)
            s_ref[...] = s_new

            rs = slice((GLA_CHUNKS_PER_STEP - 1 - sc) * c, (GLA_CHUNKS_PER_STEP - sc) * c)
            s_ref = s_scr[GLA_HEADS + hd]
            inter, s_new = _gla_inter(qb_ref[0, rs, ks], kb_ref[0, rs, ks], vb_ref[0, rs, vs], gb_ref[0, rs, ks],
                                      s_ref[...], True)
            ob_ref[0, rs, vs] = inter.astype(BF16)
            s_ref[...] = s_new

    @pl.when(n == pl.num_programs(1) - 1)
    def _():
        for k, s_ref in enumerate(s_scr):
            sfin_ref[0, k] = s_ref[...]


def _gla_scan(p, s0, tm):
    b, l, _ = p["q"].shape
    amat = _gla_intra(p, tm)
    c = GLA_CHUNK * GLA_CHUNKS_PER_STEP
    nc = l // c
    fwd = lambda w, col=0: pl.BlockSpec((1, c, w), lambda bi, n: (bi, n, col))
    bwd = lambda w, col=0: pl.BlockSpec((1, c, w), lambda bi, n: (bi, nc - 1 - n, col))
    st = pl.BlockSpec((1, 2 * GLA_HEADS, GLA_DK, GLA_DV), lambda bi, n: (bi, 0, 0, 0))
    a_spec = pl.BlockSpec((1, GLA_HEADS, c, GLA_CHUNK), lambda bi, n: (bi, 0, n, 0))
    oa, ob, sfin = pl.pallas_call(
        _gla_kernel,
        grid=(b, nc),
        in_specs=[fwd(GLA_QK), fwd(GLA_QK), fwd(GLA_V), fwd(GLA_QK, 0), a_spec,
                  bwd(GLA_QK), bwd(GLA_QK), bwd(GLA_V), bwd(GLA_QK, 1), st],
        out_specs=[fwd(GLA_V), bwd(GLA_V), st],
        out_shape=[jax.ShapeDtypeStruct((b, l, GLA_V), BF16), jax.ShapeDtypeStruct((b, l, GLA_V), BF16),
                   jax.ShapeDtypeStruct(s0.shape, F32)],
        scratch_shapes=[pltpu.VMEM((GLA_DK, GLA_DV), F32)] * (2 * GLA_HEADS),
        compiler_params=_cparams(("parallel", "arbitrary")),
        name="gla_scan",
    )(p["q"], p["k"], p["v"], p["gcum"], amat, p["q"], p["k"], p["v"], p["gcum"], s0)
    return oa, ob, sfin


def _split_hi_lo(x):
    hi = x.astype(BF16)
    lo = (x - hi.astype(F32)).astype(BF16)
    return jnp.concatenate([hi, lo], axis=1)


def _ssd_factors(sd, ex, d):
    c = sd.shape[0]
    nh = SSD_HEADS
    lane = lax.broadcasted_iota(jnp.int32, (1, LANES), 1)
    sel = (lane >= (2 + d) * nh) & (lane < (3 + d) * nh)
    tot = sd[0:1, :] if d == 1 else sd[c - 1:c, :]
    dt_under_cum = pltpu.roll(sd, 2 * nh, 1)
    e_in = jnp.where(sel, jnp.exp(sd), 0.0)
    w_st = jnp.where(sel, dt_under_cum * jnp.exp(tot - sd), 0.0)
    e_tot = jnp.broadcast_to(jnp.where(sel, jnp.exp(tot), 0.0), (SUBLANES, LANES))
    expand = lambda f: _dot(_split_hi_lo(f), ex)
    return expand(e_in), expand(w_st), expand(e_tot)[0:1, :]


def _ssd_kernel(xsa_ref, bca_ref, bTa_ref, sda_ref, sdTa_ref, xsb_ref, bcb_ref, bTb_ref, sdb_ref,
                ex_ref, s0_ref, ya_ref, yb_ref, sfin_ref, *s_scr):
    c = SSD_CHUNK
    nh = SSD_HEADS
    gw = HEADS_PER_GROUP * SSD_HEADDIM
    n = pl.program_id(1)

    @pl.when(n == 0)
    def _():
        for k, s_ref in enumerate(s_scr):
            s_ref[...] = s0_ref[0, k]

    ii = lax.broadcasted_iota(jnp.int32, (c, c), 0)
    jj = lax.broadcasted_iota(jnp.int32, (c, c), 1)
    lower = jj <= ii
    upper = jj >= ii
    lo_half = lax.broadcasted_iota(jnp.int32, (1, LANES), 1) < SSD_HEADDIM

    for sc in range(SSD_CHUNKS_PER_STEP):
        ra = slice(sc * c, (sc + 1) * c)
        rb = slice((SSD_CHUNKS_PER_STEP - 1 - sc) * c, (SSD_CHUNKS_PER_STEP - sc) * c)
        sda = sda_ref[0, ra, :]
        sdTa = sdTa_ref[0, :, ra]
        ein_f, wst_f, dec_f = _ssd_factors(sda, ex_ref[0], 0)
        ein_b, wst_b, dec_b = _ssd_factors(sdb_ref[0, rb, :], ex_ref[1], 1)

        def weights(h, d, mask, sda=sda, sdTa=sdTa):
            col = (2 + d) * nh + h
            ci = sda[:, col:col + 1]
            cj = sdTa[col:col + 1, :]
            dtj = sdTa[d * nh + h:d * nh + h + 1, :]
            return jnp.where(mask, jnp.exp(jnp.minimum(ci - cj, 0.0)) * dtj, 0.0)

        for gi in range(SSD_GROUPS):
            gs = slice(gi * gw, (gi + 1) * gw)
            ns = slice(gi * SSD_STATE, (gi + 1) * SSD_STATE)
            bm = bca_ref[0, ra, ns]
            cm = bca_ref[0, ra, SSD_BC + gi * SSD_STATE:SSD_BC + (gi + 1) * SSD_STATE]
            gmat = _dot_nt(cm, bm)
            s_f = s_scr[gi]
            s_b = s_scr[SSD_GROUPS + gi]
            y_inter = _dot(cm, s_f[...].astype(BF16)) * ein_f[:, gs]
            for hp in range(HEADS_PER_GROUP // 2):
                col0 = (gi * HEADS_PER_GROUP + 2 * hp) * SSD_HEADDIM
                xpair = xsa_ref[0, ra, col0:col0 + LANES]
                scores = []
                for t in range(2):
                    h = gi * HEADS_PER_GROUP + 2 * hp + t
                    scores.append((gmat * (weights(h, 0, lower) + weights(h, 1, upper))).astype(BF16))
                zero = jnp.zeros_like(xpair)
                rhs = jnp.concatenate([jnp.where(lo_half, xpair, zero), jnp.where(lo_half, zero, xpair)], axis=0)
                y_pair = _dot(jnp.concatenate(scores, axis=1), rhs) + y_inter[:, hp * LANES:(hp + 1) * LANES]
                ya_ref[0, ra, col0:col0 + LANES] = y_pair.astype(BF16)
            wx = (xsa_ref[0, ra, gs].astype(F32) * wst_f[:, gs]).astype(BF16)
            s_f[...] = s_f[...] * dec_f[:, gs] + _dot(bTa_ref[0, ns, ra], wx)

            cmb = bcb_ref[0, rb, SSD_BC + gi * SSD_STATE:SSD_BC + (gi + 1) * SSD_STATE]
            yb_ref[0, rb, gs] = (_dot(cmb, s_b[...].astype(BF16)) * ein_b[:, gs]).astype(BF16)
            wxb = (xsb_ref[0, rb, gs].astype(F32) * wst_b[:, gs]).astype(BF16)
            s_b[...] = s_b[...] * dec_b[:, gs] + _dot(bTb_ref[0, ns, rb], wxb)

    @pl.when(n == pl.num_programs(1) - 1)
    def _():
        for k, s_ref in enumerate(s_scr):
            sfin_ref[0, k] = s_ref[...]


def _ssd_expander():
    r = lax.broadcasted_iota(jnp.int32, (2, 2 * LANES, SSD_INNER), 1) % LANES
    d = lax.broadcasted_iota(jnp.int32, (2, 2 * LANES, SSD_INNER), 0)
    h = lax.broadcasted_iota(jnp.int32, (2, 2 * LANES, SSD_INNER), 2) // SSD_HEADDIM
    return (r == (2 + d) * SSD_HEADS + h).astype(BF16)


def _ssd_scan(p, s0):
    b, l, _ = p["xs"].shape
    c = SSD_CHUNK * SSD_CHUNKS_PER_STEP
    nc = l // c
    specs = []
    for rev in (False, True):
        idx = (lambda n: nc - 1 - n) if rev else (lambda n: n)
        tok = lambda w, idx=idx: pl.BlockSpec((1, c, w), lambda bi, n: (bi, idx(n), 0))
        chan = lambda r, idx=idx: pl.BlockSpec((1, r, c), lambda bi, n: (bi, 0, idx(n)))
        specs.append([tok(SSD_INNER), tok(2 * SSD_BC), chan(SSD_BC), tok(LANES), chan(4 * SSD_HEADS)])
    st = pl.BlockSpec((1, 2 * SSD_GROUPS, SSD_STATE, HEADS_PER_GROUP * SSD_HEADDIM), lambda bi, n: (bi, 0, 0, 0))
    ex = _ssd_expander()
    ya, yb, sfin = pl.pallas_call(
        _ssd_kernel,
        grid=(b, nc),
        in_specs=specs[0] + specs[1][:4] + [_const_spec(ex.shape), st],
        out_specs=[specs[0][0], specs[1][0], st],
        out_shape=[jax.ShapeDtypeStruct((b, l, SSD_INNER), BF16), jax.ShapeDtypeStruct((b, l, SSD_INNER), BF16),
                   jax.ShapeDtypeStruct(s0.shape, F32)],
        scratch_shapes=[pltpu.VMEM((SSD_STATE, HEADS_PER_GROUP * SSD_HEADDIM), F32)] * (2 * SSD_GROUPS),
        compiler_params=_cparams(("parallel", "arbitrary")),
        name="ssd_scan",
    )(p["xs"], p["bc"], p["bT"], p["sd"], p["sdT"], p["xs"], p["bc"], p["bT"], p["sd"], ex, s0)
    return ya, yb, sfin


def _post0_kernel(of_ref, ob_ref, g_ref, yf_ref, yb_ref, xs_ref, z_ref, x_ref, mod_ref,
                  gn_ref, dsk_ref, sn_ref, wo_ref, out_ref):
    o = of_ref[0].astype(F32) + ob_ref[0].astype(F32)
    parts = []
    for hd in range(GLA_HEADS):
        oh = o[:, hd * GLA_DV:(hd + 1) * GLA_DV]
        parts.append(oh * lax.rsqrt(jnp.mean(oh * oh, axis=-1, keepdims=True) + EPS))
    gla = jnp.concatenate(parts, axis=1) * gn_ref[...] * _silu(g_ref[0].astype(F32))
    xs = xs_ref[0].astype(F32)
    y = (yf_ref[0].astype(F32) + yb_ref[0].astype(F32) + xs * dsk_ref[...]) * _silu(z_ref[0].astype(F32))
    gw = SSD_INNER // SSD_GROUPS
    parts = []
    for gi in range(SSD_GROUPS):
        yg = y[:, gi * gw:(gi + 1) * gw]
        parts.append(yg * lax.rsqrt(jnp.mean(yg * yg, axis=-1, keepdims=True) + EPS))
    ssd = jnp.concatenate(parts, axis=1) * sn_ref[...]
    res = _dot(gla.astype(BF16), wo_ref[:GLA_V, :]) + _dot(ssd.astype(BF16), wo_ref[GLA_V:, :])
    out_ref[0] = x_ref[0] + mod_ref[0, 2:3, :] * res


def _post0(p, of, ob, yf, yb, x, mod, wp, tm):
    b, l, _ = x.shape
    tm = min(tm, l)
    bm = mod.shape[0]
    mod_idx = (lambda bi, i: (bi, 0, 0)) if bm > 1 else (lambda bi, i: (0, 0, 0))
    tok = lambda w: pl.BlockSpec((1, tm, w), lambda bi, i: (bi, i, 0))
    return pl.pallas_call(
        _post0_kernel,
        grid=(b, l // tm),
        in_specs=[tok(GLA_V), tok(GLA_V), tok(GLA_V), tok(SSD_INNER), tok(SSD_INNER), tok(SSD_INNER),
                  tok(SSD_INNER), tok(D_MODEL), pl.BlockSpec((1, N_MOD, D_MODEL), mod_idx),
                  _const_spec((1, GLA_V)), _const_spec((1, SSD_INNER)), _const_spec((1, SSD_INNER)),
                  _const_spec(wp["w_out"].shape)],
        out_specs=tok(D_MODEL),
        out_shape=jax.ShapeDtypeStruct((b, l, D_MODEL), F32),
        compiler_params=_cparams(("parallel", "arbitrary")),
        name="post0",
    )(of, ob, p["g"], yf, yb, p["xs"], p["z"], x, mod, wp["gla_norm"], wp["d_skip"], wp["ssd_norm"], wp["w_out"])


def _ffn_kernel(x_ref, mod_ref, nw_ref, w1_ref, w3_ref, w2_ref, fn_ref, out_ref, *, final):
    x = x_ref[0]
    h = _norm_mod(x, nw_ref[...], mod_ref[0, 3:4, :], mod_ref[0, 4:5, :]).astype(BF16)
    half = D_FF // 2
    y = jnp.zeros_like(x)
    for j in range(2):
        cs = slice(j * half, (j + 1) * half)
        u = (_silu(_dot(h, w1_ref[0, :, cs])) * _dot(h, w3_ref[0, :, cs])).astype(BF16)
        y = y + _dot(u, w2_ref[0, cs, :])
    out = x + mod_ref[0, 5:6, :] * y
    if final:
        out = out * lax.rsqrt(jnp.mean(out * out, axis=-1, keepdims=True) + EPS) * fn_ref[...]
    out_ref[0] = out


def _ffn(x, mod, nw, w1, w3, w2, layer, final_norm, final, tm):
    b, l, _ = x.shape
    tm = min(tm, l)
    bm = mod.shape[0]
    mod_idx = (lambda bi, i: (bi, 0, 0)) if bm > 1 else (lambda bi, i: (0, 0, 0))
    tok = pl.BlockSpec((1, tm, D_MODEL), lambda bi, i: (bi, i, 0))
    wspec = lambda w: pl.BlockSpec((1,) + w.shape[1:], lambda bi, i: (layer, 0, 0), pipeline_mode=pl.Buffered(1))
    return pl.pallas_call(
        functools.partial(_ffn_kernel, final=final),
        grid=(b, l // tm),
        in_specs=[tok, pl.BlockSpec((1, N_MOD, D_MODEL), mod_idx), _const_spec((1, D_MODEL)),
                  wspec(w1), wspec(w3), wspec(w2), _const_spec((1, D_MODEL))],
        out_specs=tok,
        out_shape=jax.ShapeDtypeStruct((b, l, D_MODEL), F32),
        compiler_params=_cparams(("parallel", "arbitrary")),
        name="ffn_final" if final else "ffn",
    )(x, mod, nw, w1, w3, w2, final_norm)


def _inproj1_kernel(*refs, tm, ncol):
    if ncol:
        x_ref, xp_ref, xn_ref, perm_ref = refs[:4]
        refs = refs[4:]
    else:
        x_ref, xp_ref, xn_ref = refs[:3]
        refs = refs[3:]
    (mod_ref, nw_ref, wup_ref, cw_ref, cb_ref, bdq_ref, bdk_ref, bdkT_ref, bdv_ref, wifq_ref, wifk_ref, wifv_ref,
     bif_ref, q_ref, kT_ref, v_ref, xc_ref, z_ref, gc_ref, gT_ref, hext, xm_scr) = refs
    i = pl.program_id(1)
    first = i == 0
    last = i == pl.num_programs(1) - 1
    shift = mod_ref[0, 0:1, :]
    scale = mod_ref[0, 1:2, :]
    nw = nw_ref[...]
    if ncol:
        hext[0:HALO, :] = _norm_mod(xp_ref[0, :, SUBLANES - 1, :], nw, shift, scale).astype(BF16)
        hn = _norm_mod(x_ref[0].reshape(tm, D_MODEL), nw, shift, scale).astype(BF16)
        hext[HALO:HALO + tm, :] = _dot(perm_ref[...], hn).astype(BF16)
        hext[HALO + tm:2 * HALO + tm, :] = _norm_mod(xn_ref[0, :, 0, :], nw, shift, scale).astype(BF16)
    else:
        hext[0:HALO, :] = _norm_mod(xp_ref[0], nw, shift, scale).astype(BF16)
        hext[HALO:HALO + tm, :] = _norm_mod(x_ref[0], nw, shift, scale).astype(BF16)
        hext[HALO + tm:2 * HALO + tm, :] = _norm_mod(xn_ref[0], nw, shift, scale).astype(BF16)
    h = hext[HALO:HALO + tm, :]

    xm_scr[...] = _dot(hext[...], wup_ref[:, :MLSTM_INNER])
    xc = _silu(_conv3(xm_scr, tm, cw_ref, cb_ref, first, last))
    xc_ref[0] = xc.astype(BF16)

    gates = jnp.zeros((tm, LANES), F32) + bif_ref[...]
    for t in range(N_BD_TILES):
        cs = slice(t * BD_TILE, (t + 1) * BD_TILE)
        xct = xc[:, cs].astype(BF16)
        xmt = xm_scr[HALO:HALO + tm, cs].astype(BF16)
        qt = _dot(xct, bdq_ref[t])
        kt = _dot(xct, bdk_ref[t])
        vt = _dot(xmt, bdv_ref[t])
        q_ref[0, :, cs] = qt.astype(BF16)
        v_ref[0, :, cs] = vt.astype(BF16)
        kT_ref[0, cs, :] = (_dot_nt(bdkT_ref[t], xct) * (MLSTM_DH ** -0.5)).astype(BF16)
        gates = gates + _dot(qt.astype(BF16), wifq_ref[cs, :]) + _dot(kt.astype(BF16), wifk_ref[cs, :]) \
            + _dot(vt.astype(BF16), wifv_ref[cs, :])

    gt = jnp.transpose(gates)[:4 * MLSTM_HEADS]
    r = lax.broadcasted_iota(jnp.int32, gt.shape, 0)
    lf = _log_sigmoid(gt)
    cum = jnp.where(r < 2 * MLSTM_HEADS, _seg_cumsum(lf, MLSTM_CHUNK, 1, False),
                    _seg_cumsum(lf, MLSTM_CHUNK, 1, True))
    packed = jnp.where(r % (2 * MLSTM_HEADS) < MLSTM_HEADS, gt, cum)
    a = gt - pltpu.roll(cum, 4 * MLSTM_HEADS - MLSTM_HEADS, 0)
    amax = jnp.where(r < 2 * MLSTM_HEADS, _seg_scan(a, MLSTM_CHUNK, 1, False, use_max=True),
                     _seg_scan(a, MLSTM_CHUNK, 1, True, use_max=True))
    full = jnp.concatenate([packed, amax], axis=0)
    gT_ref[0] = full
    gc_ref[0] = jnp.transpose(jnp.concatenate([full, jnp.zeros((LANES - MLSTM_GATE_ROWS, tm), F32)], axis=0))

    z_ref[0] = _dot(h, wup_ref[:, MLSTM_INNER:]).astype(BF16)


def _inproj1(x, mod, nw, wp, tm, colmajor):
    b, l, _ = x.shape
    tm = min(tm, l)
    nt = l // tm
    bm = mod.shape[0]
    mod_idx = (lambda bi, i: (bi, 0, 0)) if bm > 1 else (lambda bi, i: (0, 0, 0))
    if colmajor:
        rows = l // GRID_W
        ncol = tm // rows
        assert ncol % SUBLANES == 0 and rows % HALO == 0
        xv = x.reshape(b, rows, GRID_W, D_MODEL)
        rb = rows // HALO
        cb = ncol // SUBLANES
        ncb = GRID_W // SUBLANES
        x_spec = pl.BlockSpec((1, rows, ncol, D_MODEL), lambda bi, i: (bi, 0, i, 0))
        xp_spec = pl.BlockSpec((1, HALO, SUBLANES, D_MODEL),
                               lambda bi, i: (bi, rb - 1, jnp.maximum(i * cb - 1, 0), 0))
        xn_spec = pl.BlockSpec((1, HALO, SUBLANES, D_MODEL),
                               lambda bi, i: (bi, 0, jnp.minimum((i + 1) * cb, ncb - 1), 0))
        dst = lax.broadcasted_iota(jnp.int32, (tm, tm), 0)
        src = lax.broadcasted_iota(jnp.int32, (tm, tm), 1)
        perm = (src == (dst % rows) * ncol + dst // rows).astype(BF16)
        lead_specs = [x_spec, xp_spec, xn_spec, _const_spec((tm, tm))]
        lead_args = [xv, xv, xv, perm]
    else:
        ncol = 0
        xv = x
        hb = tm // HALO
        nhb = l // HALO
        x_spec = pl.BlockSpec((1, tm, D_MODEL), lambda bi, i: (bi, i, 0))
        xp_spec = pl.BlockSpec((1, HALO, D_MODEL), lambda bi, i: (bi, jnp.maximum(i * hb - 1, 0), 0))
        xn_spec = pl.BlockSpec((1, HALO, D_MODEL), lambda bi, i: (bi, jnp.minimum((i + 1) * hb, nhb - 1), 0))
        lead_specs = [x_spec, xp_spec, xn_spec]
        lead_args = [xv, xv, xv]
    tok = lambda w: pl.BlockSpec((1, tm, w), lambda bi, i: (bi, i, 0))
    names = ["w_up", "conv_w", "conv_b", "bdq", "bdk", "bdkT", "bdv", "wif_q", "wif_k", "wif_v", "b_if"]
    q, kT, v, xc, z, gc, gT = pl.pallas_call(
        functools.partial(_inproj1_kernel, tm=tm, ncol=ncol),
        grid=(b, nt),
        in_specs=lead_specs + [pl.BlockSpec((1, N_MOD, D_MODEL), mod_idx), _const_spec((1, D_MODEL))]
        + [_const_spec(wp[n].shape) for n in names],
        out_specs=[tok(MLSTM_INNER), pl.BlockSpec((1, MLSTM_INNER, tm), lambda bi, i: (bi, 0, i)),
                   tok(MLSTM_INNER), tok(MLSTM_INNER), tok(MLSTM_INNER), tok(LANES),
                   pl.BlockSpec((1, MLSTM_GATE_ROWS, tm), lambda bi, i: (bi, 0, i))],
        out_shape=[jax.ShapeDtypeStruct((b, l, MLSTM_INNER), BF16), jax.ShapeDtypeStruct((b, MLSTM_INNER, l), BF16),
                   jax.ShapeDtypeStruct((b, l, MLSTM_INNER), BF16), jax.ShapeDtypeStruct((b, l, MLSTM_INNER), BF16),
                   jax.ShapeDtypeStruct((b, l, MLSTM_INNER), BF16), jax.ShapeDtypeStruct((b, l, LANES), F32),
                   jax.ShapeDtypeStruct((b, MLSTM_GATE_ROWS, l), F32)],
        scratch_shapes=[pltpu.VMEM((tm + 2 * HALO, D_MODEL), BF16),
                        pltpu.VMEM((tm + 2 * HALO, MLSTM_INNER), F32)],
        compiler_params=_cparams(("parallel", "arbitrary")),
        name="inproj1",
    )(*lead_args, mod, nw, *[wp[n] for n in names])
    return dict(q=q, kT=kT, v=v, xc=xc, z=z, gc=gc, gT=gT)


def _mlstm_head(q, kT, v, gc, gT, c_scr, cb_scr, n_scr, m_scr, d, hd, reverse):
    c = MLSTM_CHUNK
    dh = MLSTM_DH
    idx = d * MLSTM_HEADS + hd
    ri = d * 2 * MLSTM_HEADS + hd
    rb = ri + MLSTM_HEADS
    rm = 4 * MLSTM_HEADS + ri
    c_scr, cb_scr, n_scr, m_scr = c_scr[idx], cb_scr[idx], n_scr[idx], m_scr[idx]
    bi = gc[:, rb:rb + 1]
    am_i = gc[:, rm:rm + 1]
    bj = gT[rb:rb + 1, :]
    a_j = gT[ri:ri + 1, :] - bj
    am_j = gT[rm:rm + 1, :]
    m0 = m_scr[0:1, 0:1]
    big_m = jnp.maximum(m0, am_i)
    ii = lax.broadcasted_iota(jnp.int32, (c, c), 0)
    jj = lax.broadcasted_iota(jnp.int32, (c, c), 1)
    causal = (jj >= ii) if reverse else (jj <= ii)
    s = _dot(q, kT) * jnp.where(causal, jnp.exp(a_j - big_m), 0.0)
    w_inter = jnp.exp(m0 - big_m)
    qf = q.astype(F32)
    den = jnp.sum(s, axis=-1, keepdims=True) \
        + w_inter * jnp.sum(qf * n_scr[0:1, :], axis=-1, keepdims=True)
    num = _dot(s.astype(BF16), v) + _dot((qf * w_inter).astype(BF16), cb_scr[...])
    h = num * (1.0 / jnp.maximum(jnp.abs(den), jnp.exp(-(bi + big_m))))

    last = 0 if reverse else c - 1
    m_inner = jnp.maximum(m0, am_j[:, last:last + 1])
    kw = kT.astype(F32) * jnp.exp(a_j - m_inner)
    decay = jnp.exp(m0 - m_inner)
    kwb = kw.astype(BF16)
    c_new = decay * c_scr[...] + _dot(kwb, v)
    c_scr[...] = c_new
    cb_scr[...] = c_new.astype(BF16)
    n_scr[...] = decay * n_scr[...] + _dot_nt(jnp.ones((SUBLANES, c), BF16), kwb)
    m_scr[...] = jnp.broadcast_to(bj[:, last:last + 1] + m_inner, m_scr.shape)
    return h


def _mlstm_kernel(*refs, has_init, emit_final):
    ins = refs[:10]
    pos = 10
    if has_init:
        c0_ref, n0_ref, m0_ref = refs[pos:pos + 3]
        pos += 3
    hf_ref, hb_ref = refs[pos:pos + 2]
    pos += 2
    if emit_final:
        cfin_ref, nfin_ref, mfin_ref = refs[pos:pos + 3]
        pos += 3
    nst = 2 * MLSTM_HEADS
    c_scr, cb_scr, n_scr, m_scr = (refs[pos + k * nst:pos + (k + 1) * nst] for k in range(4))
    n = pl.program_id(1)

    @pl.when(n == 0)
    def _():
        for k in range(nst):
            if has_init:
                c_scr[k][...] = c0_ref[0, k]
                cb_scr[k][...] = c0_ref[0, k].astype(BF16)
                n_scr[k][...] = n0_ref[0, k]
                m_scr[k][...] = m0_ref[0, k]
            else:
                c_scr[k][...] = jnp.zeros_like(c_scr[k])
                cb_scr[k][...] = jnp.zeros_like(cb_scr[k])
                n_scr[k][...] = jnp.zeros_like(n_scr[k])
                m_scr[k][...] = jnp.full(m_scr[k].shape, M_INIT, F32)

    for d, h_ref in enumerate((hf_ref, hb_ref)):
        q_ref, kT_ref, v_ref, gc_ref, gT_ref = ins[5 * d:5 * d + 5]
        gc = gc_ref[0]
        gT = gT_ref[0]
        for hd in range(MLSTM_HEADS):
            cs = slice(hd * MLSTM_DH, (hd + 1) * MLSTM_DH)
            h = _mlstm_head(q_ref[0, :, cs], kT_ref[0, cs, :], v_ref[0, :, cs], gc, gT,
                            c_scr, cb_scr, n_scr, m_scr, d, hd, d == 1)
            h_ref[0, :, cs] = h.astype(BF16)

    if emit_final:
        @pl.when(n == pl.num_programs(1) - 1)
        def _():
            for k in range(nst):
                cfin_ref[0, k] = c_scr[k][...]
                nfin_ref[0, k] = n_scr[k][...]
                mfin_ref[0, k] = m_scr[k][...]


def _mlstm_scan(p, init, emit_final):
    b, l, _ = p["q"].shape
    c = MLSTM_CHUNK
    nc = l // c
    dh = MLSTM_DH
    nst = 2 * MLSTM_HEADS
    specs = []
    for rev in (False, True):
        idx = (lambda n: nc - 1 - n) if rev else (lambda n: n)
        specs.append([
            pl.BlockSpec((1, c, MLSTM_INNER), lambda bi, n, idx=idx: (bi, idx(n), 0)),
            pl.BlockSpec((1, MLSTM_INNER, c), lambda bi, n, idx=idx: (bi, 0, idx(n))),
            pl.BlockSpec((1, c, MLSTM_INNER), lambda bi, n, idx=idx: (bi, idx(n), 0)),
            pl.BlockSpec((1, c, LANES), lambda bi, n, idx=idx: (bi, idx(n), 0)),
            pl.BlockSpec((1, MLSTM_GATE_ROWS, c), lambda bi, n, idx=idx: (bi, 0, idx(n))),
        ])
    st_shapes = [(b, nst, dh, dh), (b, nst, SUBLANES, dh), (b, nst, SUBLANES, LANES)]
    st_index = lambda bi, n: (bi, 0, 0, 0)
    args = (p["q"], p["kT"], p["v"], p["gc"], p["gT"])
    operands = [*args, *args]
    in_specs = specs[0] + specs[1]
    if init is not None:
        operands += list(init)
        in_specs += [pl.BlockSpec((1,) + s[1:], st_index, pipeline_mode=pl.Buffered(1)) for s in st_shapes]
    out_specs = [specs[0][0], specs[1][0]]
    out_shape = [jax.ShapeDtypeStruct((b, l, MLSTM_INNER), BF16), jax.ShapeDtypeStruct((b, l, MLSTM_INNER), BF16)]
    if emit_final:
        out_specs += [pl.BlockSpec((1,) + s[1:], st_index) for s in st_shapes]
        out_shape += [jax.ShapeDtypeStruct(s, F32) for s in st_shapes]
    res = pl.pallas_call(
        functools.partial(_mlstm_kernel, has_init=init is not None, emit_final=emit_final),
        grid=(b, nc),
        in_specs=in_specs,
        out_specs=out_specs,
        out_shape=out_shape,
        scratch_shapes=[pltpu.VMEM((dh, dh), F32)] * nst + [pltpu.VMEM((dh, dh), BF16)] * nst
        + [pltpu.VMEM((SUBLANES, dh), F32)] * nst + [pltpu.VMEM((SUBLANES, LANES), F32)] * nst,
        compiler_params=_cparams(("parallel", "arbitrary")),
        name="mlstm_scan",
    )(*operands)
    return res[0], res[1], tuple(res[2:])


def _post1_kernel(hf_ref, hb_ref, xc_ref, z_ref, x_ref, mod_ref, mh_ref, sk_ref, wd_ref, out_ref, *, tm, ncol):
    hs = hf_ref[0].astype(F32) + hb_ref[0].astype(F32)
    parts = []
    for hd in range(MLSTM_HEADS):
        hh = hs[:, hd * MLSTM_DH:(hd + 1) * MLSTM_DH]
        mu = jnp.mean(hh, axis=-1, keepdims=True)
        cen = hh - mu
        parts.append(cen * lax.rsqrt(jnp.mean(cen * cen, axis=-1, keepdims=True) + EPS))
    feat = (jnp.concatenate(parts, axis=1) * mh_ref[...] + sk_ref[...] * xc_ref[0].astype(F32)) \
        * _silu(z_ref[0].astype(F32))
    res = mod_ref[0, 2:3, :] * _dot(feat.astype(BF16), wd_ref[...])
    if ncol:
        rows = tm // ncol
        for j in range(ncol):
            out_ref[0, :, j, :] = x_ref[0, :, j, :] + res[j * rows:(j + 1) * rows, :]
    else:
        out_ref[0] = x_ref[0] + res


def _post1(p, hf, hb, x, mod, wp, tm, colmajor):
    b, l, _ = x.shape
    tm = min(tm, l)
    bm = mod.shape[0]
    mod_idx = (lambda bi, i: (bi, 0, 0)) if bm > 1 else (lambda bi, i: (0, 0, 0))
    tok = lambda w: pl.BlockSpec((1, tm, w), lambda bi, i: (bi, i, 0))
    if colmajor:
        rows = l // GRID_W
        ncol = tm // rows
        xv = x.reshape(b, rows, GRID_W, D_MODEL)
        x_spec = pl.BlockSpec((1, rows, ncol, D_MODEL), lambda bi, i: (bi, 0, i, 0))
    else:
        ncol = 0
        xv = x
        x_spec = tok(D_MODEL)
    out = pl.pallas_call(
        functools.partial(_post1_kernel, tm=tm, ncol=ncol),
        grid=(b, l // tm),
        in_specs=[tok(MLSTM_INNER), tok(MLSTM_INNER), tok(MLSTM_INNER), tok(MLSTM_INNER), x_spec,
                  pl.BlockSpec((1, N_MOD, D_MODEL), mod_idx),
                  _const_spec((1, MLSTM_INNER)), _const_spec((1, MLSTM_INNER)), _const_spec(wp["w_down"].shape)],
        out_specs=x_spec,
        out_shape=jax.ShapeDtypeStruct(xv.shape, F32),
        compiler_params=_cparams(("parallel", "arbitrary")),
        name="post1",
    )(hf, hb, p["xc"], p["z"], xv, mod, wp["mh_norm"], wp["skip"], wp["w_down"])
    return out.reshape(b, l, D_MODEL)


def _prep_even(w_in, w_gate, b_gate, gla_norm, conv_w, conv_b, a_log, dt_bias, d_skip, ssd_norm, w_out):
    sizes = (GLA_QK, GLA_QK, GLA_V, GLA_V, 2 * GLA_GATE_RANK, SSD_INNER, SSD_XBC, 2 * SSD_HEADS)
    offs = [0]
    for s in sizes:
        offs.append(offs[-1] + s)
    w16 = w_in.astype(BF16)
    w_main = jnp.concatenate([w16[:, offs[0]:offs[4]], w16[:, offs[5]:offs[7]]], axis=1)
    pad = jnp.zeros((D_MODEL, LANES - 2 * GLA_GATE_RANK - 2 * SSD_HEADS), BF16)
    w_small = jnp.concatenate([w16[:, offs[4]:offs[5]], w16[:, offs[7]:offs[8]], pad], axis=1)
    zero = jnp.zeros((GLA_GATE_RANK, GLA_QK), w_gate.dtype)
    wg = jnp.concatenate([jnp.concatenate([w_gate[0], zero], axis=1),
                          jnp.concatenate([zero, w_gate[1]], axis=1)], axis=0).astype(BF16)
    return dict(
        w_main=w_main, w_small=w_small, w_dtT=jnp.transpose(w_in[:, offs[7]:offs[8]]).astype(BF16),
        w_gate=wg, b_gate=b_gate.reshape(1, 2 * GLA_QK),
        conv_w=conv_w, conv_b=conv_b.reshape(1, SSD_XBC),
        dt_bias=dt_bias.reshape(1, 2 * SSD_HEADS), a_log=a_log.reshape(1, 2 * SSD_HEADS),
        dt_biasT=dt_bias.reshape(2 * SSD_HEADS, 1), a_logT=a_log.reshape(2 * SSD_HEADS, 1),
        gla_norm=jnp.tile(gla_norm, GLA_HEADS).reshape(1, GLA_V),
        d_skip=jnp.repeat(d_skip, SSD_HEADDIM).reshape(1, SSD_INNER),
        ssd_norm=ssd_norm.reshape(1, SSD_INNER),
        w_out=w_out.astype(BF16),
    )


def _block_diag_tiles(w, transposed=False):
    per = BD_TILE // QKV_BLOCK
    w4 = w.reshape(N_BD_TILES, per, QKV_BLOCK, QKV_BLOCK)
    blk = lax.broadcasted_iota(jnp.int32, (BD_TILE, BD_TILE), 0) // QKV_BLOCK
    same_block = blk == jnp.transpose(blk)
    if transposed:
        cols = jnp.transpose(w4, (0, 3, 1, 2)).reshape(N_BD_TILES, QKV_BLOCK, BD_TILE)
        full = jnp.tile(cols, (1, per, 1))
    else:
        rows = w4.reshape(N_BD_TILES, BD_TILE, QKV_BLOCK)
        full = jnp.tile(rows, (1, 1, per))
    return jnp.where(same_block, full, 0.0).astype(BF16)


def _prep_odd(w_up, conv_w, conv_b, wq, wk, wv, w_if, b_if, mh_norm, skip, w_down):
    ng = 4 * MLSTM_HEADS
    wif = jnp.transpose(w_if, (1, 0, 2)).reshape(3 * MLSTM_INNER, ng)
    wif = jnp.concatenate([wif, jnp.zeros((3 * MLSTM_INNER, LANES - ng), w_if.dtype)], axis=1).astype(BF16)
    bif = jnp.concatenate([b_if.reshape(1, ng), jnp.zeros((1, LANES - ng), b_if.dtype)], axis=1)
    return dict(
        w_up=w_up.astype(BF16), conv_w=conv_w, conv_b=conv_b.reshape(1, MLSTM_INNER),
        bdq=_block_diag_tiles(wq), bdk=_block_diag_tiles(wk),
        bdkT=_block_diag_tiles(wk, transposed=True), bdv=_block_diag_tiles(wv),
        wif_q=wif[:MLSTM_INNER], wif_k=wif[MLSTM_INNER:2 * MLSTM_INNER], wif_v=wif[2 * MLSTM_INNER:], b_if=bif,
        mh_norm=mh_norm.reshape(1, MLSTM_INNER), skip=skip.reshape(1, MLSTM_INNER),
        w_down=w_down.astype(BF16),
    )


TOKEN_TILE = 512


def kernel(x, c, ctx, c_ctx, w_mod, b_mod, norm_mix, norm_ffn, ffn_w1, ffn_w3, ffn_w2, a_w_in, a_gla_w_gate,
           a_gla_b_gate, a_gla_norm, a_ssd_conv_w, a_ssd_conv_b, a_ssd_A_log, a_ssd_dt_bias, a_ssd_D, a_ssd_norm,
           a_w_out, b_w_up, b_conv_w, b_conv_b, b_wq, b_wk, b_wv, b_w_if, b_b_if, b_mh_norm, b_skip, b_w_down,
           final_norm):
    tile = TOKEN_TILE
    bsz = x.shape[0]
    depth = w_mod.shape[0]
    pad_rows = (-(bsz + 1)) % 8
    cc = jnp.concatenate([c, c_ctx[None, :], jnp.zeros((pad_rows, D_MODEL), c.dtype)], axis=0)
    mods = _modulation(cc, w_mod, b_mod).reshape(depth, bsz + 1 + pad_rows, N_MOD, D_MODEL)
    fnorm = final_norm.reshape(1, D_MODEL)
    w1, w3, w2 = ffn_w1.astype(BF16), ffn_w3.astype(BF16), ffn_w2.astype(BF16)
    for layer in range(depth):
        last = layer == depth - 1
        j = layer // 2
        mx = mods[layer, :bsz]
        mc = mods[layer, bsz:bsz + 1]
        nmix = norm_mix[layer].reshape(1, D_MODEL)
        nffn = norm_ffn[layer].reshape(1, D_MODEL)
        if layer % 2 == 0:
            wp = _prep_even(a_w_in[j], a_gla_w_gate[j], a_gla_b_gate[j], a_gla_norm[j], a_ssd_conv_w[j],
                            a_ssd_conv_b[j], a_ssd_A_log[j], a_ssd_dt_bias[j], a_ssd_D[j], a_ssd_norm[j],
                            a_w_out[j])
            pc = _inproj0(ctx, mc, nmix, wp, tile)
            gla0 = jnp.zeros((bsz, 2 * GLA_HEADS, GLA_DK, GLA_DV), F32)
            ssd0 = jnp.zeros((bsz, 2 * SSD_GROUPS, SSD_STATE, HEADS_PER_GROUP * SSD_HEADDIM), F32)
            ofc, obc, gla_st = _gla_scan(pc, gla0, GLA_INTRA_TILE)
            yfc, ybc, ssd_st = _ssd_scan(pc, ssd0)
            px = _inproj0(x, mx, nmix, wp, tile)
            ofx, obx, _ = _gla_scan(px, gla_st, GLA_INTRA_TILE)
            yfx, ybx, _ = _ssd_scan(px, ssd_st)
            x = _post0(px, ofx, obx, yfx, ybx, x, mx, wp, tile)
            if not last:
                ctx = _post0(pc, ofc, obc, yfc, ybc, ctx, mc, wp, tile)
        else:
            wp = _prep_odd(b_w_up[j], b_conv_w[j], b_conv_b[j], b_wq[j], b_wk[j], b_wv[j], b_w_if[j], b_b_if[j],
                           b_mh_norm[j], b_skip[j], b_w_down[j])
            pc = _inproj1(ctx, mc, nmix, wp, tile, colmajor=False)
            hfc, hbc, state = _mlstm_scan(pc, None, True)
            px = _inproj1(x, mx, nmix, wp, tile, colmajor=True)
            hfx, hbx, _ = _mlstm_scan(px, state, False)
            x = _post1(px, hfx, hbx, x, mx, wp, tile, colmajor=True)
            if not last:
                ctx = _post1(pc, hfc, hbc, ctx, mc, wp, tile, colmajor=False)
        x = _ffn(x, mx, nffn, w1, w3, w2, layer, fnorm, last, tile)
        if not last:
            ctx = _ffn(ctx, mc, nffn, w1, w3, w2, layer, fnorm, False, tile)
    return x
```

```python
import functools

import jax
import jax.numpy as jnp
from jax import lax
from jax.experimental import pallas as pl
from jax.experimental.pallas import tpu as pltpu

F32 = jnp.float32
BF16 = jnp.bfloat16

D_MODEL = 1024
GRID_W = 64
EPS = 1e-6
LOG2_E = 1.4426950408889634
M_INIT = -1e30
N_MOD = 6

GLA_HEADS = 4
GLA_DK = 128
GLA_DV = 256
GLA_QK = GLA_HEADS * GLA_DK
GLA_V = GLA_HEADS * GLA_DV
GLA_GATE_RANK = 16
GLA_GATE_TAU = 16.0

SSD_HEADDIM = 64
SSD_HEADS = 16
SSD_STATE = 128
SSD_GROUPS = 2
SSD_INNER = SSD_HEADS * SSD_HEADDIM
SSD_BC = SSD_GROUPS * SSD_STATE
SSD_XBC = SSD_INNER + 2 * SSD_BC
HEADS_PER_GROUP = SSD_HEADS // SSD_GROUPS

MLSTM_INNER = 2 * D_MODEL
MLSTM_HEADS = 4
MLSTM_DH = MLSTM_INNER // MLSTM_HEADS
QKV_BLOCK = 4
BD_TILE = 256
N_BD_TILES = MLSTM_INNER // BD_TILE
MLSTM_GATE_ROWS = 32

D_FF = 2816

GLA_CHUNK = 64
GLA_CHUNKS_PER_STEP = 4
GLA_INTRA_TILE = 256
SSD_CHUNK = 128
SSD_CHUNKS_PER_STEP = 2
MLSTM_CHUNK = 256
HALO = 16
LANES = 128
SUBLANES = 8
TAIL_ROWS = 128

VMEM_LIMIT = 56 * 1024 * 1024


def _cparams(sem):
    return pltpu.CompilerParams(dimension_semantics=sem, vmem_limit_bytes=VMEM_LIMIT)


def _const_spec(shape):
    nd = len(shape)
    return pl.BlockSpec(shape, lambda *_: (0,) * nd, pipeline_mode=pl.Buffered(1))


def _softplus(x):
    return jnp.maximum(x, 0.0) + jnp.log(1.0 + jnp.exp(-jnp.abs(x)))


def _log_sigmoid(x):
    return -_softplus(-x)


def _silu(x):
    return x / (1.0 + jnp.exp(-x))


def _seg_scan(x, seg, axis, reverse, use_max=False):
    n = x.shape[axis]
    idx = lax.broadcasted_iota(jnp.int32, x.shape, axis) % seg
    s = 1
    while s < seg:
        if reverse:
            shifted = pltpu.roll(x, n - s, axis)
            keep = idx < seg - s
        else:
            shifted = pltpu.roll(x, s, axis)
            keep = idx >= s
        if use_max:
            x = jnp.where(keep, jnp.maximum(x, shifted), x)
        else:
            x = x + jnp.where(keep, shifted, 0.0)
        s *= 2
    return x


def _seg_cumsum(x, seg, axis, reverse):
    return _seg_scan(x, seg, axis, reverse)


def _chunk_cumsum_mxu(x, chunk, reverse):
    rows = x.shape[0]
    hi = x.astype(BF16)
    lo = (x - hi.astype(F32)).astype(BF16)
    i = lax.broadcasted_iota(jnp.int32, (chunk, 2 * chunk), 0)
    j = lax.broadcasted_iota(jnp.int32, (chunk, 2 * chunk), 1) % chunk
    tri = ((j >= i) if reverse else (j <= i)).astype(BF16)
    parts = []
    for c0 in range(0, rows, chunk):
        parts.append(_dot(tri, jnp.concatenate([hi[c0:c0 + chunk], lo[c0:c0 + chunk]], axis=0)))
    return jnp.concatenate(parts, axis=0)


def _norm_mod(xv, nw, shift, scale):
    ms = jnp.mean(xv * xv, axis=-1, keepdims=True)
    return (xv * lax.rsqrt(ms + EPS) * nw) * (1.0 + scale) + shift


def _dot(a, b):
    return jnp.dot(a, b, preferred_element_type=F32)


def _dot_nt(a, b):
    return lax.dot_general(a, b, (((1,), (1,)), ((), ())), preferred_element_type=F32)


def _dot_tn(a, b):
    return lax.dot_general(a, b, (((0,), (0,)), ((), ())), preferred_element_type=F32)


def _mod_kernel(c_ref, w_ref, b_ref, o_ref):
    a = _silu(c_ref[...]).astype(BF16)
    o_ref[0] = _dot(a, w_ref[0].astype(BF16)) + b_ref[0]


def _modulation(cc, w_mod, b_mod):
    depth = w_mod.shape[0]
    n = w_mod.shape[2]
    tn = n // 4
    rows = cc.shape[0]
    return pl.pallas_call(
        _mod_kernel,
        grid=(depth, n // tn),
        in_specs=[
            pl.BlockSpec((rows, D_MODEL), lambda l, j: (0, 0)),
            pl.BlockSpec((1, D_MODEL, tn), lambda l, j: (l, 0, j)),
            pl.BlockSpec((1, 1, tn), lambda l, j: (l, 0, j)),
        ],
        out_specs=pl.BlockSpec((1, rows, tn), lambda l, j: (l, 0, j)),
        out_shape=jax.ShapeDtypeStruct((depth, rows, n), F32),
        compiler_params=_cparams(("arbitrary", "arbitrary")),
        name="modulation",
    )(cc, w_mod, b_mod.reshape(depth, 1, n))


def _conv_pad(scr, tm, first, last):
    scr[HALO - 1:HALO, :] = scr[HALO - 1:HALO, :] * jnp.where(first, 0.0, 1.0)
    scr[HALO + tm:HALO + tm + 1, :] = scr[HALO + tm:HALO + tm + 1, :] * jnp.where(last, 0.0, 1.0)


def _conv_rows(scr, r0, n, cw_ref, cb_ref):
    prev = scr[HALO - 1 + r0:HALO - 1 + r0 + n, :]
    cur = scr[HALO + r0:HALO + r0 + n, :]
    nxt = scr[HALO + 1 + r0:HALO + 1 + r0 + n, :]
    return cw_ref[0:1, :] * prev + cw_ref[1:2, :] * cur + cw_ref[2:3, :] * nxt + cb_ref[...]


def _conv3(scr, tm, cw_ref, cb_ref, first, last):
    _conv_pad(scr, tm, first, last)
    return _conv_rows(scr, 0, tm, cw_ref, cb_ref)


def _inproj0_kernel(x_ref, xp_ref, xn_ref, mod_ref, nw_ref, wmain_ref, wsmall_ref, wdtT_ref, wg_ref, bg_ref,
                    cw_ref, cb_ref, dtb_ref, alog_ref, dtbT_ref, alogT_ref,
                    q_ref, k_ref, v_ref, g_ref, z_ref, gcum_ref, xs_ref, bc_ref, bT_ref, sd_ref, sdT_ref,
                    hext, xbc_scr, *, tm):
    i = pl.program_id(1)
    first = i == 0
    last = i == pl.num_programs(1) - 1
    shift = mod_ref[0, 0:1, :]
    scale = mod_ref[0, 1:2, :]
    nw = nw_ref[...]
    hext[0:HALO, :] = _norm_mod(xp_ref[0], nw, shift, scale).astype(BF16)
    hext[HALO:HALO + tm, :] = _norm_mod(x_ref[0], nw, shift, scale).astype(BF16)
    hext[HALO + tm:2 * HALO + tm, :] = _norm_mod(xn_ref[0], nw, shift, scale).astype(BF16)

    o_k = GLA_QK
    o_v = o_k + GLA_QK
    o_g = o_v + GLA_V
    o_z = o_g + GLA_V
    o_x = o_z + SSD_INNER
    nd = 2 * SSD_HEADS
    h = hext[HALO:HALO + tm, :]

    xbc_scr[...] = _dot(hext[...], wmain_ref[:, o_x:o_x + SSD_XBC])
    _conv_pad(xbc_scr, tm, first, last)
    small = _dot(h, wsmall_ref[...])
    dt_raw = _dot_nt(wdtT_ref[...], h)
    logits = _dot(small[:, :2 * GLA_GATE_RANK].astype(BF16), wg_ref[...]) + bg_ref[...]

    def tail(j):
        rows = slice(j * TAIL_ROWS, (j + 1) * TAIL_ROWS)
        lg = logits[rows]
        la = (jnp.minimum(lg, 0.0) * LOG2_E - jnp.log2(1.0 + jnp.exp2(jnp.abs(lg) * (-LOG2_E)))) \
            * (1.0 / GLA_GATE_TAU)
        gcum_ref[0, rows, :GLA_QK] = _chunk_cumsum_mxu(la[:, :GLA_QK], GLA_CHUNK, False)
        gcum_ref[0, rows, GLA_QK:] = _chunk_cumsum_mxu(la[:, GLA_QK:], GLA_CHUNK, True)
        dtT = _softplus(dt_raw[:, rows] + dtbT_ref[...])
        laT = dtT * (-LOG2_E * jnp.exp(alogT_ref[...]))
        cumT = jnp.concatenate([_seg_cumsum(laT[:SSD_HEADS], SSD_CHUNK, 1, False),
                                _seg_cumsum(laT[SSD_HEADS:], SSD_CHUNK, 1, True)], axis=0)
        packed = jnp.concatenate([dtT, cumT, jnp.zeros((LANES - 2 * nd, TAIL_ROWS), F32)], axis=0)
        sdT_ref[0, :, rows] = packed[:2 * nd]
        sd_ref[0, rows, :] = jnp.transpose(packed)
        y = _silu(_conv_rows(xbc_scr, j * TAIL_ROWS, TAIL_ROWS, cw_ref, cb_ref))
        xs_ref[0, rows, :] = y[:, :SSD_INNER].astype(BF16)
        bc_ref[0, rows, :] = y[:, SSD_INNER:].astype(BF16)
        bT_ref[0, :, rows] = jnp.transpose(y[:, SSD_INNER:SSD_INNER + SSD_BC]).astype(BF16)

    wide = [(q_ref, 0, o_k, GLA_DK ** -0.5), (k_ref, o_k, o_v, None), (v_ref, o_v, o_g, None),
            (g_ref, o_g, o_z, None), (z_ref, o_z, o_x, None)]
    n_tail = tm // TAIL_ROWS
    done = 0
    for idx, (ref, c0, c1, scl) in enumerate(wide):
        res = _dot(h, wmain_ref[:, c0:c1])
        ref[0] = (res if scl is None else res * scl).astype(BF16)
        upto = (n_tail * (idx + 1)) // (len(wide) - 1) if idx < len(wide) - 1 else n_tail
        for j in range(done, min(upto, n_tail)):
            tail(j)
        done = max(done, min(upto, n_tail))


def _inproj0(x, mod, nw, wp, tm):
    b, l, _ = x.shape
    tm = min(tm, l)
    nt = l // tm
    hb = tm // HALO
    nhb = l // HALO
    bm = mod.shape[0]
    mod_idx = (lambda bi, i: (bi, 0, 0)) if bm > 1 else (lambda bi, i: (0, 0, 0))
    tok = lambda w: pl.BlockSpec((1, tm, w), lambda bi, i: (bi, i, 0))
    in_specs = [
        tok(D_MODEL),
        pl.BlockSpec((1, HALO, D_MODEL), lambda bi, i: (bi, jnp.maximum(i * hb - 1, 0), 0)),
        pl.BlockSpec((1, HALO, D_MODEL), lambda bi, i: (bi, jnp.minimum((i + 1) * hb, nhb - 1), 0)),
        pl.BlockSpec((1, N_MOD, D_MODEL), mod_idx),
        _const_spec((1, D_MODEL)),
        _const_spec(wp["w_main"].shape), _const_spec(wp["w_small"].shape), _const_spec(wp["w_dtT"].shape),
        _const_spec(wp["w_gate"].shape), _const_spec(wp["b_gate"].shape),
        _const_spec(wp["conv_w"].shape), _const_spec(wp["conv_b"].shape),
        _const_spec(wp["dt_bias"].shape), _const_spec(wp["a_log"].shape),
        _const_spec(wp["dt_biasT"].shape), _const_spec(wp["a_logT"].shape),
    ]
    outs = [
        ("q", GLA_QK, BF16), ("k", GLA_QK, BF16), ("v", GLA_V, BF16), ("g", GLA_V, BF16),
        ("z", SSD_INNER, BF16), ("gcum", 2 * GLA_QK, F32), ("xs", SSD_INNER, BF16),
        ("bc", 2 * SSD_BC, BF16),
    ]
    out_specs = [tok(w) for _, w, _ in outs]
    out_shape = [jax.ShapeDtypeStruct((b, l, w), dt) for _, w, dt in outs]
    out_specs += [pl.BlockSpec((1, SSD_BC, tm), lambda bi, i: (bi, 0, i)),
                  tok(LANES),
                  pl.BlockSpec((1, 4 * SSD_HEADS, tm), lambda bi, i: (bi, 0, i))]
    out_shape += [jax.ShapeDtypeStruct((b, SSD_BC, l), BF16),
                  jax.ShapeDtypeStruct((b, l, LANES), F32),
                  jax.ShapeDtypeStruct((b, 4 * SSD_HEADS, l), F32)]
    res = pl.pallas_call(
        functools.partial(_inproj0_kernel, tm=tm),
        grid=(b, nt),
        in_specs=in_specs,
        out_specs=out_specs,
        out_shape=out_shape,
        scratch_shapes=[pltpu.VMEM((tm + 2 * HALO, D_MODEL), BF16),
                        pltpu.VMEM((tm + 2 * HALO, SSD_XBC), F32)],
        compiler_params=_cparams(("parallel", "arbitrary")),
        name="inproj0",
    )(x, x, x, mod, nw, wp["w_main"], wp["w_small"], wp["w_dtT"], wp["w_gate"], wp["b_gate"],
      wp["conv_w"], wp["conv_b"], wp["dt_bias"], wp["a_log"], wp["dt_biasT"], wp["a_logT"])
    names = [n for n, _, _ in outs] + ["bT", "sd", "sdT"]
    return dict(zip(names, res))


def _level_ref(g, hsz, reverse):
    c, w = g.shape
    grp = 2 * hsz
    r = hsz if reverse else hsz - 1
    if grp >= 8:
        g3 = g.reshape(c // grp, grp, w)
        return jnp.broadcast_to(g3[:, r:r + 1, :], g3.shape).reshape(c, w)
    g3 = g.reshape(c // 8, 8, w)
    sub = lax.broadcasted_iota(jnp.int32, g3.shape, 1)
    out = None
    for t in range(8 // grp):
        cand = jnp.broadcast_to(g3[:, t * grp + r:t * grp + r + 1, :], g3.shape)
        out = cand if out is None else jnp.where(sub >= t * grp, cand, out)
    return out.reshape(c, w)


def _gla_masks(c, w, reverse):
    row = lax.broadcasted_iota(jnp.int32, (c, w), 0)
    ii = lax.broadcasted_iota(jnp.int32, (c, c), 0)
    jj = lax.broadcasted_iota(jnp.int32, (c, c), 1)
    par = 0 if reverse else 1
    levels = []
    hsz = 1
    while hsz < c:
        is_q = (row // hsz) % 2 == par
        keep = ((ii // (2 * hsz)) == (jj // (2 * hsz))) & ((ii // hsz) % 2 == par) & ((jj // hsz) % 2 != par)
        levels.append((hsz, is_q, keep.astype(F32)))
        hsz *= 2
    return (ii == jj).astype(F32), levels


def _gla_level_operand(q, k, g, hsz, reverse, is_q):
    c = g.shape[0]
    par = 0 if reverse else 1
    if hsz >= SUBLANES:
        parts = []
        for grp in range(c // (2 * hsz)):
            base = grp * 2 * hsz
            ref_row = base + (hsz if reverse else hsz - 1)
            ref = g[ref_row:ref_row + 1, :]
            for half in range(2):
                rows = slice(base + half * hsz, base + (half + 1) * hsz)
                if half == par:
                    parts.append(q[rows] * jnp.exp2(g[rows] - ref))
                else:
                    parts.append(k[rows] * jnp.exp2(ref - g[rows]))
        return jnp.concatenate(parts, axis=0).astype(BF16)
    if hsz == 1:
        step = g - pltpu.roll(g, (c - 1) if reverse else 1, 0)
        return jnp.where(is_q, q * jnp.exp2(step), k).astype(BF16)
    e = jnp.exp2(-jnp.abs(g - _level_ref(g, hsz, reverse)))
    return (jnp.where(is_q, q, k) * e).astype(BF16)


def _gla_intra_kernel(q_ref, k_ref, g_ref, a_ref):
    c = GLA_CHUNK
    masks = [_gla_masks(c, GLA_DK, d == 1) for d in range(2)]
    for sc in range(q_ref.shape[1] // c):
        rs = slice(sc * c, (sc + 1) * c)
        for hd in range(GLA_HEADS):
            ks = slice(hd * GLA_DK, (hd + 1) * GLA_DK)
            qb = q_ref[0, rs, ks]
            kb = k_ref[0, rs, ks]
            q = qb.astype(F32)
            k = kb.astype(F32)
            a = (2.0 * masks[0][0]) * _dot_nt(qb, kb)
            for d in range(2):
                g = g_ref[0, rs, d * GLA_QK + hd * GLA_DK:d * GLA_QK + (hd + 1) * GLA_DK]
                for hsz, is_q, keep in masks[d][1]:
                    mixed = _gla_level_operand(q, k, g, hsz, d == 1, is_q)
                    a = a + keep * _dot_nt(mixed, mixed)
            a_ref[0, hd, rs, :] = a.astype(BF16)


def _gla_intra(p, tm):
    b, l, _ = p["q"].shape
    tm = min(tm, l)
    tok = lambda w: pl.BlockSpec((1, tm, w), lambda bi, i: (bi, i, 0))
    return pl.pallas_call(
        _gla_intra_kernel,
        grid=(b, l // tm),
        in_specs=[tok(GLA_QK), tok(GLA_QK), tok(2 * GLA_QK)],
        out_specs=pl.BlockSpec((1, GLA_HEADS, tm, GLA_CHUNK), lambda bi, i: (bi, 0, i, 0)),
        out_shape=jax.ShapeDtypeStruct((b, GLA_HEADS, l, GLA_CHUNK), BF16),
        compiler_params=_cparams(("parallel", "arbitrary")),
        name="gla_intra",
    )(p["q"], p["k"], p["gcum"])


def _gla_inter(q, k, v, g, s, reverse):
    c = q.shape[0]
    dk = g.shape[1]
    g_tot = g[0:1, :] if reverse else g[c - 1:c, :]
    qi = (q.astype(F32) * jnp.exp2(g)).astype(BF16)
    ki = (k.astype(F32) * jnp.exp2(g_tot - g)).astype(BF16)
    decay = jnp.transpose(jnp.broadcast_to(jnp.exp2(g_tot), (dk, dk)))
    decay = jnp.concatenate([decay] * (v.shape[1] // dk), axis=1)
    return _dot(qi, s.astype(BF16)), s * decay + _dot_tn(ki, v)


def _gla_kernel(qf_ref, kf_ref, vf_ref, gf_ref, af_ref, qb_ref, kb_ref, vb_ref, gb_ref, s0_ref,
                oa_ref, ob_ref, sfin_ref, *s_scr):
    n = pl.program_id(1)

    @pl.when(n == 0)
    def _():
        for k, s_ref in enumerate(s_scr):
            s_ref[...] = s0_ref[0, k]

    c = GLA_CHUNK
    for sc in range(GLA_CHUNKS_PER_STEP):
        for hd in range(GLA_HEADS):
            ks = slice(hd * GLA_DK, (hd + 1) * GLA_DK)
            vs = slice(hd * GLA_DV, (hd + 1) * GLA_DV)
            rs = slice(sc * c, (sc + 1) * c)
            s_ref = s_scr[hd]
            v = vf_ref[0, rs, vs]
            inter, s_new = _gla_inter(qf_ref[0, rs, ks], kf_ref[0, rs, ks], v, gf_ref[0, rs, ks], s_ref[...], False)
            oa_ref[0, rs, vs] = (_dot(af_ref[0, hd, rs, :], v) + inter).astype(BF16)
            s_ref[...] = s_new

            rs = slice((GLA_CHUNKS_PER_STEP - 1 - sc) * c, (GLA_CHUNKS_PER_STEP - sc) * c)
            s_ref = s_scr[GLA_HEADS + hd]
            inter, s_new = _gla_inter(qb_ref[0, rs, ks], kb_ref[0, rs, ks], vb_ref[0, rs, vs], gb_ref[0, rs, ks],
                                      s_ref[...], True)
            ob_ref[0, rs, vs] = inter.astype(BF16)
            s_ref[...] = s_new

    @pl.when(n == pl.num_programs(1) - 1)
    def _():
        for k, s_ref in enumerate(s_scr):
            sfin_ref[0, k] = s_ref[...]


def _gla_scan(p, s0, tm):
    b, l, _ = p["q"].shape
    amat = _gla_intra(p, tm)
    c = GLA_CHUNK * GLA_CHUNKS_PER_STEP
    nc = l // c
    fwd = lambda w, col=0: pl.BlockSpec((1, c, w), lambda bi, n: (bi, n, col))
    bwd = lambda w, col=0: pl.BlockSpec((1, c, w), lambda bi, n: (bi, nc - 1 - n, col))
    st = pl.BlockSpec((1, 2 * GLA_HEADS, GLA_DK, GLA_DV), lambda bi, n: (bi, 0, 0, 0))
    a_spec = pl.BlockSpec((1, GLA_HEADS, c, GLA_CHUNK), lambda bi, n: (bi, 0, n, 0))
    oa, ob, sfin = pl.pallas_call(
        _gla_kernel,
        grid=(b, nc),
        in_specs=[fwd(GLA_QK), fwd(GLA_QK), fwd(GLA_V), fwd(GLA_QK, 0), a_spec,
                  bwd(GLA_QK), bwd(GLA_QK), bwd(GLA_V), bwd(GLA_QK, 1), st],
        out_specs=[fwd(GLA_V), bwd(GLA_V), st],
        out_shape=[jax.ShapeDtypeStruct((b, l, GLA_V), BF16), jax.ShapeDtypeStruct((b, l, GLA_V), BF16),
                   jax.ShapeDtypeStruct(s0.shape, F32)],
        scratch_shapes=[pltpu.VMEM((GLA_DK, GLA_DV), F32)] * (2 * GLA_HEADS),
        compiler_params=_cparams(("parallel", "arbitrary")),
        name="gla_scan",
    )(p["q"], p["k"], p["v"], p["gcum"], amat, p["q"], p["k"], p["v"], p["gcum"], s0)
    return oa, ob, sfin


def _split_hi_lo(x):
    hi = x.astype(BF16)
    lo = (x - hi.astype(F32)).astype(BF16)
    return jnp.concatenate([hi, lo], axis=1)


def _ssd_factors(sd, ex, d):
    c = sd.shape[0]
    nh = SSD_HEADS
    lane = lax.broadcasted_iota(jnp.int32, (1, LANES), 1)
    sel = (lane >= (2 + d) * nh) & (lane < (3 + d) * nh)
    tot = sd[0:1, :] if d == 1 else sd[c - 1:c, :]
    dt_under_cum = pltpu.roll(sd, 2 * nh, 1)
    e_in = jnp.where(sel, jnp.exp2(sd), 0.0)
    w_st = jnp.where(sel, dt_under_cum * jnp.exp2(tot - sd), 0.0)
    e_tot = jnp.broadcast_to(jnp.where(sel, jnp.exp2(tot), 0.0), (SUBLANES, LANES))
    expand = lambda f: _dot(_split_hi_lo(f), ex)
    return expand(e_in), expand(w_st), expand(e_tot)[0:1, :]


def _ssd_kernel(xsa_ref, bca_ref, bTa_ref, sda_ref, sdTa_ref, xsb_ref, bcb_ref, bTb_ref, sdb_ref,
                ex_ref, s0_ref, ya_ref, yb_ref, sfin_ref, *s_scr):
    c = SSD_CHUNK
    nh = SSD_HEADS
    gw = HEADS_PER_GROUP * SSD_HEADDIM
    n = pl.program_id(1)

    @pl.when(n == 0)
    def _():
        for k, s_ref in enumerate(s_scr):
            s_ref[...] = s0_ref[0, k]

    ii = lax.broadcasted_iota(jnp.int32, (c, c), 0)
    jj = lax.broadcasted_iota(jnp.int32, (c, c), 1)
    lower = jj <= ii
    upper = jj >= ii
    lo_half = lax.broadcasted_iota(jnp.int32, (1, LANES), 1) < SSD_HEADDIM

    for sc in range(SSD_CHUNKS_PER_STEP):
        ra = slice(sc * c, (sc + 1) * c)
        rb = slice((SSD_CHUNKS_PER_STEP - 1 - sc) * c, (SSD_CHUNKS_PER_STEP - sc) * c)
        sda = sda_ref[0, ra, :]
        sdTa = sdTa_ref[0, :, ra]
        ein_f, wst_f, dec_f = _ssd_factors(sda, ex_ref[0], 0)
        ein_b, wst_b, dec_b = _ssd_factors(sdb_ref[0, rb, :], ex_ref[1], 1)

        def weights(h, d, mask, sda=sda, sdTa=sdTa):
            col = (2 + d) * nh + h
            ci = sda[:, col:col + 1]
            cj = sdTa[col:col + 1, :]
            dtj = sdTa[d * nh + h:d * nh + h + 1, :]
            return jnp.where(mask, jnp.exp2(jnp.minimum(ci - cj, 0.0)) * dtj, 0.0)

        for gi in range(SSD_GROUPS):
            gs = slice(gi * gw, (gi + 1) * gw)
            ns = slice(gi * SSD_STATE, (gi + 1) * SSD_STATE)
            bm = bca_ref[0, ra, ns]
            cm = bca_ref[0, ra, SSD_BC + gi * SSD_STATE:SSD_BC + (gi + 1) * SSD_STATE]
            gmat = _dot_nt(cm, bm)
            s_f = s_scr[gi]
            s_b = s_scr[SSD_GROUPS + gi]
            y_inter = _dot(cm, s_f[...].astype(BF16)) * ein_f[:, gs]
            for hp in range(HEADS_PER_GROUP // 2):
                col0 = (gi * HEADS_PER_GROUP + 2 * hp) * SSD_HEADDIM
                xpair = xsa_ref[0, ra, col0:col0 + LANES]
                scores = []
                for t in range(2):
                    h = gi * HEADS_PER_GROUP + 2 * hp + t
                    scores.append((gmat * (weights(h, 0, lower) + weights(h, 1, upper))).astype(BF16))
                zero = jnp.zeros_like(xpair)
                rhs = jnp.concatenate([jnp.where(lo_half, xpair, zero), jnp.where(lo_half, zero, xpair)], axis=0)
                y_pair = _dot(jnp.concatenate(scores, axis=1), rhs) + y_inter[:, hp * LANES:(hp + 1) * LANES]
                ya_ref[0, ra, col0:col0 + LANES] = y_pair.astype(BF16)
            wx = (xsa_ref[0, ra, gs].astype(F32) * wst_f[:, gs]).astype(BF16)
            s_f[...] = s_f[...] * dec_f[:, gs] + _dot(bTa_ref[0, ns, ra], wx)

            cmb = bcb_ref[0, rb, SSD_BC + gi * SSD_STATE:SSD_BC + (gi + 1) * SSD_STATE]
            yb_ref[0, rb, gs] = (_dot(cmb, s_b[...].astype(BF16)) * ein_b[:, gs]).astype(BF16)
            wxb = (xsb_ref[0, rb, gs].astype(F32) * wst_b[:, gs]).astype(BF16)
            s_b[...] = s_b[...] * dec_b[:, gs] + _dot(bTb_ref[0, ns, rb], wxb)

    @pl.when(n == pl.num_programs(1) - 1)
    def _():
        for k, s_ref in enumerate(s_scr):
            sfin_ref[0, k] = s_ref[...]


def _ssd_expander():
    r = lax.broadcasted_iota(jnp.int32, (2, 2 * LANES, SSD_INNER), 1) % LANES
    d = lax.broadcasted_iota(jnp.int32, (2, 2 * LANES, SSD_INNER), 0)
    h = lax.broadcasted_iota(jnp.int32, (2, 2 * LANES, SSD_INNER), 2) // SSD_HEADDIM
    return (r == (2 + d) * SSD_HEADS + h).astype(BF16)


def _ssd_scan(p, s0):
    b, l, _ = p["xs"].shape
    c = SSD_CHUNK * SSD_CHUNKS_PER_STEP
    nc = l // c
    specs = []
    for rev in (False, True):
        idx = (lambda n: nc - 1 - n) if rev else (lambda n: n)
        tok = lambda w, idx=idx: pl.BlockSpec((1, c, w), lambda bi, n: (bi, idx(n), 0))
        chan = lambda r, idx=idx: pl.BlockSpec((1, r, c), lambda bi, n: (bi, 0, idx(n)))
        specs.append([tok(SSD_INNER), tok(2 * SSD_BC), chan(SSD_BC), tok(LANES), chan(4 * SSD_HEADS)])
    st = pl.BlockSpec((1, 2 * SSD_GROUPS, SSD_STATE, HEADS_PER_GROUP * SSD_HEADDIM), lambda bi, n: (bi, 0, 0, 0))
    ex = _ssd_expander()
    ya, yb, sfin = pl.pallas_call(
        _ssd_kernel,
        grid=(b, nc),
        in_specs=specs[0] + specs[1][:4] + [_const_spec(ex.shape), st],
        out_specs=[specs[0][0], specs[1][0], st],
        out_shape=[jax.ShapeDtypeStruct((b, l, SSD_INNER), BF16), jax.ShapeDtypeStruct((b, l, SSD_INNER), BF16),
                   jax.ShapeDtypeStruct(s0.shape, F32)],
        scratch_shapes=[pltpu.VMEM((SSD_STATE, HEADS_PER_GROUP * SSD_HEADDIM), F32)] * (2 * SSD_GROUPS),
        compiler_params=_cparams(("parallel", "arbitrary")),
        name="ssd_scan",
    )(p["xs"], p["bc"], p["bT"], p["sd"], p["sdT"], p["xs"], p["bc"], p["bT"], p["sd"], ex, s0)
    return ya, yb, sfin


def _post0_kernel(of_ref, ob_ref, g_ref, yf_ref, yb_ref, xs_ref, z_ref, x_ref, mod_ref,
                  gn_ref, dsk_ref, sn_ref, wo_ref, out_ref):
    o = of_ref[0].astype(F32) + ob_ref[0].astype(F32)
    parts = []
    for hd in range(GLA_HEADS):
        oh = o[:, hd * GLA_DV:(hd + 1) * GLA_DV]
        parts.append(oh * lax.rsqrt(jnp.mean(oh * oh, axis=-1, keepdims=True) + EPS))
    gla = jnp.concatenate(parts, axis=1) * gn_ref[...] * _silu(g_ref[0].astype(F32))
    xs = xs_ref[0].astype(F32)
    y = (yf_ref[0].astype(F32) + yb_ref[0].astype(F32) + xs * dsk_ref[...]) * _silu(z_ref[0].astype(F32))
    gw = SSD_INNER // SSD_GROUPS
    parts = []
    for gi in range(SSD_GROUPS):
        yg = y[:, gi * gw:(gi + 1) * gw]
        parts.append(yg * lax.rsqrt(jnp.mean(yg * yg, axis=-1, keepdims=True) + EPS))
    ssd = jnp.concatenate(parts, axis=1) * sn_ref[...]
    res = _dot(gla.astype(BF16), wo_ref[:GLA_V, :]) + _dot(ssd.astype(BF16), wo_ref[GLA_V:, :])
    out_ref[0] = x_ref[0] + mod_ref[0, 2:3, :] * res


def _post0(p, of, ob, yf, yb, x, mod, wp, tm):
    b, l, _ = x.shape
    tm = min(tm, l)
    bm = mod.shape[0]
    mod_idx = (lambda bi, i: (bi, 0, 0)) if bm > 1 else (lambda bi, i: (0, 0, 0))
    tok = lambda w: pl.BlockSpec((1, tm, w), lambda bi, i: (bi, i, 0))
    return pl.pallas_call(
        _post0_kernel,
        grid=(b, l // tm),
        in_specs=[tok(GLA_V), tok(GLA_V), tok(GLA_V), tok(SSD_INNER), tok(SSD_INNER), tok(SSD_INNER),
                  tok(SSD_INNER), tok(D_MODEL), pl.BlockSpec((1, N_MOD, D_MODEL), mod_idx),
                  _const_spec((1, GLA_V)), _const_spec((1, SSD_INNER)), _const_spec((1, SSD_INNER)),
                  _const_spec(wp["w_out"].shape)],
        out_specs=tok(D_MODEL),
        out_shape=jax.ShapeDtypeStruct((b, l, D_MODEL), F32),
        compiler_params=_cparams(("parallel", "arbitrary")),
        name="post0",
    )(of, ob, p["g"], yf, yb, p["xs"], p["z"], x, mod, wp["gla_norm"], wp["d_skip"], wp["ssd_norm"], wp["w_out"])


def _ffn_kernel(x_ref, mod_ref, nw_ref, w1_ref, w3_ref, w2_ref, fn_ref, out_ref, *, final):
    x = x_ref[0]
    h = _norm_mod(x, nw_ref[...], mod_ref[0, 3:4, :], mod_ref[0, 4:5, :]).astype(BF16)
    half = D_FF // 2
    y = jnp.zeros_like(x)
    for j in range(2):
        cs = slice(j * half, (j + 1) * half)
        u = (_silu(_dot(h, w1_ref[0, :, cs])) * _dot(h, w3_ref[0, :, cs])).astype(BF16)
        y = y + _dot(u, w2_ref[0, cs, :])
    out = x + mod_ref[0, 5:6, :] * y
    if final:
        out = out * lax.rsqrt(jnp.mean(out * out, axis=-1, keepdims=True) + EPS) * fn_ref[...]
    out_ref[0] = out


def _ffn(x, mod, nw, w1, w3, w2, layer, final_norm, final, tm):
    b, l, _ = x.shape
    tm = min(tm, l)
    bm = mod.shape[0]
    mod_idx = (lambda bi, i: (bi, 0, 0)) if bm > 1 else (lambda bi, i: (0, 0, 0))
    tok = pl.BlockSpec((1, tm, D_MODEL), lambda bi, i: (bi, i, 0))
    wspec = lambda w: pl.BlockSpec((1,) + w.shape[1:], lambda bi, i: (layer, 0, 0), pipeline_mode=pl.Buffered(1))
    return pl.pallas_call(
        functools.partial(_ffn_kernel, final=final),
        grid=(b, l // tm),
        in_specs=[tok, pl.BlockSpec((1, N_MOD, D_MODEL), mod_idx), _const_spec((1, D_MODEL)),
                  wspec(w1), wspec(w3), wspec(w2), _const_spec((1, D_MODEL))],
        out_specs=tok,
        out_shape=jax.ShapeDtypeStruct((b, l, D_MODEL), F32),
        compiler_params=_cparams(("parallel", "arbitrary")),
        name="ffn_final" if final else "ffn",
    )(x, mod, nw, w1, w3, w2, final_norm)


def _inproj1_kernel(*refs, tm, ncol):
    if ncol:
        x_ref, xp_ref, xn_ref, perm_ref = refs[:4]
        refs = refs[4:]
    else:
        x_ref, xp_ref, xn_ref = refs[:3]
        refs = refs[3:]
    (mod_ref, nw_ref, wup_ref, cw_ref, cb_ref, bdq_ref, bdk_ref, bdkT_ref, bdv_ref, wifq_ref, wifk_ref, wifv_ref,
     bif_ref, q_ref, kT_ref, v_ref, xc_ref, z_ref, gc_ref, gT_ref, hext, xm_scr) = refs
    i = pl.program_id(1)
    first = i == 0
    last = i == pl.num_programs(1) - 1
    shift = mod_ref[0, 0:1, :]
    scale = mod_ref[0, 1:2, :]
    nw = nw_ref[...]
    if ncol:
        hext[0:HALO, :] = _norm_mod(xp_ref[0, :, SUBLANES - 1, :], nw, shift, scale).astype(BF16)
        hn = _norm_mod(x_ref[0].reshape(tm, D_MODEL), nw, shift, scale).astype(BF16)
        hext[HALO:HALO + tm, :] = _dot(perm_ref[...], hn).astype(BF16)
        hext[HALO + tm:2 * HALO + tm, :] = _norm_mod(xn_ref[0, :, 0, :], nw, shift, scale).astype(BF16)
    else:
        hext[0:HALO, :] = _norm_mod(xp_ref[0], nw, shift, scale).astype(BF16)
        hext[HALO:HALO + tm, :] = _norm_mod(x_ref[0], nw, shift, scale).astype(BF16)
        hext[HALO + tm:2 * HALO + tm, :] = _norm_mod(xn_ref[0], nw, shift, scale).astype(BF16)
    h = hext[HALO:HALO + tm, :]

    xm_scr[...] = _dot(hext[...], wup_ref[:, :MLSTM_INNER])
    xc = _silu(_conv3(xm_scr, tm, cw_ref, cb_ref, first, last))
    xc_ref[0] = xc.astype(BF16)

    gates = jnp.zeros((tm, LANES), F32) + bif_ref[...]
    for t in range(N_BD_TILES):
        cs = slice(t * BD_TILE, (t + 1) * BD_TILE)
        xct = xc[:, cs].astype(BF16)
        xmt = xm_scr[HALO:HALO + tm, cs].astype(BF16)
        qt = _dot(xct, bdq_ref[t])
        kt = _dot(xct, bdk_ref[t])
        vt = _dot(xmt, bdv_ref[t])
        q_ref[0, :, cs] = qt.astype(BF16)
        v_ref[0, :, cs] = vt.astype(BF16)
        kT_ref[0, cs, :] = (_dot_nt(bdkT_ref[t], xct) * (MLSTM_DH ** -0.5)).astype(BF16)
        gates = gates + _dot(qt.astype(BF16), wifq_ref[cs, :]) + _dot(kt.astype(BF16), wifk_ref[cs, :]) \
            + _dot(vt.astype(BF16), wifv_ref[cs, :])

    gt = jnp.transpose(gates)[:4 * MLSTM_HEADS]
    r = lax.broadcasted_iota(jnp.int32, gt.shape, 0)
    lf = _log_sigmoid(gt)
    cum = jnp.where(r < 2 * MLSTM_HEADS, _seg_cumsum(lf, MLSTM_CHUNK, 1, False),
                    _seg_cumsum(lf, MLSTM_CHUNK, 1, True))
    packed = jnp.where(r % (2 * MLSTM_HEADS) < MLSTM_HEADS, gt, cum)
    a = gt - pltpu.roll(cum, 4 * MLSTM_HEADS - MLSTM_HEADS, 0)
    amax = jnp.where(r < 2 * MLSTM_HEADS, _seg_scan(a, MLSTM_CHUNK, 1, False, use_max=True),
                     _seg_scan(a, MLSTM_CHUNK, 1, True, use_max=True))
    full = jnp.concatenate([packed, amax], axis=0)
    gT_ref[0] = full
    gc_ref[0] = jnp.transpose(jnp.concatenate([full, jnp.zeros((LANES - MLSTM_GATE_ROWS, tm), F32)], axis=0))

    z_ref[0] = _dot(h, wup_ref[:, MLSTM_INNER:]).astype(BF16)


def _inproj1(x, mod, nw, wp, tm, colmajor):
    b, l, _ = x.shape
    tm = min(tm, l)
    nt = l // tm
    bm = mod.shape[0]
    mod_idx = (lambda bi, i: (bi, 0, 0)) if bm > 1 else (lambda bi, i: (0, 0, 0))
    if colmajor:
        rows = l // GRID_W
        ncol = tm // rows
        assert ncol % SUBLANES == 0 and rows % HALO == 0
        xv = x.reshape(b, rows, GRID_W, D_MODEL)
        rb = rows // HALO
        cb = ncol // SUBLANES
        ncb = GRID_W // SUBLANES
        x_spec = pl.BlockSpec((1, rows, ncol, D_MODEL), lambda bi, i: (bi, 0, i, 0))
        xp_spec = pl.BlockSpec((1, HALO, SUBLANES, D_MODEL),
                               lambda bi, i: (bi, rb - 1, jnp.maximum(i * cb - 1, 0), 0))
        xn_spec = pl.BlockSpec((1, HALO, SUBLANES, D_MODEL),
                               lambda bi, i: (bi, 0, jnp.minimum((i + 1) * cb, ncb - 1), 0))
        dst = lax.broadcasted_iota(jnp.int32, (tm, tm), 0)
        src = lax.broadcasted_iota(jnp.int32, (tm, tm), 1)
        perm = (src == (dst % rows) * ncol + dst // rows).astype(BF16)
        lead_specs = [x_spec, xp_spec, xn_spec, _const_spec((tm, tm))]
        lead_args = [xv, xv, xv, perm]
    else:
        ncol = 0
        xv = x
        hb = tm // HALO
        nhb = l // HALO
        x_spec = pl.BlockSpec((1, tm, D_MODEL), lambda bi, i: (bi, i, 0))
        xp_spec = pl.BlockSpec((1, HALO, D_MODEL), lambda bi, i: (bi, jnp.maximum(i * hb - 1, 0), 0))
        xn_spec = pl.BlockSpec((1, HALO, D_MODEL), lambda bi, i: (bi, jnp.minimum((i + 1) * hb, nhb - 1), 0))
        lead_specs = [x_spec, xp_spec, xn_spec]
        lead_args = [xv, xv, xv]
    tok = lambda w: pl.BlockSpec((1, tm, w), lambda bi, i: (bi, i, 0))
    names = ["w_up", "conv_w", "conv_b", "bdq", "bdk", "bdkT", "bdv", "wif_q", "wif_k", "wif_v", "b_if"]
    q, kT, v, xc, z, gc, gT = pl.pallas_call(
        functools.partial(_inproj1_kernel, tm=tm, ncol=ncol),
        grid=(b, nt),
        in_specs=lead_specs + [pl.BlockSpec((1, N_MOD, D_MODEL), mod_idx), _const_spec((1, D_MODEL))]
        + [_const_spec(wp[n].shape) for n in names],
        out_specs=[tok(MLSTM_INNER), pl.BlockSpec((1, MLSTM_INNER, tm), lambda bi, i: (bi, 0, i)),
                   tok(MLSTM_INNER), tok(MLSTM_INNER), tok(MLSTM_INNER), tok(LANES),
                   pl.BlockSpec((1, MLSTM_GATE_ROWS, tm), lambda bi, i: (bi, 0, i))],
        out_shape=[jax.ShapeDtypeStruct((b, l, MLSTM_INNER), BF16), jax.ShapeDtypeStruct((b, MLSTM_INNER, l), BF16),
                   jax.ShapeDtypeStruct((b, l, MLSTM_INNER), BF16), jax.ShapeDtypeStruct((b, l, MLSTM_INNER), BF16),
                   jax.ShapeDtypeStruct((b, l, MLSTM_INNER), BF16), jax.ShapeDtypeStruct((b, l, LANES), F32),
                   jax.ShapeDtypeStruct((b, MLSTM_GATE_ROWS, l), F32)],
        scratch_shapes=[pltpu.VMEM((tm + 2 * HALO, D_MODEL), BF16),
                        pltpu.VMEM((tm + 2 * HALO, MLSTM_INNER), F32)],
        compiler_params=_cparams(("parallel", "arbitrary")),
        name="inproj1",
    )(*lead_args, mod, nw, *[wp[n] for n in names])
    return dict(q=q, kT=kT, v=v, xc=xc, z=z, gc=gc, gT=gT)


def _mlstm_head(q, kT, v, gc, gT, c_scr, cb_scr, n_scr, m_scr, d, hd, reverse):
    c = MLSTM_CHUNK
    dh = MLSTM_DH
    idx = d * MLSTM_HEADS + hd
    ri = d * 2 * MLSTM_HEADS + hd
    rb = ri + MLSTM_HEADS
    rm = 4 * MLSTM_HEADS + ri
    c_scr, cb_scr, n_scr, m_scr = c_scr[idx], cb_scr[idx], n_scr[idx], m_scr[idx]
    bi = gc[:, rb:rb + 1]
    am_i = gc[:, rm:rm + 1]
    bj = gT[rb:rb + 1, :]
    a_j = gT[ri:ri + 1, :] - bj
    am_j = gT[rm:rm + 1, :]
    m0 = m_scr[0:1, 0:1]
    big_m = jnp.maximum(m0, am_i)
    ii = lax.broadcasted_iota(jnp.int32, (c, c), 0)
    jj = lax.broadcasted_iota(jnp.int32, (c, c), 1)
    causal = (jj >= ii) if reverse else (jj <= ii)
    s = _dot(q, kT) * jnp.where(causal, jnp.exp(a_j - big_m), 0.0)
    w_inter = jnp.exp(m0 - big_m)
    qf = q.astype(F32)
    den = jnp.sum(s, axis=-1, keepdims=True) \
        + w_inter * jnp.sum(qf * n_scr[0:1, :], axis=-1, keepdims=True)
    num = _dot(s.astype(BF16), v) + _dot((qf * w_inter).astype(BF16), cb_scr[...])
    h = num * (1.0 / jnp.maximum(jnp.abs(den), jnp.exp(-(bi + big_m))))

    last = 0 if reverse else c - 1
    m_inner = jnp.maximum(m0, am_j[:, last:last + 1])
    kw = kT.astype(F32) * jnp.exp(a_j - m_inner)
    decay = jnp.exp(m0 - m_inner)
    kwb = kw.astype(BF16)
    c_new = decay * c_scr[...] + _dot(kwb, v)
    c_scr[...] = c_new
    cb_scr[...] = c_new.astype(BF16)
    n_scr[...] = decay * n_scr[...] + _dot_nt(jnp.ones((SUBLANES, c), BF16), kwb)
    m_scr[...] = jnp.broadcast_to(bj[:, last:last + 1] + m_inner, m_scr.shape)
    return h


def _mlstm_kernel(*refs, has_init, emit_final):
    ins = refs[:10]
    pos = 10
    if has_init:
        c0_ref, n0_ref, m0_ref = refs[pos:pos + 3]
        pos += 3
    hf_ref, hb_ref = refs[pos:pos + 2]
    pos += 2
    if emit_final:
        cfin_ref, nfin_ref, mfin_ref = refs[pos:pos + 3]
        pos += 3
    nst = 2 * MLSTM_HEADS
    c_scr, cb_scr, n_scr, m_scr = (refs[pos + k * nst:pos + (k + 1) * nst] for k in range(4))
    n = pl.program_id(1)

    @pl.when(n == 0)
    def _():
        for k in range(nst):
            if has_init:
                c_scr[k][...] = c0_ref[0, k]
                cb_scr[k][...] = c0_ref[0, k].astype(BF16)
                n_scr[k][...] = n0_ref[0, k]
                m_scr[k][...] = m0_ref[0, k]
            else:
                c_scr[k][...] = jnp.zeros_like(c_scr[k])
                cb_scr[k][...] = jnp.zeros_like(cb_scr[k])
                n_scr[k][...] = jnp.zeros_like(n_scr[k])
                m_scr[k][...] = jnp.full(m_scr[k].shape, M_INIT, F32)

    for d, h_ref in enumerate((hf_ref, hb_ref)):
        q_ref, kT_ref, v_ref, gc_ref, gT_ref = ins[5 * d:5 * d + 5]
        gc = gc_ref[0]
        gT = gT_ref[0]
        for hd in range(MLSTM_HEADS):
            cs = slice(hd * MLSTM_DH, (hd + 1) * MLSTM_DH)
            h = _mlstm_head(q_ref[0, :, cs], kT_ref[0, cs, :], v_ref[0, :, cs], gc, gT,
                            c_scr, cb_scr, n_scr, m_scr, d, hd, d == 1)
            h_ref[0, :, cs] = h.astype(BF16)

    if emit_final:
        @pl.when(n == pl.num_programs(1) - 1)
        def _():
            for k in range(nst):
                cfin_ref[0, k] = c_scr[k][...]
                nfin_ref[0, k] = n_scr[k][...]
                mfin_ref[0, k] = m_scr[k][...]


def _mlstm_scan(p, init, emit_final):
    b, l, _ = p["q"].shape
    c = MLSTM_CHUNK
    nc = l // c
    dh = MLSTM_DH
    nst = 2 * MLSTM_HEADS
    specs = []
    for rev in (False, True):
        idx = (lambda n: nc - 1 - n) if rev else (lambda n: n)
        specs.append([
            pl.BlockSpec((1, c, MLSTM_INNER), lambda bi, n, idx=idx: (bi, idx(n), 0)),
            pl.BlockSpec((1, MLSTM_INNER, c), lambda bi, n, idx=idx: (bi, 0, idx(n))),
            pl.BlockSpec((1, c, MLSTM_INNER), lambda bi, n, idx=idx: (bi, idx(n), 0)),
            pl.BlockSpec((1, c, LANES), lambda bi, n, idx=idx: (bi, idx(n), 0)),
            pl.BlockSpec((1, MLSTM_GATE_ROWS, c), lambda bi, n, idx=idx: (bi, 0, idx(n))),
        ])
    st_shapes = [(b, nst, dh, dh), (b, nst, SUBLANES, dh), (b, nst, SUBLANES, LANES)]
    st_index = lambda bi, n: (bi, 0, 0, 0)
    args = (p["q"], p["kT"], p["v"], p["gc"], p["gT"])
    operands = [*args, *args]
    in_specs = specs[0] + specs[1]
    if init is not None:
        operands += list(init)
        in_specs += [pl.BlockSpec((1,) + s[1:], st_index, pipeline_mode=pl.Buffered(1)) for s in st_shapes]
    out_specs = [specs[0][0], specs[1][0]]
    out_shape = [jax.ShapeDtypeStruct((b, l, MLSTM_INNER), BF16), jax.ShapeDtypeStruct((b, l, MLSTM_INNER), BF16)]
    if emit_final:
        out_specs += [pl.BlockSpec((1,) + s[1:], st_index) for s in st_shapes]
        out_shape += [jax.ShapeDtypeStruct(s, F32) for s in st_shapes]
    res = pl.pallas_call(
        functools.partial(_mlstm_kernel, has_init=init is not None, emit_final=emit_final),
        grid=(b, nc),
        in_specs=in_specs,
        out_specs=out_specs,
        out_shape=out_shape,
        scratch_shapes=[pltpu.VMEM((dh, dh), F32)] * nst + [pltpu.VMEM((dh, dh), BF16)] * nst
        + [pltpu.VMEM((SUBLANES, dh), F32)] * nst + [pltpu.VMEM((SUBLANES, LANES), F32)] * nst,
        compiler_params=_cparams(("parallel", "arbitrary")),
        name="mlstm_scan",
    )(*operands)
    return res[0], res[1], tuple(res[2:])


def _post1_kernel(hf_ref, hb_ref, xc_ref, z_ref, x_ref, mod_ref, mh_ref, sk_ref, wd_ref, out_ref, *, tm, ncol):
    hs = hf_ref[0].astype(F32) + hb_ref[0].astype(F32)
    parts = []
    for hd in range(MLSTM_HEADS):
        hh = hs[:, hd * MLSTM_DH:(hd + 1) * MLSTM_DH]
        mu = jnp.mean(hh, axis=-1, keepdims=True)
        cen = hh - mu
        parts.append(cen * lax.rsqrt(jnp.mean(cen * cen, axis=-1, keepdims=True) + EPS))
    feat = (jnp.concatenate(parts, axis=1) * mh_ref[...] + sk_ref[...] * xc_ref[0].astype(F32)) \
        * _silu(z_ref[0].astype(F32))
    res = mod_ref[0, 2:3, :] * _dot(feat.astype(BF16), wd_ref[...])
    if ncol:
        rows = tm // ncol
        for j in range(ncol):
            out_ref[0, :, j, :] = x_ref[0, :, j, :] + res[j * rows:(j + 1) * rows, :]
    else:
        out_ref[0] = x_ref[0] + res


def _post1(p, hf, hb, x, mod, wp, tm, colmajor):
    b, l, _ = x.shape
    tm = min(tm, l)
    bm = mod.shape[0]
    mod_idx = (lambda bi, i: (bi, 0, 0)) if bm > 1 else (lambda bi, i: (0, 0, 0))
    tok = lambda w: pl.BlockSpec((1, tm, w), lambda bi, i: (bi, i, 0))
    if colmajor:
        rows = l // GRID_W
        ncol = tm // rows
        xv = x.reshape(b, rows, GRID_W, D_MODEL)
        x_spec = pl.BlockSpec((1, rows, ncol, D_MODEL), lambda bi, i: (bi, 0, i, 0))
    else:
        ncol = 0
        xv = x
        x_spec = tok(D_MODEL)
    out = pl.pallas_call(
        functools.partial(_post1_kernel, tm=tm, ncol=ncol),
        grid=(b, l // tm),
        in_specs=[tok(MLSTM_INNER), tok(MLSTM_INNER), tok(MLSTM_INNER), tok(MLSTM_INNER), x_spec,
                  pl.BlockSpec((1, N_MOD, D_MODEL), mod_idx),
                  _const_spec((1, MLSTM_INNER)), _const_spec((1, MLSTM_INNER)), _const_spec(wp["w_down"].shape)],
        out_specs=x_spec,
        out_shape=jax.ShapeDtypeStruct(xv.shape, F32),
        compiler_params=_cparams(("parallel", "arbitrary")),
        name="post1",
    )(hf, hb, p["xc"], p["z"], xv, mod, wp["mh_norm"], wp["skip"], wp["w_down"])
    return out.reshape(b, l, D_MODEL)


def _prep_even(w_in, w_gate, b_gate, gla_norm, conv_w, conv_b, a_log, dt_bias, d_skip, ssd_norm, w_out):
    sizes = (GLA_QK, GLA_QK, GLA_V, GLA_V, 2 * GLA_GATE_RANK, SSD_INNER, SSD_XBC, 2 * SSD_HEADS)
    offs = [0]
    for s in sizes:
        offs.append(offs[-1] + s)
    w16 = w_in.astype(BF16)
    w_main = jnp.concatenate([w16[:, offs[0]:offs[4]], w16[:, offs[5]:offs[7]]], axis=1)
    pad = jnp.zeros((D_MODEL, LANES - 2 * GLA_GATE_RANK - 2 * SSD_HEADS), BF16)
    w_small = jnp.concatenate([w16[:, offs[4]:offs[5]], w16[:, offs[7]:offs[8]], pad], axis=1)
    zero = jnp.zeros((GLA_GATE_RANK, GLA_QK), w_gate.dtype)
    wg = jnp.concatenate([jnp.concatenate([w_gate[0], zero], axis=1),
                          jnp.concatenate([zero, w_gate[1]], axis=1)], axis=0).astype(BF16)
    return dict(
        w_main=w_main, w_small=w_small, w_dtT=jnp.transpose(w_in[:, offs[7]:offs[8]]).astype(BF16),
        w_gate=wg, b_gate=b_gate.reshape(1, 2 * GLA_QK),
        conv_w=conv_w, conv_b=conv_b.reshape(1, SSD_XBC),
        dt_bias=dt_bias.reshape(1, 2 * SSD_HEADS), a_log=a_log.reshape(1, 2 * SSD_HEADS),
        dt_biasT=dt_bias.reshape(2 * SSD_HEADS, 1), a_logT=a_log.reshape(2 * SSD_HEADS, 1),
        gla_norm=jnp.tile(gla_norm, GLA_HEADS).reshape(1, GLA_V),
        d_skip=jnp.repeat(d_skip, SSD_HEADDIM).reshape(1, SSD_INNER),
        ssd_norm=ssd_norm.reshape(1, SSD_INNER),
        w_out=w_out.astype(BF16),
    )


def _block_diag_tiles(w, transposed=False):
    per = BD_TILE // QKV_BLOCK
    w4 = w.reshape(N_BD_TILES, per, QKV_BLOCK, QKV_BLOCK)
    blk = lax.broadcasted_iota(jnp.int32, (BD_TILE, BD_TILE), 0) // QKV_BLOCK
    same_block = blk == jnp.transpose(blk)
    if transposed:
        cols = jnp.transpose(w4, (0, 3, 1, 2)).reshape(N_BD_TILES, QKV_BLOCK, BD_TILE)
        full = jnp.tile(cols, (1, per, 1))
    else:
        rows = w4.reshape(N_BD_TILES, BD_TILE, QKV_BLOCK)
        full = jnp.tile(rows, (1, 1, per))
    return jnp.where(same_block, full, 0.0).astype(BF16)


def _prep_odd(w_up, conv_w, conv_b, wq, wk, wv, w_if, b_if, mh_norm, skip, w_down):
    ng = 4 * MLSTM_HEADS
    wif = jnp.transpose(w_if, (1, 0, 2)).reshape(3 * MLSTM_INNER, ng)
    wif = jnp.concatenate([wif, jnp.zeros((3 * MLSTM_INNER, LANES - ng), w_if.dtype)], axis=1).astype(BF16)
    bif = jnp.concatenate([b_if.reshape(1, ng), jnp.zeros((1, LANES - ng), b_if.dtype)], axis=1)
    return dict(
        w_up=w_up.astype(BF16), conv_w=conv_w, conv_b=conv_b.reshape(1, MLSTM_INNER),
        bdq=_block_diag_tiles(wq), bdk=_block_diag_tiles(wk),
        bdkT=_block_diag_tiles(wk, transposed=True), bdv=_block_diag_tiles(wv),
        wif_q=wif[:MLSTM_INNER], wif_k=wif[MLSTM_INNER:2 * MLSTM_INNER], wif_v=wif[2 * MLSTM_INNER:], b_if=bif,
        mh_norm=mh_norm.reshape(1, MLSTM_INNER), skip=skip.reshape(1, MLSTM_INNER),
        w_down=w_down.astype(BF16),
    )


TOKEN_TILE = 512


def kernel(x, c, ctx, c_ctx, w_mod, b_mod, norm_mix, norm_ffn, ffn_w1, ffn_w3, ffn_w2, a_w_in, a_gla_w_gate,
           a_gla_b_gate, a_gla_norm, a_ssd_conv_w, a_ssd_conv_b, a_ssd_A_log, a_ssd_dt_bias, a_ssd_D, a_ssd_norm,
           a_w_out, b_w_up, b_conv_w, b_conv_b, b_wq, b_wk, b_wv, b_w_if, b_b_if, b_mh_norm, b_skip, b_w_down,
           final_norm):
    tile = TOKEN_TILE
    bsz = x.shape[0]
    depth = w_mod.shape[0]
    pad_rows = (-(bsz + 1)) % 8
    cc = jnp.concatenate([c, c_ctx[None, :], jnp.zeros((pad_rows, D_MODEL), c.dtype)], axis=0)
    mods = _modulation(cc, w_mod, b_mod).reshape(depth, bsz + 1 + pad_rows, N_MOD, D_MODEL)
    fnorm = final_norm.reshape(1, D_MODEL)
    w1, w3, w2 = ffn_w1.astype(BF16), ffn_w3.astype(BF16), ffn_w2.astype(BF16)
    for layer in range(depth):
        last = layer == depth - 1
        j = layer // 2
        mx = mods[layer, :bsz]
        mc = mods[layer, bsz:bsz + 1]
        nmix = norm_mix[layer].reshape(1, D_MODEL)
        nffn = norm_ffn[layer].reshape(1, D_MODEL)
        if layer % 2 == 0:
            wp = _prep_even(a_w_in[j], a_gla_w_gate[j], a_gla_b_gate[j], a_gla_norm[j], a_ssd_conv_w[j],
                            a_ssd_conv_b[j], a_ssd_A_log[j], a_ssd_dt_bias[j], a_ssd_D[j], a_ssd_norm[j],
                            a_w_out[j])
            pc = _inproj0(ctx, mc, nmix, wp, tile)
            gla0 = jnp.zeros((bsz, 2 * GLA_HEADS, GLA_DK, GLA_DV), F32)
            ssd0 = jnp.zeros((bsz, 2 * SSD_GROUPS, SSD_STATE, HEADS_PER_GROUP * SSD_HEADDIM), F32)
            ofc, obc, gla_st = _gla_scan(pc, gla0, GLA_INTRA_TILE)
            yfc, ybc, ssd_st = _ssd_scan(pc, ssd0)
            px = _inproj0(x, mx, nmix, wp, tile)
            ofx, obx, _ = _gla_scan(px, gla_st, GLA_INTRA_TILE)
            yfx, ybx, _ = _ssd_scan(px, ssd_st)
            x = _post0(px, ofx, obx, yfx, ybx, x, mx, wp, tile)
            if not last:
                ctx = _post0(pc, ofc, obc, yfc, ybc, ctx, mc, wp, tile)
        else:
            wp = _prep_odd(b_w_up[j], b_conv_w[j], b_conv_b[j], b_wq[j], b_wk[j], b_wv[j], b_w_if[j], b_b_if[j],
                           b_mh_norm[j], b_skip[j], b_w_down[j])
            pc = _inproj1(ctx, mc, nmix, wp, tile, colmajor=False)
            hfc, hbc, state = _mlstm_scan(pc, None, True)
            px = _inproj1(x, mx, nmix, wp, tile, colmajor=True)
            hfx, hbx, _ = _mlstm_scan(px, state, False)
            x = _post1(px, hfx, hbx, x, mx, wp, tile, colmajor=True)
            if not last:
                ctx = _post1(pc, hfc, hbc, ctx, mc, wp, tile, colmajor=False)
        x = _ffn(x, mx, nffn, w1, w3, w2, layer, fnorm, last, tile)
        if not last:
            ctx = _ffn(ctx, mc, nffn, w1, w3, w2, layer, fnorm, False, tile)
    return x
```

```python
import functools

import jax
import jax.numpy as jnp
from jax import lax
from jax.experimental import pallas as pl
from jax.experimental.pallas import tpu as pltpu

F32 = jnp.float32
BF16 = jnp.bfloat16

D_MODEL = 1024
GRID_W = 64
EPS = 1e-6
LOG2_E = 1.4426950408889634
M_INIT = -1e30
N_MOD = 6

GLA_HEADS = 4
GLA_DK = 128
GLA_DV = 256
GLA_QK = GLA_HEADS * GLA_DK
GLA_V = GLA_HEADS * GLA_DV
GLA_GATE_RANK = 16
GLA_GATE_TAU = 16.0

SSD_HEADDIM = 64
SSD_HEADS = 16
SSD_STATE = 128
SSD_GROUPS = 2
SSD_INNER = SSD_HEADS * SSD_HEADDIM
SSD_BC = SSD_GROUPS * SSD_STATE
SSD_XBC = SSD_INNER + 2 * SSD_BC
HEADS_PER_GROUP = SSD_HEADS // SSD_GROUPS

MLSTM_INNER = 2 * D_MODEL
MLSTM_HEADS = 4
MLSTM_DH = MLSTM_INNER // MLSTM_HEADS
QKV_BLOCK = 4
BD_TILE = 256
N_BD_TILES = MLSTM_INNER // BD_TILE
MLSTM_GATE_ROWS = 32

D_FF = 2816

GLA_CHUNK = 64
GLA_CHUNKS_PER_STEP = 8
GLA_INTRA_TILE = 512
SSD_CHUNK = 128
SSD_CHUNKS_PER_STEP = 4
MLSTM_CHUNK = 256
HALO = 16
LANES = 128
SUBLANES = 8
TAIL_ROWS = 128

VMEM_LIMIT = 56 * 1024 * 1024


def _cparams(sem):
    return pltpu.CompilerParams(dimension_semantics=sem, vmem_limit_bytes=VMEM_LIMIT)


def _const_spec(shape):
    nd = len(shape)
    return pl.BlockSpec(shape, lambda *_: (0,) * nd, pipeline_mode=pl.Buffered(1))


def _softplus(x):
    return jnp.maximum(x, 0.0) + jnp.log(1.0 + jnp.exp(-jnp.abs(x)))


def _log_sigmoid(x):
    return -_softplus(-x)


def _silu(x):
    return x / (1.0 + jnp.exp(-x))


def _seg_scan(x, seg, axis, reverse, use_max=False):
    n = x.shape[axis]
    idx = lax.broadcasted_iota(jnp.int32, x.shape, axis) % seg
    s = 1
    while s < seg:
        if reverse:
            shifted = pltpu.roll(x, n - s, axis)
            keep = idx < seg - s
        else:
            shifted = pltpu.roll(x, s, axis)
            keep = idx >= s
        if use_max:
            x = jnp.where(keep, jnp.maximum(x, shifted), x)
        else:
            x = x + jnp.where(keep, shifted, 0.0)
        s *= 2
    return x


def _seg_cumsum(x, seg, axis, reverse):
    return _seg_scan(x, seg, axis, reverse)


def _chunk_cumsum_mxu(x, chunk, reverse):
    rows = x.shape[0]
    hi = x.astype(BF16)
    lo = (x - hi.astype(F32)).astype(BF16)
    i = lax.broadcasted_iota(jnp.int32, (chunk, 2 * chunk), 0)
    j = lax.broadcasted_iota(jnp.int32, (chunk, 2 * chunk), 1) % chunk
    tri = ((j >= i) if reverse else (j <= i)).astype(BF16)
    parts = []
    for c0 in range(0, rows, chunk):
        parts.append(_dot(tri, jnp.concatenate([hi[c0:c0 + chunk], lo[c0:c0 + chunk]], axis=0)))
    return jnp.concatenate(parts, axis=0)


def _norm_mod(xv, nw, shift, scale):
    ms = jnp.mean(xv * xv, axis=-1, keepdims=True)
    return (xv * lax.rsqrt(ms + EPS) * nw) * (1.0 + scale) + shift


def _dot(a, b):
    return jnp.dot(a, b, preferred_element_type=F32)


def _dot_nt(a, b):
    return lax.dot_general(a, b, (((1,), (1,)), ((), ())), preferred_element_type=F32)


def _dot_tn(a, b):
    return lax.dot_general(a, b, (((0,), (0,)), ((), ())), preferred_element_type=F32)


def _mod_kernel(c_ref, w_ref, b_ref, o_ref):
    a = _silu(c_ref[...]).astype(BF16)
    o_ref[0] = _dot(a, w_ref[0].astype(BF16)) + b_ref[0]


def _modulation(cc, w_mod, b_mod):
    depth = w_mod.shape[0]
    n = w_mod.shape[2]
    tn = n // 4
    rows = cc.shape[0]
    return pl.pallas_call(
        _mod_kernel,
        grid=(depth, n // tn),
        in_specs=[
            pl.BlockSpec((rows, D_MODEL), lambda l, j: (0, 0)),
            pl.BlockSpec((1, D_MODEL, tn), lambda l, j: (l, 0, j)),
            pl.BlockSpec((1, 1, tn), lambda l, j: (l, 0, j)),
        ],
        out_specs=pl.BlockSpec((1, rows, tn), lambda l, j: (l, 0, j)),
        out_shape=jax.ShapeDtypeStruct((depth, rows, n), F32),
        compiler_params=_cparams(("arbitrary", "arbitrary")),
        name="modulation",
    )(cc, w_mod, b_mod.reshape(depth, 1, n))


def _conv_pad(scr, tm, first, last):
    scr[HALO - 1:HALO, :] = scr[HALO - 1:HALO, :] * jnp.where(first, 0.0, 1.0)
    scr[HALO + tm:HALO + tm + 1, :] = scr[HALO + tm:HALO + tm + 1, :] * jnp.where(last, 0.0, 1.0)


def _conv_rows(scr, r0, n, cw_ref, cb_ref):
    prev = scr[HALO - 1 + r0:HALO - 1 + r0 + n, :]
    cur = scr[HALO + r0:HALO + r0 + n, :]
    nxt = scr[HALO + 1 + r0:HALO + 1 + r0 + n, :]
    return cw_ref[0:1, :] * prev + cw_ref[1:2, :] * cur + cw_ref[2:3, :] * nxt + cb_ref[...]


def _conv3(scr, tm, cw_ref, cb_ref, first, last):
    _conv_pad(scr, tm, first, last)
    return _conv_rows(scr, 0, tm, cw_ref, cb_ref)


def _inproj0_kernel(x_ref, xp_ref, xn_ref, mod_ref, nw_ref, wmain_ref, wsmall_ref, wdtT_ref, wg_ref, bg_ref,
                    cw_ref, cb_ref, dtb_ref, alog_ref, dtbT_ref, alogT_ref,
                    q_ref, k_ref, v_ref, g_ref, z_ref, gcum_ref, xs_ref, bc_ref, bT_ref, sd_ref, sdT_ref,
                    hext, xbc_scr, *, tm):
    i = pl.program_id(1)
    first = i == 0
    last = i == pl.num_programs(1) - 1
    shift = mod_ref[0, 0:1, :]
    scale = mod_ref[0, 1:2, :]
    nw = nw_ref[...]
    hext[0:HALO, :] = _norm_mod(xp_ref[0], nw, shift, scale).astype(BF16)
    hext[HALO:HALO + tm, :] = _norm_mod(x_ref[0], nw, shift, scale).astype(BF16)
    hext[HALO + tm:2 * HALO + tm, :] = _norm_mod(xn_ref[0], nw, shift, scale).astype(BF16)

    o_k = GLA_QK
    o_v = o_k + GLA_QK
    o_g = o_v + GLA_V
    o_z = o_g + GLA_V
    o_x = o_z + SSD_INNER
    nd = 2 * SSD_HEADS
    h = hext[HALO:HALO + tm, :]

    xbc_scr[...] = _dot(hext[...], wmain_ref[:, o_x:o_x + SSD_XBC])
    _conv_pad(xbc_scr, tm, first, last)
    small = _dot(h, wsmall_ref[...])
    dt_raw = _dot_nt(wdtT_ref[...], h)
    logits = _dot(small[:, :2 * GLA_GATE_RANK].astype(BF16), wg_ref[...]) + bg_ref[...]

    def tail(j):
        rows = slice(j * TAIL_ROWS, (j + 1) * TAIL_ROWS)
        lg = logits[rows]
        la = (jnp.minimum(lg, 0.0) * LOG2_E - jnp.log2(1.0 + jnp.exp2(jnp.abs(lg) * (-LOG2_E)))) \
            * (1.0 / GLA_GATE_TAU)
        gcum_ref[0, rows, :GLA_QK] = _chunk_cumsum_mxu(la[:, :GLA_QK], GLA_CHUNK, False)
        gcum_ref[0, rows, GLA_QK:] = _chunk_cumsum_mxu(la[:, GLA_QK:], GLA_CHUNK, True)
        dtT = _softplus(dt_raw[:, rows] + dtbT_ref[...])
        laT = dtT * (-LOG2_E * jnp.exp(alogT_ref[...]))
        cumT = jnp.concatenate([_seg_cumsum(laT[:SSD_HEADS], SSD_CHUNK, 1, False),
                                _seg_cumsum(laT[SSD_HEADS:], SSD_CHUNK, 1, True)], axis=0)
        packed = jnp.concatenate([dtT, cumT, jnp.zeros((LANES - 2 * nd, TAIL_ROWS), F32)], axis=0)
        sdT_ref[0, :, rows] = packed[:2 * nd]
        sd_ref[0, rows, :] = jnp.transpose(packed)
        y = _silu(_conv_rows(xbc_scr, j * TAIL_ROWS, TAIL_ROWS, cw_ref, cb_ref))
        xs_ref[0, rows, :] = y[:, :SSD_INNER].astype(BF16)
        bc_ref[0, rows, :] = y[:, SSD_INNER:].astype(BF16)
        bT_ref[0, :, rows] = jnp.transpose(y[:, SSD_INNER:SSD_INNER + SSD_BC]).astype(BF16)

    wide = [(q_ref, 0, o_k, GLA_DK ** -0.5), (k_ref, o_k, o_v, None), (v_ref, o_v, o_g, None),
            (g_ref, o_g, o_z, None), (z_ref, o_z, o_x, None)]
    n_tail = tm // TAIL_ROWS
    done = 0
    for idx, (ref, c0, c1, scl) in enumerate(wide):
        res = _dot(h, wmain_ref[:, c0:c1])
        ref[0] = (res if scl is None else res * scl).astype(BF16)
        upto = (n_tail * (idx + 1)) // (len(wide) - 1) if idx < len(wide) - 1 else n_tail
        for j in range(done, min(upto, n_tail)):
            tail(j)
        done = max(done, min(upto, n_tail))


def _inproj0(x, mod, nw, wp, tm):
    b, l, _ = x.shape
    tm = min(tm, l)
    nt = l // tm
    hb = tm // HALO
    nhb = l // HALO
    bm = mod.shape[0]
    mod_idx = (lambda bi, i: (bi, 0, 0)) if bm > 1 else (lambda bi, i: (0, 0, 0))
    tok = lambda w: pl.BlockSpec((1, tm, w), lambda bi, i: (bi, i, 0))
    in_specs = [
        tok(D_MODEL),
        pl.BlockSpec((1, HALO, D_MODEL), lambda bi, i: (bi, jnp.maximum(i * hb - 1, 0), 0)),
        pl.BlockSpec((1, HALO, D_MODEL), lambda bi, i: (bi, jnp.minimum((i + 1) * hb, nhb - 1), 0)),
        pl.BlockSpec((1, N_MOD, D_MODEL), mod_idx),
        _const_spec((1, D_MODEL)),
        _const_spec(wp["w_main"].shape), _const_spec(wp["w_small"].shape), _const_spec(wp["w_dtT"].shape),
        _const_spec(wp["w_gate"].shape), _const_spec(wp["b_gate"].shape),
        _const_spec(wp["conv_w"].shape), _const_spec(wp["conv_b"].shape),
        _const_spec(wp["dt_bias"].shape), _const_spec(wp["a_log"].shape),
        _const_spec(wp["dt_biasT"].shape), _const_spec(wp["a_logT"].shape),
    ]
    outs = [
        ("q", GLA_QK, BF16), ("k", GLA_QK, BF16), ("v", GLA_V, BF16), ("g", GLA_V, BF16),
        ("z", SSD_INNER, BF16), ("gcum", 2 * GLA_QK, F32), ("xs", SSD_INNER, BF16),
        ("bc", 2 * SSD_BC, BF16),
    ]
    out_specs = [tok(w) for _, w, _ in outs]
    out_shape = [jax.ShapeDtypeStruct((b, l, w), dt) for _, w, dt in outs]
    out_specs += [pl.BlockSpec((1, SSD_BC, tm), lambda bi, i: (bi, 0, i)),
                  tok(LANES),
                  pl.BlockSpec((1, 4 * SSD_HEADS, tm), lambda bi, i: (bi, 0, i))]
    out_shape += [jax.ShapeDtypeStruct((b, SSD_BC, l), BF16),
                  jax.ShapeDtypeStruct((b, l, LANES), F32),
                  jax.ShapeDtypeStruct((b, 4 * SSD_HEADS, l), F32)]
    res = pl.pallas_call(
        functools.partial(_inproj0_kernel, tm=tm),
        grid=(b, nt),
        in_specs=in_specs,
        out_specs=out_specs,
        out_shape=out_shape,
        scratch_shapes=[pltpu.VMEM((tm + 2 * HALO, D_MODEL), BF16),
                        pltpu.VMEM((tm + 2 * HALO, SSD_XBC), F32)],
        compiler_params=_cparams(("parallel", "arbitrary")),
        name="inproj0",
    )(x, x, x, mod, nw, wp["w_main"], wp["w_small"], wp["w_dtT"], wp["w_gate"], wp["b_gate"],
      wp["conv_w"], wp["conv_b"], wp["dt_bias"], wp["a_log"], wp["dt_biasT"], wp["a_logT"])
    names = [n for n, _, _ in outs] + ["bT", "sd", "sdT"]
    return dict(zip(names, res))


def _level_ref(g, hsz, reverse):
    c, w = g.shape
    grp = 2 * hsz
    r = hsz if reverse else hsz - 1
    if grp >= 8:
        g3 = g.reshape(c // grp, grp, w)
        return jnp.broadcast_to(g3[:, r:r + 1, :], g3.shape).reshape(c, w)
    g3 = g.reshape(c // 8, 8, w)
    sub = lax.broadcasted_iota(jnp.int32, g3.shape, 1)
    out = None
    for t in range(8 // grp):
        cand = jnp.broadcast_to(g3[:, t * grp + r:t * grp + r + 1, :], g3.shape)
        out = cand if out is None else jnp.where(sub >= t * grp, cand, out)
    return out.reshape(c, w)


def _gla_masks(c, w, reverse):
    row = lax.broadcasted_iota(jnp.int32, (c, w), 0)
    ii = lax.broadcasted_iota(jnp.int32, (c, c), 0)
    jj = lax.broadcasted_iota(jnp.int32, (c, c), 1)
    par = 0 if reverse else 1
    levels = []
    hsz = 1
    while hsz < c:
        is_q = (row // hsz) % 2 == par
        keep = ((ii // (2 * hsz)) == (jj // (2 * hsz))) & ((ii // hsz) % 2 == par) & ((jj // hsz) % 2 != par)
        levels.append((hsz, is_q, keep.astype(F32)))
        hsz *= 2
    return (ii == jj).astype(F32), levels


def _gla_level_operand(q, k, g, hsz, reverse, is_q):
    c = g.shape[0]
    par = 0 if reverse else 1
    if hsz >= SUBLANES:
        parts = []
        for grp in range(c // (2 * hsz)):
            base = grp * 2 * hsz
            ref_row = base + (hsz if reverse else hsz - 1)
            ref = g[ref_row:ref_row + 1, :]
            for half in range(2):
                rows = slice(base + half * hsz, base + (half + 1) * hsz)
                if half == par:
                    parts.append(q[rows] * jnp.exp2(g[rows] - ref))
                else:
                    parts.append(k[rows] * jnp.exp2(ref - g[rows]))
        return jnp.concatenate(parts, axis=0).astype(BF16)
    if hsz == 1:
        step = g - pltpu.roll(g, (c - 1) if reverse else 1, 0)
        return jnp.where(is_q, q * jnp.exp2(step), k).astype(BF16)
    e = jnp.exp2(-jnp.abs(g - _level_ref(g, hsz, reverse)))
    return (jnp.where(is_q, q, k) * e).astype(BF16)


def _gla_intra_kernel(q_ref, k_ref, g_ref, a_ref):
    c = GLA_CHUNK
    masks = [_gla_masks(c, GLA_DK, d == 1) for d in range(2)]
    for sc in range(q_ref.shape[1] // c):
        rs = slice(sc * c, (sc + 1) * c)
        for hd in range(GLA_HEADS):
            ks = slice(hd * GLA_DK, (hd + 1) * GLA_DK)
            qb = q_ref[0, rs, ks]
            kb = k_ref[0, rs, ks]
            q = qb.astype(F32)
            k = kb.astype(F32)
            a = (2.0 * masks[0][0]) * _dot_nt(qb, kb)
            for d in range(2):
                g = g_ref[0, rs, d * GLA_QK + hd * GLA_DK:d * GLA_QK + (hd + 1) * GLA_DK]
                for hsz, is_q, keep in masks[d][1]:
                    mixed = _gla_level_operand(q, k, g, hsz, d == 1, is_q)
                    a = a + keep * _dot_nt(mixed, mixed)
            a_ref[0, hd, rs, :] = a.astype(BF16)


def _gla_intra(p, tm):
    b, l, _ = p["q"].shape
    tm = min(tm, l)
    tok = lambda w: pl.BlockSpec((1, tm, w), lambda bi, i: (bi, i, 0))
    return pl.pallas_call(
        _gla_intra_kernel,
        grid=(b, l // tm),
        in_specs=[tok(GLA_QK), tok(GLA_QK), tok(2 * GLA_QK)],
        out_specs=pl.BlockSpec((1, GLA_HEADS, tm, GLA_CHUNK), lambda bi, i: (bi, 0, i, 0)),
        out_shape=jax.ShapeDtypeStruct((b, GLA_HEADS, l, GLA_CHUNK), BF16),
        compiler_params=_cparams(("parallel", "arbitrary")),
        name="gla_intra",
    )(p["q"], p["k"], p["gcum"])


def _gla_inter(q, k, v, g, s, reverse):
    c = q.shape[0]
    dk = g.shape[1]
    g_tot = g[0:1, :] if reverse else g[c - 1:c, :]
    qi = (q.astype(F32) * jnp.exp2(g)).astype(BF16)
    ki = (k.astype(F32) * jnp.exp2(g_tot - g)).astype(BF16)
    decay = jnp.transpose(jnp.broadcast_to(jnp.exp2(g_tot), (dk, dk)))
    decay = jnp.concatenate([decay] * (v.shape[1] // dk), axis=1)
    return _dot(qi, s.astype(BF16)), s * decay + _dot_tn(ki, v)


def _gla_kernel(qf_ref, kf_ref, vf_ref, gf_ref, af_ref, qb_ref, kb_ref, vb_ref, gb_ref, s0_ref,
                oa_ref, ob_ref, sfin_ref, *s_scr):
    n = pl.program_id(1)

    @pl.when(n == 0)
    def _():
        for k, s_ref in enumerate(s_scr):
            s_ref[...] = s0_ref[0, k]

    c = GLA_CHUNK
    cps = qf_ref.shape[1] // c
    for sc in range(cps):
        for hd in range(GLA_HEADS):
            ks = slice(hd * GLA_DK, (hd + 1) * GLA_DK)
            vs = slice(hd * GLA_DV, (hd + 1) * GLA_DV)
            rs = slice(sc * c, (sc + 1) * c)
            s_ref = s_scr[hd]
            v = vf_ref[0, rs, vs]
            inter, s_new = _gla_inter(qf_ref[0, rs, ks], kf_ref[0, rs, ks], v, gf_ref[0, rs, ks], s_ref[...], False)
            oa_ref[0, rs, vs] = (_dot(af_ref[0, hd, rs, :], v) + inter).astype(BF16)
            s_ref[...] = s_new

            rs = slice((cps - 1 - sc) * c, (cps - sc) * c)
            s_ref = s_scr[GLA_HEADS + hd]
            inter, s_new = _gla_inter(qb_ref[0, rs, ks], kb_ref[0, rs, ks], vb_ref[0, rs, vs], gb_ref[0, rs, ks],
                                      s_ref[...], True)
            ob_ref[0, rs, vs] = inter.astype(BF16)
            s_ref[...] = s_new

    @pl.when(n == pl.num_programs(1) - 1)
    def _():
        for k, s_ref in enumerate(s_scr):
            sfin_ref[0, k] = s_ref[...]


def _gla_scan(p, s0, tm):
    b, l, _ = p["q"].shape
    amat = _gla_intra(p, tm)
    c = GLA_CHUNK * min(GLA_CHUNKS_PER_STEP, l // GLA_CHUNK)
    nc = l // c
    fwd = lambda w, col=0: pl.BlockSpec((1, c, w), lambda bi, n: (bi, n, col))
    bwd = lambda w, col=0: pl.BlockSpec((1, c, w), lambda bi, n: (bi, nc - 1 - n, col))
    st = pl.BlockSpec((1, 2 * GLA_HEADS, GLA_DK, GLA_DV), lambda bi, n: (bi, 0, 0, 0))
    a_spec = pl.BlockSpec((1, GLA_HEADS, c, GLA_CHUNK), lambda bi, n: (bi, 0, n, 0))
    oa, ob, sfin = pl.pallas_call(
        _gla_kernel,
        grid=(b, nc),
        in_specs=[fwd(GLA_QK), fwd(GLA_QK), fwd(GLA_V), fwd(GLA_QK, 0), a_spec,
                  bwd(GLA_QK), bwd(GLA_QK), bwd(GLA_V), bwd(GLA_QK, 1), st],
        out_specs=[fwd(GLA_V), bwd(GLA_V), st],
        out_shape=[jax.ShapeDtypeStruct((b, l, GLA_V), BF16), jax.ShapeDtypeStruct((b, l, GLA_V), BF16),
                   jax.ShapeDtypeStruct(s0.shape, F32)],
        scratch_shapes=[pltpu.VMEM((GLA_DK, GLA_DV), F32)] * (2 * GLA_HEADS),
        compiler_params=_cparams(("parallel", "arbitrary")),
        name="gla_scan",
    )(p["q"], p["k"], p["v"], p["gcum"], amat, p["q"], p["k"], p["v"], p["gcum"], s0)
    return oa, ob, sfin


def _split_hi_lo(x):
    hi = x.astype(BF16)
    lo = (x - hi.astype(F32)).astype(BF16)
    return jnp.concatenate([hi, lo], axis=1)


def _ssd_factors(sd, ex, d):
    c = sd.shape[0]
    nh = SSD_HEADS
    lane = lax.broadcasted_iota(jnp.int32, (1, LANES), 1)
    sel = (lane >= (2 + d) * nh) & (lane < (3 + d) * nh)
    tot = sd[0:1, :] if d == 1 else sd[c - 1:c, :]
    dt_under_cum = pltpu.roll(sd, 2 * nh, 1)
    e_in = jnp.where(sel, jnp.exp2(sd), 0.0)
    w_st = jnp.where(sel, dt_under_cum * jnp.exp2(tot - sd), 0.0)
    e_tot = jnp.broadcast_to(jnp.where(sel, jnp.exp2(tot), 0.0), (SUBLANES, LANES))
    expand = lambda f: _dot(_split_hi_lo(f), ex)
    return expand(e_in), expand(w_st), expand(e_tot)[0:1, :]


def _ssd_kernel(xsa_ref, bca_ref, bTa_ref, sda_ref, sdTa_ref, xsb_ref, bcb_ref, bTb_ref, sdb_ref,
                ex_ref, s0_ref, ya_ref, yb_ref, sfin_ref, *s_scr):
    c = SSD_CHUNK
    nh = SSD_HEADS
    gw = HEADS_PER_GROUP * SSD_HEADDIM
    n = pl.program_id(1)

    @pl.when(n == 0)
    def _():
        for k, s_ref in enumerate(s_scr):
            s_ref[...] = s0_ref[0, k]

    ii = lax.broadcasted_iota(jnp.int32, (c, c), 0)
    jj = lax.broadcasted_iota(jnp.int32, (c, c), 1)
    lower = jj <= ii
    upper = jj >= ii
    lo_half = lax.broadcasted_iota(jnp.int32, (1, LANES), 1) < SSD_HEADDIM

    cps = xsa_ref.shape[1] // c
    for sc in range(cps):
        ra = slice(sc * c, (sc + 1) * c)
        rb = slice((cps - 1 - sc) * c, (cps - sc) * c)
        sda = sda_ref[0, ra, :]
        sdTa = sdTa_ref[0, :, ra]
        ein_f, wst_f, dec_f = _ssd_factors(sda, ex_ref[0], 0)
        ein_b, wst_b, dec_b = _ssd_factors(sdb_ref[0, rb, :], ex_ref[1], 1)

        def weights(h, d, mask, sda=sda, sdTa=sdTa):
            col = (2 + d) * nh + h
            ci = sda[:, col:col + 1]
            cj = sdTa[col:col + 1, :]
            dtj = sdTa[d * nh + h:d * nh + h + 1, :]
            return jnp.where(mask, jnp.exp2(jnp.minimum(ci - cj, 0.0)) * dtj, 0.0)

        for gi in range(SSD_GROUPS):
            gs = slice(gi * gw, (gi + 1) * gw)
            ns = slice(gi * SSD_STATE, (gi + 1) * SSD_STATE)
            bm = bca_ref[0, ra, ns]
            cm = bca_ref[0, ra, SSD_BC + gi * SSD_STATE:SSD_BC + (gi + 1) * SSD_STATE]
            gmat = _dot_nt(cm, bm)
            s_f = s_scr[gi]
            s_b = s_scr[SSD_GROUPS + gi]
            y_inter = _dot(cm, s_f[...].astype(BF16)) * ein_f[:, gs]
            for hp in range(HEADS_PER_GROUP // 2):
                col0 = (gi * HEADS_PER_GROUP + 2 * hp) * SSD_HEADDIM
                xpair = xsa_ref[0, ra, col0:col0 + LANES]
                scores = []
                for t in range(2):
                    h = gi * HEADS_PER_GROUP + 2 * hp + t
                    scores.append((gmat * (weights(h, 0, lower) + weights(h, 1, upper))).astype(BF16))
                zero = jnp.zeros_like(xpair)
                rhs = jnp.concatenate([jnp.where(lo_half, xpair, zero), jnp.where(lo_half, zero, xpair)], axis=0)
                y_pair = _dot(jnp.concatenate(scores, axis=1), rhs) + y_inter[:, hp * LANES:(hp + 1) * LANES]
                ya_ref[0, ra, col0:col0 + LANES] = y_pair.astype(BF16)
            wx = (xsa_ref[0, ra, gs].astype(F32) * wst_f[:, gs]).astype(BF16)
            s_f[...] = s_f[...] * dec_f[:, gs] + _dot(bTa_ref[0, ns, ra], wx)

            cmb = bcb_ref[0, rb, SSD_BC + gi * SSD_STATE:SSD_BC + (gi + 1) * SSD_STATE]
            yb_ref[0, rb, gs] = (_dot(cmb, s_b[...].astype(BF16)) * ein_b[:, gs]).astype(BF16)
            wxb = (xsb_ref[0, rb, gs].astype(F32) * wst_b[:, gs]).astype(BF16)
            s_b[...] = s_b[...] * dec_b[:, gs] + _dot(bTb_ref[0, ns, rb], wxb)

    @pl.when(n == pl.num_programs(1) - 1)
    def _():
        for k, s_ref in enumerate(s_scr):
            sfin_ref[0, k] = s_ref[...]


def _ssd_expander():
    r = lax.broadcasted_iota(jnp.int32, (2, 2 * LANES, SSD_INNER), 1) % LANES
    d = lax.broadcasted_iota(jnp.int32, (2, 2 * LANES, SSD_INNER), 0)
    h = lax.broadcasted_iota(jnp.int32, (2, 2 * LANES, SSD_INNER), 2) // SSD_HEADDIM
    return (r == (2 + d) * SSD_HEADS + h).astype(BF16)


def _ssd_scan(p, s0):
    b, l, _ = p["xs"].shape
    c = SSD_CHUNK * min(SSD_CHUNKS_PER_STEP, l // SSD_CHUNK)
    nc = l // c
    specs = []
    for rev in (False, True):
        idx = (lambda n: nc - 1 - n) if rev else (lambda n: n)
        tok = lambda w, idx=idx: pl.BlockSpec((1, c, w), lambda bi, n: (bi, idx(n), 0))
        chan = lambda r, idx=idx: pl.BlockSpec((1, r, c), lambda bi, n: (bi, 0, idx(n)))
        specs.append([tok(SSD_INNER), tok(2 * SSD_BC), chan(SSD_BC), tok(LANES), chan(4 * SSD_HEADS)])
    st = pl.BlockSpec((1, 2 * SSD_GROUPS, SSD_STATE, HEADS_PER_GROUP * SSD_HEADDIM), lambda bi, n: (bi, 0, 0, 0))
    ex = _ssd_expander()
    ya, yb, sfin = pl.pallas_call(
        _ssd_kernel,
        grid=(b, nc),
        in_specs=specs[0] + specs[1][:4] + [_const_spec(ex.shape), st],
        out_specs=[specs[0][0], specs[1][0], st],
        out_shape=[jax.ShapeDtypeStruct((b, l, SSD_INNER), BF16), jax.ShapeDtypeStruct((b, l, SSD_INNER), BF16),
                   jax.ShapeDtypeStruct(s0.shape, F32)],
        scratch_shapes=[pltpu.VMEM((SSD_STATE, HEADS_PER_GROUP * SSD_HEADDIM), F32)] * (2 * SSD_GROUPS),
        compiler_params=_cparams(("parallel", "arbitrary")),
        name="ssd_scan",
    )(p["xs"], p["bc"], p["bT"], p["sd"], p["sdT"], p["xs"], p["bc"], p["bT"], p["sd"], ex, s0)
    return ya, yb, sfin


def _post0_kernel(of_ref, ob_ref, g_ref, yf_ref, yb_ref, xs_ref, z_ref, x_ref, mod_ref,
                  gn_ref, dsk_ref, sn_ref, wo_ref, out_ref):
    o = of_ref[0].astype(F32) + ob_ref[0].astype(F32)
    parts = []
    for hd in range(GLA_HEADS):
        oh = o[:, hd * GLA_DV:(hd + 1) * GLA_DV]
        parts.append(oh * lax.rsqrt(jnp.mean(oh * oh, axis=-1, keepdims=True) + EPS))
    gla = jnp.concatenate(parts, axis=1) * gn_ref[...] * _silu(g_ref[0].astype(F32))
    xs = xs_ref[0].astype(F32)
    y = (yf_ref[0].astype(F32) + yb_ref[0].astype(F32) + xs * dsk_ref[...]) * _silu(z_ref[0].astype(F32))
    gw = SSD_INNER // SSD_GROUPS
    parts = []
    for gi in range(SSD_GROUPS):
        yg = y[:, gi * gw:(gi + 1) * gw]
        parts.append(yg * lax.rsqrt(jnp.mean(yg * yg, axis=-1, keepdims=True) + EPS))
    ssd = jnp.concatenate(parts, axis=1) * sn_ref[...]
    res = _dot(gla.astype(BF16), wo_ref[:GLA_V, :]) + _dot(ssd.astype(BF16), wo_ref[GLA_V:, :])
    out_ref[0] = x_ref[0] + mod_ref[0, 2:3, :] * res


def _post0(p, of, ob, yf, yb, x, mod, wp, tm):
    b, l, _ = x.shape
    tm = min(tm, l)
    bm = mod.shape[0]
    mod_idx = (lambda bi, i: (bi, 0, 0)) if bm > 1 else (lambda bi, i: (0, 0, 0))
    tok = lambda w: pl.BlockSpec((1, tm, w), lambda bi, i: (bi, i, 0))
    return pl.pallas_call(
        _post0_kernel,
        grid=(b, l // tm),
        in_specs=[tok(GLA_V), tok(GLA_V), tok(GLA_V), tok(SSD_INNER), tok(SSD_INNER), tok(SSD_INNER),
                  tok(SSD_INNER), tok(D_MODEL), pl.BlockSpec((1, N_MOD, D_MODEL), mod_idx),
                  _const_spec((1, GLA_V)), _const_spec((1, SSD_INNER)), _const_spec((1, SSD_INNER)),
                  _const_spec(wp["w_out"].shape)],
        out_specs=tok(D_MODEL),
        out_shape=jax.ShapeDtypeStruct((b, l, D_MODEL), F32),
        compiler_params=_cparams(("parallel", "arbitrary")),
        name="post0",
    )(of, ob, p["g"], yf, yb, p["xs"], p["z"], x, mod, wp["gla_norm"], wp["d_skip"], wp["ssd_norm"], wp["w_out"])


def _ffn_kernel(x_ref, mod_ref, nw_ref, w1_ref, w3_ref, w2_ref, fn_ref, out_ref, *, final):
    x = x_ref[0]
    h = _norm_mod(x, nw_ref[...], mod_ref[0, 3:4, :], mod_ref[0, 4:5, :]).astype(BF16)
    half = D_FF // 2
    y = jnp.zeros_like(x)
    for j in range(2):
        cs = slice(j * half, (j + 1) * half)
        u = (_silu(_dot(h, w1_ref[0, :, cs])) * _dot(h, w3_ref[0, :, cs])).astype(BF16)
        y = y + _dot(u, w2_ref[0, cs, :])
    out = x + mod_ref[0, 5:6, :] * y
    if final:
        out = out * lax.rsqrt(jnp.mean(out * out, axis=-1, keepdims=True) + EPS) * fn_ref[...]
    out_ref[0] = out


def _ffn(x, mod, nw, w1, w3, w2, layer, final_norm, final, tm):
    b, l, _ = x.shape
    tm = min(tm, l)
    bm = mod.shape[0]
    mod_idx = (lambda bi, i: (bi, 0, 0)) if bm > 1 else (lambda bi, i: (0, 0, 0))
    tok = pl.BlockSpec((1, tm, D_MODEL), lambda bi, i: (bi, i, 0))
    wspec = lambda w: pl.BlockSpec((1,) + w.shape[1:], lambda bi, i: (layer, 0, 0), pipeline_mode=pl.Buffered(1))
    return pl.pallas_call(
        functools.partial(_ffn_kernel, final=final),
        grid=(b, l // tm),
        in_specs=[tok, pl.BlockSpec((1, N_MOD, D_MODEL), mod_idx), _const_spec((1, D_MODEL)),
                  wspec(w1), wspec(w3), wspec(w2), _const_spec((1, D_MODEL))],
        out_specs=tok,
        out_shape=jax.ShapeDtypeStruct((b, l, D_MODEL), F32),
        compiler_params=_cparams(("parallel", "arbitrary")),
        name="ffn_final" if final else "ffn",
    )(x, mod, nw, w1, w3, w2, final_norm)


def _inproj1_kernel(*refs, tm, ncol):
    if ncol:
        x_ref, xp_ref, xn_ref, perm_ref = refs[:4]
        refs = refs[4:]
    else:
        x_ref, xp_ref, xn_ref = refs[:3]
        refs = refs[3:]
    (mod_ref, nw_ref, wup_ref, cw_ref, cb_ref, bdq_ref, bdkT_ref, bdv_ref, wifq_ref, wifk_ref, wifv_ref,
     bif_ref, q_ref, kT_ref, v_ref, xc_ref, z_ref, gc_ref, gT_ref, hext, xm_scr) = refs
    i = pl.program_id(1)
    first = i == 0
    last = i == pl.num_programs(1) - 1
    shift = mod_ref[0, 0:1, :]
    scale = mod_ref[0, 1:2, :]
    nw = nw_ref[...]
    if ncol:
        hext[0:HALO, :] = _norm_mod(xp_ref[0, :, SUBLANES - 1, :], nw, shift, scale).astype(BF16)
        hn = _norm_mod(x_ref[0].reshape(tm, D_MODEL), nw, shift, scale).astype(BF16)
        hext[HALO:HALO + tm, :] = _dot(perm_ref[...], hn).astype(BF16)
        hext[HALO + tm:2 * HALO + tm, :] = _norm_mod(xn_ref[0, :, 0, :], nw, shift, scale).astype(BF16)
    else:
        hext[0:HALO, :] = _norm_mod(xp_ref[0], nw, shift, scale).astype(BF16)
        hext[HALO:HALO + tm, :] = _norm_mod(x_ref[0], nw, shift, scale).astype(BF16)
        hext[HALO + tm:2 * HALO + tm, :] = _norm_mod(xn_ref[0], nw, shift, scale).astype(BF16)
    h = hext[HALO:HALO + tm, :]

    xm_scr[...] = _dot(hext[...], wup_ref[:, :MLSTM_INNER])
    xc = _silu(_conv3(xm_scr, tm, cw_ref, cb_ref, first, last))
    xc_ref[0] = xc.astype(BF16)

    gates = jnp.zeros((tm, LANES), F32) + bif_ref[...]
    for t in range(N_BD_TILES):
        cs = slice(t * BD_TILE, (t + 1) * BD_TILE)
        xct = xc[:, cs].astype(BF16)
        xmt = xm_scr[HALO:HALO + tm, cs].astype(BF16)
        qt = _dot(xct, bdq_ref[t])
        vt = _dot(xmt, bdv_ref[t])
        q_ref[0, :, cs] = qt.astype(BF16)
        v_ref[0, :, cs] = vt.astype(BF16)
        kT_ref[0, cs, :] = (_dot_nt(bdkT_ref[t], xct) * (MLSTM_DH ** -0.5)).astype(BF16)
        gates = gates + _dot(qt.astype(BF16), wifq_ref[cs, :]) + _dot(xct, wifk_ref[cs, :]) \
            + _dot(vt.astype(BF16), wifv_ref[cs, :])

    gt = jnp.transpose(gates)[:4 * MLSTM_HEADS]
    r = lax.broadcasted_iota(jnp.int32, gt.shape, 0)
    lf = _log_sigmoid(gt)
    cum = jnp.where(r < 2 * MLSTM_HEADS, _seg_cumsum(lf, MLSTM_CHUNK, 1, False),
                    _seg_cumsum(lf, MLSTM_CHUNK, 1, True))
    packed = jnp.where(r % (2 * MLSTM_HEADS) < MLSTM_HEADS, gt, cum)
    a = gt - pltpu.roll(cum, 4 * MLSTM_HEADS - MLSTM_HEADS, 0)
    amax = jnp.where(r < 2 * MLSTM_HEADS, _seg_scan(a, MLSTM_CHUNK, 1, False, use_max=True),
                     _seg_scan(a, MLSTM_CHUNK, 1, True, use_max=True))
    full = jnp.concatenate([packed, amax], axis=0)
    gT_ref[0] = full
    gc_ref[0] = jnp.transpose(jnp.concatenate([full, jnp.zeros((LANES - MLSTM_GATE_ROWS, tm), F32)], axis=0))

    z_ref[0] = _dot(h, wup_ref[:, MLSTM_INNER:]).astype(BF16)


def _inproj1(x, mod, nw, wp, tm, colmajor):
    b, l, _ = x.shape
    tm = min(tm, l)
    nt = l // tm
    bm = mod.shape[0]
    mod_idx = (lambda bi, i: (bi, 0, 0)) if bm > 1 else (lambda bi, i: (0, 0, 0))
    if colmajor:
        rows = l // GRID_W
        ncol = tm // rows
        assert ncol % SUBLANES == 0 and rows % HALO == 0
        xv = x.reshape(b, rows, GRID_W, D_MODEL)
        rb = rows // HALO
        cb = ncol // SUBLANES
        ncb = GRID_W // SUBLANES
        x_spec = pl.BlockSpec((1, rows, ncol, D_MODEL), lambda bi, i: (bi, 0, i, 0))
        xp_spec = pl.BlockSpec((1, HALO, SUBLANES, D_MODEL),
                               lambda bi, i: (bi, rb - 1, jnp.maximum(i * cb - 1, 0), 0))
        xn_spec = pl.BlockSpec((1, HALO, SUBLANES, D_MODEL),
                               lambda bi, i: (bi, 0, jnp.minimum((i + 1) * cb, ncb - 1), 0))
        dst = lax.broadcasted_iota(jnp.int32, (tm, tm), 0)
        src = lax.broadcasted_iota(jnp.int32, (tm, tm), 1)
        perm = (src == (dst % rows) * ncol + dst // rows).astype(BF16)
        lead_specs = [x_spec, xp_spec, xn_spec, _const_spec((tm, tm))]
        lead_args = [xv, xv, xv, perm]
    else:
        ncol = 0
        xv = x
        hb = tm // HALO
        nhb = l // HALO
        x_spec = pl.BlockSpec((1, tm, D_MODEL), lambda bi, i: (bi, i, 0))
        xp_spec = pl.BlockSpec((1, HALO, D_MODEL), lambda bi, i: (bi, jnp.maximum(i * hb - 1, 0), 0))
        xn_spec = pl.BlockSpec((1, HALO, D_MODEL), lambda bi, i: (bi, jnp.minimum((i + 1) * hb, nhb - 1), 0))
        lead_specs = [x_spec, xp_spec, xn_spec]
        lead_args = [xv, xv, xv]
    tok = lambda w: pl.BlockSpec((1, tm, w), lambda bi, i: (bi, i, 0))
    names = ["w_up", "conv_w", "conv_b", "bdq", "bdkT", "bdv", "wif_q", "wif_k", "wif_v", "b_if"]
    q, kT, v, xc, z, gc, gT = pl.pallas_call(
        functools.partial(_inproj1_kernel, tm=tm, ncol=ncol),
        grid=(b, nt),
        in_specs=lead_specs + [pl.BlockSpec((1, N_MOD, D_MODEL), mod_idx), _const_spec((1, D_MODEL))]
        + [_const_spec(wp[n].shape) for n in names],
        out_specs=[tok(MLSTM_INNER), pl.BlockSpec((1, MLSTM_INNER, tm), lambda bi, i: (bi, 0, i)),
                   tok(MLSTM_INNER), tok(MLSTM_INNER), tok(MLSTM_INNER), tok(LANES),
                   pl.BlockSpec((1, MLSTM_GATE_ROWS, tm), lambda bi, i: (bi, 0, i))],
        out_shape=[jax.ShapeDtypeStruct((b, l, MLSTM_INNER), BF16), jax.ShapeDtypeStruct((b, MLSTM_INNER, l), BF16),
                   jax.ShapeDtypeStruct((b, l, MLSTM_INNER), BF16), jax.ShapeDtypeStruct((b, l, MLSTM_INNER), BF16),
                   jax.ShapeDtypeStruct((b, l, MLSTM_INNER), BF16), jax.ShapeDtypeStruct((b, l, LANES), F32),
                   jax.ShapeDtypeStruct((b, MLSTM_GATE_ROWS, l), F32)],
        scratch_shapes=[pltpu.VMEM((tm + 2 * HALO, D_MODEL), BF16),
                        pltpu.VMEM((tm + 2 * HALO, MLSTM_INNER), F32)],
        compiler_params=_cparams(("parallel", "arbitrary")),
        name="inproj1",
    )(*lead_args, mod, nw, *[wp[n] for n in names])
    return dict(q=q, kT=kT, v=v, xc=xc, z=z, gc=gc, gT=gT)


def _mlstm_head(q, kT, v, gc, gT, c_scr, cb_scr, n_scr, m_scr, d, hd, reverse):
    c = MLSTM_CHUNK
    dh = MLSTM_DH
    idx = d * MLSTM_HEADS + hd
    ri = d * 2 * MLSTM_HEADS + hd
    rb = ri + MLSTM_HEADS
    rm = 4 * MLSTM_HEADS + ri
    c_scr, cb_scr, n_scr, m_scr = c_scr[idx], cb_scr[idx], n_scr[idx], m_scr[idx]
    bi = gc[:, rb:rb + 1]
    am_i = gc[:, rm:rm + 1]
    bj = gT[rb:rb + 1, :]
    a_j = gT[ri:ri + 1, :] - bj
    am_j = gT[rm:rm + 1, :]
    m0 = m_scr[0:1, 0:1]
    big_m = jnp.maximum(m0, am_i)
    ii = lax.broadcasted_iota(jnp.int32, (c, c), 0)
    jj = lax.broadcasted_iota(jnp.int32, (c, c), 1)
    causal = (jj >= ii) if reverse else (jj <= ii)
    s = _dot(q, kT) * jnp.where(causal, jnp.exp(a_j - big_m), 0.0)
    w_inter = jnp.exp(m0 - big_m)
    qf = q.astype(F32)
    den = jnp.sum(s, axis=-1, keepdims=True) \
        + w_inter * jnp.sum(qf * n_scr[0:1, :], axis=-1, keepdims=True)
    num = _dot(s.astype(BF16), v) + _dot((qf * w_inter).astype(BF16), cb_scr[...])
    h = num * (1.0 / jnp.maximum(jnp.abs(den), jnp.exp(-(bi + big_m))))

    last = 0 if reverse else c - 1
    m_inner = jnp.maximum(m0, am_j[:, last:last + 1])
    kw = kT.astype(F32) * jnp.exp(a_j - m_inner)
    decay = jnp.exp(m0 - m_inner)
    kwb = kw.astype(BF16)
    c_new = decay * c_scr[...] + _dot(kwb, v)
    c_scr[...] = c_new
    cb_scr[...] = c_new.astype(BF16)
    n_scr[...] = decay * n_scr[...] + _dot_nt(jnp.ones((SUBLANES, c), BF16), kwb)
    m_scr[...] = jnp.broadcast_to(bj[:, last:last + 1] + m_inner, m_scr.shape)
    return h


def _mlstm_kernel(*refs, has_init, emit_final):
    ins = refs[:10]
    pos = 10
    if has_init:
        c0_ref, n0_ref, m0_ref = refs[pos:pos + 3]
        pos += 3
    hf_ref, hb_ref = refs[pos:pos + 2]
    pos += 2
    if emit_final:
        cfin_ref, nfin_ref, mfin_ref = refs[pos:pos + 3]
        pos += 3
    nst = 2 * MLSTM_HEADS
    c_scr, cb_scr, n_scr, m_scr = (refs[pos + k * nst:pos + (k + 1) * nst] for k in range(4))
    n = pl.program_id(1)

    @pl.when(n == 0)
    def _():
        for k in range(nst):
            if has_init:
                c_scr[k][...] = c0_ref[0, k]
                cb_scr[k][...] = c0_ref[0, k].astype(BF16)
                n_scr[k][...] = n0_ref[0, k]
                m_scr[k][...] = m0_ref[0, k]
            else:
                c_scr[k][...] = jnp.zeros_like(c_scr[k])
                cb_scr[k][...] = jnp.zeros_like(cb_scr[k])
                n_scr[k][...] = jnp.zeros_like(n_scr[k])
                m_scr[k][...] = jnp.full(m_scr[k].shape, M_INIT, F32)

    for d, h_ref in enumerate((hf_ref, hb_ref)):
        q_ref, kT_ref, v_ref, gc_ref, gT_ref = ins[5 * d:5 * d + 5]
        gc = gc_ref[0]
        gT = gT_ref[0]
        for hd in range(MLSTM_HEADS):
            cs = slice(hd * MLSTM_DH, (hd + 1) * MLSTM_DH)
            h = _mlstm_head(q_ref[0, :, cs], kT_ref[0, cs, :], v_ref[0, :, cs], gc, gT,
                            c_scr, cb_scr, n_scr, m_scr, d, hd, d == 1)
            h_ref[0, :, cs] = h.astype(BF16)

    if emit_final:
        @pl.when(n == pl.num_programs(1) - 1)
        def _():
            for k in range(nst):
                cfin_ref[0, k] = c_scr[k][...]
                nfin_ref[0, k] = n_scr[k][...]
                mfin_ref[0, k] = m_scr[k][...]


def _mlstm_scan(p, init, emit_final):
    b, l, _ = p["q"].shape
    c = MLSTM_CHUNK
    nc = l // c
    dh = MLSTM_DH
    nst = 2 * MLSTM_HEADS
    specs = []
    for rev in (False, True):
        idx = (lambda n: nc - 1 - n) if rev else (lambda n: n)
        specs.append([
            pl.BlockSpec((1, c, MLSTM_INNER), lambda bi, n, idx=idx: (bi, idx(n), 0)),
            pl.BlockSpec((1, MLSTM_INNER, c), lambda bi, n, idx=idx: (bi, 0, idx(n))),
            pl.BlockSpec((1, c, MLSTM_INNER), lambda bi, n, idx=idx: (bi, idx(n), 0)),
            pl.BlockSpec((1, c, LANES), lambda bi, n, idx=idx: (bi, idx(n), 0)),
            pl.BlockSpec((1, MLSTM_GATE_ROWS, c), lambda bi, n, idx=idx: (bi, 0, idx(n))),
        ])
    st_shapes = [(b, nst, dh, dh), (b, nst, SUBLANES, dh), (b, nst, SUBLANES, LANES)]
    st_index = lambda bi, n: (bi, 0, 0, 0)
    args = (p["q"], p["kT"], p["v"], p["gc"], p["gT"])
    operands = [*args, *args]
    in_specs = specs[0] + specs[1]
    if init is not None:
        operands += list(init)
        in_specs += [pl.BlockSpec((1,) + s[1:], st_index, pipeline_mode=pl.Buffered(1)) for s in st_shapes]
    out_specs = [specs[0][0], specs[1][0]]
    out_shape = [jax.ShapeDtypeStruct((b, l, MLSTM_INNER), BF16), jax.ShapeDtypeStruct((b, l, MLSTM_INNER), BF16)]
    if emit_final:
        out_specs += [pl.BlockSpec((1,) + s[1:], st_index) for s in st_shapes]
        out_shape += [jax.ShapeDtypeStruct(s, F32) for s in st_shapes]
    res = pl.pallas_call(
        functools.partial(_mlstm_kernel, has_init=init is not None, emit_final=emit_final),
        grid=(b, nc),
        in_specs=in_specs,
        out_specs=out_specs,
        out_shape=out_shape,
        scratch_shapes=[pltpu.VMEM((dh, dh), F32)] * nst + [pltpu.VMEM((dh, dh), BF16)] * nst
        + [pltpu.VMEM((SUBLANES, dh), F32)] * nst + [pltpu.VMEM((SUBLANES, LANES), F32)] * nst,
        compiler_params=_cparams(("parallel", "arbitrary")),
        name="mlstm_scan",
    )(*operands)
    return res[0], res[1], tuple(res[2:])


def _post1_kernel(hf_ref, hb_ref, xc_ref, z_ref, x_ref, mod_ref, mh_ref, sk_ref, wd_ref, out_ref, *, tm, ncol):
    hs = hf_ref[0].astype(F32) + hb_ref[0].astype(F32)
    parts = []
    for hd in range(MLSTM_HEADS):
        hh = hs[:, hd * MLSTM_DH:(hd + 1) * MLSTM_DH]
        mu = jnp.mean(hh, axis=-1, keepdims=True)
        cen = hh - mu
        parts.append(cen * lax.rsqrt(jnp.mean(cen * cen, axis=-1, keepdims=True) + EPS))
    feat = (jnp.concatenate(parts, axis=1) * mh_ref[...] + sk_ref[...] * xc_ref[0].astype(F32)) \
        * _silu(z_ref[0].astype(F32))
    res = mod_ref[0, 2:3, :] * _dot(feat.astype(BF16), wd_ref[...])
    if ncol:
        rows = tm // ncol
        for j in range(ncol):
            out_ref[0, :, j, :] = x_ref[0, :, j, :] + res[j * rows:(j + 1) * rows, :]
    else:
        out_ref[0] = x_ref[0] + res


def _post1(p, hf, hb, x, mod, wp, tm, colmajor):
    b, l, _ = x.shape
    tm = min(tm, l)
    bm = mod.shape[0]
    mod_idx = (lambda bi, i: (bi, 0, 0)) if bm > 1 else (lambda bi, i: (0, 0, 0))
    tok = lambda w: pl.BlockSpec((1, tm, w), lambda bi, i: (bi, i, 0))
    if colmajor:
        rows = l // GRID_W
        ncol = tm // rows
        xv = x.reshape(b, rows, GRID_W, D_MODEL)
        x_spec = pl.BlockSpec((1, rows, ncol, D_MODEL), lambda bi, i: (bi, 0, i, 0))
    else:
        ncol = 0
        xv = x
        x_spec = tok(D_MODEL)
    out = pl.pallas_call(
        functools.partial(_post1_kernel, tm=tm, ncol=ncol),
        grid=(b, l // tm),
        in_specs=[tok(MLSTM_INNER), tok(MLSTM_INNER), tok(MLSTM_INNER), tok(MLSTM_INNER), x_spec,
                  pl.BlockSpec((1, N_MOD, D_MODEL), mod_idx),
                  _const_spec((1, MLSTM_INNER)), _const_spec((1, MLSTM_INNER)), _const_spec(wp["w_down"].shape)],
        out_specs=x_spec,
        out_shape=jax.ShapeDtypeStruct(xv.shape, F32),
        compiler_params=_cparams(("parallel", "arbitrary")),
        name="post1",
    )(hf, hb, p["xc"], p["z"], xv, mod, wp["mh_norm"], wp["skip"], wp["w_down"])
    return out.reshape(b, l, D_MODEL)


def _prep_even(w_in, w_gate, b_gate, gla_norm, conv_w, conv_b, a_log, dt_bias, d_skip, ssd_norm, w_out):
    sizes = (GLA_QK, GLA_QK, GLA_V, GLA_V, 2 * GLA_GATE_RANK, SSD_INNER, SSD_XBC, 2 * SSD_HEADS)
    offs = [0]
    for s in sizes:
        offs.append(offs[-1] + s)
    w16 = w_in.astype(BF16)
    w_main = jnp.concatenate([w16[:, offs[0]:offs[4]], w16[:, offs[5]:offs[7]]], axis=1)
    pad = jnp.zeros((D_MODEL, LANES - 2 * GLA_GATE_RANK - 2 * SSD_HEADS), BF16)
    w_small = jnp.concatenate([w16[:, offs[4]:offs[5]], w16[:, offs[7]:offs[8]], pad], axis=1)
    zero = jnp.zeros((GLA_GATE_RANK, GLA_QK), w_gate.dtype)
    wg = jnp.concatenate([jnp.concatenate([w_gate[0], zero], axis=1),
                          jnp.concatenate([zero, w_gate[1]], axis=1)], axis=0).astype(BF16)
    return dict(
        w_main=w_main, w_small=w_small, w_dtT=jnp.transpose(w_in[:, offs[7]:offs[8]]).astype(BF16),
        w_gate=wg, b_gate=b_gate.reshape(1, 2 * GLA_QK),
        conv_w=conv_w, conv_b=conv_b.reshape(1, SSD_XBC),
        dt_bias=dt_bias.reshape(1, 2 * SSD_HEADS), a_log=a_log.reshape(1, 2 * SSD_HEADS),
        dt_biasT=dt_bias.reshape(2 * SSD_HEADS, 1), a_logT=a_log.reshape(2 * SSD_HEADS, 1),
        gla_norm=jnp.tile(gla_norm, GLA_HEADS).reshape(1, GLA_V),
        d_skip=jnp.repeat(d_skip, SSD_HEADDIM).reshape(1, SSD_INNER),
        ssd_norm=ssd_norm.reshape(1, SSD_INNER),
        w_out=w_out.astype(BF16),
    )


def _block_diag_tiles(w, transposed=False):
    per = BD_TILE // QKV_BLOCK
    w4 = w.reshape(N_BD_TILES, per, QKV_BLOCK, QKV_BLOCK)
    blk = lax.broadcasted_iota(jnp.int32, (BD_TILE, BD_TILE), 0) // QKV_BLOCK
    same_block = blk == jnp.transpose(blk)
    if transposed:
        cols = jnp.transpose(w4, (0, 3, 1, 2)).reshape(N_BD_TILES, QKV_BLOCK, BD_TILE)
        full = jnp.tile(cols, (1, per, 1))
    else:
        rows = w4.reshape(N_BD_TILES, BD_TILE, QKV_BLOCK)
        full = jnp.tile(rows, (1, 1, per))
    return jnp.where(same_block, full, 0.0).astype(BF16)


def _prep_odd(w_up, conv_w, conv_b, wq, wk, wv, w_if, b_if, mh_norm, skip, w_down):
    ng = 4 * MLSTM_HEADS
    wif = jnp.transpose(w_if, (1, 0, 2)).reshape(3 * MLSTM_INNER, ng)
    wif = jnp.concatenate([wif, jnp.zeros((3 * MLSTM_INNER, LANES - ng), w_if.dtype)], axis=1)
    wif_k = jnp.einsum("nde,neg->ndg", wk, wif[MLSTM_INNER:2 * MLSTM_INNER].reshape(-1, QKV_BLOCK, LANES),
                       precision=lax.Precision.HIGHEST)
    wif_k = wif_k.reshape(MLSTM_INNER, LANES)
    bif = jnp.concatenate([b_if.reshape(1, ng), jnp.zeros((1, LANES - ng), b_if.dtype)], axis=1)
    return dict(
        w_up=w_up.astype(BF16), conv_w=conv_w, conv_b=conv_b.reshape(1, MLSTM_INNER),
        bdq=_block_diag_tiles(wq), bdkT=_block_diag_tiles(wk, transposed=True), bdv=_block_diag_tiles(wv),
        wif_q=wif[:MLSTM_INNER].astype(BF16), wif_k=wif_k.astype(BF16), wif_v=wif[2 * MLSTM_INNER:].astype(BF16),
        b_if=bif,
        mh_norm=mh_norm.reshape(1, MLSTM_INNER), skip=skip.reshape(1, MLSTM_INNER),
        w_down=w_down.astype(BF16),
    )


TOKEN_TILE = 512


def kernel(x, c, ctx, c_ctx, w_mod, b_mod, norm_mix, norm_ffn, ffn_w1, ffn_w3, ffn_w2, a_w_in, a_gla_w_gate,
           a_gla_b_gate, a_gla_norm, a_ssd_conv_w, a_ssd_conv_b, a_ssd_A_log, a_ssd_dt_bias, a_ssd_D, a_ssd_norm,
           a_w_out, b_w_up, b_conv_w, b_conv_b, b_wq, b_wk, b_wv, b_w_if, b_b_if, b_mh_norm, b_skip, b_w_down,
           final_norm):
    tile = TOKEN_TILE
    bsz = x.shape[0]
    depth = w_mod.shape[0]
    pad_rows = (-(bsz + 1)) % 8
    cc = jnp.concatenate([c, c_ctx[None, :], jnp.zeros((pad_rows, D_MODEL), c.dtype)], axis=0)
    mods = _modulation(cc, w_mod, b_mod).reshape(depth, bsz + 1 + pad_rows, N_MOD, D_MODEL)
    fnorm = final_norm.reshape(1, D_MODEL)
    w1, w3, w2 = ffn_w1.astype(BF16), ffn_w3.astype(BF16), ffn_w2.astype(BF16)
    for layer in range(depth):
        last = layer == depth - 1
        j = layer // 2
        mx = mods[layer, :bsz]
        mc = mods[layer, bsz:bsz + 1]
        nmix = norm_mix[layer].reshape(1, D_MODEL)
        nffn = norm_ffn[layer].reshape(1, D_MODEL)
        if layer % 2 == 0:
            wp = _prep_even(a_w_in[j], a_gla_w_gate[j], a_gla_b_gate[j], a_gla_norm[j], a_ssd_conv_w[j],
                            a_ssd_conv_b[j], a_ssd_A_log[j], a_ssd_dt_bias[j], a_ssd_D[j], a_ssd_norm[j],
                            a_w_out[j])
            pc = _inproj0(ctx, mc, nmix, wp, tile)
            gla0 = jnp.zeros((bsz, 2 * GLA_HEADS, GLA_DK, GLA_DV), F32)
            ssd0 = jnp.zeros((bsz, 2 * SSD_GROUPS, SSD_STATE, HEADS_PER_GROUP * SSD_HEADDIM), F32)
            ofc, obc, gla_st = _gla_scan(pc, gla0, GLA_INTRA_TILE)
            yfc, ybc, ssd_st = _ssd_scan(pc, ssd0)
            px = _inproj0(x, mx, nmix, wp, tile)
            ofx, obx, _ = _gla_scan(px, gla_st, GLA_INTRA_TILE)
            yfx, ybx, _ = _ssd_scan(px, ssd_st)
            x = _post0(px, ofx, obx, yfx, ybx, x, mx, wp, tile)
            if not last:
                ctx = _post0(pc, ofc, obc, yfc, ybc, ctx, mc, wp, tile)
        else:
            wp = _prep_odd(b_w_up[j], b_conv_w[j], b_conv_b[j], b_wq[j], b_wk[j], b_wv[j], b_w_if[j], b_b_if[j],
                           b_mh_norm[j], b_skip[j], b_w_down[j])
            pc = _inproj1(ctx, mc, nmix, wp, tile, colmajor=False)
            hfc, hbc, state = _mlstm_scan(pc, None, True)
            px = _inproj1(x, mx, nmix, wp, tile, colmajor=True)
            hfx, hbx, _ = _mlstm_scan(px, state, False)
            x = _post1(px, hfx, hbx, x, mx, wp, tile, colmajor=True)
            if not last:
                ctx = _post1(pc, hfc, hbc, ctx, mc, wp, tile, colmajor=False)
        x = _ffn(x, mx, nffn, w1, w3, w2, layer, fnorm, last, tile)
        if not last:
            ctx = _ffn(ctx, mc, nffn, w1, w3, w2, layer, fnorm, False, tile)
    return x
```

```python
import functools

import jax
import jax.numpy as jnp
from jax import lax
from jax.experimental import pallas as pl
from jax.experimental.pallas import tpu as pltpu

F32 = jnp.float32
BF16 = jnp.bfloat16

D_MODEL = 1024
GRID_W = 64
EPS = 1e-6
LOG2_E = 1.4426950408889634
M_INIT = -1e30
N_MOD = 6

GLA_HEADS = 4
GLA_DK = 128
GLA_DV = 256
GLA_QK = GLA_HEADS * GLA_DK
GLA_V = GLA_HEADS * GLA_DV
GLA_GATE_RANK = 16
GLA_GATE_TAU = 16.0

SSD_HEADDIM = 64
SSD_HEADS = 16
SSD_STATE = 128
SSD_GROUPS = 2
SSD_INNER = SSD_HEADS * SSD_HEADDIM
SSD_BC = SSD_GROUPS * SSD_STATE
SSD_XBC = SSD_INNER + 2 * SSD_BC
HEADS_PER_GROUP = SSD_HEADS // SSD_GROUPS

MLSTM_INNER = 2 * D_MODEL
MLSTM_HEADS = 4
MLSTM_DH = MLSTM_INNER // MLSTM_HEADS
QKV_BLOCK = 4
BD_TILE = 256
N_BD_TILES = MLSTM_INNER // BD_TILE
MLSTM_GATE_ROWS = 32

D_FF = 2816

GLA_CHUNK = 64
GLA_CHUNKS_PER_STEP = 8
GLA_INTRA_TILE = 512
SSD_CHUNK = 128
SSD_CHUNKS_PER_STEP = 4
MLSTM_CHUNK = 256
HALO = 16
LANES = 128
SUBLANES = 8
TAIL_ROWS = 128

VMEM_LIMIT = 56 * 1024 * 1024


def _cparams(sem):
    return pltpu.CompilerParams(dimension_semantics=sem, vmem_limit_bytes=VMEM_LIMIT)


def _const_spec(shape):
    nd = len(shape)
    return pl.BlockSpec(shape, lambda *_: (0,) * nd, pipeline_mode=pl.Buffered(1))


def _softplus(x):
    return jnp.maximum(x, 0.0) + jnp.log(1.0 + jnp.exp(-jnp.abs(x)))


def _log_sigmoid(x):
    return -_softplus(-x)


def _silu(x):
    return x / (1.0 + jnp.exp(-x))


def _seg_scan(x, seg, axis, reverse, use_max=False):
    n = x.shape[axis]
    idx = lax.broadcasted_iota(jnp.int32, x.shape, axis) % seg
    s = 1
    while s < seg:
        if reverse:
            shifted = pltpu.roll(x, n - s, axis)
            keep = idx < seg - s
        else:
            shifted = pltpu.roll(x, s, axis)
            keep = idx >= s
        if use_max:
            x = jnp.where(keep, jnp.maximum(x, shifted), x)
        else:
            x = x + jnp.where(keep, shifted, 0.0)
        s *= 2
    return x


def _seg_cumsum(x, seg, axis, reverse):
    return _seg_scan(x, seg, axis, reverse)


def _chunk_cumsum_mxu(x, chunk, reverse):
    rows = x.shape[0]
    hi = x.astype(BF16)
    lo = (x - hi.astype(F32)).astype(BF16)
    i = lax.broadcasted_iota(jnp.int32, (chunk, 2 * chunk), 0)
    j = lax.broadcasted_iota(jnp.int32, (chunk, 2 * chunk), 1) % chunk
    tri = ((j >= i) if reverse else (j <= i)).astype(BF16)
    parts = []
    for c0 in range(0, rows, chunk):
        parts.append(_dot(tri, jnp.concatenate([hi[c0:c0 + chunk], lo[c0:c0 + chunk]], axis=0)))
    return jnp.concatenate(parts, axis=0)


def _norm_mod(xv, nw, shift, scale):
    ms = jnp.mean(xv * xv, axis=-1, keepdims=True)
    return (xv * lax.rsqrt(ms + EPS) * nw) * (1.0 + scale) + shift


def _dot(a, b):
    return jnp.dot(a, b, preferred_element_type=F32)


def _dot_nt(a, b):
    return lax.dot_general(a, b, (((1,), (1,)), ((), ())), preferred_element_type=F32)


def _dot_tn(a, b):
    return lax.dot_general(a, b, (((0,), (0,)), ((), ())), preferred_element_type=F32)


def _mod_kernel(c_ref, w_ref, b_ref, o_ref):
    a = _silu(c_ref[...]).astype(BF16)
    o_ref[0] = _dot(a, w_ref[0].astype(BF16)) + b_ref[0]


def _modulation(cc, w_mod, b_mod):
    depth = w_mod.shape[0]
    n = w_mod.shape[2]
    tn = n // 4
    rows = cc.shape[0]
    return pl.pallas_call(
        _mod_kernel,
        grid=(depth, n // tn),
        in_specs=[
            pl.BlockSpec((rows, D_MODEL), lambda l, j: (0, 0)),
            pl.BlockSpec((1, D_MODEL, tn), lambda l, j: (l, 0, j)),
            pl.BlockSpec((1, 1, tn), lambda l, j: (l, 0, j)),
        ],
        out_specs=pl.BlockSpec((1, rows, tn), lambda l, j: (l, 0, j)),
        out_shape=jax.ShapeDtypeStruct((depth, rows, n), F32),
        compiler_params=_cparams(("arbitrary", "arbitrary")),
        name="modulation",
    )(cc, w_mod, b_mod.reshape(depth, 1, n))


def _conv_pad(scr, tm, first, last):
    scr[HALO - 1:HALO, :] = scr[HALO - 1:HALO, :] * jnp.where(first, 0.0, 1.0)
    scr[HALO + tm:HALO + tm + 1, :] = scr[HALO + tm:HALO + tm + 1, :] * jnp.where(last, 0.0, 1.0)


def _conv_rows(scr, r0, n, cw_ref, cb_ref):
    prev = scr[HALO - 1 + r0:HALO - 1 + r0 + n, :]
    cur = scr[HALO + r0:HALO + r0 + n, :]
    nxt = scr[HALO + 1 + r0:HALO + 1 + r0 + n, :]
    return cw_ref[0:1, :] * prev + cw_ref[1:2, :] * cur + cw_ref[2:3, :] * nxt + cb_ref[...]


def _conv3(scr, tm, cw_ref, cb_ref, first, last):
    _conv_pad(scr, tm, first, last)
    return _conv_rows(scr, 0, tm, cw_ref, cb_ref)


def _inproj0_kernel(x_ref, xp_ref, xn_ref, mod_ref, nw_ref, wmain_ref, wsmall_ref, wdtT_ref, wg_ref, bg_ref,
                    cw_ref, cb_ref, dtb_ref, alog_ref, dtbT_ref, alogT_ref,
                    q_ref, k_ref, v_ref, g_ref, z_ref, gcum_ref, xs_ref, bc_ref, bT_ref, sd_ref, sdT_ref,
                    hext, xbc_scr, *, tm):
    i = pl.program_id(1)
    first = i == 0
    last = i == pl.num_programs(1) - 1
    shift = mod_ref[0, 0:1, :]
    scale = mod_ref[0, 1:2, :]
    nw = nw_ref[...]
    hext[0:HALO, :] = _norm_mod(xp_ref[0], nw, shift, scale).astype(BF16)
    hext[HALO:HALO + tm, :] = _norm_mod(x_ref[0], nw, shift, scale).astype(BF16)
    hext[HALO + tm:2 * HALO + tm, :] = _norm_mod(xn_ref[0], nw, shift, scale).astype(BF16)

    o_k = GLA_QK
    o_v = o_k + GLA_QK
    o_g = o_v + GLA_V
    o_z = o_g + GLA_V
    o_x = o_z + SSD_INNER
    nd = 2 * SSD_HEADS
    h = hext[HALO:HALO + tm, :]

    xbc_scr[...] = _dot(hext[...], wmain_ref[:, o_x:o_x + SSD_XBC])
    _conv_pad(xbc_scr, tm, first, last)
    small = _dot(h, wsmall_ref[...])
    dt_raw = _dot_nt(wdtT_ref[...], h)
    logits = _dot(small[:, :2 * GLA_GATE_RANK].astype(BF16), wg_ref[...]) + bg_ref[...]

    def tail(j):
        rows = slice(j * TAIL_ROWS, (j + 1) * TAIL_ROWS)
        lg = logits[rows]
        la = (jnp.minimum(lg, 0.0) * LOG2_E - jnp.log2(1.0 + jnp.exp2(jnp.abs(lg) * (-LOG2_E)))) \
            * (1.0 / GLA_GATE_TAU)
        gcum_ref[0, rows, :GLA_QK] = _chunk_cumsum_mxu(la[:, :GLA_QK], GLA_CHUNK, False)
        gcum_ref[0, rows, GLA_QK:] = _chunk_cumsum_mxu(la[:, GLA_QK:], GLA_CHUNK, True)
        dtT = _softplus(dt_raw[:, rows] + dtbT_ref[...])
        laT = dtT * (-LOG2_E * jnp.exp(alogT_ref[...]))
        cumT = jnp.concatenate([_seg_cumsum(laT[:SSD_HEADS], SSD_CHUNK, 1, False),
                                _seg_cumsum(laT[SSD_HEADS:], SSD_CHUNK, 1, True)], axis=0)
        packed = jnp.concatenate([dtT, cumT, jnp.zeros((LANES - 2 * nd, TAIL_ROWS), F32)], axis=0)
        sdT_ref[0, :, rows] = packed[:2 * nd]
        sd_ref[0, rows, :] = jnp.transpose(packed)
        y = _silu(_conv_rows(xbc_scr, j * TAIL_ROWS, TAIL_ROWS, cw_ref, cb_ref))
        xs_ref[0, rows, :] = y[:, :SSD_INNER].astype(BF16)
        bc_ref[0, rows, :] = y[:, SSD_INNER:].astype(BF16)
        bT_ref[0, :, rows] = jnp.transpose(y[:, SSD_INNER:SSD_INNER + SSD_BC]).astype(BF16)

    wide = [(q_ref, 0, o_k, GLA_DK ** -0.5), (k_ref, o_k, o_v, None), (v_ref, o_v, o_g, None),
            (g_ref, o_g, o_z, None), (z_ref, o_z, o_x, None)]
    n_tail = tm // TAIL_ROWS
    done = 0
    for idx, (ref, c0, c1, scl) in enumerate(wide):
        res = _dot(h, wmain_ref[:, c0:c1])
        ref[0] = (res if scl is None else res * scl).astype(BF16)
        upto = (n_tail * (idx + 1)) // (len(wide) - 1) if idx < len(wide) - 1 else n_tail
        for j in range(done, min(upto, n_tail)):
            tail(j)
        done = max(done, min(upto, n_tail))


def _inproj0(x, mod, nw, wp, tm):
    b, l, _ = x.shape
    tm = min(tm, l)
    nt = l // tm
    hb = tm // HALO
    nhb = l // HALO
    bm = mod.shape[0]
    mod_idx = (lambda bi, i: (bi, 0, 0)) if bm > 1 else (lambda bi, i: (0, 0, 0))
    tok = lambda w: pl.BlockSpec((1, tm, w), lambda bi, i: (bi, i, 0))
    in_specs = [
        tok(D_MODEL),
        pl.BlockSpec((1, HALO, D_MODEL), lambda bi, i: (bi, jnp.maximum(i * hb - 1, 0), 0)),
        pl.BlockSpec((1, HALO, D_MODEL), lambda bi, i: (bi, jnp.minimum((i + 1) * hb, nhb - 1), 0)),
        pl.BlockSpec((1, N_MOD, D_MODEL), mod_idx),
        _const_spec((1, D_MODEL)),
        _const_spec(wp["w_main"].shape), _const_spec(wp["w_small"].shape), _const_spec(wp["w_dtT"].shape),
        _const_spec(wp["w_gate"].shape), _const_spec(wp["b_gate"].shape),
        _const_spec(wp["conv_w"].shape), _const_spec(wp["conv_b"].shape),
        _const_spec(wp["dt_bias"].shape), _const_spec(wp["a_log"].shape),
        _const_spec(wp["dt_biasT"].shape), _const_spec(wp["a_logT"].shape),
    ]
    outs = [
        ("q", GLA_QK, BF16), ("k", GLA_QK, BF16), ("v", GLA_V, BF16), ("g", GLA_V, BF16),
        ("z", SSD_INNER, BF16), ("gcum", 2 * GLA_QK, F32), ("xs", SSD_INNER, BF16),
        ("bc", 2 * SSD_BC, BF16),
    ]
    out_specs = [tok(w) for _, w, _ in outs]
    out_shape = [jax.ShapeDtypeStruct((b, l, w), dt) for _, w, dt in outs]
    out_specs += [pl.BlockSpec((1, SSD_BC, tm), lambda bi, i: (bi, 0, i)),
                  tok(LANES),
                  pl.BlockSpec((1, 4 * SSD_HEADS, tm), lambda bi, i: (bi, 0, i))]
    out_shape += [jax.ShapeDtypeStruct((b, SSD_BC, l), BF16),
                  jax.ShapeDtypeStruct((b, l, LANES), F32),
                  jax.ShapeDtypeStruct((b, 4 * SSD_HEADS, l), F32)]
    res = pl.pallas_call(
        functools.partial(_inproj0_kernel, tm=tm),
        grid=(b, nt),
        in_specs=in_specs,
        out_specs=out_specs,
        out_shape=out_shape,
        scratch_shapes=[pltpu.VMEM((tm + 2 * HALO, D_MODEL), BF16),
                        pltpu.VMEM((tm + 2 * HALO, SSD_XBC), F32)],
        compiler_params=_cparams(("parallel", "arbitrary")),
        name="inproj0",
    )(x, x, x, mod, nw, wp["w_main"], wp["w_small"], wp["w_dtT"], wp["w_gate"], wp["b_gate"],
      wp["conv_w"], wp["conv_b"], wp["dt_bias"], wp["a_log"], wp["dt_biasT"], wp["a_logT"])
    names = [n for n, _, _ in outs] + ["bT", "sd", "sdT"]
    return dict(zip(names, res))


def _level_ref(g, hsz, reverse):
    c, w = g.shape
    grp = 2 * hsz
    r = hsz if reverse else hsz - 1
    if grp >= 8:
        g3 = g.reshape(c // grp, grp, w)
        return jnp.broadcast_to(g3[:, r:r + 1, :], g3.shape).reshape(c, w)
    g3 = g.reshape(c // 8, 8, w)
    sub = lax.broadcasted_iota(jnp.int32, g3.shape, 1)
    out = None
    for t in range(8 // grp):
        cand = jnp.broadcast_to(g3[:, t * grp + r:t * grp + r + 1, :], g3.shape)
        out = cand if out is None else jnp.where(sub >= t * grp, cand, out)
    return out.reshape(c, w)


def _gla_masks(c, w, reverse):
    row = lax.broadcasted_iota(jnp.int32, (c, w), 0)
    ii = lax.broadcasted_iota(jnp.int32, (c, c), 0)
    jj = lax.broadcasted_iota(jnp.int32, (c, c), 1)
    par = 0 if reverse else 1
    levels = []
    hsz = 1
    while hsz < c:
        is_q = (row // hsz) % 2 == par
        keep = ((ii // (2 * hsz)) == (jj // (2 * hsz))) & ((ii // hsz) % 2 == par) & ((jj // hsz) % 2 != par)
        levels.append((hsz, is_q, keep.astype(F32)))
        hsz *= 2
    return (ii == jj).astype(F32), levels


def _gla_level_operand(q, k, g, hsz, reverse, is_q):
    c = g.shape[0]
    par = 0 if reverse else 1
    if hsz >= SUBLANES:
        parts = []
        for grp in range(c // (2 * hsz)):
            base = grp * 2 * hsz
            ref_row = base + (hsz if reverse else hsz - 1)
            ref = g[ref_row:ref_row + 1, :]
            for half in range(2):
                rows = slice(base + half * hsz, base + (half + 1) * hsz)
                if half == par:
                    parts.append(q[rows] * jnp.exp2(g[rows] - ref))
                else:
                    parts.append(k[rows] * jnp.exp2(ref - g[rows]))
        return jnp.concatenate(parts, axis=0).astype(BF16)
    if hsz == 1:
        step = g - pltpu.roll(g, (c - 1) if reverse else 1, 0)
        return jnp.where(is_q, q * jnp.exp2(step), k).astype(BF16)
    e = jnp.exp2(-jnp.abs(g - _level_ref(g, hsz, reverse)))
    return (jnp.where(is_q, q, k) * e).astype(BF16)


def _gla_intra_kernel(q_ref, k_ref, g_ref, a_ref):
    c = GLA_CHUNK
    masks = [_gla_masks(c, GLA_DK, d == 1) for d in range(2)]
    for sc in range(q_ref.shape[1] // c):
        rs = slice(sc * c, (sc + 1) * c)
        for hd in range(GLA_HEADS):
            ks = slice(hd * GLA_DK, (hd + 1) * GLA_DK)
            qb = q_ref[0, rs, ks]
            kb = k_ref[0, rs, ks]
            q = qb.astype(F32)
            k = kb.astype(F32)
            a = (2.0 * masks[0][0]) * _dot_nt(qb, kb)
            for d in range(2):
                g = g_ref[0, rs, d * GLA_QK + hd * GLA_DK:d * GLA_QK + (hd + 1) * GLA_DK]
                for hsz, is_q, keep in masks[d][1]:
                    mixed = _gla_level_operand(q, k, g, hsz, d == 1, is_q)
                    a = a + keep * _dot_nt(mixed, mixed)
            a_ref[0, hd, rs, :] = a.astype(BF16)


def _gla_intra(p, tm):
    b, l, _ = p["q"].shape
    tm = min(tm, l)
    tok = lambda w: pl.BlockSpec((1, tm, w), lambda bi, i: (bi, i, 0))
    return pl.pallas_call(
        _gla_intra_kernel,
        grid=(b, l // tm),
        in_specs=[tok(GLA_QK), tok(GLA_QK), tok(2 * GLA_QK)],
        out_specs=pl.BlockSpec((1, GLA_HEADS, tm, GLA_CHUNK), lambda bi, i: (bi, 0, i, 0)),
        out_shape=jax.ShapeDtypeStruct((b, GLA_HEADS, l, GLA_CHUNK), BF16),
        compiler_params=_cparams(("parallel", "arbitrary")),
        name="gla_intra",
    )(p["q"], p["k"], p["gcum"])


def _gla_inter(q, k, v, g, s, reverse):
    c = q.shape[0]
    dk = g.shape[1]
    g_tot = g[0:1, :] if reverse else g[c - 1:c, :]
    qi = (q.astype(F32) * jnp.exp2(g)).astype(BF16)
    ki = (k.astype(F32) * jnp.exp2(g_tot - g)).astype(BF16)
    decay = jnp.transpose(jnp.broadcast_to(jnp.exp2(g_tot), (dk, dk)))
    decay = jnp.concatenate([decay] * (v.shape[1] // dk), axis=1)
    return _dot(qi, s.astype(BF16)), s * decay + _dot_tn(ki, v)


def _gla_kernel(qf_ref, kf_ref, vf_ref, gf_ref, af_ref, qb_ref, kb_ref, vb_ref, gb_ref, s0_ref,
                oa_ref, ob_ref, sfin_ref, *s_scr):
    n = pl.program_id(1)

    @pl.when(n == 0)
    def _():
        for k, s_ref in enumerate(s_scr):
            s_ref[...] = s0_ref[0, k]

    c = GLA_CHUNK
    cps = qf_ref.shape[1] // c
    for sc in range(cps):
        for hd in range(GLA_HEADS):
            ks = slice(hd * GLA_DK, (hd + 1) * GLA_DK)
            vs = slice(hd * GLA_DV, (hd + 1) * GLA_DV)
            rs = slice(sc * c, (sc + 1) * c)
            s_ref = s_scr[hd]
            v = vf_ref[0, rs, vs]
            inter, s_new = _gla_inter(qf_ref[0, rs, ks], kf_ref[0, rs, ks], v, gf_ref[0, rs, ks], s_ref[...], False)
            oa_ref[0, rs, vs] = (_dot(af_ref[0, hd, rs, :], v) + inter).astype(BF16)
            s_ref[...] = s_new

            rs = slice((cps - 1 - sc) * c, (cps - sc) * c)
            s_ref = s_scr[GLA_HEADS + hd]
            inter, s_new = _gla_inter(qb_ref[0, rs, ks], kb_ref[0, rs, ks], vb_ref[0, rs, vs], gb_ref[0, rs, ks],
                                      s_ref[...], True)
            ob_ref[0, rs, vs] = inter.astype(BF16)
            s_ref[...] = s_new

    @pl.when(n == pl.num_programs(1) - 1)
    def _():
        for k, s_ref in enumerate(s_scr):
            sfin_ref[0, k] = s_ref[...]


def _gla_scan(p, s0, tm):
    b, l, _ = p["q"].shape
    amat = _gla_intra(p, tm)
    c = GLA_CHUNK * min(GLA_CHUNKS_PER_STEP, l // GLA_CHUNK)
    nc = l // c
    fwd = lambda w, col=0: pl.BlockSpec((1, c, w), lambda bi, n: (bi, n, col))
    bwd = lambda w, col=0: pl.BlockSpec((1, c, w), lambda bi, n: (bi, nc - 1 - n, col))
    st = pl.BlockSpec((1, 2 * GLA_HEADS, GLA_DK, GLA_DV), lambda bi, n: (bi, 0, 0, 0))
    a_spec = pl.BlockSpec((1, GLA_HEADS, c, GLA_CHUNK), lambda bi, n: (bi, 0, n, 0))
    oa, ob, sfin = pl.pallas_call(
        _gla_kernel,
        grid=(b, nc),
        in_specs=[fwd(GLA_QK), fwd(GLA_QK), fwd(GLA_V), fwd(GLA_QK, 0), a_spec,
                  bwd(GLA_QK), bwd(GLA_QK), bwd(GLA_V), bwd(GLA_QK, 1), st],
        out_specs=[fwd(GLA_V), bwd(GLA_V), st],
        out_shape=[jax.ShapeDtypeStruct((b, l, GLA_V), BF16), jax.ShapeDtypeStruct((b, l, GLA_V), BF16),
                   jax.ShapeDtypeStruct(s0.shape, F32)],
        scratch_shapes=[pltpu.VMEM((GLA_DK, GLA_DV), F32)] * (2 * GLA_HEADS),
        compiler_params=_cparams(("parallel", "arbitrary")),
        name="gla_scan",
    )(p["q"], p["k"], p["v"], p["gcum"], amat, p["q"], p["k"], p["v"], p["gcum"], s0)
    return oa, ob, sfin


def _split_hi_lo(x):
    hi = x.astype(BF16)
    lo = (x - hi.astype(F32)).astype(BF16)
    return jnp.concatenate([hi, lo], axis=1)


def _ssd_factors(sd, ex, d):
    c = sd.shape[0]
    nh = SSD_HEADS
    lane = lax.broadcasted_iota(jnp.int32, (1, LANES), 1)
    sel = (lane >= (2 + d) * nh) & (lane < (3 + d) * nh)
    tot = sd[0:1, :] if d == 1 else sd[c - 1:c, :]
    dt_under_cum = pltpu.roll(sd, 2 * nh, 1)
    e_in = jnp.where(sel, jnp.exp2(sd), 0.0)
    w_st = jnp.where(sel, dt_under_cum * jnp.exp2(tot - sd), 0.0)
    e_tot = jnp.broadcast_to(jnp.where(sel, jnp.exp2(tot), 0.0), (SUBLANES, LANES))
    expand = lambda f: _dot(_split_hi_lo(f), ex)
    return expand(e_in), expand(w_st), expand(e_tot)[0:1, :]


def _ssd_kernel(xsa_ref, bca_ref, bTa_ref, sda_ref, sdTa_ref, xsb_ref, bcb_ref, bTb_ref, sdb_ref,
                ex_ref, s0_ref, ya_ref, yb_ref, sfin_ref, *s_scr):
    c = SSD_CHUNK
    nh = SSD_HEADS
    gw = HEADS_PER_GROUP * SSD_HEADDIM
    n = pl.program_id(1)

    @pl.when(n == 0)
    def _():
        for k, s_ref in enumerate(s_scr):
            s_ref[...] = s0_ref[0, k]

    ii = lax.broadcasted_iota(jnp.int32, (c, c), 0)
    jj = lax.broadcasted_iota(jnp.int32, (c, c), 1)
    lower = jj <= ii
    upper = jj >= ii
    lo_half = lax.broadcasted_iota(jnp.int32, (1, LANES), 1) < SSD_HEADDIM

    cps = xsa_ref.shape[1] // c
    for sc in range(cps):
        ra = slice(sc * c, (sc + 1) * c)
        rb = slice((cps - 1 - sc) * c, (cps - sc) * c)
        sda = sda_ref[0, ra, :]
        sdTa = sdTa_ref[0, :, ra]
        ein_f, wst_f, dec_f = _ssd_factors(sda, ex_ref[0], 0)
        ein_b, wst_b, dec_b = _ssd_factors(sdb_ref[0, rb, :], ex_ref[1], 1)

        def weights(h, d, mask, sda=sda, sdTa=sdTa):
            col = (2 + d) * nh + h
            ci = sda[:, col:col + 1]
            cj = sdTa[col:col + 1, :]
            dtj = sdTa[d * nh + h:d * nh + h + 1, :]
            return jnp.where(mask, jnp.exp2(jnp.minimum(ci - cj, 0.0)) * dtj, 0.0)

        for gi in range(SSD_GROUPS):
            gs = slice(gi * gw, (gi + 1) * gw)
            ns = slice(gi * SSD_STATE, (gi + 1) * SSD_STATE)
            bm = bca_ref[0, ra, ns]
            cm = bca_ref[0, ra, SSD_BC + gi * SSD_STATE:SSD_BC + (gi + 1) * SSD_STATE]
            gmat = _dot_nt(cm, bm)
            s_f = s_scr[gi]
            s_b = s_scr[SSD_GROUPS + gi]
            y_inter = _dot(cm, s_f[...].astype(BF16)) * ein_f[:, gs]
            for hp in range(HEADS_PER_GROUP // 2):
                col0 = (gi * HEADS_PER_GROUP + 2 * hp) * SSD_HEADDIM
                xpair = xsa_ref[0, ra, col0:col0 + LANES]
                scores = []
                for t in range(2):
                    h = gi * HEADS_PER_GROUP + 2 * hp + t
                    scores.append((gmat * (weights(h, 0, lower) + weights(h, 1, upper))).astype(BF16))
                zero = jnp.zeros_like(xpair)
                rhs = jnp.concatenate([jnp.where(lo_half, xpair, zero), jnp.where(lo_half, zero, xpair)], axis=0)
                y_pair = _dot(jnp.concatenate(scores, axis=1), rhs) + y_inter[:, hp * LANES:(hp + 1) * LANES]
                ya_ref[0, ra, col0:col0 + LANES] = y_pair.astype(BF16)
            wx = (xsa_ref[0, ra, gs].astype(F32) * wst_f[:, gs]).astype(BF16)
            s_f[...] = s_f[...] * dec_f[:, gs] + _dot(bTa_ref[0, ns, ra], wx)

            cmb = bcb_ref[0, rb, SSD_BC + gi * SSD_STATE:SSD_BC + (gi + 1) * SSD_STATE]
            yb_ref[0, rb, gs] = (_dot(cmb, s_b[...].astype(BF16)) * ein_b[:, gs]).astype(BF16)
            wxb = (xsb_ref[0, rb, gs].astype(F32) * wst_b[:, gs]).astype(BF16)
            s_b[...] = s_b[...] * dec_b[:, gs] + _dot(bTb_ref[0, ns, rb], wxb)

    @pl.when(n == pl.num_programs(1) - 1)
    def _():
        for k, s_ref in enumerate(s_scr):
            sfin_ref[0, k] = s_ref[...]


def _ssd_expander():
    r = lax.broadcasted_iota(jnp.int32, (2, 2 * LANES, SSD_INNER), 1) % LANES
    d = lax.broadcasted_iota(jnp.int32, (2, 2 * LANES, SSD_INNER), 0)
    h = lax.broadcasted_iota(jnp.int32, (2, 2 * LANES, SSD_INNER), 2) // SSD_HEADDIM
    return (r == (2 + d) * SSD_HEADS + h).astype(BF16)


def _ssd_scan(p, s0):
    b, l, _ = p["xs"].shape
    c = SSD_CHUNK * min(SSD_CHUNKS_PER_STEP, l // SSD_CHUNK)
    nc = l // c
    specs = []
    for rev in (False, True):
        idx = (lambda n: nc - 1 - n) if rev else (lambda n: n)
        tok = lambda w, idx=idx: pl.BlockSpec((1, c, w), lambda bi, n: (bi, idx(n), 0))
        chan = lambda r, idx=idx: pl.BlockSpec((1, r, c), lambda bi, n: (bi, 0, idx(n)))
        specs.append([tok(SSD_INNER), tok(2 * SSD_BC), chan(SSD_BC), tok(LANES), chan(4 * SSD_HEADS)])
    st = pl.BlockSpec((1, 2 * SSD_GROUPS, SSD_STATE, HEADS_PER_GROUP * SSD_HEADDIM), lambda bi, n: (bi, 0, 0, 0))
    ex = _ssd_expander()
    ya, yb, sfin = pl.pallas_call(
        _ssd_kernel,
        grid=(b, nc),
        in_specs=specs[0] + specs[1][:4] + [_const_spec(ex.shape), st],
        out_specs=[specs[0][0], specs[1][0], st],
        out_shape=[jax.ShapeDtypeStruct((b, l, SSD_INNER), BF16), jax.ShapeDtypeStruct((b, l, SSD_INNER), BF16),
                   jax.ShapeDtypeStruct(s0.shape, F32)],
        scratch_shapes=[pltpu.VMEM((SSD_STATE, HEADS_PER_GROUP * SSD_HEADDIM), F32)] * (2 * SSD_GROUPS),
        compiler_params=_cparams(("parallel", "arbitrary")),
        name="ssd_scan",
    )(p["xs"], p["bc"], p["bT"], p["sd"], p["sdT"], p["xs"], p["bc"], p["bT"], p["sd"], ex, s0)
    return ya, yb, sfin


def _post0_kernel(of_ref, ob_ref, g_ref, yf_ref, yb_ref, xs_ref, z_ref, x_ref, mod_ref,
                  gn_ref, dsk_ref, sn_ref, wo_ref, out_ref):
    o = of_ref[0].astype(F32) + ob_ref[0].astype(F32)
    parts = []
    for hd in range(GLA_HEADS):
        oh = o[:, hd * GLA_DV:(hd + 1) * GLA_DV]
        parts.append(oh * lax.rsqrt(jnp.mean(oh * oh, axis=-1, keepdims=True) + EPS))
    gla = jnp.concatenate(parts, axis=1) * gn_ref[...] * _silu(g_ref[0].astype(F32))
    xs = xs_ref[0].astype(F32)
    y = (yf_ref[0].astype(F32) + yb_ref[0].astype(F32) + xs * dsk_ref[...]) * _silu(z_ref[0].astype(F32))
    gw = SSD_INNER // SSD_GROUPS
    parts = []
    for gi in range(SSD_GROUPS):
        yg = y[:, gi * gw:(gi + 1) * gw]
        parts.append(yg * lax.rsqrt(jnp.mean(yg * yg, axis=-1, keepdims=True) + EPS))
    ssd = jnp.concatenate(parts, axis=1) * sn_ref[...]
    res = _dot(gla.astype(BF16), wo_ref[:GLA_V, :]) + _dot(ssd.astype(BF16), wo_ref[GLA_V:, :])
    out_ref[0] = x_ref[0] + mod_ref[0, 2:3, :] * res


def _post0(p, of, ob, yf, yb, x, mod, wp, tm):
    b, l, _ = x.shape
    tm = min(tm, l)
    bm = mod.shape[0]
    mod_idx = (lambda bi, i: (bi, 0, 0)) if bm > 1 else (lambda bi, i: (0, 0, 0))
    tok = lambda w: pl.BlockSpec((1, tm, w), lambda bi, i: (bi, i, 0))
    return pl.pallas_call(
        _post0_kernel,
        grid=(b, l // tm),
        in_specs=[tok(GLA_V), tok(GLA_V), tok(GLA_V), tok(SSD_INNER), tok(SSD_INNER), tok(SSD_INNER),
                  tok(SSD_INNER), tok(D_MODEL), pl.BlockSpec((1, N_MOD, D_MODEL), mod_idx),
                  _const_spec((1, GLA_V)), _const_spec((1, SSD_INNER)), _const_spec((1, SSD_INNER)),
                  _const_spec(wp["w_out"].shape)],
        out_specs=tok(D_MODEL),
        out_shape=jax.ShapeDtypeStruct((b, l, D_MODEL), F32),
        compiler_params=_cparams(("parallel", "arbitrary")),
        name="post0",
    )(of, ob, p["g"], yf, yb, p["xs"], p["z"], x, mod, wp["gla_norm"], wp["d_skip"], wp["ssd_norm"], wp["w_out"])


def _ffn_kernel(x_ref, mod_ref, nw_ref, w1_ref, w3_ref, w2_ref, fn_ref, out_ref, *, final):
    x = x_ref[0]
    h = _norm_mod(x, nw_ref[...], mod_ref[0, 3:4, :], mod_ref[0, 4:5, :]).astype(BF16)
    half = D_FF // 2
    y = jnp.zeros_like(x)
    for j in range(2):
        cs = slice(j * half, (j + 1) * half)
        u = (_silu(_dot(h, w1_ref[0, :, cs])) * _dot(h, w3_ref[0, :, cs])).astype(BF16)
        y = y + _dot(u, w2_ref[0, cs, :])
    out = x + mod_ref[0, 5:6, :] * y
    if final:
        out = out * lax.rsqrt(jnp.mean(out * out, axis=-1, keepdims=True) + EPS) * fn_ref[...]
    out_ref[0] = out


def _ffn(x, mod, nw, w1, w3, w2, layer, final_norm, final, tm):
    b, l, _ = x.shape
    tm = min(tm, l)
    bm = mod.shape[0]
    mod_idx = (lambda bi, i: (bi, 0, 0)) if bm > 1 else (lambda bi, i: (0, 0, 0))
    tok = pl.BlockSpec((1, tm, D_MODEL), lambda bi, i: (bi, i, 0))
    wspec = lambda w: pl.BlockSpec((1,) + w.shape[1:], lambda bi, i: (layer, 0, 0), pipeline_mode=pl.Buffered(1))
    return pl.pallas_call(
        functools.partial(_ffn_kernel, final=final),
        grid=(b, l // tm),
        in_specs=[tok, pl.BlockSpec((1, N_MOD, D_MODEL), mod_idx), _const_spec((1, D_MODEL)),
                  wspec(w1), wspec(w3), wspec(w2), _const_spec((1, D_MODEL))],
        out_specs=tok,
        out_shape=jax.ShapeDtypeStruct((b, l, D_MODEL), F32),
        compiler_params=_cparams(("parallel", "arbitrary")),
        name="ffn_final" if final else "ffn",
    )(x, mod, nw, w1, w3, w2, final_norm)


def _inproj1_kernel(*refs, tm, ncol):
    if ncol:
        x_ref, xp_ref, xn_ref, perm_ref = refs[:4]
        refs = refs[4:]
    else:
        x_ref, xp_ref, xn_ref = refs[:3]
        refs = refs[3:]
    (mod_ref, nw_ref, wup_ref, cw_ref, cb_ref, bdq_ref, bdkT_ref, bdv_ref, wifq_ref, wifk_ref, wifv_ref,
     bif_ref, q_ref, kT_ref, v_ref, xc_ref, z_ref, gc_ref, gT_ref, hext, xm_scr) = refs
    i = pl.program_id(1)
    first = i == 0
    last = i == pl.num_programs(1) - 1
    shift = mod_ref[0, 0:1, :]
    scale = mod_ref[0, 1:2, :]
    nw = nw_ref[...]
    if ncol:
        hext[0:HALO, :] = _norm_mod(xp_ref[0, :, SUBLANES - 1, :], nw, shift, scale).astype(BF16)
        hn = _norm_mod(x_ref[0].reshape(tm, D_MODEL), nw, shift, scale).astype(BF16)
        hext[HALO:HALO + tm, :] = _dot(perm_ref[...], hn).astype(BF16)
        hext[HALO + tm:2 * HALO + tm, :] = _norm_mod(xn_ref[0, :, 0, :], nw, shift, scale).astype(BF16)
    else:
        hext[0:HALO, :] = _norm_mod(xp_ref[0], nw, shift, scale).astype(BF16)
        hext[HALO:HALO + tm, :] = _norm_mod(x_ref[0], nw, shift, scale).astype(BF16)
        hext[HALO + tm:2 * HALO + tm, :] = _norm_mod(xn_ref[0], nw, shift, scale).astype(BF16)
    h = hext[HALO:HALO + tm, :]

    xm_scr[...] = _dot(hext[...], wup_ref[:, :MLSTM_INNER])
    xc = _silu(_conv3(xm_scr, tm, cw_ref, cb_ref, first, last))
    xc_ref[0] = xc.astype(BF16)

    gates = jnp.zeros((tm, LANES), F32) + bif_ref[...]
    for t in range(N_BD_TILES):
        cs = slice(t * BD_TILE, (t + 1) * BD_TILE)
        xct = xc[:, cs].astype(BF16)
        xmt = xm_scr[HALO:HALO + tm, cs].astype(BF16)
        qt = _dot(xct, bdq_ref[t])
        vt = _dot(xmt, bdv_ref[t])
        q_ref[0, :, cs] = qt.astype(BF16)
        v_ref[0, :, cs] = vt.astype(BF16)
        ktt = (_dot_nt(bdkT_ref[t], xct) * (MLSTM_DH ** -0.5)).astype(BF16)
        for ck in range(tm // MLSTM_CHUNK):
            kT_ref[0, ck, cs, :] = ktt[:, ck * MLSTM_CHUNK:(ck + 1) * MLSTM_CHUNK]
        gates = gates + _dot(qt.astype(BF16), wifq_ref[cs, :]) + _dot(xct, wifk_ref[cs, :]) \
            + _dot(vt.astype(BF16), wifv_ref[cs, :])

    gt = jnp.transpose(gates)[:4 * MLSTM_HEADS]
    r = lax.broadcasted_iota(jnp.int32, gt.shape, 0)
    lf = _log_sigmoid(gt)
    cum = jnp.where(r < 2 * MLSTM_HEADS, _seg_cumsum(lf, MLSTM_CHUNK, 1, False),
                    _seg_cumsum(lf, MLSTM_CHUNK, 1, True))
    packed = jnp.where(r % (2 * MLSTM_HEADS) < MLSTM_HEADS, gt, cum)
    a = gt - pltpu.roll(cum, 4 * MLSTM_HEADS - MLSTM_HEADS, 0)
    amax = jnp.where(r < 2 * MLSTM_HEADS, _seg_scan(a, MLSTM_CHUNK, 1, False, use_max=True),
                     _seg_scan(a, MLSTM_CHUNK, 1, True, use_max=True))
    full = jnp.concatenate([packed, amax], axis=0)
    gT_ref[0] = full
    gc_ref[0] = jnp.transpose(jnp.concatenate([full, jnp.zeros((LANES - MLSTM_GATE_ROWS, tm), F32)], axis=0))

    z_ref[0] = _dot(h, wup_ref[:, MLSTM_INNER:]).astype(BF16)


def _inproj1(x, mod, nw, wp, tm, colmajor):
    b, l, _ = x.shape
    tm = min(tm, l)
    nt = l // tm
    bm = mod.shape[0]
    mod_idx = (lambda bi, i: (bi, 0, 0)) if bm > 1 else (lambda bi, i: (0, 0, 0))
    if colmajor:
        rows = l // GRID_W
        ncol = tm // rows
        assert ncol % SUBLANES == 0 and rows % HALO == 0
        xv = x.reshape(b, rows, GRID_W, D_MODEL)
        rb = rows // HALO
        cb = ncol // SUBLANES
        ncb = GRID_W // SUBLANES
        x_spec = pl.BlockSpec((1, rows, ncol, D_MODEL), lambda bi, i: (bi, 0, i, 0))
        xp_spec = pl.BlockSpec((1, HALO, SUBLANES, D_MODEL),
                               lambda bi, i: (bi, rb - 1, jnp.maximum(i * cb - 1, 0), 0))
        xn_spec = pl.BlockSpec((1, HALO, SUBLANES, D_MODEL),
                               lambda bi, i: (bi, 0, jnp.minimum((i + 1) * cb, ncb - 1), 0))
        dst = lax.broadcasted_iota(jnp.int32, (tm, tm), 0)
        src = lax.broadcasted_iota(jnp.int32, (tm, tm), 1)
        perm = (src == (dst % rows) * ncol + dst // rows).astype(BF16)
        lead_specs = [x_spec, xp_spec, xn_spec, _const_spec((tm, tm))]
        lead_args = [xv, xv, xv, perm]
    else:
        ncol = 0
        xv = x
        hb = tm // HALO
        nhb = l // HALO
        x_spec = pl.BlockSpec((1, tm, D_MODEL), lambda bi, i: (bi, i, 0))
        xp_spec = pl.BlockSpec((1, HALO, D_MODEL), lambda bi, i: (bi, jnp.maximum(i * hb - 1, 0), 0))
        xn_spec = pl.BlockSpec((1, HALO, D_MODEL), lambda bi, i: (bi, jnp.minimum((i + 1) * hb, nhb - 1), 0))
        lead_specs = [x_spec, xp_spec, xn_spec]
        lead_args = [xv, xv, xv]
    tok = lambda w: pl.BlockSpec((1, tm, w), lambda bi, i: (bi, i, 0))
    names = ["w_up", "conv_w", "conv_b", "bdq", "bdkT", "bdv", "wif_q", "wif_k", "wif_v", "b_if"]
    q, kT, v, xc, z, gc, gT = pl.pallas_call(
        functools.partial(_inproj1_kernel, tm=tm, ncol=ncol),
        grid=(b, nt),
        in_specs=lead_specs + [pl.BlockSpec((1, N_MOD, D_MODEL), mod_idx), _const_spec((1, D_MODEL))]
        + [_const_spec(wp[n].shape) for n in names],
        out_specs=[tok(MLSTM_INNER),
                   pl.BlockSpec((1, tm // MLSTM_CHUNK, MLSTM_INNER, MLSTM_CHUNK), lambda bi, i: (bi, i, 0, 0)),
                   tok(MLSTM_INNER), tok(MLSTM_INNER), tok(MLSTM_INNER), tok(LANES),
                   pl.BlockSpec((1, MLSTM_GATE_ROWS, tm), lambda bi, i: (bi, 0, i))],
        out_shape=[jax.ShapeDtypeStruct((b, l, MLSTM_INNER), BF16),
                   jax.ShapeDtypeStruct((b, l // MLSTM_CHUNK, MLSTM_INNER, MLSTM_CHUNK), BF16),
                   jax.ShapeDtypeStruct((b, l, MLSTM_INNER), BF16), jax.ShapeDtypeStruct((b, l, MLSTM_INNER), BF16),
                   jax.ShapeDtypeStruct((b, l, MLSTM_INNER), BF16), jax.ShapeDtypeStruct((b, l, LANES), F32),
                   jax.ShapeDtypeStruct((b, MLSTM_GATE_ROWS, l), F32)],
        scratch_shapes=[pltpu.VMEM((tm + 2 * HALO, D_MODEL), BF16),
                        pltpu.VMEM((tm + 2 * HALO, MLSTM_INNER), F32)],
        compiler_params=_cparams(("parallel", "arbitrary")),
        name="inproj1",
    )(*lead_args, mod, nw, *[wp[n] for n in names])
    return dict(q=q, kT=kT, v=v, xc=xc, z=z, gc=gc, gT=gT)


def _mlstm_head(q, kT, v, gc, gT, c_scr, cb_scr, n_scr, m_scr, d, hd, reverse):
    c = MLSTM_CHUNK
    dh = MLSTM_DH
    idx = d * MLSTM_HEADS + hd
    ri = d * 2 * MLSTM_HEADS + hd
    rb = ri + MLSTM_HEADS
    rm = 4 * MLSTM_HEADS + ri
    c_scr, cb_scr, n_scr, m_scr = c_scr[idx], cb_scr[idx], n_scr[idx], m_scr[idx]
    bi = gc[:, rb:rb + 1]
    am_i = gc[:, rm:rm + 1]
    bj = gT[rb:rb + 1, :]
    a_j = gT[ri:ri + 1, :] - bj
    am_j = gT[rm:rm + 1, :]
    m0 = m_scr[0:1, 0:1]
    big_m = jnp.maximum(m0, am_i)
    ii = lax.broadcasted_iota(jnp.int32, (c, c), 0)
    jj = lax.broadcasted_iota(jnp.int32, (c, c), 1)
    causal = (jj >= ii) if reverse else (jj <= ii)
    s = _dot(q, kT) * jnp.where(causal, jnp.exp(a_j - big_m), 0.0)
    w_inter = jnp.exp(m0 - big_m)
    qf = q.astype(F32)
    den = jnp.sum(s, axis=-1, keepdims=True) \
        + w_inter * jnp.sum(qf * n_scr[0:1, :], axis=-1, keepdims=True)
    num = _dot(s.astype(BF16), v) + _dot((qf * w_inter).astype(BF16), cb_scr[...])
    h = num * (1.0 / jnp.maximum(jnp.abs(den), jnp.exp(-(bi + big_m))))

    last = 0 if reverse else c - 1
    m_inner = jnp.maximum(m0, am_j[:, last:last + 1])
    kw = kT.astype(F32) * jnp.exp(a_j - m_inner)
    decay = jnp.exp(m0 - m_inner)
    kwb = kw.astype(BF16)
    c_new = decay * c_scr[...] + _dot(kwb, v)
    c_scr[...] = c_new
    cb_scr[...] = c_new.astype(BF16)
    n_scr[...] = decay * n_scr[...] + _dot_nt(jnp.ones((SUBLANES, c), BF16), kwb)
    m_scr[...] = jnp.broadcast_to(bj[:, last:last + 1] + m_inner, m_scr.shape)
    return h


def _mlstm_kernel(*refs, has_init, emit_final):
    ins = refs[:10]
    pos = 10
    if has_init:
        c0_ref, n0_ref, m0_ref = refs[pos:pos + 3]
        pos += 3
    hf_ref, hb_ref = refs[pos:pos + 2]
    pos += 2
    if emit_final:
        cfin_ref, nfin_ref, mfin_ref = refs[pos:pos + 3]
        pos += 3
    nst = 2 * MLSTM_HEADS
    c_scr, cb_scr, n_scr, m_scr = (refs[pos + k * nst:pos + (k + 1) * nst] for k in range(4))
    n = pl.program_id(1)

    @pl.when(n == 0)
    def _():
        for k in range(nst):
            if has_init:
                c_scr[k][...] = c0_ref[0, k]
                cb_scr[k][...] = c0_ref[0, k].astype(BF16)
                n_scr[k][...] = n0_ref[0, k]
                m_scr[k][...] = m0_ref[0, k]
            else:
                c_scr[k][...] = jnp.zeros_like(c_scr[k])
                cb_scr[k][...] = jnp.zeros_like(cb_scr[k])
                n_scr[k][...] = jnp.zeros_like(n_scr[k])
                m_scr[k][...] = jnp.full(m_scr[k].shape, M_INIT, F32)

    for d, h_ref in enumerate((hf_ref, hb_ref)):
        q_ref, kT_ref, v_ref, gc_ref, gT_ref = ins[5 * d:5 * d + 5]
        gc = gc_ref[0]
        gT = gT_ref[0]
        for hd in range(MLSTM_HEADS):
            cs = slice(hd * MLSTM_DH, (hd + 1) * MLSTM_DH)
            h = _mlstm_head(q_ref[0, :, cs], kT_ref[0, 0, cs, :], v_ref[0, :, cs], gc, gT,
                            c_scr, cb_scr, n_scr, m_scr, d, hd, d == 1)
            h_ref[0, :, cs] = h.astype(BF16)

    if emit_final:
        @pl.when(n == pl.num_programs(1) - 1)
        def _():
            for k in range(nst):
                cfin_ref[0, k] = c_scr[k][...]
                nfin_ref[0, k] = n_scr[k][...]
                mfin_ref[0, k] = m_scr[k][...]


def _mlstm_scan(p, init, emit_final):
    b, l, _ = p["q"].shape
    c = MLSTM_CHUNK
    nc = l // c
    dh = MLSTM_DH
    nst = 2 * MLSTM_HEADS
    specs = []
    for rev in (False, True):
        idx = (lambda n: nc - 1 - n) if rev else (lambda n: n)
        specs.append([
            pl.BlockSpec((1, c, MLSTM_INNER), lambda bi, n, idx=idx: (bi, idx(n), 0)),
            pl.BlockSpec((1, 1, MLSTM_INNER, c), lambda bi, n, idx=idx: (bi, idx(n), 0, 0)),
            pl.BlockSpec((1, c, MLSTM_INNER), lambda bi, n, idx=idx: (bi, idx(n), 0)),
            pl.BlockSpec((1, c, LANES), lambda bi, n, idx=idx: (bi, idx(n), 0)),
            pl.BlockSpec((1, MLSTM_GATE_ROWS, c), lambda bi, n, idx=idx: (bi, 0, idx(n))),
        ])
    st_shapes = [(b, nst, dh, dh), (b, nst, SUBLANES, dh), (b, nst, SUBLANES, LANES)]
    st_index = lambda bi, n: (bi, 0, 0, 0)
    args = (p["q"], p["kT"], p["v"], p["gc"], p["gT"])
    operands = [*args, *args]
    in_specs = specs[0] + specs[1]
    if init is not None:
        operands += list(init)
        in_specs += [pl.BlockSpec((1,) + s[1:], st_index, pipeline_mode=pl.Buffered(1)) for s in st_shapes]
    out_specs = [specs[0][0], specs[1][0]]
    out_shape = [jax.ShapeDtypeStruct((b, l, MLSTM_INNER), BF16), jax.ShapeDtypeStruct((b, l, MLSTM_INNER), BF16)]
    if emit_final:
        out_specs += [pl.BlockSpec((1,) + s[1:], st_index) for s in st_shapes]
        out_shape += [jax.ShapeDtypeStruct(s, F32) for s in st_shapes]
    res = pl.pallas_call(
        functools.partial(_mlstm_kernel, has_init=init is not None, emit_final=emit_final),
        grid=(b, nc),
        in_specs=in_specs,
        out_specs=out_specs,
        out_shape=out_shape,
        scratch_shapes=[pltpu.VMEM((dh, dh), F32)] * nst + [pltpu.VMEM((dh, dh), BF16)] * nst
        + [pltpu.VMEM((SUBLANES, dh), F32)] * nst + [pltpu.VMEM((SUBLANES, LANES), F32)] * nst,
        compiler_params=_cparams(("parallel", "arbitrary")),
        name="mlstm_scan",
    )(*operands)
    return res[0], res[1], tuple(res[2:])


def _post1_kernel(hf_ref, hb_ref, xc_ref, z_ref, x_ref, mod_ref, mh_ref, sk_ref, wd_ref, out_ref, *, tm, ncol):
    hs = hf_ref[0].astype(F32) + hb_ref[0].astype(F32)
    parts = []
    for hd in range(MLSTM_HEADS):
        hh = hs[:, hd * MLSTM_DH:(hd + 1) * MLSTM_DH]
        mu = jnp.mean(hh, axis=-1, keepdims=True)
        cen = hh - mu
        parts.append(cen * lax.rsqrt(jnp.mean(cen * cen, axis=-1, keepdims=True) + EPS))
    feat = (jnp.concatenate(parts, axis=1) * mh_ref[...] + sk_ref[...] * xc_ref[0].astype(F32)) \
        * _silu(z_ref[0].astype(F32))
    res = mod_ref[0, 2:3, :] * _dot(feat.astype(BF16), wd_ref[...])
    if ncol:
        rows = tm // ncol
        for j in range(ncol):
            out_ref[0, :, j, :] = x_ref[0, :, j, :] + res[j * rows:(j + 1) * rows, :]
    else:
        out_ref[0] = x_ref[0] + res


def _post1(p, hf, hb, x, mod, wp, tm, colmajor):
    b, l, _ = x.shape
    tm = min(tm, l)
    bm = mod.shape[0]
    mod_idx = (lambda bi, i: (bi, 0, 0)) if bm > 1 else (lambda bi, i: (0, 0, 0))
    tok = lambda w: pl.BlockSpec((1, tm, w), lambda bi, i: (bi, i, 0))
    if colmajor:
        rows = l // GRID_W
        ncol = tm // rows
        xv = x.reshape(b, rows, GRID_W, D_MODEL)
        x_spec = pl.BlockSpec((1, rows, ncol, D_MODEL), lambda bi, i: (bi, 0, i, 0))
    else:
        ncol = 0
        xv = x
        x_spec = tok(D_MODEL)
    out = pl.pallas_call(
        functools.partial(_post1_kernel, tm=tm, ncol=ncol),
        grid=(b, l // tm),
        in_specs=[tok(MLSTM_INNER), tok(MLSTM_INNER), tok(MLSTM_INNER), tok(MLSTM_INNER), x_spec,
                  pl.BlockSpec((1, N_MOD, D_MODEL), mod_idx),
                  _const_spec((1, MLSTM_INNER)), _const_spec((1, MLSTM_INNER)), _const_spec(wp["w_down"].shape)],
        out_specs=x_spec,
        out_shape=jax.ShapeDtypeStruct(xv.shape, F32),
        compiler_params=_cparams(("parallel", "arbitrary")),
        name="post1",
    )(hf, hb, p["xc"], p["z"], xv, mod, wp["mh_norm"], wp["skip"], wp["w_down"])
    return out.reshape(b, l, D_MODEL)


def _prep_even(w_in, w_gate, b_gate, gla_norm, conv_w, conv_b, a_log, dt_bias, d_skip, ssd_norm, w_out):
    sizes = (GLA_QK, GLA_QK, GLA_V, GLA_V, 2 * GLA_GATE_RANK, SSD_INNER, SSD_XBC, 2 * SSD_HEADS)
    offs = [0]
    for s in sizes:
        offs.append(offs[-1] + s)
    w16 = w_in.astype(BF16)
    w_main = jnp.concatenate([w16[:, offs[0]:offs[4]], w16[:, offs[5]:offs[7]]], axis=1)
    pad = jnp.zeros((D_MODEL, LANES - 2 * GLA_GATE_RANK - 2 * SSD_HEADS), BF16)
    w_small = jnp.concatenate([w16[:, offs[4]:offs[5]], w16[:, offs[7]:offs[8]], pad], axis=1)
    zero = jnp.zeros((GLA_GATE_RANK, GLA_QK), w_gate.dtype)
    wg = jnp.concatenate([jnp.concatenate([w_gate[0], zero], axis=1),
                          jnp.concatenate([zero, w_gate[1]], axis=1)], axis=0).astype(BF16)
    return dict(
        w_main=w_main, w_small=w_small, w_dtT=jnp.transpose(w_in[:, offs[7]:offs[8]]).astype(BF16),
        w_gate=wg, b_gate=b_gate.reshape(1, 2 * GLA_QK),
        conv_w=conv_w, conv_b=conv_b.reshape(1, SSD_XBC),
        dt_bias=dt_bias.reshape(1, 2 * SSD_HEADS), a_log=a_log.reshape(1, 2 * SSD_HEADS),
        dt_biasT=dt_bias.reshape(2 * SSD_HEADS, 1), a_logT=a_log.reshape(2 * SSD_HEADS, 1),
        gla_norm=jnp.tile(gla_norm, GLA_HEADS).reshape(1, GLA_V),
        d_skip=jnp.repeat(d_skip, SSD_HEADDIM).reshape(1, SSD_INNER),
        ssd_norm=ssd_norm.reshape(1, SSD_INNER),
        w_out=w_out.astype(BF16),
    )


def _block_diag_tiles(w, transposed=False):
    per = BD_TILE // QKV_BLOCK
    w4 = w.reshape(N_BD_TILES, per, QKV_BLOCK, QKV_BLOCK)
    blk = lax.broadcasted_iota(jnp.int32, (BD_TILE, BD_TILE), 0) // QKV_BLOCK
    same_block = blk == jnp.transpose(blk)
    if transposed:
        cols = jnp.transpose(w4, (0, 3, 1, 2)).reshape(N_BD_TILES, QKV_BLOCK, BD_TILE)
        full = jnp.tile(cols, (1, per, 1))
    else:
        rows = w4.reshape(N_BD_TILES, BD_TILE, QKV_BLOCK)
        full = jnp.tile(rows, (1, 1, per))
    return jnp.where(same_block, full, 0.0).astype(BF16)


def _prep_odd(w_up, conv_w, conv_b, wq, wk, wv, w_if, b_if, mh_norm, skip, w_down):
    ng = 4 * MLSTM_HEADS
    wif = jnp.transpose(w_if, (1, 0, 2)).reshape(3 * MLSTM_INNER, ng)
    wif = jnp.concatenate([wif, jnp.zeros((3 * MLSTM_INNER, LANES - ng), w_if.dtype)], axis=1)
    wif_k = jnp.einsum("nde,neg->ndg", wk, wif[MLSTM_INNER:2 * MLSTM_INNER].reshape(-1, QKV_BLOCK, LANES),
                       precision=lax.Precision.HIGHEST)
    wif_k = wif_k.reshape(MLSTM_INNER, LANES)
    bif = jnp.concatenate([b_if.reshape(1, ng), jnp.zeros((1, LANES - ng), b_if.dtype)], axis=1)
    return dict(
        w_up=w_up.astype(BF16), conv_w=conv_w, conv_b=conv_b.reshape(1, MLSTM_INNER),
        bdq=_block_diag_tiles(wq), bdkT=_block_diag_tiles(wk, transposed=True), bdv=_block_diag_tiles(wv),
        wif_q=wif[:MLSTM_INNER].astype(BF16), wif_k=wif_k.astype(BF16), wif_v=wif[2 * MLSTM_INNER:].astype(BF16),
        b_if=bif,
        mh_norm=mh_norm.reshape(1, MLSTM_INNER), skip=skip.reshape(1, MLSTM_INNER),
        w_down=w_down.astype(BF16),
    )


TOKEN_TILE = 512


def kernel(x, c, ctx, c_ctx, w_mod, b_mod, norm_mix, norm_ffn, ffn_w1, ffn_w3, ffn_w2, a_w_in, a_gla_w_gate,
           a_gla_b_gate, a_gla_norm, a_ssd_conv_w, a_ssd_conv_b, a_ssd_A_log, a_ssd_dt_bias, a_ssd_D, a_ssd_norm,
           a_w_out, b_w_up, b_conv_w, b_conv_b, b_wq, b_wk, b_wv, b_w_if, b_b_if, b_mh_norm, b_skip, b_w_down,
           final_norm):
    tile = TOKEN_TILE
    bsz = x.shape[0]
    depth = w_mod.shape[0]
    pad_rows = (-(bsz + 1)) % 8
    cc = jnp.concatenate([c, c_ctx[None, :], jnp.zeros((pad_rows, D_MODEL), c.dtype)], axis=0)
    mods = _modulation(cc, w_mod, b_mod).reshape(depth, bsz + 1 + pad_rows, N_MOD, D_MODEL)
    fnorm = final_norm.reshape(1, D_MODEL)
    w1, w3, w2 = ffn_w1.astype(BF16), ffn_w3.astype(BF16), ffn_w2.astype(BF16)
    for layer in range(depth):
        last = layer == depth - 1
        j = layer // 2
        mx = mods[layer, :bsz]
        mc = mods[layer, bsz:bsz + 1]
        nmix = norm_mix[layer].reshape(1, D_MODEL)
        nffn = norm_ffn[layer].reshape(1, D_MODEL)
        if layer % 2 == 0:
            wp = _prep_even(a_w_in[j], a_gla_w_gate[j], a_gla_b_gate[j], a_gla_norm[j], a_ssd_conv_w[j],
                            a_ssd_conv_b[j], a_ssd_A_log[j], a_ssd_dt_bias[j], a_ssd_D[j], a_ssd_norm[j],
                            a_w_out[j])
            pc = _inproj0(ctx, mc, nmix, wp, tile)
            gla0 = jnp.zeros((bsz, 2 * GLA_HEADS, GLA_DK, GLA_DV), F32)
            ssd0 = jnp.zeros((bsz, 2 * SSD_GROUPS, SSD_STATE, HEADS_PER_GROUP * SSD_HEADDIM), F32)
            ofc, obc, gla_st = _gla_scan(pc, gla0, GLA_INTRA_TILE)
            yfc, ybc, ssd_st = _ssd_scan(pc, ssd0)
            px = _inproj0(x, mx, nmix, wp, tile)
            ofx, obx, _ = _gla_scan(px, gla_st, GLA_INTRA_TILE)
            yfx, ybx, _ = _ssd_scan(px, ssd_st)
            x = _post0(px, ofx, obx, yfx, ybx, x, mx, wp, tile)
            if not last:
                ctx = _post0(pc, ofc, obc, yfc, ybc, ctx, mc, wp, tile)
        else:
            wp = _prep_odd(b_w_up[j], b_conv_w[j], b_conv_b[j], b_wq[j], b_wk[j], b_wv[j], b_w_if[j], b_b_if[j],
                           b_mh_norm[j], b_skip[j], b_w_down[j])
            pc = _inproj1(ctx, mc, nmix, wp, tile, colmajor=False)
            hfc, hbc, state = _mlstm_scan(pc, None, True)
            px = _inproj1(x, mx, nmix, wp, tile, colmajor=True)
            hfx, hbx, _ = _mlstm_scan(px, state, False)
            x = _post1(px, hfx, hbx, x, mx, wp, tile, colmajor=True)
            if not last:
                ctx = _post1(pc, hfc, hbc, ctx, mc, wp, tile, colmajor=False)
        x = _ffn(x, mx, nffn, w1, w3, w2, layer, fnorm, last, tile)
        if not last:
            ctx = _ffn(ctx, mc, nffn, w1, w3, w2, layer, fnorm, False, tile)
    return x
```

```python
import functools

import jax
import jax.numpy as jnp
from jax import lax
from jax.experimental import pallas as pl
from jax.experimental.pallas import tpu as pltpu

F32 = jnp.float32
BF16 = jnp.bfloat16

D_MODEL = 1024
GRID_W = 64
EPS = 1e-6
LOG2_E = 1.4426950408889634
M_INIT = -1e30
N_MOD = 6

GLA_HEADS = 4
GLA_DK = 128
GLA_DV = 256
GLA_QK = GLA_HEADS * GLA_DK
GLA_V = GLA_HEADS * GLA_DV
GLA_GATE_RANK = 16
GLA_GATE_TAU = 16.0

SSD_HEADDIM = 64
SSD_HEADS = 16
SSD_STATE = 128
SSD_GROUPS = 2
SSD_INNER = SSD_HEADS * SSD_HEADDIM
SSD_BC = SSD_GROUPS * SSD_STATE
SSD_XBC = SSD_INNER + 2 * SSD_BC
HEADS_PER_GROUP = SSD_HEADS // SSD_GROUPS

MLSTM_INNER = 2 * D_MODEL
MLSTM_HEADS = 4
MLSTM_DH = MLSTM_INNER // MLSTM_HEADS
QKV_BLOCK = 4
BD_TILE = 256
N_BD_TILES = MLSTM_INNER // BD_TILE
MLSTM_GATE_ROWS = 32

D_FF = 2816

GLA_CHUNK = 64
GLA_CHUNKS_PER_STEP = 8
GLA_INTRA_TILE = 512
SSD_CHUNK = 128
SSD_CHUNKS_PER_STEP = 4
MLSTM_CHUNK = 256
HALO = 16
LANES = 128
SUBLANES = 8
TAIL_ROWS = 128

VMEM_LIMIT = 56 * 1024 * 1024


def _cparams(sem):
    return pltpu.CompilerParams(dimension_semantics=sem, vmem_limit_bytes=VMEM_LIMIT)


def _const_spec(shape):
    nd = len(shape)
    return pl.BlockSpec(shape, lambda *_: (0,) * nd, pipeline_mode=pl.Buffered(1))


def _softplus(x):
    return jnp.maximum(x, 0.0) + jnp.log(1.0 + jnp.exp(-jnp.abs(x)))


def _log_sigmoid(x):
    return -_softplus(-x)


def _silu(x):
    return x / (1.0 + jnp.exp(-x))


def _seg_scan(x, seg, axis, reverse, use_max=False):
    n = x.shape[axis]
    idx = lax.broadcasted_iota(jnp.int32, x.shape, axis) % seg
    s = 1
    while s < seg:
        if reverse:
            shifted = pltpu.roll(x, n - s, axis)
            keep = idx < seg - s
        else:
            shifted = pltpu.roll(x, s, axis)
            keep = idx >= s
        if use_max:
            x = jnp.where(keep, jnp.maximum(x, shifted), x)
        else:
            x = x + jnp.where(keep, shifted, 0.0)
        s *= 2
    return x


def _seg_cumsum(x, seg, axis, reverse):
    return _seg_scan(x, seg, axis, reverse)


def _chunk_cumsum_mxu(x, chunk, reverse):
    rows = x.shape[0]
    hi = x.astype(BF16)
    lo = (x - hi.astype(F32)).astype(BF16)
    i = lax.broadcasted_iota(jnp.int32, (chunk, 2 * chunk), 0)
    j = lax.broadcasted_iota(jnp.int32, (chunk, 2 * chunk), 1) % chunk
    tri = ((j >= i) if reverse else (j <= i)).astype(BF16)
    parts = []
    for c0 in range(0, rows, chunk):
        parts.append(_dot(tri, jnp.concatenate([hi[c0:c0 + chunk], lo[c0:c0 + chunk]], axis=0)))
    return jnp.concatenate(parts, axis=0)


def _norm_mod(xv, nw, shift, scale):
    ms = jnp.mean(xv * xv, axis=-1, keepdims=True)
    return (xv * lax.rsqrt(ms + EPS) * nw) * (1.0 + scale) + shift


def _dot(a, b):
    return jnp.dot(a, b, preferred_element_type=F32)


def _dot_nt(a, b):
    return lax.dot_general(a, b, (((1,), (1,)), ((), ())), preferred_element_type=F32)


def _dot_tn(a, b):
    return lax.dot_general(a, b, (((0,), (0,)), ((), ())), preferred_element_type=F32)


def _mod_kernel(c_ref, w_ref, b_ref, o_ref):
    a = _silu(c_ref[...]).astype(BF16)
    o_ref[0] = _dot(a, w_ref[0].astype(BF16)) + b_ref[0]


def _modulation(cc, w_mod, b_mod):
    depth = w_mod.shape[0]
    n = w_mod.shape[2]
    tn = n // 4
    rows = cc.shape[0]
    return pl.pallas_call(
        _mod_kernel,
        grid=(depth, n // tn),
        in_specs=[
            pl.BlockSpec((rows, D_MODEL), lambda l, j: (0, 0)),
            pl.BlockSpec((1, D_MODEL, tn), lambda l, j: (l, 0, j)),
            pl.BlockSpec((1, 1, tn), lambda l, j: (l, 0, j)),
        ],
        out_specs=pl.BlockSpec((1, rows, tn), lambda l, j: (l, 0, j)),
        out_shape=jax.ShapeDtypeStruct((depth, rows, n), F32),
        compiler_params=_cparams(("arbitrary", "arbitrary")),
        name="modulation",
    )(cc, w_mod, b_mod.reshape(depth, 1, n))


def _conv_pad(scr, tm, first, last):
    scr[HALO - 1:HALO, :] = scr[HALO - 1:HALO, :] * jnp.where(first, 0.0, 1.0)
    scr[HALO + tm:HALO + tm + 1, :] = scr[HALO + tm:HALO + tm + 1, :] * jnp.where(last, 0.0, 1.0)


def _conv_rows(scr, r0, n, cw_ref, cb_ref):
    prev = scr[HALO - 1 + r0:HALO - 1 + r0 + n, :]
    cur = scr[HALO + r0:HALO + r0 + n, :]
    nxt = scr[HALO + 1 + r0:HALO + 1 + r0 + n, :]
    return cw_ref[0:1, :] * prev + cw_ref[1:2, :] * cur + cw_ref[2:3, :] * nxt + cb_ref[...]


def _conv3(scr, tm, cw_ref, cb_ref, first, last):
    _conv_pad(scr, tm, first, last)
    return _conv_rows(scr, 0, tm, cw_ref, cb_ref)


def _inproj0_kernel(x_ref, xp_ref, xn_ref, mod_ref, nw_ref, wmain_ref, wsmall_ref, wdtT_ref, wg_ref, bg_ref,
                    cw_ref, cb_ref, dtb_ref, alog_ref, dtbT_ref, alogT_ref,
                    q_ref, k_ref, v_ref, g_ref, z_ref, gcum_ref, xs_ref, bc_ref, bT_ref, sd_ref, sdT_ref,
                    hext, xbc_scr, *, tm):
    i = pl.program_id(1)
    first = i == 0
    last = i == pl.num_programs(1) - 1
    shift = mod_ref[0, 0:1, :]
    scale = mod_ref[0, 1:2, :]
    nw = nw_ref[...]
    hext[0:HALO, :] = _norm_mod(xp_ref[0], nw, shift, scale).astype(BF16)
    hext[HALO:HALO + tm, :] = _norm_mod(x_ref[0], nw, shift, scale).astype(BF16)
    hext[HALO + tm:2 * HALO + tm, :] = _norm_mod(xn_ref[0], nw, shift, scale).astype(BF16)

    o_k = GLA_QK
    o_v = o_k + GLA_QK
    o_g = o_v + GLA_V
    o_z = o_g + GLA_V
    o_x = o_z + SSD_INNER
    nd = 2 * SSD_HEADS
    h = hext[HALO:HALO + tm, :]

    xbc_scr[...] = _dot(hext[...], wmain_ref[:, o_x:o_x + SSD_XBC])
    _conv_pad(xbc_scr, tm, first, last)
    small = _dot(h, wsmall_ref[...])
    dt_raw = _dot_nt(wdtT_ref[...], h)
    logits = _dot(small[:, :2 * GLA_GATE_RANK].astype(BF16), wg_ref[...]) + bg_ref[...]

    def tail(j):
        rows = slice(j * TAIL_ROWS, (j + 1) * TAIL_ROWS)
        lg = logits[rows]
        la = (jnp.minimum(lg, 0.0) * LOG2_E - jnp.log2(1.0 + jnp.exp2(jnp.abs(lg) * (-LOG2_E)))) \
            * (1.0 / GLA_GATE_TAU)
        gcum_ref[0, rows, :GLA_QK] = _chunk_cumsum_mxu(la[:, :GLA_QK], GLA_CHUNK, False)
        gcum_ref[0, rows, GLA_QK:] = _chunk_cumsum_mxu(la[:, GLA_QK:], GLA_CHUNK, True)
        dtT = _softplus(dt_raw[:, rows] + dtbT_ref[...])
        laT = dtT * (-LOG2_E * jnp.exp(alogT_ref[...]))
        cumT = jnp.concatenate([_seg_cumsum(laT[:SSD_HEADS], SSD_CHUNK, 1, False),
                                _seg_cumsum(laT[SSD_HEADS:], SSD_CHUNK, 1, True)], axis=0)
        packed = jnp.concatenate([dtT, cumT, jnp.zeros((LANES - 2 * nd, TAIL_ROWS), F32)], axis=0)
        sdT_ref[0, :, rows] = packed[:2 * nd]
        sd_ref[0, rows, :] = jnp.transpose(packed)
        y = _silu(_conv_rows(xbc_scr, j * TAIL_ROWS, TAIL_ROWS, cw_ref, cb_ref))
        xs_ref[0, rows, :] = y[:, :SSD_INNER].astype(BF16)
        bc_ref[0, rows, :] = y[:, SSD_INNER:].astype(BF16)
        bT_ref[0, :, rows] = jnp.transpose(y[:, SSD_INNER:SSD_INNER + SSD_BC]).astype(BF16)

    wide = [(q_ref, 0, o_k, GLA_DK ** -0.5), (k_ref, o_k, o_v, None), (v_ref, o_v, o_g, None),
            (g_ref, o_g, o_z, None), (z_ref, o_z, o_x, None)]
    n_tail = tm // TAIL_ROWS
    done = 0
    for idx, (ref, c0, c1, scl) in enumerate(wide):
        res = _dot(h, wmain_ref[:, c0:c1])
        ref[0] = (res if scl is None else res * scl).astype(BF16)
        upto = (n_tail * (idx + 1)) // (len(wide) - 1) if idx < len(wide) - 1 else n_tail
        for j in range(done, min(upto, n_tail)):
            tail(j)
        done = max(done, min(upto, n_tail))


def _inproj0(x, mod, nw, wp, tm):
    b, l, _ = x.shape
    tm = min(tm, l)
    nt = l // tm
    hb = tm // HALO
    nhb = l // HALO
    bm = mod.shape[0]
    mod_idx = (lambda bi, i: (bi, 0, 0)) if bm > 1 else (lambda bi, i: (0, 0, 0))
    tok = lambda w: pl.BlockSpec((1, tm, w), lambda bi, i: (bi, i, 0))
    in_specs = [
        tok(D_MODEL),
        pl.BlockSpec((1, HALO, D_MODEL), lambda bi, i: (bi, jnp.maximum(i * hb - 1, 0), 0)),
        pl.BlockSpec((1, HALO, D_MODEL), lambda bi, i: (bi, jnp.minimum((i + 1) * hb, nhb - 1), 0)),
        pl.BlockSpec((1, N_MOD, D_MODEL), mod_idx),
        _const_spec((1, D_MODEL)),
        _const_spec(wp["w_main"].shape), _const_spec(wp["w_small"].shape), _const_spec(wp["w_dtT"].shape),
        _const_spec(wp["w_gate"].shape), _const_spec(wp["b_gate"].shape),
        _const_spec(wp["conv_w"].shape), _const_spec(wp["conv_b"].shape),
        _const_spec(wp["dt_bias"].shape), _const_spec(wp["a_log"].shape),
        _const_spec(wp["dt_biasT"].shape), _const_spec(wp["a_logT"].shape),
    ]
    outs = [
        ("q", GLA_QK, BF16), ("k", GLA_QK, BF16), ("v", GLA_V, BF16), ("g", GLA_V, BF16),
        ("z", SSD_INNER, BF16), ("gcum", 2 * GLA_QK, F32), ("xs", SSD_INNER, BF16),
        ("bc", 2 * SSD_BC, BF16),
    ]
    out_specs = [tok(w) for _, w, _ in outs]
    out_shape = [jax.ShapeDtypeStruct((b, l, w), dt) for _, w, dt in outs]
    out_specs += [pl.BlockSpec((1, SSD_BC, tm), lambda bi, i: (bi, 0, i)),
                  tok(LANES),
                  pl.BlockSpec((1, 4 * SSD_HEADS, tm), lambda bi, i: (bi, 0, i))]
    out_shape += [jax.ShapeDtypeStruct((b, SSD_BC, l), BF16),
                  jax.ShapeDtypeStruct((b, l, LANES), F32),
                  jax.ShapeDtypeStruct((b, 4 * SSD_HEADS, l), F32)]
    res = pl.pallas_call(
        functools.partial(_inproj0_kernel, tm=tm),
        grid=(b, nt),
        in_specs=in_specs,
        out_specs=out_specs,
        out_shape=out_shape,
        scratch_shapes=[pltpu.VMEM((tm + 2 * HALO, D_MODEL), BF16),
                        pltpu.VMEM((tm + 2 * HALO, SSD_XBC), F32)],
        compiler_params=_cparams(("parallel", "arbitrary")),
        name="inproj0",
    )(x, x, x, mod, nw, wp["w_main"], wp["w_small"], wp["w_dtT"], wp["w_gate"], wp["b_gate"],
      wp["conv_w"], wp["conv_b"], wp["dt_bias"], wp["a_log"], wp["dt_biasT"], wp["a_logT"])
    names = [n for n, _, _ in outs] + ["bT", "sd", "sdT"]
    return dict(zip(names, res))


def _level_ref(g, hsz, reverse):
    c, w = g.shape
    grp = 2 * hsz
    r = hsz if reverse else hsz - 1
    if grp >= 8:
        g3 = g.reshape(c // grp, grp, w)
        return jnp.broadcast_to(g3[:, r:r + 1, :], g3.shape).reshape(c, w)
    g3 = g.reshape(c // 8, 8, w)
    sub = lax.broadcasted_iota(jnp.int32, g3.shape, 1)
    out = None
    for t in range(8 // grp):
        cand = jnp.broadcast_to(g3[:, t * grp + r:t * grp + r + 1, :], g3.shape)
        out = cand if out is None else jnp.where(sub >= t * grp, cand, out)
    return out.reshape(c, w)


def _gla_masks(c, w, reverse):
    row = lax.broadcasted_iota(jnp.int32, (c, w), 0)
    ii = lax.broadcasted_iota(jnp.int32, (c, c), 0)
    jj = lax.broadcasted_iota(jnp.int32, (c, c), 1)
    par = 0 if reverse else 1
    levels = []
    hsz = 1
    while hsz < c:
        is_q = (row // hsz) % 2 == par
        keep = ((ii // (2 * hsz)) == (jj // (2 * hsz))) & ((ii // hsz) % 2 == par) & ((jj // hsz) % 2 != par)
        levels.append((hsz, is_q, keep.astype(F32)))
        hsz *= 2
    return (ii == jj).astype(F32), levels


def _gla_level_operand(q, k, g, hsz, reverse, is_q):
    c = g.shape[0]
    par = 0 if reverse else 1
    if hsz >= SUBLANES:
        parts = []
        for grp in range(c // (2 * hsz)):
            base = grp * 2 * hsz
            ref_row = base + (hsz if reverse else hsz - 1)
            ref = g[ref_row:ref_row + 1, :]
            for half in range(2):
                rows = slice(base + half * hsz, base + (half + 1) * hsz)
                if half == par:
                    parts.append(q[rows] * jnp.exp2(g[rows] - ref))
                else:
                    parts.append(k[rows] * jnp.exp2(ref - g[rows]))
        return jnp.concatenate(parts, axis=0).astype(BF16)
    if hsz == 1:
        step = g - pltpu.roll(g, (c - 1) if reverse else 1, 0)
        return jnp.where(is_q, q * jnp.exp2(step), k).astype(BF16)
    e = jnp.exp2(-jnp.abs(g - _level_ref(g, hsz, reverse)))
    return (jnp.where(is_q, q, k) * e).astype(BF16)


def _gla_intra_kernel(q_ref, k_ref, g_ref, a_ref):
    c = GLA_CHUNK
    masks = [_gla_masks(c, GLA_DK, d == 1) for d in range(2)]
    for sc in range(q_ref.shape[1] // c):
        rs = slice(sc * c, (sc + 1) * c)
        for hd in range(GLA_HEADS):
            ks = slice(hd * GLA_DK, (hd + 1) * GLA_DK)
            qb = q_ref[0, rs, ks]
            kb = k_ref[0, rs, ks]
            q = qb.astype(F32)
            k = kb.astype(F32)
            a = (2.0 * masks[0][0]) * _dot_nt(qb, kb)
            for d in range(2):
                g = g_ref[0, rs, d * GLA_QK + hd * GLA_DK:d * GLA_QK + (hd + 1) * GLA_DK]
                for hsz, is_q, keep in masks[d][1]:
                    mixed = _gla_level_operand(q, k, g, hsz, d == 1, is_q)
                    a = a + keep * _dot_nt(mixed, mixed)
            a_ref[0, hd, rs, :] = a.astype(BF16)


def _gla_intra(p, tm):
    b, l, _ = p["q"].shape
    tm = min(tm, l)
    tok = lambda w: pl.BlockSpec((1, tm, w), lambda bi, i: (bi, i, 0))
    return pl.pallas_call(
        _gla_intra_kernel,
        grid=(b, l // tm),
        in_specs=[tok(GLA_QK), tok(GLA_QK), tok(2 * GLA_QK)],
        out_specs=pl.BlockSpec((1, GLA_HEADS, tm, GLA_CHUNK), lambda bi, i: (bi, 0, i, 0)),
        out_shape=jax.ShapeDtypeStruct((b, GLA_HEADS, l, GLA_CHUNK), BF16),
        compiler_params=_cparams(("parallel", "arbitrary")),
        name="gla_intra",
    )(p["q"], p["k"], p["gcum"])


def _gla_inter(q, k, v, g, s, reverse):
    c = q.shape[0]
    dk = g.shape[1]
    g_tot = g[0:1, :] if reverse else g[c - 1:c, :]
    qi = (q.astype(F32) * jnp.exp2(g)).astype(BF16)
    ki = (k.astype(F32) * jnp.exp2(g_tot - g)).astype(BF16)
    decay = jnp.transpose(jnp.broadcast_to(jnp.exp2(g_tot), (dk, dk)))
    decay = jnp.concatenate([decay] * (v.shape[1] // dk), axis=1)
    return _dot(qi, s.astype(BF16)), s * decay + _dot_tn(ki, v)


def _gla_kernel(qf_ref, kf_ref, vf_ref, gf_ref, af_ref, qb_ref, kb_ref, vb_ref, gb_ref, s0_ref,
                oa_ref, ob_ref, sfin_ref, *s_scr):
    n = pl.program_id(1)

    @pl.when(n == 0)
    def _():
        for k, s_ref in enumerate(s_scr):
            s_ref[...] = s0_ref[0, k]

    c = GLA_CHUNK
    cps = qf_ref.shape[1] // c
    for sc in range(cps):
        for hd in range(GLA_HEADS):
            ks = slice(hd * GLA_DK, (hd + 1) * GLA_DK)
            vs = slice(hd * GLA_DV, (hd + 1) * GLA_DV)
            rs = slice(sc * c, (sc + 1) * c)
            s_ref = s_scr[hd]
            v = vf_ref[0, rs, vs]
            inter, s_new = _gla_inter(qf_ref[0, rs, ks], kf_ref[0, rs, ks], v, gf_ref[0, rs, ks], s_ref[...], False)
            oa_ref[0, rs, vs] = (_dot(af_ref[0, hd, rs, :], v) + inter).astype(BF16)
            s_ref[...] = s_new

            rs = slice((cps - 1 - sc) * c, (cps - sc) * c)
            s_ref = s_scr[GLA_HEADS + hd]
            inter, s_new = _gla_inter(qb_ref[0, rs, ks], kb_ref[0, rs, ks], vb_ref[0, rs, vs], gb_ref[0, rs, ks],
                                      s_ref[...], True)
            ob_ref[0, rs, vs] = inter.astype(BF16)
            s_ref[...] = s_new

    @pl.when(n == pl.num_programs(1) - 1)
    def _():
        for k, s_ref in enumerate(s_scr):
            sfin_ref[0, k] = s_ref[...]


def _gla_scan(p, s0, tm):
    b, l, _ = p["q"].shape
    amat = _gla_intra(p, tm)
    c = GLA_CHUNK * min(GLA_CHUNKS_PER_STEP, l // GLA_CHUNK)
    nc = l // c
    fwd = lambda w, col=0: pl.BlockSpec((1, c, w), lambda bi, n: (bi, n, col))
    bwd = lambda w, col=0: pl.BlockSpec((1, c, w), lambda bi, n: (bi, nc - 1 - n, col))
    st = pl.BlockSpec((1, 2 * GLA_HEADS, GLA_DK, GLA_DV), lambda bi, n: (bi, 0, 0, 0))
    a_spec = pl.BlockSpec((1, GLA_HEADS, c, GLA_CHUNK), lambda bi, n: (bi, 0, n, 0))
    oa, ob, sfin = pl.pallas_call(
        _gla_kernel,
        grid=(b, nc),
        in_specs=[fwd(GLA_QK), fwd(GLA_QK), fwd(GLA_V), fwd(GLA_QK, 0), a_spec,
                  bwd(GLA_QK), bwd(GLA_QK), bwd(GLA_V), bwd(GLA_QK, 1), st],
        out_specs=[fwd(GLA_V), bwd(GLA_V), st],
        out_shape=[jax.ShapeDtypeStruct((b, l, GLA_V), BF16), jax.ShapeDtypeStruct((b, l, GLA_V), BF16),
                   jax.ShapeDtypeStruct(s0.shape, F32)],
        scratch_shapes=[pltpu.VMEM((GLA_DK, GLA_DV), F32)] * (2 * GLA_HEADS),
        compiler_params=_cparams(("parallel", "arbitrary")),
        name="gla_scan",
    )(p["q"], p["k"], p["v"], p["gcum"], amat, p["q"], p["k"], p["v"], p["gcum"], s0)
    return oa, ob, sfin


def _split_hi_lo(x):
    hi = x.astype(BF16)
    lo = (x - hi.astype(F32)).astype(BF16)
    return jnp.concatenate([hi, lo], axis=1)


def _ssd_factors(sd, ex, d):
    c = sd.shape[0]
    nh = SSD_HEADS
    lane = lax.broadcasted_iota(jnp.int32, (1, LANES), 1)
    sel = (lane >= (2 + d) * nh) & (lane < (3 + d) * nh)
    tot = sd[0:1, :] if d == 1 else sd[c - 1:c, :]
    dt_under_cum = pltpu.roll(sd, 2 * nh, 1)
    e_in = jnp.where(sel, jnp.exp2(sd), 0.0)
    w_st = jnp.where(sel, dt_under_cum * jnp.exp2(tot - sd), 0.0)
    e_tot = jnp.broadcast_to(jnp.where(sel, jnp.exp2(tot), 0.0), (SUBLANES, LANES))
    expand = lambda f: _dot(_split_hi_lo(f), ex)
    return expand(e_in), expand(w_st), expand(e_tot)[0:1, :]


def _ssd_kernel(xsa_ref, bca_ref, bTa_ref, sda_ref, sdTa_ref, xsb_ref, bcb_ref, bTb_ref, sdb_ref,
                ex_ref, s0_ref, ya_ref, yb_ref, sfin_ref, *s_scr):
    c = SSD_CHUNK
    nh = SSD_HEADS
    gw = HEADS_PER_GROUP * SSD_HEADDIM
    n = pl.program_id(1)

    @pl.when(n == 0)
    def _():
        for k, s_ref in enumerate(s_scr):
            s_ref[...] = s0_ref[0, k]

    ii = lax.broadcasted_iota(jnp.int32, (c, c), 0)
    jj = lax.broadcasted_iota(jnp.int32, (c, c), 1)
    lower = jj <= ii
    upper = jj >= ii
    lo_half = lax.broadcasted_iota(jnp.int32, (1, LANES), 1) < SSD_HEADDIM

    cps = xsa_ref.shape[1] // c
    for sc in range(cps):
        ra = slice(sc * c, (sc + 1) * c)
        rb = slice((cps - 1 - sc) * c, (cps - sc) * c)
        sda = sda_ref[0, ra, :]
        sdTa = sdTa_ref[0, :, ra]
        ein_f, wst_f, dec_f = _ssd_factors(sda, ex_ref[0], 0)
        ein_b, wst_b, dec_b = _ssd_factors(sdb_ref[0, rb, :], ex_ref[1], 1)

        def weights(h, d, mask, sda=sda, sdTa=sdTa):
            col = (2 + d) * nh + h
            ci = sda[:, col:col + 1]
            cj = sdTa[col:col + 1, :]
            dtj = sdTa[d * nh + h:d * nh + h + 1, :]
            return jnp.where(mask, jnp.exp2(jnp.minimum(ci - cj, 0.0)) * dtj, 0.0)

        for gi in range(SSD_GROUPS):
            gs = slice(gi * gw, (gi + 1) * gw)
            ns = slice(gi * SSD_STATE, (gi + 1) * SSD_STATE)
            bm = bca_ref[0, ra, ns]
            cm = bca_ref[0, ra, SSD_BC + gi * SSD_STATE:SSD_BC + (gi + 1) * SSD_STATE]
            gmat = _dot_nt(cm, bm)
            s_f = s_scr[gi]
            s_b = s_scr[SSD_GROUPS + gi]
            y_inter = _dot(cm, s_f[...].astype(BF16)) * ein_f[:, gs]
            for hp in range(HEADS_PER_GROUP // 2):
                col0 = (gi * HEADS_PER_GROUP + 2 * hp) * SSD_HEADDIM
                xpair = xsa_ref[0, ra, col0:col0 + LANES]
                scores = []
                for t in range(2):
                    h = gi * HEADS_PER_GROUP + 2 * hp + t
                    scores.append((gmat * (weights(h, 0, lower) + weights(h, 1, upper))).astype(BF16))
                zero = jnp.zeros_like(xpair)
                rhs = jnp.concatenate([jnp.where(lo_half, xpair, zero), jnp.where(lo_half, zero, xpair)], axis=0)
                y_pair = _dot(jnp.concatenate(scores, axis=1), rhs) + y_inter[:, hp * LANES:(hp + 1) * LANES]
                ya_ref[0, ra, col0:col0 + LANES] = y_pair.astype(BF16)
            wx = (xsa_ref[0, ra, gs].astype(F32) * wst_f[:, gs]).astype(BF16)
            s_f[...] = s_f[...] * dec_f[:, gs] + _dot(bTa_ref[0, ns, ra], wx)

            cmb = bcb_ref[0, rb, SSD_BC + gi * SSD_STATE:SSD_BC + (gi + 1) * SSD_STATE]
            yb_ref[0, rb, gs] = (_dot(cmb, s_b[...].astype(BF16)) * ein_b[:, gs]).astype(BF16)
            wxb = (xsb_ref[0, rb, gs].astype(F32) * wst_b[:, gs]).astype(BF16)
            s_b[...] = s_b[...] * dec_b[:, gs] + _dot(bTb_ref[0, ns, rb], wxb)

    @pl.when(n == pl.num_programs(1) - 1)
    def _():
        for k, s_ref in enumerate(s_scr):
            sfin_ref[0, k] = s_ref[...]


def _ssd_expander():
    r = lax.broadcasted_iota(jnp.int32, (2, 2 * LANES, SSD_INNER), 1) % LANES
    d = lax.broadcasted_iota(jnp.int32, (2, 2 * LANES, SSD_INNER), 0)
    h = lax.broadcasted_iota(jnp.int32, (2, 2 * LANES, SSD_INNER), 2) // SSD_HEADDIM
    return (r == (2 + d) * SSD_HEADS + h).astype(BF16)


def _ssd_scan(p, s0):
    b, l, _ = p["xs"].shape
    c = SSD_CHUNK * min(SSD_CHUNKS_PER_STEP, l // SSD_CHUNK)
    nc = l // c
    specs = []
    for rev in (False, True):
        idx = (lambda n: nc - 1 - n) if rev else (lambda n: n)
        tok = lambda w, idx=idx: pl.BlockSpec((1, c, w), lambda bi, n: (bi, idx(n), 0))
        chan = lambda r, idx=idx: pl.BlockSpec((1, r, c), lambda bi, n: (bi, 0, idx(n)))
        specs.append([tok(SSD_INNER), tok(2 * SSD_BC), chan(SSD_BC), tok(LANES), chan(4 * SSD_HEADS)])
    st = pl.BlockSpec((1, 2 * SSD_GROUPS, SSD_STATE, HEADS_PER_GROUP * SSD_HEADDIM), lambda bi, n: (bi, 0, 0, 0))
    ex = _ssd_expander()
    ya, yb, sfin = pl.pallas_call(
        _ssd_kernel,
        grid=(b, nc),
        in_specs=specs[0] + specs[1][:4] + [_const_spec(ex.shape), st],
        out_specs=[specs[0][0], specs[1][0], st],
        out_shape=[jax.ShapeDtypeStruct((b, l, SSD_INNER), BF16), jax.ShapeDtypeStruct((b, l, SSD_INNER), BF16),
                   jax.ShapeDtypeStruct(s0.shape, F32)],
        scratch_shapes=[pltpu.VMEM((SSD_STATE, HEADS_PER_GROUP * SSD_HEADDIM), F32)] * (2 * SSD_GROUPS),
        compiler_params=_cparams(("parallel", "arbitrary")),
        name="ssd_scan",
    )(p["xs"], p["bc"], p["bT"], p["sd"], p["sdT"], p["xs"], p["bc"], p["bT"], p["sd"], ex, s0)
    return ya, yb, sfin


def _post0_kernel(of_ref, ob_ref, g_ref, yf_ref, yb_ref, xs_ref, z_ref, x_ref, mod_ref,
                  gn_ref, dsk_ref, sn_ref, wo_ref, out_ref):
    o = of_ref[0].astype(F32) + ob_ref[0].astype(F32)
    parts = []
    for hd in range(GLA_HEADS):
        oh = o[:, hd * GLA_DV:(hd + 1) * GLA_DV]
        parts.append(oh * lax.rsqrt(jnp.mean(oh * oh, axis=-1, keepdims=True) + EPS))
    gla = jnp.concatenate(parts, axis=1) * gn_ref[...] * _silu(g_ref[0].astype(F32))
    xs = xs_ref[0].astype(F32)
    y = (yf_ref[0].astype(F32) + yb_ref[0].astype(F32) + xs * dsk_ref[...]) * _silu(z_ref[0].astype(F32))
    gw = SSD_INNER // SSD_GROUPS
    parts = []
    for gi in range(SSD_GROUPS):
        yg = y[:, gi * gw:(gi + 1) * gw]
        parts.append(yg * lax.rsqrt(jnp.mean(yg * yg, axis=-1, keepdims=True) + EPS))
    ssd = jnp.concatenate(parts, axis=1) * sn_ref[...]
    res = _dot(gla.astype(BF16), wo_ref[:GLA_V, :]) + _dot(ssd.astype(BF16), wo_ref[GLA_V:, :])
    out_ref[0] = x_ref[0] + mod_ref[0, 2:3, :] * res


def _post0(p, of, ob, yf, yb, x, mod, wp, tm):
    b, l, _ = x.shape
    tm = min(tm, l)
    bm = mod.shape[0]
    mod_idx = (lambda bi, i: (bi, 0, 0)) if bm > 1 else (lambda bi, i: (0, 0, 0))
    tok = lambda w: pl.BlockSpec((1, tm, w), lambda bi, i: (bi, i, 0))
    return pl.pallas_call(
        _post0_kernel,
        grid=(b, l // tm),
        in_specs=[tok(GLA_V), tok(GLA_V), tok(GLA_V), tok(SSD_INNER), tok(SSD_INNER), tok(SSD_INNER),
                  tok(SSD_INNER), tok(D_MODEL), pl.BlockSpec((1, N_MOD, D_MODEL), mod_idx),
                  _const_spec((1, GLA_V)), _const_spec((1, SSD_INNER)), _const_spec((1, SSD_INNER)),
                  _const_spec(wp["w_out"].shape)],
        out_specs=tok(D_MODEL),
        out_shape=jax.ShapeDtypeStruct((b, l, D_MODEL), F32),
        compiler_params=_cparams(("parallel", "arbitrary")),
        name="post0",
    )(of, ob, p["g"], yf, yb, p["xs"], p["z"], x, mod, wp["gla_norm"], wp["d_skip"], wp["ssd_norm"], wp["w_out"])


def _ffn_kernel(x_ref, mod_ref, nw_ref, w1_ref, w3_ref, w2_ref, fn_ref, out_ref, *, final):
    x = x_ref[0]
    h = _norm_mod(x, nw_ref[...], mod_ref[0, 3:4, :], mod_ref[0, 4:5, :]).astype(BF16)
    half = D_FF // 2
    y = jnp.zeros_like(x)
    for j in range(2):
        cs = slice(j * half, (j + 1) * half)
        u = (_silu(_dot(h, w1_ref[0, :, cs])) * _dot(h, w3_ref[0, :, cs])).astype(BF16)
        y = y + _dot(u, w2_ref[0, cs, :])
    out = x + mod_ref[0, 5:6, :] * y
    if final:
        out = out * lax.rsqrt(jnp.mean(out * out, axis=-1, keepdims=True) + EPS) * fn_ref[...]
    out_ref[0] = out


def _ffn(x, mod, nw, w1, w3, w2, layer, final_norm, final, tm):
    b, l, _ = x.shape
    tm = min(tm, l)
    bm = mod.shape[0]
    mod_idx = (lambda bi, i: (bi, 0, 0)) if bm > 1 else (lambda bi, i: (0, 0, 0))
    tok = pl.BlockSpec((1, tm, D_MODEL), lambda bi, i: (bi, i, 0))
    wspec = lambda w: pl.BlockSpec((1,) + w.shape[1:], lambda bi, i: (layer, 0, 0), pipeline_mode=pl.Buffered(1))
    return pl.pallas_call(
        functools.partial(_ffn_kernel, final=final),
        grid=(b, l // tm),
        in_specs=[tok, pl.BlockSpec((1, N_MOD, D_MODEL), mod_idx), _const_spec((1, D_MODEL)),
                  wspec(w1), wspec(w3), wspec(w2), _const_spec((1, D_MODEL))],
        out_specs=tok,
        out_shape=jax.ShapeDtypeStruct((b, l, D_MODEL), F32),
        compiler_params=_cparams(("parallel", "arbitrary")),
        name="ffn_final" if final else "ffn",
    )(x, mod, nw, w1, w3, w2, final_norm)


def _inproj1_kernel(*refs, tm, ncol):
    if ncol:
        x_ref, xp_ref, xn_ref, perm_ref = refs[:4]
        refs = refs[4:]
    else:
        x_ref, xp_ref, xn_ref = refs[:3]
        refs = refs[3:]
    (mod_ref, nw_ref, wup_ref, cw_ref, cb_ref, bdq_ref, bdkT_ref, bdv_ref, wgm_ref,
     bif_ref, q_ref, kT_ref, v_ref, xc_ref, z_ref, gc_ref, gT_ref, hext, xm_scr) = refs
    i = pl.program_id(1)
    first = i == 0
    last = i == pl.num_programs(1) - 1
    shift = mod_ref[0, 0:1, :]
    scale = mod_ref[0, 1:2, :]
    nw = nw_ref[...]
    if ncol:
        hext[0:HALO, :] = _norm_mod(xp_ref[0, :, SUBLANES - 1, :], nw, shift, scale).astype(BF16)
        hn = _norm_mod(x_ref[0].reshape(tm, D_MODEL), nw, shift, scale).astype(BF16)
        hext[HALO:HALO + tm, :] = _dot(perm_ref[...], hn).astype(BF16)
        hext[HALO + tm:2 * HALO + tm, :] = _norm_mod(xn_ref[0, :, 0, :], nw, shift, scale).astype(BF16)
    else:
        hext[0:HALO, :] = _norm_mod(xp_ref[0], nw, shift, scale).astype(BF16)
        hext[HALO:HALO + tm, :] = _norm_mod(x_ref[0], nw, shift, scale).astype(BF16)
        hext[HALO + tm:2 * HALO + tm, :] = _norm_mod(xn_ref[0], nw, shift, scale).astype(BF16)
    h = hext[HALO:HALO + tm, :]

    xm_scr[...] = _dot(hext[...], wup_ref[:, :MLSTM_INNER])
    xc = _silu(_conv3(xm_scr, tm, cw_ref, cb_ref, first, last))
    xc_ref[0] = xc.astype(BF16)

    ng = 4 * MLSTM_HEADS
    gt = jnp.zeros((ng, tm), F32) + bif_ref[...]
    for t in range(N_BD_TILES):
        cs = slice(t * BD_TILE, (t + 1) * BD_TILE)
        xct = xc[:, cs].astype(BF16)
        xmt = xm_scr[HALO:HALO + tm, cs].astype(BF16)
        q_ref[0, :, cs] = _dot(xct, bdq_ref[t]).astype(BF16)
        v_ref[0, :, cs] = _dot(xmt, bdv_ref[t]).astype(BF16)
        kg = _dot_nt(bdkT_ref[t], xct)
        kT_ref[0, cs, :] = (kg[:BD_TILE] * (MLSTM_DH ** -0.5)).astype(BF16)
        gt = gt + kg[BD_TILE:] + _dot_nt(wgm_ref[t], xmt)
    r = lax.broadcasted_iota(jnp.int32, gt.shape, 0)
    lf = _log_sigmoid(gt)
    cum = jnp.where(r < 2 * MLSTM_HEADS, _seg_cumsum(lf, MLSTM_CHUNK, 1, False),
                    _seg_cumsum(lf, MLSTM_CHUNK, 1, True))
    packed = jnp.where(r % (2 * MLSTM_HEADS) < MLSTM_HEADS, gt, cum)
    a = gt - pltpu.roll(cum, 4 * MLSTM_HEADS - MLSTM_HEADS, 0)
    amax = jnp.where(r < 2 * MLSTM_HEADS, _seg_scan(a, MLSTM_CHUNK, 1, False, use_max=True),
                     _seg_scan(a, MLSTM_CHUNK, 1, True, use_max=True))
    full = jnp.concatenate([packed, amax], axis=0)
    gT_ref[0] = full
    gc_ref[0] = jnp.transpose(jnp.concatenate([full, jnp.zeros((LANES - MLSTM_GATE_ROWS, tm), F32)], axis=0))

    z_ref[0] = _dot(h, wup_ref[:, MLSTM_INNER:]).astype(BF16)


def _inproj1(x, mod, nw, wp, tm, colmajor):
    b, l, _ = x.shape
    tm = min(tm, l)
    nt = l // tm
    bm = mod.shape[0]
    mod_idx = (lambda bi, i: (bi, 0, 0)) if bm > 1 else (lambda bi, i: (0, 0, 0))
    if colmajor:
        rows = l // GRID_W
        ncol = tm // rows
        assert ncol % SUBLANES == 0 and rows % HALO == 0
        xv = x.reshape(b, rows, GRID_W, D_MODEL)
        rb = rows // HALO
        cb = ncol // SUBLANES
        ncb = GRID_W // SUBLANES
        x_spec = pl.BlockSpec((1, rows, ncol, D_MODEL), lambda bi, i: (bi, 0, i, 0))
        xp_spec = pl.BlockSpec((1, HALO, SUBLANES, D_MODEL),
                               lambda bi, i: (bi, rb - 1, jnp.maximum(i * cb - 1, 0), 0))
        xn_spec = pl.BlockSpec((1, HALO, SUBLANES, D_MODEL),
                               lambda bi, i: (bi, 0, jnp.minimum((i + 1) * cb, ncb - 1), 0))
        dst = lax.broadcasted_iota(jnp.int32, (tm, tm), 0)
        src = lax.broadcasted_iota(jnp.int32, (tm, tm), 1)
        perm = (src == (dst % rows) * ncol + dst // rows).astype(BF16)
        lead_specs = [x_spec, xp_spec, xn_spec, _const_spec((tm, tm))]
        lead_args = [xv, xv, xv, perm]
    else:
        ncol = 0
        xv = x
        hb = tm // HALO
        nhb = l // HALO
        x_spec = pl.BlockSpec((1, tm, D_MODEL), lambda bi, i: (bi, i, 0))
        xp_spec = pl.BlockSpec((1, HALO, D_MODEL), lambda bi, i: (bi, jnp.maximum(i * hb - 1, 0), 0))
        xn_spec = pl.BlockSpec((1, HALO, D_MODEL), lambda bi, i: (bi, jnp.minimum((i + 1) * hb, nhb - 1), 0))
        lead_specs = [x_spec, xp_spec, xn_spec]
        lead_args = [xv, xv, xv]
    tok = lambda w: pl.BlockSpec((1, tm, w), lambda bi, i: (bi, i, 0))
    names = ["w_up", "conv_w", "conv_b", "bdq", "bdkT", "bdv", "wgm", "b_if"]
    q, kT, v, xc, z, gc, gT = pl.pallas_call(
        functools.partial(_inproj1_kernel, tm=tm, ncol=ncol),
        grid=(b, nt),
        in_specs=lead_specs + [pl.BlockSpec((1, N_MOD, D_MODEL), mod_idx), _const_spec((1, D_MODEL))]
        + [_const_spec(wp[n].shape) for n in names],
        out_specs=[tok(MLSTM_INNER), pl.BlockSpec((1, MLSTM_INNER, tm), lambda bi, i: (bi, 0, i)),
                   tok(MLSTM_INNER), tok(MLSTM_INNER), tok(MLSTM_INNER), tok(LANES),
                   pl.BlockSpec((1, MLSTM_GATE_ROWS, tm), lambda bi, i: (bi, 0, i))],
        out_shape=[jax.ShapeDtypeStruct((b, l, MLSTM_INNER), BF16), jax.ShapeDtypeStruct((b, MLSTM_INNER, l), BF16),
                   jax.ShapeDtypeStruct((b, l, MLSTM_INNER), BF16), jax.ShapeDtypeStruct((b, l, MLSTM_INNER), BF16),
                   jax.ShapeDtypeStruct((b, l, MLSTM_INNER), BF16), jax.ShapeDtypeStruct((b, l, LANES), F32),
                   jax.ShapeDtypeStruct((b, MLSTM_GATE_ROWS, l), F32)],
        scratch_shapes=[pltpu.VMEM((tm + 2 * HALO, D_MODEL), BF16),
                        pltpu.VMEM((tm + 2 * HALO, MLSTM_INNER), F32)],
        compiler_params=_cparams(("parallel", "arbitrary")),
        name="inproj1",
    )(*lead_args, mod, nw, *[wp[n] for n in names])
    return dict(q=q, kT=kT, v=v, xc=xc, z=z, gc=gc, gT=gT)


def _mlstm_head(q, kT, v, gc, gT, c_scr, cb_scr, n_scr, m_scr, d, hd, reverse):
    c = MLSTM_CHUNK
    dh = MLSTM_DH
    idx = d * MLSTM_HEADS + hd
    ri = d * 2 * MLSTM_HEADS + hd
    rb = ri + MLSTM_HEADS
    rm = 4 * MLSTM_HEADS + ri
    c_scr, cb_scr, n_scr, m_scr = c_scr[idx], cb_scr[idx], n_scr[idx], m_scr[idx]
    bi = gc[:, rb:rb + 1]
    am_i = gc[:, rm:rm + 1]
    bj = gT[rb:rb + 1, :]
    a_j = gT[ri:ri + 1, :] - bj
    am_j = gT[rm:rm + 1, :]
    m0 = m_scr[0:1, 0:1]
    big_m = jnp.maximum(m0, am_i)
    ii = lax.broadcasted_iota(jnp.int32, (c, c), 0)
    jj = lax.broadcasted_iota(jnp.int32, (c, c), 1)
    causal = (jj >= ii) if reverse else (jj <= ii)
    s = _dot(q, kT) * jnp.where(causal, jnp.exp(a_j - big_m), 0.0)
    w_inter = jnp.exp(m0 - big_m)
    qf = q.astype(F32)
    den = jnp.sum(s, axis=-1, keepdims=True) \
        + w_inter * jnp.sum(qf * n_scr[0:1, :], axis=-1, keepdims=True)
    num = _dot(s.astype(BF16), v) + _dot((qf * w_inter).astype(BF16), cb_scr[...])
    h = num * (1.0 / jnp.maximum(jnp.abs(den), jnp.exp(-(bi + big_m))))

    last = 0 if reverse else c - 1
    m_inner = jnp.maximum(m0, am_j[:, last:last + 1])
    kw = kT.astype(F32) * jnp.exp(a_j - m_inner)
    decay = jnp.exp(m0 - m_inner)
    kwb = kw.astype(BF16)
    c_new = decay * c_scr[...] + _dot(kwb, v)
    c_scr[...] = c_new
    cb_scr[...] = c_new.astype(BF16)
    n_scr[...] = decay * n_scr[...] + _dot_nt(jnp.ones((SUBLANES, c), BF16), kwb)
    m_scr[...] = jnp.broadcast_to(bj[:, last:last + 1] + m_inner, m_scr.shape)
    return h


def _mlstm_kernel(*refs, has_init, emit_final):
    ins = refs[:10]
    pos = 10
    if has_init:
        c0_ref, n0_ref, m0_ref = refs[pos:pos + 3]
        pos += 3
    hf_ref, hb_ref = refs[pos:pos + 2]
    pos += 2
    if emit_final:
        cfin_ref, nfin_ref, mfin_ref = refs[pos:pos + 3]
        pos += 3
    nst = 2 * MLSTM_HEADS
    c_scr, cb_scr, n_scr, m_scr = (refs[pos + k * nst:pos + (k + 1) * nst] for k in range(4))
    n = pl.program_id(1)

    @pl.when(n == 0)
    def _():
        for k in range(nst):
            if has_init:
                c_scr[k][...] = c0_ref[0, k]
                cb_scr[k][...] = c0_ref[0, k].astype(BF16)
                n_scr[k][...] = n0_ref[0, k]
                m_scr[k][...] = m0_ref[0, k]
            else:
                c_scr[k][...] = jnp.zeros_like(c_scr[k])
                cb_scr[k][...] = jnp.zeros_like(cb_scr[k])
                n_scr[k][...] = jnp.zeros_like(n_scr[k])
                m_scr[k][...] = jnp.full(m_scr[k].shape, M_INIT, F32)

    for d, h_ref in enumerate((hf_ref, hb_ref)):
        q_ref, kT_ref, v_ref, gc_ref, gT_ref = ins[5 * d:5 * d + 5]
        gc = gc_ref[0]
        gT = gT_ref[0]
        for hd in range(MLSTM_HEADS):
            cs = slice(hd * MLSTM_DH, (hd + 1) * MLSTM_DH)
            h = _mlstm_head(q_ref[0, :, cs], kT_ref[0, cs, :], v_ref[0, :, cs], gc, gT,
                            c_scr, cb_scr, n_scr, m_scr, d, hd, d == 1)
            h_ref[0, :, cs] = h.astype(BF16)

    if emit_final:
        @pl.when(n == pl.num_programs(1) - 1)
        def _():
            for k in range(nst):
                cfin_ref[0, k] = c_scr[k][...]
                nfin_ref[0, k] = n_scr[k][...]
                mfin_ref[0, k] = m_scr[k][...]


def _mlstm_scan(p, init, emit_final):
    b, l, _ = p["q"].shape
    c = MLSTM_CHUNK
    nc = l // c
    dh = MLSTM_DH
    nst = 2 * MLSTM_HEADS
    specs = []
    for rev in (False, True):
        idx = (lambda n: nc - 1 - n) if rev else (lambda n: n)
        specs.append([
            pl.BlockSpec((1, c, MLSTM_INNER), lambda bi, n, idx=idx: (bi, idx(n), 0)),
            pl.BlockSpec((1, MLSTM_INNER, c), lambda bi, n, idx=idx: (bi, 0, idx(n))),
            pl.BlockSpec((1, c, MLSTM_INNER), lambda bi, n, idx=idx: (bi, idx(n), 0)),
            pl.BlockSpec((1, c, LANES), lambda bi, n, idx=idx: (bi, idx(n), 0)),
            pl.BlockSpec((1, MLSTM_GATE_ROWS, c), lambda bi, n, idx=idx: (bi, 0, idx(n))),
        ])
    st_shapes = [(b, nst, dh, dh), (b, nst, SUBLANES, dh), (b, nst, SUBLANES, LANES)]
    st_index = lambda bi, n: (bi, 0, 0, 0)
    args = (p["q"], p["kT"], p["v"], p["gc"], p["gT"])
    operands = [*args, *args]
    in_specs = specs[0] + specs[1]
    if init is not None:
        operands += list(init)
        in_specs += [pl.BlockSpec((1,) + s[1:], st_index, pipeline_mode=pl.Buffered(1)) for s in st_shapes]
    out_specs = [specs[0][0], specs[1][0]]
    out_shape = [jax.ShapeDtypeStruct((b, l, MLSTM_INNER), BF16), jax.ShapeDtypeStruct((b, l, MLSTM_INNER), BF16)]
    if emit_final:
        out_specs += [pl.BlockSpec((1,) + s[1:], st_index) for s in st_shapes]
        out_shape += [jax.ShapeDtypeStruct(s, F32) for s in st_shapes]
    res = pl.pallas_call(
        functools.partial(_mlstm_kernel, has_init=init is not None, emit_final=emit_final),
        grid=(b, nc),
        in_specs=in_specs,
        out_specs=out_specs,
        out_shape=out_shape,
        scratch_shapes=[pltpu.VMEM((dh, dh), F32)] * nst + [pltpu.VMEM((dh, dh), BF16)] * nst
        + [pltpu.VMEM((SUBLANES, dh), F32)] * nst + [pltpu.VMEM((SUBLANES, LANES), F32)] * nst,
        compiler_params=_cparams(("parallel", "arbitrary")),
        name="mlstm_scan",
    )(*operands)
    return res[0], res[1], tuple(res[2:])


def _post1_kernel(hf_ref, hb_ref, xc_ref, z_ref, x_ref, mod_ref, mh_ref, sk_ref, wd_ref, out_ref, *, tm, ncol):
    hs = hf_ref[0].astype(F32) + hb_ref[0].astype(F32)
    parts = []
    for hd in range(MLSTM_HEADS):
        hh = hs[:, hd * MLSTM_DH:(hd + 1) * MLSTM_DH]
        mu = jnp.mean(hh, axis=-1, keepdims=True)
        cen = hh - mu
        parts.append(cen * lax.rsqrt(jnp.mean(cen * cen, axis=-1, keepdims=True) + EPS))
    feat = (jnp.concatenate(parts, axis=1) * mh_ref[...] + sk_ref[...] * xc_ref[0].astype(F32)) \
        * _silu(z_ref[0].astype(F32))
    res = mod_ref[0, 2:3, :] * _dot(feat.astype(BF16), wd_ref[...])
    if ncol:
        rows = tm // ncol
        for j in range(ncol):
            out_ref[0, :, j, :] = x_ref[0, :, j, :] + res[j * rows:(j + 1) * rows, :]
    else:
        out_ref[0] = x_ref[0] + res


def _post1(p, hf, hb, x, mod, wp, tm, colmajor):
    b, l, _ = x.shape
    tm = min(tm, l)
    bm = mod.shape[0]
    mod_idx = (lambda bi, i: (bi, 0, 0)) if bm > 1 else (lambda bi, i: (0, 0, 0))
    tok = lambda w: pl.BlockSpec((1, tm, w), lambda bi, i: (bi, i, 0))
    if colmajor:
        rows = l // GRID_W
        ncol = tm // rows
        xv = x.reshape(b, rows, GRID_W, D_MODEL)
        x_spec = pl.BlockSpec((1, rows, ncol, D_MODEL), lambda bi, i: (bi, 0, i, 0))
    else:
        ncol = 0
        xv = x
        x_spec = tok(D_MODEL)
    out = pl.pallas_call(
        functools.partial(_post1_kernel, tm=tm, ncol=ncol),
        grid=(b, l // tm),
        in_specs=[tok(MLSTM_INNER), tok(MLSTM_INNER), tok(MLSTM_INNER), tok(MLSTM_INNER), x_spec,
                  pl.BlockSpec((1, N_MOD, D_MODEL), mod_idx),
                  _const_spec((1, MLSTM_INNER)), _const_spec((1, MLSTM_INNER)), _const_spec(wp["w_down"].shape)],
        out_specs=x_spec,
        out_shape=jax.ShapeDtypeStruct(xv.shape, F32),
        compiler_params=_cparams(("parallel", "arbitrary")),
        name="post1",
    )(hf, hb, p["xc"], p["z"], xv, mod, wp["mh_norm"], wp["skip"], wp["w_down"])
    return out.reshape(b, l, D_MODEL)


def _prep_even(w_in, w_gate, b_gate, gla_norm, conv_w, conv_b, a_log, dt_bias, d_skip, ssd_norm, w_out):
    sizes = (GLA_QK, GLA_QK, GLA_V, GLA_V, 2 * GLA_GATE_RANK, SSD_INNER, SSD_XBC, 2 * SSD_HEADS)
    offs = [0]
    for s in sizes:
        offs.append(offs[-1] + s)
    w16 = w_in.astype(BF16)
    w_main = jnp.concatenate([w16[:, offs[0]:offs[4]], w16[:, offs[5]:offs[7]]], axis=1)
    pad = jnp.zeros((D_MODEL, LANES - 2 * GLA_GATE_RANK - 2 * SSD_HEADS), BF16)
    w_small = jnp.concatenate([w16[:, offs[4]:offs[5]], w16[:, offs[7]:offs[8]], pad], axis=1)
    zero = jnp.zeros((GLA_GATE_RANK, GLA_QK), w_gate.dtype)
    wg = jnp.concatenate([jnp.concatenate([w_gate[0], zero], axis=1),
                          jnp.concatenate([zero, w_gate[1]], axis=1)], axis=0).astype(BF16)
    return dict(
        w_main=w_main, w_small=w_small, w_dtT=jnp.transpose(w_in[:, offs[7]:offs[8]]).astype(BF16),
        w_gate=wg, b_gate=b_gate.reshape(1, 2 * GLA_QK),
        conv_w=conv_w, conv_b=conv_b.reshape(1, SSD_XBC),
        dt_bias=dt_bias.reshape(1, 2 * SSD_HEADS), a_log=a_log.reshape(1, 2 * SSD_HEADS),
        dt_biasT=dt_bias.reshape(2 * SSD_HEADS, 1), a_logT=a_log.reshape(2 * SSD_HEADS, 1),
        gla_norm=jnp.tile(gla_norm, GLA_HEADS).reshape(1, GLA_V),
        d_skip=jnp.repeat(d_skip, SSD_HEADDIM).reshape(1, SSD_INNER),
        ssd_norm=ssd_norm.reshape(1, SSD_INNER),
        w_out=w_out.astype(BF16),
    )


def _block_diag_tiles(w, transposed=False, dtype=BF16):
    per = BD_TILE // QKV_BLOCK
    w4 = w.reshape(N_BD_TILES, per, QKV_BLOCK, QKV_BLOCK)
    blk = lax.broadcasted_iota(jnp.int32, (BD_TILE, BD_TILE), 0) // QKV_BLOCK
    same_block = blk == jnp.transpose(blk)
    if transposed:
        cols = jnp.transpose(w4, (0, 3, 1, 2)).reshape(N_BD_TILES, QKV_BLOCK, BD_TILE)
        full = jnp.tile(cols, (1, per, 1))
    else:
        rows = w4.reshape(N_BD_TILES, BD_TILE, QKV_BLOCK)
        full = jnp.tile(rows, (1, 1, per))
    return jnp.where(same_block, full, 0.0).astype(dtype)


def _prep_odd(w_up, conv_w, conv_b, wq, wk, wv, w_if, b_if, mh_norm, skip, w_down):
    ng = 4 * MLSTM_HEADS
    wif = jnp.transpose(w_if, (1, 0, 2)).reshape(3, N_BD_TILES, BD_TILE, ng)

    def fold(wblk, wg):
        return jnp.einsum("tio,tog->tgi", _block_diag_tiles(wblk, dtype=F32), wg, precision=lax.Precision.HIGHEST)

    wgc = (fold(wq, wif[0]) + fold(wk, wif[1])).astype(BF16)
    bdkT = jnp.concatenate([_block_diag_tiles(wk, transposed=True), wgc], axis=1)
    return dict(
        w_up=w_up.astype(BF16), conv_w=conv_w, conv_b=conv_b.reshape(1, MLSTM_INNER),
        bdq=_block_diag_tiles(wq), bdkT=bdkT, bdv=_block_diag_tiles(wv),
        wgm=fold(wv, wif[2]).astype(BF16), b_if=b_if.reshape(ng, 1),
        mh_norm=mh_norm.reshape(1, MLSTM_INNER), skip=skip.reshape(1, MLSTM_INNER),
        w_down=w_down.astype(BF16),
    )


TOKEN_TILE = 512


def kernel(x, c, ctx, c_ctx, w_mod, b_mod, norm_mix, norm_ffn, ffn_w1, ffn_w3, ffn_w2, a_w_in, a_gla_w_gate,
           a_gla_b_gate, a_gla_norm, a_ssd_conv_w, a_ssd_conv_b, a_ssd_A_log, a_ssd_dt_bias, a_ssd_D, a_ssd_norm,
           a_w_out, b_w_up, b_conv_w, b_conv_b, b_wq, b_wk, b_wv, b_w_if, b_b_if, b_mh_norm, b_skip, b_w_down,
           final_norm):
    tile = TOKEN_TILE
    bsz = x.shape[0]
    depth = w_mod.shape[0]
    pad_rows = (-(bsz + 1)) % 8
    cc = jnp.concatenate([c, c_ctx[None, :], jnp.zeros((pad_rows, D_MODEL), c.dtype)], axis=0)
    mods = _modulation(cc, w_mod, b_mod).reshape(depth, bsz + 1 + pad_rows, N_MOD, D_MODEL)
    fnorm = final_norm.reshape(1, D_MODEL)
    w1, w3, w2 = ffn_w1.astype(BF16), ffn_w3.astype(BF16), ffn_w2.astype(BF16)
    for layer in range(depth):
        last = layer == depth - 1
        j = layer // 2
        mx = mods[layer, :bsz]
        mc = mods[layer, bsz:bsz + 1]
        nmix = norm_mix[layer].reshape(1, D_MODEL)
        nffn = norm_ffn[layer].reshape(1, D_MODEL)
        if layer % 2 == 0:
            wp = _prep_even(a_w_in[j], a_gla_w_gate[j], a_gla_b_gate[j], a_gla_norm[j], a_ssd_conv_w[j],
                            a_ssd_conv_b[j], a_ssd_A_log[j], a_ssd_dt_bias[j], a_ssd_D[j], a_ssd_norm[j],
                            a_w_out[j])
            pc = _inproj0(ctx, mc, nmix, wp, tile)
            gla0 = jnp.zeros((bsz, 2 * GLA_HEADS, GLA_DK, GLA_DV), F32)
            ssd0 = jnp.zeros((bsz, 2 * SSD_GROUPS, SSD_STATE, HEADS_PER_GROUP * SSD_HEADDIM), F32)
            ofc, obc, gla_st = _gla_scan(pc, gla0, GLA_INTRA_TILE)
            yfc, ybc, ssd_st = _ssd_scan(pc, ssd0)
            px = _inproj0(x, mx, nmix, wp, tile)
            ofx, obx, _ = _gla_scan(px, gla_st, GLA_INTRA_TILE)
            yfx, ybx, _ = _ssd_scan(px, ssd_st)
            x = _post0(px, ofx, obx, yfx, ybx, x, mx, wp, tile)
            if not last:
                ctx = _post0(pc, ofc, obc, yfc, ybc, ctx, mc, wp, tile)
        else:
            wp = _prep_odd(b_w_up[j], b_conv_w[j], b_conv_b[j], b_wq[j], b_wk[j], b_wv[j], b_w_if[j], b_b_if[j],
                           b_mh_norm[j], b_skip[j], b_w_down[j])
            pc = _inproj1(ctx, mc, nmix, wp, tile, colmajor=False)
            hfc, hbc, state = _mlstm_scan(pc, None, True)
            px = _inproj1(x, mx, nmix, wp, tile, colmajor=True)
            hfx, hbx, _ = _mlstm_scan(px, state, False)
            x = _post1(px, hfx, hbx, x, mx, wp, tile, colmajor=True)
            if not last:
                ctx = _post1(pc, hfc, hbc, ctx, mc, wp, tile, colmajor=False)
        x = _ffn(x, mx, nffn, w1, w3, w2, layer, fnorm, last, tile)
        if not last:
            ctx = _ffn(ctx, mc, nffn, w1, w3, w2, layer, fnorm, False, tile)
    return x
```

```python
import functools

import jax
import jax.numpy as jnp
from jax import lax
from jax.experimental import pallas as pl
from jax.experimental.pallas import tpu as pltpu

F32 = jnp.float32
BF16 = jnp.bfloat16

D_MODEL = 1024
GRID_W = 64
EPS = 1e-6
LOG2_E = 1.4426950408889634
M_INIT = -1e30
N_MOD = 6

GLA_HEADS = 4
GLA_DK = 128
GLA_DV = 256
GLA_QK = GLA_HEADS * GLA_DK
GLA_V = GLA_HEADS * GLA_DV
GLA_GATE_RANK = 16
GLA_GATE_TAU = 16.0

SSD_HEADDIM = 64
SSD_HEADS = 16
SSD_STATE = 128
SSD_GROUPS = 2
SSD_INNER = SSD_HEADS * SSD_HEADDIM
SSD_BC = SSD_GROUPS * SSD_STATE
SSD_XBC = SSD_INNER + 2 * SSD_BC
HEADS_PER_GROUP = SSD_HEADS // SSD_GROUPS

MLSTM_INNER = 2 * D_MODEL
MLSTM_HEADS = 4
MLSTM_DH = MLSTM_INNER // MLSTM_HEADS
QKV_BLOCK = 4
BD_TILE = 256
N_BD_TILES = MLSTM_INNER // BD_TILE
MLSTM_GATE_ROWS = 32

D_FF = 2816

GLA_CHUNK = 64
GLA_CHUNKS_PER_STEP = 8
GLA_INTRA_TILE = 512
SSD_CHUNK = 128
SSD_CHUNKS_PER_STEP = 4
MLSTM_CHUNK = 256
HALO = 16
LANES = 128
SUBLANES = 8
TAIL_ROWS = 128

VMEM_LIMIT = 56 * 1024 * 1024


def _cparams(sem):
    return pltpu.CompilerParams(dimension_semantics=sem, vmem_limit_bytes=VMEM_LIMIT)


def _const_spec(shape):
    nd = len(shape)
    return pl.BlockSpec(shape, lambda *_: (0,) * nd, pipeline_mode=pl.Buffered(1))


def _softplus(x):
    return jnp.maximum(x, 0.0) + jnp.log(1.0 + jnp.exp(-jnp.abs(x)))


def _log_sigmoid(x):
    return -_softplus(-x)


def _silu(x):
    return x / (1.0 + jnp.exp(-x))


def _seg_scan(x, seg, axis, reverse, use_max=False):
    n = x.shape[axis]
    idx = lax.broadcasted_iota(jnp.int32, x.shape, axis) % seg
    s = 1
    while s < seg:
        if reverse:
            shifted = pltpu.roll(x, n - s, axis)
            keep = idx < seg - s
        else:
            shifted = pltpu.roll(x, s, axis)
            keep = idx >= s
        if use_max:
            x = jnp.where(keep, jnp.maximum(x, shifted), x)
        else:
            x = x + jnp.where(keep, shifted, 0.0)
        s *= 2
    return x


def _seg_cumsum(x, seg, axis, reverse):
    return _seg_scan(x, seg, axis, reverse)


def _chunk_cumsum_mxu(x, chunk, reverse):
    rows = x.shape[0]
    hi = x.astype(BF16)
    lo = (x - hi.astype(F32)).astype(BF16)
    i = lax.broadcasted_iota(jnp.int32, (chunk, 2 * chunk), 0)
    j = lax.broadcasted_iota(jnp.int32, (chunk, 2 * chunk), 1) % chunk
    tri = ((j >= i) if reverse else (j <= i)).astype(BF16)
    parts = []
    for c0 in range(0, rows, chunk):
        parts.append(_dot(tri, jnp.concatenate([hi[c0:c0 + chunk], lo[c0:c0 + chunk]], axis=0)))
    return jnp.concatenate(parts, axis=0)


def _norm_mod(xv, nw, shift, scale):
    ms = jnp.mean(xv * xv, axis=-1, keepdims=True)
    return (xv * lax.rsqrt(ms + EPS) * nw) * (1.0 + scale) + shift


def _dot(a, b):
    return jnp.dot(a, b, preferred_element_type=F32)


def _dot_nt(a, b):
    return lax.dot_general(a, b, (((1,), (1,)), ((), ())), preferred_element_type=F32)


def _dot_tn(a, b):
    return lax.dot_general(a, b, (((0,), (0,)), ((), ())), preferred_element_type=F32)


def _mod_kernel(c_ref, w_ref, b_ref, o_ref):
    a = _silu(c_ref[...]).astype(BF16)
    o_ref[0] = _dot(a, w_ref[0].astype(BF16)) + b_ref[0]


def _modulation(cc, w_mod, b_mod):
    depth = w_mod.shape[0]
    n = w_mod.shape[2]
    tn = n // 4
    rows = cc.shape[0]
    return pl.pallas_call(
        _mod_kernel,
        grid=(depth, n // tn),
        in_specs=[
            pl.BlockSpec((rows, D_MODEL), lambda l, j: (0, 0)),
            pl.BlockSpec((1, D_MODEL, tn), lambda l, j: (l, 0, j)),
            pl.BlockSpec((1, 1, tn), lambda l, j: (l, 0, j)),
        ],
        out_specs=pl.BlockSpec((1, rows, tn), lambda l, j: (l, 0, j)),
        out_shape=jax.ShapeDtypeStruct((depth, rows, n), F32),
        compiler_params=_cparams(("arbitrary", "arbitrary")),
        name="modulation",
    )(cc, w_mod, b_mod.reshape(depth, 1, n))


def _conv_pad(scr, tm, first, last):
    scr[HALO - 1:HALO, :] = scr[HALO - 1:HALO, :] * jnp.where(first, 0.0, 1.0)
    scr[HALO + tm:HALO + tm + 1, :] = scr[HALO + tm:HALO + tm + 1, :] * jnp.where(last, 0.0, 1.0)


def _conv_rows(scr, r0, n, cw_ref, cb_ref):
    prev = scr[HALO - 1 + r0:HALO - 1 + r0 + n, :]
    cur = scr[HALO + r0:HALO + r0 + n, :]
    nxt = scr[HALO + 1 + r0:HALO + 1 + r0 + n, :]
    return cw_ref[0:1, :] * prev + cw_ref[1:2, :] * cur + cw_ref[2:3, :] * nxt + cb_ref[...]


def _conv3(scr, tm, cw_ref, cb_ref, first, last):
    _conv_pad(scr, tm, first, last)
    return _conv_rows(scr, 0, tm, cw_ref, cb_ref)


def _inproj0_kernel(x_ref, xp_ref, xn_ref, mod_ref, nw_ref, wmain_ref, wsmall_ref, wdtT_ref, wg_ref, bg_ref,
                    cw_ref, cb_ref, dtb_ref, alog_ref, dtbT_ref, alogT_ref,
                    q_ref, k_ref, v_ref, g_ref, z_ref, gcum_ref, xs_ref, bc_ref, bT_ref, sd_ref, sdT_ref,
                    hext, xbc_scr, *, tm):
    i = pl.program_id(1)
    first = i == 0
    last = i == pl.num_programs(1) - 1
    shift = mod_ref[0, 0:1, :]
    scale = mod_ref[0, 1:2, :]
    nw = nw_ref[...]
    hext[0:HALO, :] = _norm_mod(xp_ref[0], nw, shift, scale).astype(BF16)
    hext[HALO:HALO + tm, :] = _norm_mod(x_ref[0], nw, shift, scale).astype(BF16)
    hext[HALO + tm:2 * HALO + tm, :] = _norm_mod(xn_ref[0], nw, shift, scale).astype(BF16)

    o_k = GLA_QK
    o_v = o_k + GLA_QK
    o_g = o_v + GLA_V
    o_z = o_g + GLA_V
    o_x = o_z + SSD_INNER
    nd = 2 * SSD_HEADS
    h = hext[HALO:HALO + tm, :]

    xbc_scr[...] = _dot(hext[...], wmain_ref[:, o_x:o_x + SSD_XBC])
    _conv_pad(xbc_scr, tm, first, last)
    small = _dot(h, wsmall_ref[...])
    dt_raw = _dot_nt(wdtT_ref[...], h)
    logits = _dot(small[:, :2 * GLA_GATE_RANK].astype(BF16), wg_ref[...]) + bg_ref[...]

    def tail(j):
        rows = slice(j * TAIL_ROWS, (j + 1) * TAIL_ROWS)
        lg = logits[rows]
        la = (jnp.minimum(lg, 0.0) * LOG2_E - jnp.log2(1.0 + jnp.exp2(jnp.abs(lg) * (-LOG2_E)))) \
            * (1.0 / GLA_GATE_TAU)
        gcum_ref[0, rows, :GLA_QK] = _chunk_cumsum_mxu(la[:, :GLA_QK], GLA_CHUNK, False)
        gcum_ref[0, rows, GLA_QK:] = _chunk_cumsum_mxu(la[:, GLA_QK:], GLA_CHUNK, True)
        dtT = _softplus(dt_raw[:, rows] + dtbT_ref[...])
        laT = dtT * (-LOG2_E * jnp.exp(alogT_ref[...]))
        cumT = jnp.concatenate([_seg_cumsum(laT[:SSD_HEADS], SSD_CHUNK, 1, False),
                                _seg_cumsum(laT[SSD_HEADS:], SSD_CHUNK, 1, True)], axis=0)
        packed = jnp.concatenate([dtT, cumT, jnp.zeros((LANES - 2 * nd, TAIL_ROWS), F32)], axis=0)
        sdT_ref[0, :, rows] = packed[:2 * nd]
        sd_ref[0, rows, :] = jnp.transpose(packed)
        y = _silu(_conv_rows(xbc_scr, j * TAIL_ROWS, TAIL_ROWS, cw_ref, cb_ref))
        xs_ref[0, rows, :] = y[:, :SSD_INNER].astype(BF16)
        bc_ref[0, rows, :] = y[:, SSD_INNER:].astype(BF16)
        bT_ref[0, :, rows] = jnp.transpose(y[:, SSD_INNER:SSD_INNER + SSD_BC]).astype(BF16)

    wide = [(q_ref, 0, o_k, GLA_DK ** -0.5), (k_ref, o_k, o_v, None), (v_ref, o_v, o_g, None),
            (g_ref, o_g, o_z, None), (z_ref, o_z, o_x, None)]
    n_tail = tm // TAIL_ROWS
    done = 0
    for idx, (ref, c0, c1, scl) in enumerate(wide):
        res = _dot(h, wmain_ref[:, c0:c1])
        ref[0] = (res if scl is None else res * scl).astype(BF16)
        upto = (n_tail * (idx + 1)) // (len(wide) - 1) if idx < len(wide) - 1 else n_tail
        for j in range(done, min(upto, n_tail)):
            tail(j)
        done = max(done, min(upto, n_tail))


def _inproj0(x, mod, nw, wp, tm):
    b, l, _ = x.shape
    tm = min(tm, l)
    nt = l // tm
    hb = tm // HALO
    nhb = l // HALO
    bm = mod.shape[0]
    mod_idx = (lambda bi, i: (bi, 0, 0)) if bm > 1 else (lambda bi, i: (0, 0, 0))
    tok = lambda w: pl.BlockSpec((1, tm, w), lambda bi, i: (bi, i, 0))
    in_specs = [
        tok(D_MODEL),
        pl.BlockSpec((1, HALO, D_MODEL), lambda bi, i: (bi, jnp.maximum(i * hb - 1, 0), 0)),
        pl.BlockSpec((1, HALO, D_MODEL), lambda bi, i: (bi, jnp.minimum((i + 1) * hb, nhb - 1), 0)),
        pl.BlockSpec((1, N_MOD, D_MODEL), mod_idx),
        _const_spec((1, D_MODEL)),
        _const_spec(wp["w_main"].shape), _const_spec(wp["w_small"].shape), _const_spec(wp["w_dtT"].shape),
        _const_spec(wp["w_gate"].shape), _const_spec(wp["b_gate"].shape),
        _const_spec(wp["conv_w"].shape), _const_spec(wp["conv_b"].shape),
        _const_spec(wp["dt_bias"].shape), _const_spec(wp["a_log"].shape),
        _const_spec(wp["dt_biasT"].shape), _const_spec(wp["a_logT"].shape),
    ]
    outs = [
        ("q", GLA_QK, BF16), ("k", GLA_QK, BF16), ("v", GLA_V, BF16), ("g", GLA_V, BF16),
        ("z", SSD_INNER, BF16), ("gcum", 2 * GLA_QK, F32), ("xs", SSD_INNER, BF16),
        ("bc", 2 * SSD_BC, BF16),
    ]
    out_specs = [tok(w) for _, w, _ in outs]
    out_shape = [jax.ShapeDtypeStruct((b, l, w), dt) for _, w, dt in outs]
    out_specs += [pl.BlockSpec((1, SSD_BC, tm), lambda bi, i: (bi, 0, i)),
                  tok(LANES),
                  pl.BlockSpec((1, 4 * SSD_HEADS, tm), lambda bi, i: (bi, 0, i))]
    out_shape += [jax.ShapeDtypeStruct((b, SSD_BC, l), BF16),
                  jax.ShapeDtypeStruct((b, l, LANES), F32),
                  jax.ShapeDtypeStruct((b, 4 * SSD_HEADS, l), F32)]
    res = pl.pallas_call(
        functools.partial(_inproj0_kernel, tm=tm),
        grid=(b, nt),
        in_specs=in_specs,
        out_specs=out_specs,
        out_shape=out_shape,
        scratch_shapes=[pltpu.VMEM((tm + 2 * HALO, D_MODEL), BF16),
                        pltpu.VMEM((tm + 2 * HALO, SSD_XBC), F32)],
        compiler_params=_cparams(("parallel", "arbitrary")),
        name="inproj0",
    )(x, x, x, mod, nw, wp["w_main"], wp["w_small"], wp["w_dtT"], wp["w_gate"], wp["b_gate"],
      wp["conv_w"], wp["conv_b"], wp["dt_bias"], wp["a_log"], wp["dt_biasT"], wp["a_logT"])
    names = [n for n, _, _ in outs] + ["bT", "sd", "sdT"]
    return dict(zip(names, res))


def _level_ref(g, hsz, reverse):
    c, w = g.shape
    grp = 2 * hsz
    r = hsz if reverse else hsz - 1
    if grp >= 8:
        g3 = g.reshape(c // grp, grp, w)
        return jnp.broadcast_to(g3[:, r:r + 1, :], g3.shape).reshape(c, w)
    g3 = g.reshape(c // 8, 8, w)
    sub = lax.broadcasted_iota(jnp.int32, g3.shape, 1)
    out = None
    for t in range(8 // grp):
        cand = jnp.broadcast_to(g3[:, t * grp + r:t * grp + r + 1, :], g3.shape)
        out = cand if out is None else jnp.where(sub >= t * grp, cand, out)
    return out.reshape(c, w)


def _gla_masks(c, w, reverse):
    row = lax.broadcasted_iota(jnp.int32, (c, w), 0)
    ii = lax.broadcasted_iota(jnp.int32, (c, c), 0)
    jj = lax.broadcasted_iota(jnp.int32, (c, c), 1)
    par = 0 if reverse else 1
    levels = []
    hsz = 1
    while hsz < c:
        is_q = (row // hsz) % 2 == par
        keep = ((ii // (2 * hsz)) == (jj // (2 * hsz))) & ((ii // hsz) % 2 == par) & ((jj // hsz) % 2 != par)
        levels.append((hsz, is_q, keep.astype(F32)))
        hsz *= 2
    return (ii == jj).astype(F32), levels


def _gla_level_operand(q, k, g, hsz, reverse, is_q):
    c = g.shape[0]
    par = 0 if reverse else 1
    if hsz >= SUBLANES:
        parts = []
        for grp in range(c // (2 * hsz)):
            base = grp * 2 * hsz
            ref_row = base + (hsz if reverse else hsz - 1)
            ref = g[ref_row:ref_row + 1, :]
            for half in range(2):
                rows = slice(base + half * hsz, base + (half + 1) * hsz)
                if half == par:
                    parts.append(q[rows] * jnp.exp2(g[rows] - ref))
                else:
                    parts.append(k[rows] * jnp.exp2(ref - g[rows]))
        return jnp.concatenate(parts, axis=0).astype(BF16)
    if hsz == 1:
        step = g - pltpu.roll(g, (c - 1) if reverse else 1, 0)
        return jnp.where(is_q, q * jnp.exp2(step), k).astype(BF16)
    e = jnp.exp2(-jnp.abs(g - _level_ref(g, hsz, reverse)))
    return (jnp.where(is_q, q, k) * e).astype(BF16)


def _gla_intra_kernel(q_ref, k_ref, g_ref, a_ref):
    c = GLA_CHUNK
    masks = [_gla_masks(c, GLA_DK, d == 1) for d in range(2)]
    for sc in range(q_ref.shape[1] // c):
        rs = slice(sc * c, (sc + 1) * c)
        for hd in range(GLA_HEADS):
            ks = slice(hd * GLA_DK, (hd + 1) * GLA_DK)
            qb = q_ref[0, rs, ks]
            kb = k_ref[0, rs, ks]
            q = qb.astype(F32)
            k = kb.astype(F32)
            a = (2.0 * masks[0][0]) * _dot_nt(qb, kb)
            for d in range(2):
                g = g_ref[0, rs, d * GLA_QK + hd * GLA_DK:d * GLA_QK + (hd + 1) * GLA_DK]
                for hsz, is_q, keep in masks[d][1]:
                    mixed = _gla_level_operand(q, k, g, hsz, d == 1, is_q)
                    a = a + keep * _dot_nt(mixed, mixed)
            a_ref[0, hd, rs, :] = a.astype(BF16)


def _gla_intra(p, tm):
    b, l, _ = p["q"].shape
    tm = min(tm, l)
    tok = lambda w: pl.BlockSpec((1, tm, w), lambda bi, i: (bi, i, 0))
    return pl.pallas_call(
        _gla_intra_kernel,
        grid=(b, l // tm),
        in_specs=[tok(GLA_QK), tok(GLA_QK), tok(2 * GLA_QK)],
        out_specs=pl.BlockSpec((1, GLA_HEADS, tm, GLA_CHUNK), lambda bi, i: (bi, 0, i, 0)),
        out_shape=jax.ShapeDtypeStruct((b, GLA_HEADS, l, GLA_CHUNK), BF16),
        compiler_params=_cparams(("parallel", "arbitrary")),
        name="gla_intra",
    )(p["q"], p["k"], p["gcum"])


def _gla_inter(q, k, v, g, s, reverse):
    c = q.shape[0]
    dk = g.shape[1]
    g_tot = g[0:1, :] if reverse else g[c - 1:c, :]
    qi = (q.astype(F32) * jnp.exp2(g)).astype(BF16)
    ki = (k.astype(F32) * jnp.exp2(g_tot - g)).astype(BF16)
    decay = jnp.transpose(jnp.broadcast_to(jnp.exp2(g_tot), (dk, dk)))
    decay = jnp.concatenate([decay] * (v.shape[1] // dk), axis=1)
    return _dot(qi, s.astype(BF16)), s * decay + _dot_tn(ki, v)


def _gla_kernel(qf_ref, kf_ref, vf_ref, gf_ref, af_ref, qb_ref, kb_ref, vb_ref, gb_ref, s0_ref,
                oa_ref, ob_ref, sfin_ref, *s_scr):
    n = pl.program_id(1)

    @pl.when(n == 0)
    def _():
        for k, s_ref in enumerate(s_scr):
            s_ref[...] = s0_ref[0, k]

    c = GLA_CHUNK
    cps = qf_ref.shape[1] // c
    for sc in range(cps):
        for hd in range(GLA_HEADS):
            ks = slice(hd * GLA_DK, (hd + 1) * GLA_DK)
            vs = slice(hd * GLA_DV, (hd + 1) * GLA_DV)
            rs = slice(sc * c, (sc + 1) * c)
            s_ref = s_scr[hd]
            v = vf_ref[0, rs, vs]
            inter, s_new = _gla_inter(qf_ref[0, rs, ks], kf_ref[0, rs, ks], v, gf_ref[0, rs, ks], s_ref[...], False)
            oa_ref[0, rs, vs] = (_dot(af_ref[0, hd, rs, :], v) + inter).astype(BF16)
            s_ref[...] = s_new

            rs = slice((cps - 1 - sc) * c, (cps - sc) * c)
            s_ref = s_scr[GLA_HEADS + hd]
            inter, s_new = _gla_inter(qb_ref[0, rs, ks], kb_ref[0, rs, ks], vb_ref[0, rs, vs], gb_ref[0, rs, ks],
                                      s_ref[...], True)
            ob_ref[0, rs, vs] = inter.astype(BF16)
            s_ref[...] = s_new

    @pl.when(n == pl.num_programs(1) - 1)
    def _():
        for k, s_ref in enumerate(s_scr):
            sfin_ref[0, k] = s_ref[...]


def _gla_scan(p, s0, tm):
    b, l, _ = p["q"].shape
    amat = _gla_intra(p, tm)
    c = GLA_CHUNK * min(GLA_CHUNKS_PER_STEP, l // GLA_CHUNK)
    nc = l // c
    fwd = lambda w, col=0: pl.BlockSpec((1, c, w), lambda bi, n: (bi, n, col))
    bwd = lambda w, col=0: pl.BlockSpec((1, c, w), lambda bi, n: (bi, nc - 1 - n, col))
    st = pl.BlockSpec((1, 2 * GLA_HEADS, GLA_DK, GLA_DV), lambda bi, n: (bi, 0, 0, 0))
    a_spec = pl.BlockSpec((1, GLA_HEADS, c, GLA_CHUNK), lambda bi, n: (bi, 0, n, 0))
    oa, ob, sfin = pl.pallas_call(
        _gla_kernel,
        grid=(b, nc),
        in_specs=[fwd(GLA_QK), fwd(GLA_QK), fwd(GLA_V), fwd(GLA_QK, 0), a_spec,
                  bwd(GLA_QK), bwd(GLA_QK), bwd(GLA_V), bwd(GLA_QK, 1), st],
        out_specs=[fwd(GLA_V), bwd(GLA_V), st],
        out_shape=[jax.ShapeDtypeStruct((b, l, GLA_V), BF16), jax.ShapeDtypeStruct((b, l, GLA_V), BF16),
                   jax.ShapeDtypeStruct(s0.shape, F32)],
        scratch_shapes=[pltpu.VMEM((GLA_DK, GLA_DV), F32)] * (2 * GLA_HEADS),
        compiler_params=_cparams(("parallel", "arbitrary")),
        name="gla_scan",
    )(p["q"], p["k"], p["v"], p["gcum"], amat, p["q"], p["k"], p["v"], p["gcum"], s0)
    return oa, ob, sfin


def _split_hi_lo(x):
    hi = x.astype(BF16)
    lo = (x - hi.astype(F32)).astype(BF16)
    return jnp.concatenate([hi, lo], axis=1)


def _ssd_factors(sd, ex, d):
    c = sd.shape[0]
    nh = SSD_HEADS
    lane = lax.broadcasted_iota(jnp.int32, (1, LANES), 1)
    sel = (lane >= (2 + d) * nh) & (lane < (3 + d) * nh)
    tot = sd[0:1, :] if d == 1 else sd[c - 1:c, :]
    dt_under_cum = pltpu.roll(sd, 2 * nh, 1)
    e_in = jnp.where(sel, jnp.exp2(sd), 0.0)
    w_st = jnp.where(sel, dt_under_cum * jnp.exp2(tot - sd), 0.0)
    e_tot = jnp.broadcast_to(jnp.where(sel, jnp.exp2(tot), 0.0), (SUBLANES, LANES))
    expand = lambda f: _dot(_split_hi_lo(f), ex)
    return expand(e_in), expand(w_st), expand(e_tot)[0:1, :]


def _ssd_kernel(xsa_ref, bca_ref, bTa_ref, sda_ref, sdTa_ref, xsb_ref, bcb_ref, bTb_ref, sdb_ref,
                ex_ref, s0_ref, ya_ref, yb_ref, sfin_ref, *s_scr):
    c = SSD_CHUNK
    nh = SSD_HEADS
    gw = HEADS_PER_GROUP * SSD_HEADDIM
    n = pl.program_id(1)

    @pl.when(n == 0)
    def _():
        for k, s_ref in enumerate(s_scr):
            s_ref[...] = s0_ref[0, k]

    ii = lax.broadcasted_iota(jnp.int32, (c, c), 0)
    jj = lax.broadcasted_iota(jnp.int32, (c, c), 1)
    lower = jj <= ii
    upper = jj >= ii
    lo_half = lax.broadcasted_iota(jnp.int32, (1, LANES), 1) < SSD_HEADDIM

    cps = xsa_ref.shape[1] // c
    for sc in range(cps):
        ra = slice(sc * c, (sc + 1) * c)
        rb = slice((cps - 1 - sc) * c, (cps - sc) * c)
        sda = sda_ref[0, ra, :]
        sdTa = sdTa_ref[0, :, ra]
        ein_f, wst_f, dec_f = _ssd_factors(sda, ex_ref[0], 0)
        ein_b, wst_b, dec_b = _ssd_factors(sdb_ref[0, rb, :], ex_ref[1], 1)

        def weights(h, d, mask, sda=sda, sdTa=sdTa):
            col = (2 + d) * nh + h
            ci = sda[:, col:col + 1]
            cj = sdTa[col:col + 1, :]
            dtj = sdTa[d * nh + h:d * nh + h + 1, :]
            return jnp.where(mask, jnp.exp2(jnp.minimum(ci - cj, 0.0)) * dtj, 0.0)

        for gi in range(SSD_GROUPS):
            gs = slice(gi * gw, (gi + 1) * gw)
            ns = slice(gi * SSD_STATE, (gi + 1) * SSD_STATE)
            bm = bca_ref[0, ra, ns]
            cm = bca_ref[0, ra, SSD_BC + gi * SSD_STATE:SSD_BC + (gi + 1) * SSD_STATE]
            gmat = _dot_nt(cm, bm)
            s_f = s_scr[gi]
            s_b = s_scr[SSD_GROUPS + gi]
            y_inter = _dot(cm, s_f[...].astype(BF16)) * ein_f[:, gs]
            for hp in range(HEADS_PER_GROUP // 2):
                col0 = (gi * HEADS_PER_GROUP + 2 * hp) * SSD_HEADDIM
                xpair = xsa_ref[0, ra, col0:col0 + LANES]
                scores = []
                for t in range(2):
                    h = gi * HEADS_PER_GROUP + 2 * hp + t
                    scores.append((gmat * (weights(h, 0, lower) + weights(h, 1, upper))).astype(BF16))
                zero = jnp.zeros_like(xpair)
                rhs = jnp.concatenate([jnp.where(lo_half, xpair, zero), jnp.where(lo_half, zero, xpair)], axis=0)
                y_pair = _dot(jnp.concatenate(scores, axis=1), rhs) + y_inter[:, hp * LANES:(hp + 1) * LANES]
                ya_ref[0, ra, col0:col0 + LANES] = y_pair.astype(BF16)
            wx = (xsa_ref[0, ra, gs].astype(F32) * wst_f[:, gs]).astype(BF16)
            s_f[...] = s_f[...] * dec_f[:, gs] + _dot(bTa_ref[0, ns, ra], wx)

            cmb = bcb_ref[0, rb, SSD_BC + gi * SSD_STATE:SSD_BC + (gi + 1) * SSD_STATE]
            yb_ref[0, rb, gs] = (_dot(cmb, s_b[...].astype(BF16)) * ein_b[:, gs]).astype(BF16)
            wxb = (xsb_ref[0, rb, gs].astype(F32) * wst_b[:, gs]).astype(BF16)
            s_b[...] = s_b[...] * dec_b[:, gs] + _dot(bTb_ref[0, ns, rb], wxb)

    @pl.when(n == pl.num_programs(1) - 1)
    def _():
        for k, s_ref in enumerate(s_scr):
            sfin_ref[0, k] = s_ref[...]


def _ssd_expander():
    r = lax.broadcasted_iota(jnp.int32, (2, 2 * LANES, SSD_INNER), 1) % LANES
    d = lax.broadcasted_iota(jnp.int32, (2, 2 * LANES, SSD_INNER), 0)
    h = lax.broadcasted_iota(jnp.int32, (2, 2 * LANES, SSD_INNER), 2) // SSD_HEADDIM
    return (r == (2 + d) * SSD_HEADS + h).astype(BF16)


def _ssd_scan(p, s0):
    b, l, _ = p["xs"].shape
    c = SSD_CHUNK * min(SSD_CHUNKS_PER_STEP, l // SSD_CHUNK)
    nc = l // c
    specs = []
    for rev in (False, True):
        idx = (lambda n: nc - 1 - n) if rev else (lambda n: n)
        tok = lambda w, idx=idx: pl.BlockSpec((1, c, w), lambda bi, n: (bi, idx(n), 0))
        chan = lambda r, idx=idx: pl.BlockSpec((1, r, c), lambda bi, n: (bi, 0, idx(n)))
        specs.append([tok(SSD_INNER), tok(2 * SSD_BC), chan(SSD_BC), tok(LANES), chan(4 * SSD_HEADS)])
    st = pl.BlockSpec((1, 2 * SSD_GROUPS, SSD_STATE, HEADS_PER_GROUP * SSD_HEADDIM), lambda bi, n: (bi, 0, 0, 0))
    ex = _ssd_expander()
    ya, yb, sfin = pl.pallas_call(
        _ssd_kernel,
        grid=(b, nc),
        in_specs=specs[0] + specs[1][:4] + [_const_spec(ex.shape), st],
        out_specs=[specs[0][0], specs[1][0], st],
        out_shape=[jax.ShapeDtypeStruct((b, l, SSD_INNER), BF16), jax.ShapeDtypeStruct((b, l, SSD_INNER), BF16),
                   jax.ShapeDtypeStruct(s0.shape, F32)],
        scratch_shapes=[pltpu.VMEM((SSD_STATE, HEADS_PER_GROUP * SSD_HEADDIM), F32)] * (2 * SSD_GROUPS),
        compiler_params=_cparams(("parallel", "arbitrary")),
        name="ssd_scan",
    )(p["xs"], p["bc"], p["bT"], p["sd"], p["sdT"], p["xs"], p["bc"], p["bT"], p["sd"], ex, s0)
    return ya, yb, sfin


def _post0_kernel(of_ref, ob_ref, g_ref, yf_ref, yb_ref, xs_ref, z_ref, x_ref, mod_ref,
                  gn_ref, dsk_ref, sn_ref, wo_ref, out_ref):
    o = of_ref[0].astype(F32) + ob_ref[0].astype(F32)
    parts = []
    for hd in range(GLA_HEADS):
        oh = o[:, hd * GLA_DV:(hd + 1) * GLA_DV]
        parts.append(oh * lax.rsqrt(jnp.mean(oh * oh, axis=-1, keepdims=True) + EPS))
    gla = jnp.concatenate(parts, axis=1) * gn_ref[...] * _silu(g_ref[0].astype(F32))
    xs = xs_ref[0].astype(F32)
    y = (yf_ref[0].astype(F32) + yb_ref[0].astype(F32) + xs * dsk_ref[...]) * _silu(z_ref[0].astype(F32))
    gw = SSD_INNER // SSD_GROUPS
    parts = []
    for gi in range(SSD_GROUPS):
        yg = y[:, gi * gw:(gi + 1) * gw]
        parts.append(yg * lax.rsqrt(jnp.mean(yg * yg, axis=-1, keepdims=True) + EPS))
    ssd = jnp.concatenate(parts, axis=1) * sn_ref[...]
    res = _dot(gla.astype(BF16), wo_ref[:GLA_V, :]) + _dot(ssd.astype(BF16), wo_ref[GLA_V:, :])
    out_ref[0] = x_ref[0] + mod_ref[0, 2:3, :] * res


def _post0(p, of, ob, yf, yb, x, mod, wp, tm):
    b, l, _ = x.shape
    tm = min(tm, l)
    bm = mod.shape[0]
    mod_idx = (lambda bi, i: (bi, 0, 0)) if bm > 1 else (lambda bi, i: (0, 0, 0))
    tok = lambda w: pl.BlockSpec((1, tm, w), lambda bi, i: (bi, i, 0))
    return pl.pallas_call(
        _post0_kernel,
        grid=(b, l // tm),
        in_specs=[tok(GLA_V), tok(GLA_V), tok(GLA_V), tok(SSD_INNER), tok(SSD_INNER), tok(SSD_INNER),
                  tok(SSD_INNER), tok(D_MODEL), pl.BlockSpec((1, N_MOD, D_MODEL), mod_idx),
                  _const_spec((1, GLA_V)), _const_spec((1, SSD_INNER)), _const_spec((1, SSD_INNER)),
                  _const_spec(wp["w_out"].shape)],
        out_specs=tok(D_MODEL),
        out_shape=jax.ShapeDtypeStruct((b, l, D_MODEL), F32),
        compiler_params=_cparams(("parallel", "arbitrary")),
        name="post0",
    )(of, ob, p["g"], yf, yb, p["xs"], p["z"], x, mod, wp["gla_norm"], wp["d_skip"], wp["ssd_norm"], wp["w_out"])


def _ffn_kernel(x_ref, mod_ref, nw_ref, w1_ref, w3_ref, w2_ref, fn_ref, out_ref, *, final):
    x = x_ref[0]
    h = _norm_mod(x, nw_ref[...], mod_ref[0, 3:4, :], mod_ref[0, 4:5, :]).astype(BF16)
    half = D_FF // 2
    y = jnp.zeros_like(x)
    for j in range(2):
        cs = slice(j * half, (j + 1) * half)
        u = (_silu(_dot(h, w1_ref[0, :, cs])) * _dot(h, w3_ref[0, :, cs])).astype(BF16)
        y = y + _dot(u, w2_ref[0, cs, :])
    out = x + mod_ref[0, 5:6, :] * y
    if final:
        out = out * lax.rsqrt(jnp.mean(out * out, axis=-1, keepdims=True) + EPS) * fn_ref[...]
    out_ref[0] = out


def _ffn(x, mod, nw, w1, w3, w2, layer, final_norm, final, tm):
    b, l, _ = x.shape
    tm = min(tm, l)
    bm = mod.shape[0]
    mod_idx = (lambda bi, i: (bi, 0, 0)) if bm > 1 else (lambda bi, i: (0, 0, 0))
    tok = pl.BlockSpec((1, tm, D_MODEL), lambda bi, i: (bi, i, 0))
    wspec = lambda w: pl.BlockSpec((1,) + w.shape[1:], lambda bi, i: (layer, 0, 0), pipeline_mode=pl.Buffered(1))
    return pl.pallas_call(
        functools.partial(_ffn_kernel, final=final),
        grid=(b, l // tm),
        in_specs=[tok, pl.BlockSpec((1, N_MOD, D_MODEL), mod_idx), _const_spec((1, D_MODEL)),
                  wspec(w1), wspec(w3), wspec(w2), _const_spec((1, D_MODEL))],
        out_specs=tok,
        out_shape=jax.ShapeDtypeStruct((b, l, D_MODEL), F32),
        compiler_params=_cparams(("parallel", "arbitrary")),
        name="ffn_final" if final else "ffn",
    )(x, mod, nw, w1, w3, w2, final_norm)


def _inproj1_kernel(*refs, tm, ncol):
    if ncol:
        x_ref, xp_ref, xn_ref, perm_ref = refs[:4]
        refs = refs[4:]
    else:
        x_ref, xp_ref, xn_ref = refs[:3]
        refs = refs[3:]
    (mod_ref, nw_ref, wup_ref, cw_ref, cb_ref, bdq_ref, bdkT_ref, bdv_ref, wgm_ref,
     bif_ref, q_ref, kT_ref, v_ref, xc_ref, z_ref, gc_ref, gT_ref, hext, xm_scr) = refs
    i = pl.program_id(1)
    first = i == 0
    last = i == pl.num_programs(1) - 1
    shift = mod_ref[0, 0:1, :]
    scale = mod_ref[0, 1:2, :]
    nw = nw_ref[...]
    if ncol:
        hext[0:HALO, :] = _norm_mod(xp_ref[0, :, SUBLANES - 1, :], nw, shift, scale).astype(BF16)
        hn = _norm_mod(x_ref[0].reshape(tm, D_MODEL), nw, shift, scale).astype(BF16)
        hext[HALO:HALO + tm, :] = _dot(perm_ref[...], hn).astype(BF16)
        hext[HALO + tm:2 * HALO + tm, :] = _norm_mod(xn_ref[0, :, 0, :], nw, shift, scale).astype(BF16)
    else:
        hext[0:HALO, :] = _norm_mod(xp_ref[0], nw, shift, scale).astype(BF16)
        hext[HALO:HALO + tm, :] = _norm_mod(x_ref[0], nw, shift, scale).astype(BF16)
        hext[HALO + tm:2 * HALO + tm, :] = _norm_mod(xn_ref[0], nw, shift, scale).astype(BF16)
    h = hext[HALO:HALO + tm, :]

    xm_scr[...] = _dot(hext[...], wup_ref[:, :MLSTM_INNER])
    xc = _silu(_conv3(xm_scr, tm, cw_ref, cb_ref, first, last))
    xc_ref[0] = xc.astype(BF16)

    ng = 4 * MLSTM_HEADS
    gt = jnp.zeros((ng, tm), F32) + bif_ref[...]
    for t in range(N_BD_TILES):
        cs = slice(t * BD_TILE, (t + 1) * BD_TILE)
        xct = xc[:, cs].astype(BF16)
        xmt = xm_scr[HALO:HALO + tm, cs].astype(BF16)
        q_ref[0, :, cs] = _dot(xct, bdq_ref[t]).astype(BF16)
        v_ref[0, :, cs] = _dot(xmt, bdv_ref[t]).astype(BF16)
        kg = _dot_nt(bdkT_ref[t], xct)
        kT_ref[0, cs, :] = (kg[:BD_TILE] * (MLSTM_DH ** -0.5)).astype(BF16)
        gt = gt + kg[BD_TILE:] + _dot_nt(wgm_ref[t], xmt)
    r = lax.broadcasted_iota(jnp.int32, gt.shape, 0)
    lf = _log_sigmoid(gt)
    cum = jnp.where(r < 2 * MLSTM_HEADS, _seg_cumsum(lf, MLSTM_CHUNK, 1, False),
                    _seg_cumsum(lf, MLSTM_CHUNK, 1, True))
    packed = jnp.where(r % (2 * MLSTM_HEADS) < MLSTM_HEADS, gt, cum)
    a = gt - pltpu.roll(cum, 4 * MLSTM_HEADS - MLSTM_HEADS, 0)
    amax = jnp.where(r < 2 * MLSTM_HEADS, _seg_scan(a, MLSTM_CHUNK, 1, False, use_max=True),
                     _seg_scan(a, MLSTM_CHUNK, 1, True, use_max=True))
    full = jnp.concatenate([packed, amax], axis=0)
    gT_ref[0] = full
    gc_ref[0] = jnp.transpose(jnp.concatenate([full, jnp.zeros((LANES - MLSTM_GATE_ROWS, tm), F32)], axis=0))

    z_ref[0] = _dot(h, wup_ref[:, MLSTM_INNER:]).astype(BF16)


def _inproj1(x, mod, nw, wp, tm, colmajor):
    b, l, _ = x.shape
    tm = min(tm, l)
    nt = l // tm
    bm = mod.shape[0]
    mod_idx = (lambda bi, i: (bi, 0, 0)) if bm > 1 else (lambda bi, i: (0, 0, 0))
    if colmajor:
        rows = l // GRID_W
        ncol = tm // rows
        assert ncol % SUBLANES == 0 and rows % HALO == 0
        xv = x.reshape(b, rows, GRID_W, D_MODEL)
        rb = rows // HALO
        cb = ncol // SUBLANES
        ncb = GRID_W // SUBLANES
        x_spec = pl.BlockSpec((1, rows, ncol, D_MODEL), lambda bi, i: (bi, 0, i, 0))
        xp_spec = pl.BlockSpec((1, HALO, SUBLANES, D_MODEL),
                               lambda bi, i: (bi, rb - 1, jnp.maximum(i * cb - 1, 0), 0))
        xn_spec = pl.BlockSpec((1, HALO, SUBLANES, D_MODEL),
                               lambda bi, i: (bi, 0, jnp.minimum((i + 1) * cb, ncb - 1), 0))
        dst = lax.broadcasted_iota(jnp.int32, (tm, tm), 0)
        src = lax.broadcasted_iota(jnp.int32, (tm, tm), 1)
        perm = (src == (dst % rows) * ncol + dst // rows).astype(BF16)
        lead_specs = [x_spec, xp_spec, xn_spec, _const_spec((tm, tm))]
        lead_args = [xv, xv, xv, perm]
    else:
        ncol = 0
        xv = x
        hb = tm // HALO
        nhb = l // HALO
        x_spec = pl.BlockSpec((1, tm, D_MODEL), lambda bi, i: (bi, i, 0))
        xp_spec = pl.BlockSpec((1, HALO, D_MODEL), lambda bi, i: (bi, jnp.maximum(i * hb - 1, 0), 0))
        xn_spec = pl.BlockSpec((1, HALO, D_MODEL), lambda bi, i: (bi, jnp.minimum((i + 1) * hb, nhb - 1), 0))
        lead_specs = [x_spec, xp_spec, xn_spec]
        lead_args = [xv, xv, xv]
    tok = lambda w: pl.BlockSpec((1, tm, w), lambda bi, i: (bi, i, 0))
    names = ["w_up", "conv_w", "conv_b", "bdq", "bdkT", "bdv", "wgm", "b_if"]
    q, kT, v, xc, z, gc, gT = pl.pallas_call(
        functools.partial(_inproj1_kernel, tm=tm, ncol=ncol),
        grid=(b, nt),
        in_specs=lead_specs + [pl.BlockSpec((1, N_MOD, D_MODEL), mod_idx), _const_spec((1, D_MODEL))]
        + [_const_spec(wp[n].shape) for n in names],
        out_specs=[tok(MLSTM_INNER), pl.BlockSpec((1, MLSTM_INNER, tm), lambda bi, i: (bi, 0, i)),
                   tok(MLSTM_INNER), tok(MLSTM_INNER), tok(MLSTM_INNER), tok(LANES),
                   pl.BlockSpec((1, MLSTM_GATE_ROWS, tm), lambda bi, i: (bi, 0, i))],
        out_shape=[jax.ShapeDtypeStruct((b, l, MLSTM_INNER), BF16), jax.ShapeDtypeStruct((b, MLSTM_INNER, l), BF16),
                   jax.ShapeDtypeStruct((b, l, MLSTM_INNER), BF16), jax.ShapeDtypeStruct((b, l, MLSTM_INNER), BF16),
                   jax.ShapeDtypeStruct((b, l, MLSTM_INNER), BF16), jax.ShapeDtypeStruct((b, l, LANES), F32),
                   jax.ShapeDtypeStruct((b, MLSTM_GATE_ROWS, l), F32)],
        scratch_shapes=[pltpu.VMEM((tm + 2 * HALO, D_MODEL), BF16),
                        pltpu.VMEM((tm + 2 * HALO, MLSTM_INNER), F32)],
        compiler_params=_cparams(("parallel", "arbitrary")),
        name="inproj1",
    )(*lead_args, mod, nw, *[wp[n] for n in names])
    return dict(q=q, kT=kT, v=v, xc=xc, z=z, gc=gc, gT=gT)


def _mlstm_head(q, kT, v, gc, gT, c_scr, cb_scr, n_scr, m_scr, d, hd, reverse):
    c = MLSTM_CHUNK
    dh = MLSTM_DH
    idx = d * MLSTM_HEADS + hd
    ri = d * 2 * MLSTM_HEADS + hd
    rb = ri + MLSTM_HEADS
    rm = 4 * MLSTM_HEADS + ri
    c_scr, cb_scr, n_scr, m_scr = c_scr[idx], cb_scr[idx], n_scr[idx], m_scr[idx]
    bi = gc[:, rb:rb + 1]
    am_i = gc[:, rm:rm + 1]
    bj = gT[rb:rb + 1, :]
    a_j = gT[ri:ri + 1, :] - bj
    am_j = gT[rm:rm + 1, :]
    m0 = m_scr[0:1, 0:1]
    big_m = jnp.maximum(m0, am_i)
    ii = lax.broadcasted_iota(jnp.int32, (c, c), 0)
    jj = lax.broadcasted_iota(jnp.int32, (c, c), 1)
    causal = (jj >= ii) if reverse else (jj <= ii)
    s = _dot(q, kT) * jnp.where(causal, jnp.exp(a_j - big_m), 0.0)
    w_inter = jnp.exp(m0 - big_m)
    qf = q.astype(F32)
    den = jnp.sum(s, axis=-1, keepdims=True) \
        + w_inter * jnp.sum(qf * n_scr[0:1, :], axis=-1, keepdims=True)
    sb = s.astype(BF16)
    qw = (qf * w_inter).astype(BF16)
    inv = 1.0 / jnp.maximum(jnp.abs(den), jnp.exp(-(bi + big_m)))
    halves = [slice(j * BD_TILE, (j + 1) * BD_TILE) for j in range(dh // BD_TILE)]
    h = jnp.concatenate([(_dot(sb, v[:, hs]) + _dot(qw, cb_scr[:, hs])) * inv for hs in halves], axis=1)

    last = 0 if reverse else c - 1
    m_inner = jnp.maximum(m0, am_j[:, last:last + 1])
    kw = kT.astype(F32) * jnp.exp(a_j - m_inner)
    decay = jnp.exp(m0 - m_inner)
    kwb = kw.astype(BF16)
    for hs in halves:
        c_new = decay * c_scr[:, hs] + _dot(kwb, v[:, hs])
        c_scr[:, hs] = c_new
        cb_scr[:, hs] = c_new.astype(BF16)
    n_scr[...] = decay * n_scr[...] + _dot_nt(jnp.ones((SUBLANES, c), BF16), kwb)
    m_scr[...] = jnp.broadcast_to(bj[:, last:last + 1] + m_inner, m_scr.shape)
    return h


def _mlstm_kernel(*refs, has_init, emit_final):
    ins = refs[:10]
    pos = 10
    if has_init:
        c0_ref, n0_ref, m0_ref = refs[pos:pos + 3]
        pos += 3
    hf_ref, hb_ref = refs[pos:pos + 2]
    pos += 2
    if emit_final:
        cfin_ref, nfin_ref, mfin_ref = refs[pos:pos + 3]
        pos += 3
    nst = 2 * MLSTM_HEADS
    c_scr, cb_scr, n_scr, m_scr = (refs[pos + k * nst:pos + (k + 1) * nst] for k in range(4))
    n = pl.program_id(1)

    @pl.when(n == 0)
    def _():
        for k in range(nst):
            if has_init:
                c_scr[k][...] = c0_ref[0, k]
                cb_scr[k][...] = c0_ref[0, k].astype(BF16)
                n_scr[k][...] = n0_ref[0, k]
                m_scr[k][...] = m0_ref[0, k]
            else:
                c_scr[k][...] = jnp.zeros_like(c_scr[k])
                cb_scr[k][...] = jnp.zeros_like(cb_scr[k])
                n_scr[k][...] = jnp.zeros_like(n_scr[k])
                m_scr[k][...] = jnp.full(m_scr[k].shape, M_INIT, F32)

    for d, h_ref in enumerate((hf_ref, hb_ref)):
        q_ref, kT_ref, v_ref, gc_ref, gT_ref = ins[5 * d:5 * d + 5]
        gc = gc_ref[0]
        gT = gT_ref[0]
        for hd in range(MLSTM_HEADS):
            cs = slice(hd * MLSTM_DH, (hd + 1) * MLSTM_DH)
            h = _mlstm_head(q_ref[0, :, cs], kT_ref[0, cs, :], v_ref[0, :, cs], gc, gT,
                            c_scr, cb_scr, n_scr, m_scr, d, hd, d == 1)
            h_ref[0, :, cs] = h.astype(BF16)

    if emit_final:
        @pl.when(n == pl.num_programs(1) - 1)
        def _():
            for k in range(nst):
                cfin_ref[0, k] = c_scr[k][...]
                nfin_ref[0, k] = n_scr[k][...]
                mfin_ref[0, k] = m_scr[k][...]


def _mlstm_scan(p, init, emit_final):
    b, l, _ = p["q"].shape
    c = MLSTM_CHUNK
    nc = l // c
    dh = MLSTM_DH
    nst = 2 * MLSTM_HEADS
    specs = []
    for rev in (False, True):
        idx = (lambda n: nc - 1 - n) if rev else (lambda n: n)
        specs.append([
            pl.BlockSpec((1, c, MLSTM_INNER), lambda bi, n, idx=idx: (bi, idx(n), 0)),
            pl.BlockSpec((1, MLSTM_INNER, c), lambda bi, n, idx=idx: (bi, 0, idx(n))),
            pl.BlockSpec((1, c, MLSTM_INNER), lambda bi, n, idx=idx: (bi, idx(n), 0)),
            pl.BlockSpec((1, c, LANES), lambda bi, n, idx=idx: (bi, idx(n), 0)),
            pl.BlockSpec((1, MLSTM_GATE_ROWS, c), lambda bi, n, idx=idx: (bi, 0, idx(n))),
        ])
    st_shapes = [(b, nst, dh, dh), (b, nst, SUBLANES, dh), (b, nst, SUBLANES, LANES)]
    st_index = lambda bi, n: (bi, 0, 0, 0)
    args = (p["q"], p["kT"], p["v"], p["gc"], p["gT"])
    operands = [*args, *args]
    in_specs = specs[0] + specs[1]
    if init is not None:
        operands += list(init)
        in_specs += [pl.BlockSpec((1,) + s[1:], st_index, pipeline_mode=pl.Buffered(1)) for s in st_shapes]
    out_specs = [specs[0][0], specs[1][0]]
    out_shape = [jax.ShapeDtypeStruct((b, l, MLSTM_INNER), BF16), jax.ShapeDtypeStruct((b, l, MLSTM_INNER), BF16)]
    if emit_final:
        out_specs += [pl.BlockSpec((1,) + s[1:], st_index) for s in st_shapes]
        out_shape += [jax.ShapeDtypeStruct(s, F32) for s in st_shapes]
    res = pl.pallas_call(
        functools.partial(_mlstm_kernel, has_init=init is not None, emit_final=emit_final),
        grid=(b, nc),
        in_specs=in_specs,
        out_specs=out_specs,
        out_shape=out_shape,
        scratch_shapes=[pltpu.VMEM((dh, dh), F32)] * nst + [pltpu.VMEM((dh, dh), BF16)] * nst
        + [pltpu.VMEM((SUBLANES, dh), F32)] * nst + [pltpu.VMEM((SUBLANES, LANES), F32)] * nst,
        compiler_params=_cparams(("parallel", "arbitrary")),
        name="mlstm_scan",
    )(*operands)
    return res[0], res[1], tuple(res[2:])


def _post1_kernel(hf_ref, hb_ref, xc_ref, z_ref, x_ref, mod_ref, mh_ref, sk_ref, wd_ref, out_ref, *, tm, ncol):
    hs = hf_ref[0].astype(F32) + hb_ref[0].astype(F32)
    parts = []
    for hd in range(MLSTM_HEADS):
        hh = hs[:, hd * MLSTM_DH:(hd + 1) * MLSTM_DH]
        mu = jnp.mean(hh, axis=-1, keepdims=True)
        cen = hh - mu
        parts.append(cen * lax.rsqrt(jnp.mean(cen * cen, axis=-1, keepdims=True) + EPS))
    feat = (jnp.concatenate(parts, axis=1) * mh_ref[...] + sk_ref[...] * xc_ref[0].astype(F32)) \
        * _silu(z_ref[0].astype(F32))
    res = mod_ref[0, 2:3, :] * _dot(feat.astype(BF16), wd_ref[...])
    if ncol:
        rows = tm // ncol
        for j in range(ncol):
            out_ref[0, :, j, :] = x_ref[0, :, j, :] + res[j * rows:(j + 1) * rows, :]
    else:
        out_ref[0] = x_ref[0] + res


def _post1(p, hf, hb, x, mod, wp, tm, colmajor):
    b, l, _ = x.shape
    tm = min(tm, l)
    bm = mod.shape[0]
    mod_idx = (lambda bi, i: (bi, 0, 0)) if bm > 1 else (lambda bi, i: (0, 0, 0))
    tok = lambda w: pl.BlockSpec((1, tm, w), lambda bi, i: (bi, i, 0))
    if colmajor:
        rows = l // GRID_W
        ncol = tm // rows
        xv = x.reshape(b, rows, GRID_W, D_MODEL)
        x_spec = pl.BlockSpec((1, rows, ncol, D_MODEL), lambda bi, i: (bi, 0, i, 0))
    else:
        ncol = 0
        xv = x
        x_spec = tok(D_MODEL)
    out = pl.pallas_call(
        functools.partial(_post1_kernel, tm=tm, ncol=ncol),
        grid=(b, l // tm),
        in_specs=[tok(MLSTM_INNER), tok(MLSTM_INNER), tok(MLSTM_INNER), tok(MLSTM_INNER), x_spec,
                  pl.BlockSpec((1, N_MOD, D_MODEL), mod_idx),
                  _const_spec((1, MLSTM_INNER)), _const_spec((1, MLSTM_INNER)), _const_spec(wp["w_down"].shape)],
        out_specs=x_spec,
        out_shape=jax.ShapeDtypeStruct(xv.shape, F32),
        compiler_params=_cparams(("parallel", "arbitrary")),
        name="post1",
    )(hf, hb, p["xc"], p["z"], xv, mod, wp["mh_norm"], wp["skip"], wp["w_down"])
    return out.reshape(b, l, D_MODEL)


def _prep_even(w_in, w_gate, b_gate, gla_norm, conv_w, conv_b, a_log, dt_bias, d_skip, ssd_norm, w_out):
    sizes = (GLA_QK, GLA_QK, GLA_V, GLA_V, 2 * GLA_GATE_RANK, SSD_INNER, SSD_XBC, 2 * SSD_HEADS)
    offs = [0]
    for s in sizes:
        offs.append(offs[-1] + s)
    w16 = w_in.astype(BF16)
    w_main = jnp.concatenate([w16[:, offs[0]:offs[4]], w16[:, offs[5]:offs[7]]], axis=1)
    pad = jnp.zeros((D_MODEL, LANES - 2 * GLA_GATE_RANK - 2 * SSD_HEADS), BF16)
    w_small = jnp.concatenate([w16[:, offs[4]:offs[5]], w16[:, offs[7]:offs[8]], pad], axis=1)
    zero = jnp.zeros((GLA_GATE_RANK, GLA_QK), w_gate.dtype)
    wg = jnp.concatenate([jnp.concatenate([w_gate[0], zero], axis=1),
                          jnp.concatenate([zero, w_gate[1]], axis=1)], axis=0).astype(BF16)
    return dict(
        w_main=w_main, w_small=w_small, w_dtT=jnp.transpose(w_in[:, offs[7]:offs[8]]).astype(BF16),
        w_gate=wg, b_gate=b_gate.reshape(1, 2 * GLA_QK),
        conv_w=conv_w, conv_b=conv_b.reshape(1, SSD_XBC),
        dt_bias=dt_bias.reshape(1, 2 * SSD_HEADS), a_log=a_log.reshape(1, 2 * SSD_HEADS),
        dt_biasT=dt_bias.reshape(2 * SSD_HEADS, 1), a_logT=a_log.reshape(2 * SSD_HEADS, 1),
        gla_norm=jnp.tile(gla_norm, GLA_HEADS).reshape(1, GLA_V),
        d_skip=jnp.repeat(d_skip, SSD_HEADDIM).reshape(1, SSD_INNER),
        ssd_norm=ssd_norm.reshape(1, SSD_INNER),
        w_out=w_out.astype(BF16),
    )


def _block_diag_tiles(w, transposed=False, dtype=BF16):
    per = BD_TILE // QKV_BLOCK
    w4 = w.reshape(N_BD_TILES, per, QKV_BLOCK, QKV_BLOCK)
    blk = lax.broadcasted_iota(jnp.int32, (BD_TILE, BD_TILE), 0) // QKV_BLOCK
    same_block = blk == jnp.transpose(blk)
    if transposed:
        cols = jnp.transpose(w4, (0, 3, 1, 2)).reshape(N_BD_TILES, QKV_BLOCK, BD_TILE)
        full = jnp.tile(cols, (1, per, 1))
    else:
        rows = w4.reshape(N_BD_TILES, BD_TILE, QKV_BLOCK)
        full = jnp.tile(rows, (1, 1, per))
    return jnp.where(same_block, full, 0.0).astype(dtype)


def _prep_odd(w_up, conv_w, conv_b, wq, wk, wv, w_if, b_if, mh_norm, skip, w_down):
    ng = 4 * MLSTM_HEADS
    wif = jnp.transpose(w_if, (1, 0, 2)).reshape(3, N_BD_TILES, BD_TILE, ng)

    def fold(wblk, wg):
        return jnp.einsum("tio,tog->tgi", _block_diag_tiles(wblk, dtype=F32), wg, precision=lax.Precision.HIGHEST)

    wgc = (fold(wq, wif[0]) + fold(wk, wif[1])).astype(BF16)
    bdkT = jnp.concatenate([_block_diag_tiles(wk, transposed=True), wgc], axis=1)
    return dict(
        w_up=w_up.astype(BF16), conv_w=conv_w, conv_b=conv_b.reshape(1, MLSTM_INNER),
        bdq=_block_diag_tiles(wq), bdkT=bdkT, bdv=_block_diag_tiles(wv),
        wgm=fold(wv, wif[2]).astype(BF16), b_if=b_if.reshape(ng, 1),
        mh_norm=mh_norm.reshape(1, MLSTM_INNER), skip=skip.reshape(1, MLSTM_INNER),
        w_down=w_down.astype(BF16),
    )


TOKEN_TILE = 512


def kernel(x, c, ctx, c_ctx, w_mod, b_mod, norm_mix, norm_ffn, ffn_w1, ffn_w3, ffn_w2, a_w_in, a_gla_w_gate,
           a_gla_b_gate, a_gla_norm, a_ssd_conv_w, a_ssd_conv_b, a_ssd_A_log, a_ssd_dt_bias, a_ssd_D, a_ssd_norm,
           a_w_out, b_w_up, b_conv_w, b_conv_b, b_wq, b_wk, b_wv, b_w_if, b_b_if, b_mh_norm, b_skip, b_w_down,
           final_norm):
    tile = TOKEN_TILE
    bsz = x.shape[0]
    depth = w_mod.shape[0]
    pad_rows = (-(bsz + 1)) % 8
    cc = jnp.concatenate([c, c_ctx[None, :], jnp.zeros((pad_rows, D_MODEL), c.dtype)], axis=0)
    mods = _modulation(cc, w_mod, b_mod).reshape(depth, bsz + 1 + pad_rows, N_MOD, D_MODEL)
    fnorm = final_norm.reshape(1, D_MODEL)
    w1, w3, w2 = ffn_w1.astype(BF16), ffn_w3.astype(BF16), ffn_w2.astype(BF16)
    for layer in range(depth):
        last = layer == depth - 1
        j = layer // 2
        mx = mods[layer, :bsz]
        mc = mods[layer, bsz:bsz + 1]
        nmix = norm_mix[layer].reshape(1, D_MODEL)
        nffn = norm_ffn[layer].reshape(1, D_MODEL)
        if layer % 2 == 0:
            wp = _prep_even(a_w_in[j], a_gla_w_gate[j], a_gla_b_gate[j], a_gla_norm[j], a_ssd_conv_w[j],
                            a_ssd_conv_b[j], a_ssd_A_log[j], a_ssd_dt_bias[j], a_ssd_D[j], a_ssd_norm[j],
                            a_w_out[j])
            pc = _inproj0(ctx, mc, nmix, wp, tile)
            gla0 = jnp.zeros((bsz, 2 * GLA_HEADS, GLA_DK, GLA_DV), F32)
            ssd0 = jnp.zeros((bsz, 2 * SSD_GROUPS, SSD_STATE, HEADS_PER_GROUP * SSD_HEADDIM), F32)
            ofc, obc, gla_st = _gla_scan(pc, gla0, GLA_INTRA_TILE)
            yfc, ybc, ssd_st = _ssd_scan(pc, ssd0)
            px = _inproj0(x, mx, nmix, wp, tile)
            ofx, obx, _ = _gla_scan(px, gla_st, GLA_INTRA_TILE)
            yfx, ybx, _ = _ssd_scan(px, ssd_st)
            x = _post0(px, ofx, obx, yfx, ybx, x, mx, wp, tile)
            if not last:
                ctx = _post0(pc, ofc, obc, yfc, ybc, ctx, mc, wp, tile)
        else:
            wp = _prep_odd(b_w_up[j], b_conv_w[j], b_conv_b[j], b_wq[j], b_wk[j], b_wv[j], b_w_if[j], b_b_if[j],
                           b_mh_norm[j], b_skip[j], b_w_down[j])
            pc = _inproj1(ctx, mc, nmix, wp, tile, colmajor=False)
            hfc, hbc, state = _mlstm_scan(pc, None, True)
            px = _inproj1(x, mx, nmix, wp, tile, colmajor=True)
            hfx, hbx, _ = _mlstm_scan(px, state, False)
            x = _post1(px, hfx, hbx, x, mx, wp, tile, colmajor=True)
            if not last:
                ctx = _post1(pc, hfc, hbc, ctx, mc, wp, tile, colmajor=False)
        x = _ffn(x, mx, nffn, w1, w3, w2, layer, fnorm, last, tile)
        if not last:
            ctx = _ffn(ctx, mc, nffn, w1, w3, w2, layer, fnorm, False, tile)
    return x
```
